```python
import math
import jax, jax.numpy as jnp
from jax import lax
import numpy as np

D_MODEL = 2048
BATCH = 4
SEQ = 2048
DEPTH = 1
DEC_BATCH = 128
DEC_SEQ = 4
PAST_LEN = 16384
PAGE_SIZE = 128

D_RG = D_MODEL
RG_BLOCKS = 16
RG_BLOCK_W = D_RG // RG_BLOCKS
RG_C = 8.0
D_SSD = D_MODEL
SSD_HEAD_DIM = 64
SSD_HEADS = D_SSD // SSD_HEAD_DIM
SSD_GROUPS = 4
SSD_STATE = 128
SSD_CHUNK = 128
CONV_W = 4
D_XBC = D_SSD + 2 * SSD_GROUPS * SSD_STATE
D_MIX = D_RG + D_SSD
D_IN_PROJ = 2 * D_RG + D_SSD + D_XBC + SSD_HEADS
N_EXPERTS = 32
TOP_K = 4
D_FF = D_MODEL
SWIGLU_ALPHA = 1.702
SWIGLU_LIMIT = 7.0
MOE_BLOCK = 128
EPS = 1e-6

kernel_name = "hymba_rglru_ssd_moe_step"


def _rmsnorm(x, g):
    xf = x.astype(jnp.float32)
    return xf * lax.rsqrt(jnp.mean(xf * xf, axis=-1, keepdims=True) + EPS) * g.astype(jnp.float32)


def _causal_conv(u, buf, w, b):
    L = u.shape[1]
    xp = jnp.concatenate([buf.astype(u.dtype), u], axis=1)
    y = b.astype(jnp.float32)
    for k in range(CONV_W):
        y = y + xp[:, k:k + L] * w[k]
    return y, xp[:, L:]


def _rglru(x, h0, wa, ba, wi, bi, lam):
    b, L, _ = x.shape
    xb = x.reshape(b, L, RG_BLOCKS, RG_BLOCK_W)
    r = jax.nn.sigmoid(jnp.einsum('blhi,hij->blhj', xb, wa).reshape(b, L, D_RG) + ba)
    i = jax.nn.sigmoid(jnp.einsum('blhi,hij->blhj', xb, wi).reshape(b, L, D_RG) + bi)
    log_a = -RG_C * r * jax.nn.softplus(-lam.astype(jnp.float32))
    a = jnp.exp(log_a)
    u = jnp.sqrt(-jnp.expm1(2.0 * log_a)) * (i * x)
    u = u.at[:, 0].add(a[:, 0] * h0)

    def combine(p, q):
        return p[0] * q[0], q[0] * p[1] + q[1]

    _, hs = lax.associative_scan(combine, (a, u), axis=1)
    return hs, hs[:, -1]


def _ssd(x, dt, A, Bm, Cm, s0):
    b, L, H, P = x.shape
    G, N = SSD_GROUPS, SSD_STATE
    E = H // G
    cs = SSD_CHUNK if L % SSD_CHUNK == 0 else L
    nc = L // cs
    x = x.reshape(b, nc, cs, G, E, P)
    dt = dt.reshape(b, nc, cs, G, E)
    Bm = Bm.reshape(b, nc, cs, G, N)
    Cm = Cm.reshape(b, nc, cs, G, N)
    acum = jnp.cumsum(dt * A.reshape(G, E), axis=2)
    acum_t = jnp.moveaxis(acum, 2, -1)
    seg = acum_t[..., :, None] - acum_t[..., None, :]
    causal = jnp.tril(jnp.ones((cs, cs), dtype=bool))
    decay = jnp.exp(jnp.where(causal, seg, -jnp.inf))
    cb = jnp.einsum('bclgn,bcsgn->bcgls', Cm, Bm)
    xdt = x * dt[..., None]
    y_diag = jnp.einsum('bcgels,bcsgep->bclgep', cb[:, :, :, None] * decay, xdt)
    decay_end = jnp.exp(acum[:, :, -1:] - acum)
    chunk_states = jnp.einsum('bclgn,bclgep->bcgepn', Bm, xdt * decay_end[..., None])
    chunk_decay = jnp.exp(acum[:, :, -1])

    def step(s, inp):
        dec, st = inp
        return dec[..., None, None] * s + st, s

    s_final, s_in = lax.scan(step, s0.reshape(b, G, E, P, N),
                             (jnp.moveaxis(chunk_decay, 1, 0), jnp.moveaxis(chunk_states, 1, 0)))
    s_in = jnp.moveaxis(s_in, 0, 1)
    y_off = jnp.einsum('bclgn,bcgepn->bclgep', Cm, s_in) * jnp.exp(acum)[..., None]
    y = (y_diag + y_off).reshape(b, L, H, P)
    return y, s_final.reshape(b, H, P, N)


def _moe(x2d, w_router, b_router, w_gu, b_gu, w_down, b_down):
    T, D = x2d.shape
    logits = (x2d @ w_router + b_router).astype(jnp.float32)
    top_v, top_i = lax.top_k(logits, TOP_K)
    gates = jax.nn.softmax(top_v, axis=-1)
    n_assign = T * TOP_K
    e_flat = top_i.reshape(-1)
    tok_flat = jnp.repeat(jnp.arange(T, dtype=jnp.int32), TOP_K)
    g_flat = gates.reshape(-1)
    order = jnp.argsort(e_flat)
    se = e_flat[order]
    counts = jnp.bincount(e_flat, length=N_EXPERTS)
    padded = ((counts + MOE_BLOCK - 1) // MOE_BLOCK) * MOE_BLOCK
    start = jnp.cumsum(counts) - counts
    pend = jnp.cumsum(padded)
    pstart = pend - padded
    dest = pstart[se] + jnp.arange(n_assign) - start[se]
    n_blocks = (n_assign + MOE_BLOCK - 1) // MOE_BLOCK + N_EXPERTS
    slot_tok = jnp.full((n_blocks * MOE_BLOCK,), T, jnp.int32).at[dest].set(tok_flat[order])
    slot_gate = jnp.zeros((n_blocks * MOE_BLOCK,), jnp.float32).at[dest].set(g_flat[order])
    block_e = jnp.minimum(jnp.searchsorted(pend, jnp.arange(n_blocks) * MOE_BLOCK, side='right'),
                          N_EXPERTS - 1)
    x_pad = jnp.concatenate([x2d, jnp.zeros((1, D), x2d.dtype)], axis=0)
    xb = x_pad[slot_tok].reshape(n_blocks, MOE_BLOCK, D)

    def expert_block(args):
        xblk, e = args
        gu = xblk @ w_gu[e] + b_gu[e]
        gate, up = gu[:, :D_FF], gu[:, D_FF:]
        gate = jnp.minimum(gate, SWIGLU_LIMIT)
        up = jnp.clip(up, -SWIGLU_LIMIT, SWIGLU_LIMIT)
        glu = gate * jax.nn.sigmoid(SWIGLU_ALPHA * gate)
        return ((up + 1.0) * glu) @ w_down[e] + b_down[e]

    out = lax.map(expert_block, (xb, block_e)).reshape(-1, D).astype(jnp.float32)
    y = jax.ops.segment_sum(out * slot_gate[:, None], slot_tok, num_segments=T + 1)
    return y[:T]


def _layer(x, rg_h0, rg_buf0, ssd_s0, ssd_buf0, p):
    (g_mix, w_in, conv_rg_w, conv_rg_b, rg_wa, rg_ba, rg_wi, rg_bi, rg_lambda,
     conv_ssd_w, conv_ssd_b, ssd_dt_bias, ssd_a_log, ssd_d, ssd_norm_g, w_out,
     g_ffn, w_router, b_router, w_gu, b_gu, w_down, b_down) = p
    b, L, _ = x.shape
    x = x.astype(jnp.float32)
    h = _rmsnorm(x, g_mix)
    proj = (h @ w_in).astype(jnp.float32)
    o1 = D_RG
    o2 = o1 + D_RG
    o3 = o2 + D_SSD
    o4 = o3 + D_XBC
    rg_x, rg_gate, z, xbc, dt_raw = jnp.split(proj, [o1, o2, o3, o4], axis=-1)
    rg_c, rg_buf = _causal_conv(rg_x, rg_buf0.astype(jnp.float32), conv_rg_w, conv_rg_b)
    rg_hs, rg_h = _rglru(rg_c, rg_h0.astype(jnp.float32), rg_wa, rg_ba, rg_wi, rg_bi, rg_lambda)
    y_rg = rg_hs * jax.nn.gelu(rg_gate)
    xbc_c, ssd_buf = _causal_conv(xbc, ssd_buf0.astype(jnp.float32), conv_ssd_w, conv_ssd_b)
    xbc_c = jax.nn.silu(xbc_c)
    gn = SSD_GROUPS * SSD_STATE
    xs = xbc_c[..., :D_SSD].reshape(b, L, SSD_HEADS, SSD_HEAD_DIM)
    Bm = xbc_c[..., D_SSD:D_SSD + gn].reshape(b, L, SSD_GROUPS, SSD_STATE)
    Cm = xbc_c[..., D_SSD + gn:].reshape(b, L, SSD_GROUPS, SSD_STATE)
    dt = jax.nn.softplus(dt_raw + ssd_dt_bias)
    A = -jnp.exp(ssd_a_log.astype(jnp.float32))
    y_ssd, ssd_s = _ssd(xs, dt, A, Bm, Cm, ssd_s0.astype(jnp.float32))
    y_ssd = (y_ssd + ssd_d[:, None] * xs).reshape(b, L, D_SSD) * jax.nn.silu(z)
    y_ssd = _rmsnorm(y_ssd.reshape(b, L, SSD_GROUPS, D_SSD // SSD_GROUPS),
                     jnp.ones((), jnp.float32)).reshape(b, L, D_SSD) * ssd_norm_g
    x = x + jnp.concatenate([y_rg, y_ssd], axis=-1) @ w_out
    h2 = _rmsnorm(x, g_ffn).reshape(b * L, D_MODEL)
    x = x + _moe(h2, w_router, b_router, w_gu, b_gu, w_down, b_down).reshape(b, L, D_MODEL)
    return x, rg_h, rg_buf, ssd_s, ssd_buf


def setup_inputs(seed: int = 0) -> dict:
    key = jax.random.key(seed)
    ks = iter(jax.random.split(key, 40))

    def nrm(shape, s):
        return jax.random.normal(next(ks), shape, jnp.float32) * s

    def unif(shape, lo, hi):
        return jax.random.uniform(next(ks), shape, jnp.float32, minval=lo, maxval=hi)

    x_prompt = nrm((BATCH, SEQ, D_MODEL), 1.0)
    x_sample = nrm((DEC_BATCH, DEC_SEQ, D_MODEL), 1.0)
    state_rglru_h = nrm((DEPTH, DEC_BATCH, D_RG), 0.5)
    state_rglru_conv = nrm((DEPTH, DEC_BATCH, CONV_W - 1, D_RG), 1.0)
    state_ssd = nrm((DEPTH, DEC_BATCH, SSD_HEADS, SSD_HEAD_DIM, SSD_STATE), 0.1)
    state_ssd_conv = nrm((DEPTH, DEC_BATCH, CONV_W - 1, D_XBC), 1.0)
    g_mix = 1.0 + nrm((DEPTH, D_MODEL), 0.02)
    w_in = nrm((DEPTH, D_MODEL, D_IN_PROJ), D_MODEL ** -0.5)
    conv_rg_w = nrm((DEPTH, CONV_W, D_RG), CONV_W ** -0.5)
    conv_rg_b = nrm((DEPTH, D_RG), 0.01)
    rg_wa = nrm((DEPTH, RG_BLOCKS, RG_BLOCK_W, RG_BLOCK_W), RG_BLOCK_W ** -0.5)
    rg_ba = nrm((DEPTH, D_RG), 0.01)
    rg_wi = nrm((DEPTH, RG_BLOCKS, RG_BLOCK_W, RG_BLOCK_W), RG_BLOCK_W ** -0.5)
    rg_bi = nrm((DEPTH, D_RG), 0.01)
    pa = unif((DEPTH, D_RG), 0.9, 0.999) ** (1.0 / RG_C)
    rg_lambda = jnp.log(pa) - jnp.log1p(-pa)
    conv_ssd_w = nrm((DEPTH, CONV_W, D_XBC), CONV_W ** -0.5)
    conv_ssd_b = nrm((DEPTH, D_XBC), 0.01)
    dt0 = jnp.exp(unif((DEPTH, SSD_HEADS), math.log(1e-3), math.log(1e-1)))
    ssd_dt_bias = dt0 + jnp.log(-jnp.expm1(-dt0))
    ssd_a_log = jnp.log(unif((DEPTH, SSD_HEADS), 1.0, 16.0))
    ssd_d = 1.0 + nrm((DEPTH, SSD_HEADS), 0.1)
    ssd_norm_g = 1.0 + nrm((DEPTH, D_SSD), 0.02)
    w_out = nrm((DEPTH, D_MIX, D_MODEL), D_MIX ** -0.5)
    g_ffn = 1.0 + nrm((DEPTH, D_MODEL), 0.02)
    w_router = nrm((DEPTH, D_MODEL, N_EXPERTS), D_MODEL ** -0.5)
    b_router = nrm((DEPTH, N_EXPERTS), 0.01)
    w_gu = nrm((DEPTH, N_EXPERTS, D_MODEL, 2 * D_FF), D_MODEL ** -0.5)
    b_gu = nrm((DEPTH, N_EXPERTS, 2 * D_FF), 0.01)
    w_down = nrm((DEPTH, N_EXPERTS, D_FF, D_MODEL), D_FF ** -0.5)
    b_down = nrm((DEPTH, N_EXPERTS, D_MODEL), 0.01)
    g_final = 1.0 + nrm((D_MODEL,), 0.02)
    return {"x_prompt": x_prompt, "x_sample": x_sample,
            "state_rglru_h": state_rglru_h, "state_rglru_conv": state_rglru_conv,
            "state_ssd": state_ssd, "state_ssd_conv": state_ssd_conv,
            "g_mix": g_mix, "w_in": w_in, "conv_rg_w": conv_rg_w, "conv_rg_b": conv_rg_b,
            "rg_wa": rg_wa, "rg_ba": rg_ba, "rg_wi": rg_wi, "rg_bi": rg_bi, "rg_lambda": rg_lambda,
            "conv_ssd_w": conv_ssd_w, "conv_ssd_b": conv_ssd_b, "ssd_dt_bias": ssd_dt_bias,
            "ssd_a_log": ssd_a_log, "ssd_d": ssd_d, "ssd_norm_g": ssd_norm_g, "w_out": w_out,
            "g_ffn": g_ffn, "w_router": w_router, "b_router": b_router, "w_gu": w_gu, "b_gu": b_gu,
            "w_down": w_down, "b_down": b_down, "g_final": g_final}


def reference(x_prompt, x_sample, state_rglru_h, state_rglru_conv, state_ssd, state_ssd_conv,
              g_mix, w_in, conv_rg_w, conv_rg_b, rg_wa, rg_ba, rg_wi, rg_bi, rg_lambda,
              conv_ssd_w, conv_ssd_b, ssd_dt_bias, ssd_a_log, ssd_d, ssd_norm_g, w_out,
              g_ffn, w_router, b_router, w_gu, b_gu, w_down, b_down, g_final):
    b_p = x_prompt.shape[0]
    xp = x_prompt
    xs = x_sample
    rgh_p, rgc_p, ssd_p, ssdc_p = [], [], [], []
    rgh_s, rgc_s, ssd_s, ssdc_s = [], [], [], []
    for l in range(DEPTH):
        p = (g_mix[l], w_in[l], conv_rg_w[l], conv_rg_b[l], rg_wa[l], rg_ba[l], rg_wi[l], rg_bi[l],
             rg_lambda[l], conv_ssd_w[l], conv_ssd_b[l], ssd_dt_bias[l], ssd_a_log[l], ssd_d[l],
             ssd_norm_g[l], w_out[l], g_ffn[l], w_router[l], b_router[l], w_gu[l], b_gu[l],
             w_down[l], b_down[l])
        xp, a1, a2, a3, a4 = _layer(
            xp,
            jnp.zeros((b_p, D_RG), jnp.float32),
            jnp.zeros((b_p, CONV_W - 1, D_RG), jnp.float32),
            jnp.zeros((b_p, SSD_HEADS, SSD_HEAD_DIM, SSD_STATE), jnp.float32),
            jnp.zeros((b_p, CONV_W - 1, D_XBC), jnp.float32), p)
        rgh_p.append(a1); rgc_p.append(a2); ssd_p.append(a3); ssdc_p.append(a4)
        xs, c1, c2, c3, c4 = _layer(xs, state_rglru_h[l], state_rglru_conv[l], state_ssd[l],
                                    state_ssd_conv[l], p)
        rgh_s.append(c1); rgc_s.append(c2); ssd_s.append(c3); ssdc_s.append(c4)
    y_prompt = _rmsnorm(xp, g_final).astype(x_prompt.dtype)
    y_sample = _rmsnorm(xs, g_final).astype(x_sample.dtype)
    return (y_prompt, y_sample,
            jnp.stack(rgh_p).astype(state_rglru_h.dtype), jnp.stack(rgc_p).astype(state_rglru_conv.dtype),
            jnp.stack(ssd_p).astype(state_ssd.dtype), jnp.stack(ssdc_p).astype(state_ssd_conv.dtype),
            jnp.stack(rgh_s).astype(state_rglru_h.dtype), jnp.stack(rgc_s).astype(state_rglru_conv.dtype),
            jnp.stack(ssd_s).astype(state_ssd.dtype), jnp.stack(ssdc_s).astype(state_ssd_conv.dtype))
```

```python
import functools

import jax
import jax.numpy as jnp
from jax import lax
from jax.experimental import pallas as pl
from jax.experimental.pallas import tpu as pltpu

F32 = jnp.float32
BF16 = jnp.bfloat16
I32 = jnp.int32
U32 = jnp.uint32
HIGHEST = lax.Precision.HIGHEST

EPS = 1e-6
D_MODEL = 2048
D_RG = 2048
RG_BLOCKS = 16
RG_BLOCK_W = 128
RG_C = 8.0
D_SSD = 2048
SSD_HEAD_DIM = 64
SSD_HEADS = 32
SSD_GROUPS = 4
SSD_STATE = 128
SSD_CHUNK = 128
CONV_W = 4
D_BC = SSD_GROUPS * SSD_STATE
D_XBC = D_SSD + 2 * D_BC
D_MAIN = 2 * D_RG + D_SSD + D_XBC
N_EXPERTS = 32
TOP_K = 4
D_FF = 2048
SWIGLU_ALPHA = 1.702
SWIGLU_LIMIT = 7.0

LANES = 128
VMEM_LIMIT = 56 * 1024 * 1024

ROW_TILE = 512
PROJ_TM = 1088
PROJ_TN = 1024
OUT_TN = 512
RG_TL = 256
SAMPLE_BB = 8
ROUTE_TM = 512
SCATTER_TM = 256
COMBINE_TM = 256
EXPERT_PAD = 256
EXPERT_BM = 256
EXPERT_R = 2048
EXPERT_TF = 256


def _cparams(sem, vmem=VMEM_LIMIT):
    return pltpu.CompilerParams(dimension_semantics=sem, vmem_limit_bytes=vmem)


def _softplus(x):
    return jnp.maximum(x, 0.0) + jnp.log1p(jnp.exp(-jnp.abs(x)))


_sigmoid = jax.nn.sigmoid
_silu = jax.nn.silu


def _gelu_tanh(x):
    return jax.nn.gelu(x, approximate=True)


def _head_expand_matrix(dtype):
    r = lax.broadcasted_iota(I32, (LANES, D_SSD), 0)
    c = lax.broadcasted_iota(I32, (LANES, D_SSD), 1)
    return jnp.where(lax.shift_right_logical(c, 6) == r, 1.0, 0.0).astype(dtype)


def _expand_heads(v, e_f32):
    return jnp.dot(v, e_f32, precision=HIGHEST, preferred_element_type=F32)


def _norm_dt_body(xp_ref, xs_ref, g_ref, wdt_ref, hb_ref, dt_ref, *, n_prompt_tiles):
    i = pl.program_id(0)

    def run(x_ref):
        x = x_ref[...]
        h = x * lax.rsqrt(jnp.mean(x * x, axis=-1, keepdims=True) + EPS) * g_ref[...]
        hb = h.astype(BF16)
        hb_ref[...] = hb
        dt_ref[...] = jnp.dot(hb, wdt_ref[...].astype(BF16), preferred_element_type=F32)

    @pl.when(i < n_prompt_tiles)
    def _():
        run(xp_ref)

    @pl.when(i >= n_prompt_tiles)
    def _():
        run(xs_ref)


def _norm_dt(xp, xs, g, wdt):
    tp, ts = xp.shape[0], xs.shape[0]
    npt, nst = tp // ROW_TILE, ts // ROW_TILE
    t = tp + ts
    return pl.pallas_call(
        functools.partial(_norm_dt_body, n_prompt_tiles=npt),
        grid=(npt + nst,),
        in_specs=[
            pl.BlockSpec((ROW_TILE, D_MODEL), lambda i: (jnp.minimum(i, npt - 1), 0)),
            pl.BlockSpec((ROW_TILE, D_MODEL), lambda i: (jnp.maximum(i - npt, 0), 0)),
            pl.BlockSpec((1, D_MODEL), lambda i: (0, 0)),
            pl.BlockSpec((D_MODEL, LANES), lambda i: (0, 0)),
        ],
        out_specs=[
            pl.BlockSpec((ROW_TILE, D_MODEL), lambda i: (i, 0)),
            pl.BlockSpec((ROW_TILE, LANES), lambda i: (i, 0)),
        ],
        out_shape=[jax.ShapeDtypeStruct((t, D_MODEL), BF16), jax.ShapeDtypeStruct((t, LANES), F32)],
        compiler_params=_cparams(("parallel",)),
        name="norm_dt",
    )(xp, xs, g, wdt)


def _cast_weight_tile(w_ref, wb_ref, rows_per_step=256):
    k = w_ref.shape[0]

    def step(s, c):
        r = pl.multiple_of(s * rows_per_step, rows_per_step)
        wb_ref[pl.ds(r, rows_per_step), :] = w_ref[pl.ds(r, rows_per_step), :].astype(BF16)
        return c

    lax.fori_loop(0, k // rows_per_step, step, 0)


def _in_proj_body(x_ref, w_ref, o_ref, wb_ref):
    @pl.when(pl.program_id(1) == 0)
    def _():
        _cast_weight_tile(w_ref, wb_ref)

    o_ref[...] = jnp.dot(x_ref[...], wb_ref[...], preferred_element_type=F32)


def _in_proj(hb, w_in):
    t = hb.shape[0]
    return pl.pallas_call(
        _in_proj_body,
        grid=(D_MAIN // PROJ_TN, t // PROJ_TM),
        in_specs=[
            pl.BlockSpec((PROJ_TM, D_MODEL), lambda j, i: (i, 0)),
            pl.BlockSpec((D_MODEL, PROJ_TN), lambda j, i: (0, j)),
        ],
        out_specs=pl.BlockSpec((PROJ_TM, PROJ_TN), lambda j, i: (i, j)),
        out_shape=jax.ShapeDtypeStruct((t, D_MAIN), F32),
        scratch_shapes=[pltpu.VMEM((D_MODEL, PROJ_TN), BF16)],
        compiler_params=_cparams(("arbitrary", "arbitrary")),
        name="in_proj",
    )(hb, w_in)


def _out_proj_body(rg_ref, ssd_ref, ms_ref, w_ref, xp_ref, xs_ref, o_ref, wb_ref, *, n_prompt_tiles):
    i = pl.program_id(1)

    @pl.when(i == 0)
    def _():
        _cast_weight_tile(w_ref, wb_ref)

    @pl.when(i < n_prompt_tiles)
    def _():
        m = jnp.concatenate([rg_ref[...], ssd_ref[...]], axis=1)
        o_ref[...] = xp_ref[...] + jnp.dot(m, wb_ref[...], preferred_element_type=F32)

    @pl.when(i >= n_prompt_tiles)
    def _():
        o_ref[...] = xs_ref[...] + jnp.dot(ms_ref[...].astype(BF16), wb_ref[...], preferred_element_type=F32)


def _out_proj(y_rg, y_ssd, mix_s, w_out, xp, xs):
    tp, ts = y_rg.shape[0], mix_s.shape[0]
    npt, nst = tp // ROW_TILE, ts // ROW_TILE
    prompt_rows = lambda j, i: (jnp.minimum(i, npt - 1), 0)
    return pl.pallas_call(
        functools.partial(_out_proj_body, n_prompt_tiles=npt),
        grid=(D_MODEL // OUT_TN, npt + nst),
        in_specs=[
            pl.BlockSpec((ROW_TILE, D_MODEL), prompt_rows),
            pl.BlockSpec((ROW_TILE, D_MODEL), prompt_rows),
            pl.BlockSpec((ROW_TILE, 2 * D_MODEL), lambda j, i: (jnp.maximum(i - npt, 0), 0)),
            pl.BlockSpec((2 * D_MODEL, OUT_TN), lambda j, i: (0, j)),
            pl.BlockSpec((ROW_TILE, OUT_TN), lambda j, i: (jnp.minimum(i, npt - 1), j)),
            pl.BlockSpec((ROW_TILE, OUT_TN), lambda j, i: (jnp.maximum(i - npt, 0), j)),
        ],
        out_specs=pl.BlockSpec((ROW_TILE, OUT_TN), lambda j, i: (i, j)),
        out_shape=jax.ShapeDtypeStruct((tp + ts, D_MODEL), F32),
        scratch_shapes=[pltpu.VMEM((2 * D_MODEL, OUT_TN), BF16)],
        compiler_params=_cparams(("arbitrary", "arbitrary")),
        name="out_proj",
    )(y_rg, y_ssd, mix_s, w_out, xp, xs)


def _rg_gates(xc, wa_ref, ba_ref, wi_ref, bi_ref, lam_ref):
    xcb = xc.astype(BF16)
    rs, is_ = [], []
    for h in range(RG_BLOCKS):
        xh = xcb[:, h * RG_BLOCK_W:(h + 1) * RG_BLOCK_W]
        rs.append(jnp.dot(xh, wa_ref[h].astype(BF16), preferred_element_type=F32))
        is_.append(jnp.dot(xh, wi_ref[h].astype(BF16), preferred_element_type=F32))
    r = _sigmoid(jnp.concatenate(rs, axis=1) + ba_ref[...])
    ig = _sigmoid(jnp.concatenate(is_, axis=1) + bi_ref[...])
    log_a = (-RG_C * r) * _softplus(-lam_ref[...])
    a = jnp.exp(log_a)
    u = jnp.sqrt(-jnp.tanh(log_a) * (a * a + 1.0)) * (ig * xc)
    return a, u


def _rglru_prompt_body(x_ref, gate_ref, cw_ref, cb_ref, wa_ref, ba_ref, wi_ref, bi_ref, lam_ref,
                       y_ref, h_ref, cs_ref, ext_ref, a_ref, u_ref, car_ref):
    c = pl.program_id(1)
    tl = x_ref.shape[0]

    @pl.when(c == 0)
    def _():
        ext_ref[0:8, :] = jnp.zeros((8, D_RG), F32)
        car_ref[...] = jnp.zeros((1, D_RG), F32)

    ext_ref[8:8 + tl, :] = x_ref[...]
    xc = cb_ref[...] + ext_ref[5:5 + tl, :] * cw_ref[0:1, :]
    xc = xc + ext_ref[6:6 + tl, :] * cw_ref[1:2, :]
    xc = xc + ext_ref[7:7 + tl, :] * cw_ref[2:3, :]
    xc = xc + ext_ref[8:8 + tl, :] * cw_ref[3:4, :]
    tail = ext_ref[tl:tl + 8, :]
    ext_ref[0:8, :] = tail

    a, u = _rg_gates(xc, wa_ref, ba_ref, wi_ref, bi_ref, lam_ref)
    a_ref[...] = a
    u_ref[...] = u

    row = lax.broadcasted_iota(I32, (8, D_RG), 0)

    def group(g, carry):
        r0 = pl.multiple_of(g * 8, 8)
        a8 = a_ref[pl.ds(r0, 8), :]
        u8 = u_ref[pl.ds(r0, 8), :]
        for s in (1, 2, 4):
            keep = row >= s
            a_sh = jnp.where(keep, pltpu.roll(a8, s, 0), 1.0)
            u_sh = jnp.where(keep, pltpu.roll(u8, s, 0), 0.0)
            u8 = a8 * u_sh + u8
            a8 = a8 * a_sh
        h8 = a8 * carry + u8
        u_ref[pl.ds(r0, 8), :] = h8
        return h8[7:8, :]

    carry = lax.fori_loop(0, tl // 8, group, car_ref[...])
    car_ref[...] = carry
    y_ref[...] = (u_ref[...] * _gelu_tanh(gate_ref[...])).astype(BF16)

    @pl.when(c == pl.num_programs(1) - 1)
    def _():
        h_ref[...] = carry
        cs_ref[...] = tail[5:8, :]


def _rglru_prompt(proj, batch, seq, conv_w, conv_b, wa, ba, wi, bi, lam):
    nc = seq // RG_TL
    t = batch * seq
    vec =pl.BlockSpec((1, D_RG), lambda b, c: (0, 0))
    blk = pl.BlockSpec((RG_BLOCKS, RG_BLOCK_W, RG_BLOCK_W), lambda b, c: (0, 0, 0))
    return pl.pallas_call(
        _rglru_prompt_body,
        grid=(batch, nc),
        in_specs=[
            pl.BlockSpec((RG_TL, D_RG), lambda b, c: (b * nc + c, 0)),
            pl.BlockSpec((RG_TL, D_RG), lambda b, c: (b * nc + c, 1)),
            pl.BlockSpec((CONV_W, D_RG), lambda b, c: (0, 0)),
            vec, blk, vec, blk, vec, vec,
        ],
        out_specs=[
            pl.BlockSpec((RG_TL, D_RG), lambda b, c: (b * nc + c, 0)),
            pl.BlockSpec((None, 1, D_RG), lambda b, c: (b, 0, 0)),
            pl.BlockSpec((None, CONV_W - 1, D_RG), lambda b, c: (b, 0, 0)),
        ],
        out_shape=[
            jax.ShapeDtypeStruct((t, D_RG), BF16),
            jax.ShapeDtypeStruct((batch, 1, D_RG), F32),
            jax.ShapeDtypeStruct((batch, CONV_W - 1, D_RG), F32),
        ],
        scratch_shapes=[
            pltpu.VMEM((RG_TL + 8, D_RG), F32),
            pltpu.VMEM((RG_TL, D_RG), F32),
            pltpu.VMEM((RG_TL, D_RG), F32),
            pltpu.VMEM((1, D_RG), F32),
        ],
        compiler_params=_cparams(("arbitrary", "arbitrary")),
        name="rglru_prompt",
    )(proj, proj, conv_w, conv_b, wa, ba, wi, bi, lam)


def _group_rmsnorm(y, g_row):
    outs = []
    for g in range(SSD_GROUPS):
        yg = y[:, g * D_BC:(g + 1) * D_BC]
        outs.append(yg * lax.rsqrt(jnp.mean(yg * yg, axis=-1, keepdims=True) + EPS))
    return jnp.concatenate(outs, axis=1) * g_row


def _ssd_prompt_body(xbc_ref, z_ref, dt_ref, cw_ref, cb_ref, dtb_ref, alog_ref, dexp_ref, ng_ref,
                     y_ref, s_out_ref, cs_ref, ext_ref, s_ref):
    c = pl.program_id(1)
    L = SSD_CHUNK

    @pl.when(c == 0)
    def _():
        ext_ref[0:8, :] = jnp.zeros((8, D_XBC), F32)
        s_ref[...] = jnp.zeros_like(s_ref)

    ext_ref[8:8 + L, :] = xbc_ref[...]
    xc = cb_ref[...] + ext_ref[5:5 + L, :] * cw_ref[0:1, :]
    xc = xc + ext_ref[6:6 + L, :] * cw_ref[1:2, :]
    xc = xc + ext_ref[7:7 + L, :] * cw_ref[2:3, :]
    xc = xc + ext_ref[8:8 + L, :] * cw_ref[3:4, :]
    tail = ext_ref[L:L + 8, :]
    ext_ref[0:8, :] = tail
    xc = _silu(xc)
    xs = xc[:, :D_SSD]
    bm = xc[:, D_SSD:D_SSD + D_BC].astype(BF16)
    cm = xc[:, D_SSD + D_BC:].astype(BF16)

    lane = lax.broadcasted_iota(I32, (1, LANES), 1)
    a_neg = jnp.where(lane < SSD_HEADS, -jnp.exp(alog_ref[...]), 0.0)
    dt = _softplus(dt_ref[...] + dtb_ref[...])
    da = dt * a_neg
    ri = lax.broadcasted_iota(I32, (L, L), 0)
    ci = lax.broadcasted_iota(I32, (L, L), 1)
    causal = ci <= ri
    tri = jnp.where(causal, 1.0, 0.0).astype(F32)
    acum = jnp.dot(tri, da, precision=HIGHEST, preferred_element_type=F32)
    acum_t = acum.T
    a_last = acum[L - 1:L, :]

    e_f32 = _head_expand_matrix(F32)
    stacked = jnp.concatenate([dt, jnp.exp(a_last - acum), jnp.exp(acum)], axis=0)
    expd = _expand_heads(stacked, e_f32)
    dt_e, dend_e, ea_e = expd[0:L], expd[L:2 * L], expd[2 * L:3 * L]
    xdt = xs * dt_e
    xdt_b = xdt.astype(BF16)
    w_b = (xdt * dend_e)

    lane_l = lax.broadcasted_iota(I32, (L, LANES), 1)
    lo = lane_l < SSD_HEAD_DIM
    y_parts = []
    for g in range(SSD_GROUPS):
        cg = cm[:, g * SSD_STATE:(g + 1) * SSD_STATE]
        bg = bm[:, g * SSD_STATE:(g + 1) * SSD_STATE]
        cb = lax.dot_general(cg, bg, (((1,), (1,)), ((), ())), preferred_element_type=F32)
        hpg = SSD_HEADS // SSD_GROUPS
        for jp in range(hpg // 2):
            ms = []
            for h in (g * hpg + 2 * jp, g * hpg + 2 * jp + 1):
                seg = acum[:, h:h + 1] - acum_t[h:h + 1, :]
                decay = jnp.exp(jnp.where(causal, seg, -jnp.inf))
                ms.append((cb * decay).astype(BF16))
            col = (g * hpg + 2 * jp) * SSD_HEAD_DIM
            xp = xdt_b[:, col:col + LANES]
            zero = jnp.zeros_like(xp)
            rhs = jnp.concatenate([jnp.where(lo, xp, zero), jnp.where(lo, zero, xp)], axis=0)
            y_parts.append(jnp.dot(jnp.concatenate(ms, axis=1), rhs, preferred_element_type=F32))
    y_diag = jnp.concatenate(y_parts, axis=1)

    y_off_parts = []
    for g in range(SSD_GROUPS):
        cg = cm[:, g * SSD_STATE:(g + 1) * SSD_STATE]
        sg = s_ref[g * D_BC:(g + 1) * D_BC, :].astype(BF16)
        y_off_parts.append(lax.dot_general(cg, sg, (((1,), (1,)), ((), ())), preferred_element_type=F32))
    y_off = jnp.concatenate(y_off_parts, axis=1) * ea_e

    cd_col = jnp.exp(acum_t[:, L - 1:L])
    for g in range(SSD_GROUPS):
        bg = bm[:, g * SSD_STATE:(g + 1) * SSD_STATE]
        wg_t = w_b[:, g * D_BC:(g + 1) * D_BC].T.astype(BF16)
        upd = jnp.dot(wg_t, bg, preferred_element_type=F32)
        hpg = SSD_HEADS // SSD_GROUPS
        for e in range(hpg):
            h = g * hpg + e
            r0 = h * SSD_HEAD_DIM
            cd = jnp.broadcast_to(cd_col[h:h + 1, :], (SSD_HEAD_DIM, SSD_STATE))
            s_ref[r0:r0 + SSD_HEAD_DIM, :] = (cd * s_ref[r0:r0 + SSD_HEAD_DIM, :]
                                              + upd[e * SSD_HEAD_DIM:(e + 1) * SSD_HEAD_DIM, :])

    y = (y_diag + y_off + dexp_ref[...] * xs) * _silu(z_ref[...])
    y_ref[...] = _group_rmsnorm(y, ng_ref[...]).astype(BF16)

    @pl.when(c == pl.num_programs(1) - 1)
    def _():
        s_out_ref[...] = s_ref[...]
        cs_ref[...] = tail[5:8, :]


def _ssd_prompt(proj, dt_raw, batch, seq, conv_w, conv_b, dt_bias, a_log, d_exp, norm_g):
    nc = seq // SSD_CHUNK
    L = SSD_CHUNK
    hp = SSD_HEADS * SSD_HEAD_DIM
    const = lambda b, c: (0, 0)
    return pl.pallas_call(
        _ssd_prompt_body,
        grid=(batch, nc),
        in_specs=[
            pl.BlockSpec((L, D_XBC), lambda b, c: (b * nc + c, 2)),
            pl.BlockSpec((L, D_SSD), lambda b, c: (b * nc + c, 2)),
            pl.BlockSpec((L, LANES), lambda b, c: (b * nc + c, 0)),
            pl.BlockSpec((CONV_W, D_XBC), const),
            pl.BlockSpec((1, D_XBC), const),
            pl.BlockSpec((1, LANES), const),
            pl.BlockSpec((1, LANES), const),
            pl.BlockSpec((1, D_SSD), const),
            pl.BlockSpec((1, D_SSD), const),
        ],
        out_specs=[
            pl.BlockSpec((L, D_SSD), lambda b, c: (b * nc + c, 0)),
            pl.BlockSpec((None, hp, SSD_STATE), lambda b, c: (b, 0, 0)),
            pl.BlockSpec((None, CONV_W - 1, D_XBC), lambda b, c: (b, 0, 0)),
        ],
        out_shape=[
            jax.ShapeDtypeStruct((batch * seq, D_SSD), BF16),
            jax.ShapeDtypeStruct((batch, hp, SSD_STATE), F32),
            jax.ShapeDtypeStruct((batch, CONV_W - 1, D_XBC), F32),
        ],
        scratch_shapes=[
            pltpu.VMEM((L + 8, D_XBC), F32),
            pltpu.VMEM((hp, SSD_STATE), F32),
        ],
        compiler_params=_cparams(("arbitrary", "arbitrary")),
        name="ssd_prompt",
    )(proj, proj, dt_raw, conv_w, conv_b, dt_bias, a_log, d_exp, norm_g)


def _mix_sample_body(p_ref, dt_ref, h0_ref, rgc0_ref, s0_ref, sc0_ref,
                     rcw_ref, rcb_ref, wa_ref, ba_ref, wi_ref, bi_ref, lam_ref,
                     scw_ref, scb_ref, dtb_ref, alog_ref, dexp_ref, ng_ref,
                     y_ref, h_ref, rgc_ref, s_ref, sc_ref):
    nl = p_ref.shape[1] // D_MAIN
    bb = p_ref.shape[0]
    o_gate, o_z, o_xbc = D_RG, 2 * D_RG, 2 * D_RG + D_SSD

    def col(l, off, width):
        return p_ref[:, l * D_MAIN + off:l * D_MAIN + off + width]

    hist = [rgc0_ref[:, k * D_RG:(k + 1) * D_RG] for k in range(CONV_W - 1)]
    raw = hist + [col(l, 0, D_RG) for l in range(nl)]
    xcs = []
    for l in range(nl):
        acc = rcb_ref[...] + raw[l] * rcw_ref[0:1, :]
        for k in range(1, CONV_W):
            acc = acc + raw[l + k] * rcw_ref[k:k + 1, :]
        xcs.append(acc)
    a, u = _rg_gates(jnp.concatenate(xcs, axis=0), wa_ref, ba_ref, wi_ref, bi_ref, lam_ref)
    h = h0_ref[...]
    for l in range(nl):
        h = a[l * bb:(l + 1) * bb] * h + u[l * bb:(l + 1) * bb]
        y_ref[:, l * 2 * D_MODEL:l * 2 * D_MODEL + D_RG] = h * _gelu_tanh(col(l, o_gate, D_RG))
    h_ref[...] = h
    for k in range(CONV_W - 1):
        rgc_ref[:, k * D_RG:(k + 1) * D_RG] = raw[nl + k]

    hist = [sc0_ref[:, k * D_XBC:(k + 1) * D_XBC] for k in range(CONV_W - 1)]
    raw = hist + [col(l, o_xbc, D_XBC) for l in range(nl)]
    for k in range(CONV_W - 1):
        sc_ref[:, k * D_XBC:(k + 1) * D_XBC] = raw[nl + k]
    xcs = []
    for l in range(nl):
        acc = scb_ref[...] + raw[l] * scw_ref[0:1, :]
        for k in range(1, CONV_W):
            acc = acc + raw[l + k] * scw_ref[k:k + 1, :]
        xcs.append(_silu(acc))
    xs = [x[:, :D_SSD] for x in xcs]
    bms = [x[:, D_SSD:D_SSD + D_BC].astype(BF16) for x in xcs]
    cms = [x[:, D_SSD + D_BC:].astype(BF16) for x in xcs]

    lane = lax.broadcasted_iota(I32, (1, LANES), 1)
    a_neg = jnp.where(lane < SSD_HEADS, -jnp.exp(alog_ref[...]), 0.0)
    dts = [_softplus(dt_ref[:, l * LANES:(l + 1) * LANES] + dtb_ref[...]) for l in range(nl)]
    acums = []
    run = jnp.zeros((bb, LANES), F32)
    for l in range(nl):
        run = run + dts[l] * a_neg
        acums.append(run)
    a_last = acums[nl - 1]

    e_f32 = _head_expand_matrix(F32)
    e_b16 = _head_expand_matrix(BF16)
    stacked = jnp.concatenate(dts + [jnp.exp(a_last - ac) for ac in acums] + [jnp.exp(ac) for ac in acums], axis=0)
    expd = _expand_heads(stacked, e_f32)
    dt_e = [expd[l * bb:(l + 1) * bb] for l in range(nl)]
    dend_e = [expd[(nl + l) * bb:(nl + l + 1) * bb] for l in range(nl)]
    ea_e = [expd[(2 * nl + l) * bb:(2 * nl + l + 1) * bb] for l in range(nl)]
    xdt = [xs[l] * dt_e[l] for l in range(nl)]
    xdt_r = [x.astype(BF16).astype(F32) for x in xdt]
    rows = nl * bb
    pad_rows = LANES - rows
    w_all = jnp.concatenate([(xdt[l] * dend_e[l]) for l in range(nl)]
                            + [jnp.zeros((pad_rows, D_SSD), F32)], axis=0)
    c_all = jnp.concatenate(cms, axis=0)
    b_all = jnp.concatenate(bms + [jnp.zeros((pad_rows, D_BC), BF16)], axis=0)

    r = lax.broadcasted_iota(I32, (D_BC, LANES), 0)
    cidx = lax.broadcasted_iota(I32, (D_BC, LANES), 1)
    gsum = jnp.where((lax.shift_right_logical(r, 7) == lax.shift_right_logical(cidx, 3)) & (cidx < SSD_HEADS),
                     1.0, 0.0).astype(F32)
    pairs = [(l, s) for l in range(nl) for s in range(l + 1)]
    prods = jnp.concatenate([cms[l].astype(F32) * bms[s].astype(F32) for (l, s) in pairs], axis=0)
    cbh = jnp.dot(prods, gsum, precision=HIGHEST, preferred_element_type=F32)
    m_list = []
    for idx, (l, s) in enumerate(pairs):
        decay = jnp.exp(acums[l] - acums[s])
        m_list.append((cbh[idx * bb:(idx + 1) * bb] * decay).astype(BF16))
    m_e = jnp.dot(jnp.concatenate(m_list, axis=0), e_b16, preferred_element_type=F32)
    y_diag = []
    for l in range(nl):
        acc = None
        for idx, (ll, s) in enumerate(pairs):
            if ll != l:
                continue
            term = m_e[idx * bb:(idx + 1) * bb] * xdt_r[s]
            acc = term if acc is None else acc + term
        y_diag.append(acc)

    cd_t = jnp.concatenate([jnp.exp(a_last), jnp.zeros((LANES - bb, LANES), F32)], axis=0).T
    row_seq = lax.broadcasted_iota(I32, (rows, 1), 0) & (bb - 1)
    row_pad = lax.broadcasted_iota(I32, (LANES, 1), 0)
    lane_q = lax.broadcasted_iota(I32, (LANES, LANES), 1)
    hpg = SSD_HEADS // SSD_GROUPS

    def seq_step(q, y_off):
        mine = row_seq == q
        mine_pad = ((row_pad & (bb - 1)) == q) & (row_pad < rows)
        cd_q = jnp.sum(jnp.where(lane_q == q, cd_t, 0.0), axis=1, keepdims=True)
        cd_q = jnp.broadcast_to(cd_q, (LANES, SSD_STATE))
        parts = []
        for g in range(SSD_GROUPS):
            s0 = s0_ref[q, g * D_BC:(g + 1) * D_BC, :]
            cg = c_all[:, g * SSD_STATE:(g + 1) * SSD_STATE]
            yq = lax.dot_general(cg, s0.astype(BF16), (((1,), (1,)), ((), ())), preferred_element_type=F32)
            parts.append(yq)
            bg = b_all[:, g * SSD_STATE:(g + 1) * SSD_STATE]
            wq = jnp.where(mine_pad, w_all[:, g * D_BC:(g + 1) * D_BC], 0.0)
            upd = jnp.dot(wq.T.astype(BF16), bg, preferred_element_type=F32)
            for e in range(hpg):
                hh = g * hpg + e
                cd = jnp.broadcast_to(cd_q[hh:hh + 1, :], (SSD_HEAD_DIM, SSD_STATE))
                s_ref[q, hh * SSD_HEAD_DIM:(hh + 1) * SSD_HEAD_DIM, :] = (
                    cd * s0[e * SSD_HEAD_DIM:(e + 1) * SSD_HEAD_DIM, :]
                    + upd[e * SSD_HEAD_DIM:(e + 1) * SSD_HEAD_DIM, :])
        yq_all = jnp.concatenate(parts, axis=1)
        return jnp.where(mine, yq_all, y_off)

    y_off = lax.fori_loop(0, bb, seq_step, jnp.zeros((rows, D_SSD), F32))

    for l in range(nl):
        y = (y_diag[l] + y_off[l * bb:(l + 1) * bb] * ea_e[l] + dexp_ref[...] * xs[l]) * _silu(col(l, o_z, D_SSD))
        y_ref[:, l * 2 * D_MODEL + D_RG:(l + 1) * 2 * D_MODEL] = _group_rmsnorm(y, ng_ref[...])


def _mix_sample(proj4, dt4, row0, n_seq, h0, rgc0, s0, sc0, rg_params, ssd_params):
    bb = SAMPLE_BB
    nl = proj4.shape[1] // D_MAIN
    hp = SSD_HEADS * SSD_HEAD_DIM
    blk0 = row0 // bb
    const2 = lambda i: (0, 0)
    rg_specs = [
        pl.BlockSpec((CONV_W, D_RG), const2), pl.BlockSpec((1, D_RG), const2),
        pl.BlockSpec((RG_BLOCKS, RG_BLOCK_W, RG_BLOCK_W), lambda i: (0, 0, 0)), pl.BlockSpec((1, D_RG), const2),
        pl.BlockSpec((RG_BLOCKS, RG_BLOCK_W, RG_BLOCK_W), lambda i: (0, 0, 0)), pl.BlockSpec((1, D_RG), const2),
        pl.BlockSpec((1, D_RG), const2),
    ]
    ssd_specs = [
        pl.BlockSpec((CONV_W, D_XBC), const2), pl.BlockSpec((1, D_XBC), const2),
        pl.BlockSpec((1, LANES), const2), pl.BlockSpec((1, LANES), const2),
        pl.BlockSpec((1, D_SSD), const2), pl.BlockSpec((1, D_SSD), const2),
    ]
    return pl.pallas_call(
        _mix_sample_body,
        grid=(n_seq // bb,),
        in_specs=[
            pl.BlockSpec((bb, nl * D_MAIN), lambda i: (blk0 + i, 0)),
            pl.BlockSpec((bb, nl * LANES), lambda i: (blk0 + i, 0)),
            pl.BlockSpec((bb, D_RG), lambda i: (i, 0)),
            pl.BlockSpec((bb, (CONV_W - 1) * D_RG), lambda i: (i, 0)),
            pl.BlockSpec((bb, hp, SSD_STATE), lambda i: (i, 0, 0)),
            pl.BlockSpec((bb, (CONV_W - 1) * D_XBC), lambda i: (i, 0)),
        ] + rg_specs + ssd_specs,
        out_specs=[
            pl.BlockSpec((bb, nl * 2 * D_MODEL), lambda i: (i, 0)),
            pl.BlockSpec((bb, D_RG), lambda i: (i, 0)),
            pl.BlockSpec((bb, (CONV_W - 1) * D_RG), lambda i: (i, 0)),
            pl.BlockSpec((bb, hp, SSD_STATE), lambda i: (i, 0, 0)),
            pl.BlockSpec((bb, (CONV_W - 1) * D_XBC), lambda i: (i, 0)),
        ],
        out_shape=[
            jax.ShapeDtypeStruct((n_seq, nl * 2 * D_MODEL), F32),
            jax.ShapeDtypeStruct((n_seq, D_RG), F32),
            jax.ShapeDtypeStruct((n_seq, (CONV_W - 1) * D_RG), F32),
            jax.ShapeDtypeStruct((n_seq, hp, SSD_STATE), F32),
            jax.ShapeDtypeStruct((n_seq, (CONV_W - 1) * D_XBC), F32),
        ],
        compiler_params=_cparams(("arbitrary",)),
        name="mix_sample",
    )(proj4, dt4, h0, rgc0, s0, sc0, *rg_params, *ssd_params)


def _route_body(x_ref, g_ref, wr_ref, br_ref, hp_ref, e_ref, gate_ref, rank_ref, cnt_ref, run_ref):
    i = pl.program_id(0)
    tm = x_ref.shape[0]

    @pl.when(i == 0)
    def _():
        run_ref[...] = jnp.zeros_like(run_ref)

    x = x_ref[...]
    h = x * lax.rsqrt(jnp.mean(x * x, axis=-1, keepdims=True) + EPS) * g_ref[...]
    hb = lax.bitcast_convert_type(h.astype(BF16).astype(F32), U32)
    half = D_MODEL // 2
    hp_ref[...] = lax.shift_right_logical(hb[:, :half], jnp.uint32(16)) | (hb[:, half:] & jnp.uint32(0xFFFF0000))

    lane = lax.broadcasted_iota(I32, (tm, LANES), 1).astype(F32)
    logits = jnp.dot(h, wr_ref[...], precision=HIGHEST, preferred_element_type=F32) + br_ref[...]
    work = jnp.where(lane < N_EXPERTS, logits, -jnp.inf)
    vals, idxs = [], []
    multi = jnp.zeros((tm, LANES), F32)
    for _ in range(TOP_K):
        m = jnp.max(work, axis=-1, keepdims=True)
        idx = jnp.min(jnp.where(work == m, lane, float(LANES)), axis=-1, keepdims=True)
        hit = lane == idx
        vals.append(m)
        idxs.append(idx)
        multi = jnp.where(hit, 1.0, multi)
        work = jnp.where(hit, -jnp.inf, work)
    ex = [jnp.exp(v - vals[0]) for v in vals]
    den = ex[0] + ex[1] + ex[2] + ex[3]
    gates = [e / den for e in ex]

    ri = lax.broadcasted_iota(I32, (tm, tm), 0)
    ci = lax.broadcasted_iota(I32, (tm, tm), 1)
    strict = jnp.where(ci < ri, 1.0, 0.0).astype(BF16)
    before = jnp.dot(strict, multi.astype(BF16), preferred_element_type=F32) + run_ref[...]
    ranks = [jnp.sum(jnp.where(lane == idx, before, 0.0), axis=-1, keepdims=True) for idx in idxs]
    run_ref[...] = run_ref[...] + jnp.sum(multi, axis=0, keepdims=True)

    lane4 = lax.broadcasted_iota(I32, (tm, TOP_K), 1)
    e_out = jnp.zeros((tm, TOP_K), I32)
    g_out = jnp.zeros((tm, TOP_K), F32)
    r_out = jnp.zeros((tm, TOP_K), I32)
    for k in range(TOP_K):
        e_out = jnp.where(lane4 == k, idxs[k].astype(I32), e_out)
        g_out = jnp.where(lane4 == k, gates[k], g_out)
        r_out = jnp.where(lane4 == k, ranks[k].astype(I32), r_out)
    e_ref[...] = e_out
    gate_ref[...] = g_out
    rank_ref[...] = r_out
    cnt_ref[...] = run_ref[...].astype(I32)


def _route(x1, g_ffn, w_router_pad, b_router_pad):
    t = x1.shape[0]
    tm = ROUTE_TM
    const = lambda i: (0, 0)
    return pl.pallas_call(
        _route_body,
        grid=(t // tm,),
        in_specs=[
            pl.BlockSpec((tm, D_MODEL), lambda i: (i, 0)),
            pl.BlockSpec((1, D_MODEL), const),
            pl.BlockSpec((D_MODEL, LANES), const),
            pl.BlockSpec((1, LANES), const),
        ],
        out_specs=[
            pl.BlockSpec((tm, D_MODEL // 2), lambda i: (i, 0)),
            pl.BlockSpec((tm, TOP_K), lambda i: (i, 0)),
            pl.BlockSpec((tm, TOP_K), lambda i: (i, 0)),
            pl.BlockSpec((tm, TOP_K), lambda i: (i, 0)),
            pl.BlockSpec((1, LANES), const),
        ],
        out_shape=[
            jax.ShapeDtypeStruct((t, D_MODEL // 2), U32),
            jax.ShapeDtypeStruct((t, TOP_K), I32),
            jax.ShapeDtypeStruct((t, TOP_K), F32),
            jax.ShapeDtypeStruct((t, TOP_K), I32),
            jax.ShapeDtypeStruct((1, LANES), I32),
        ],
        scratch_shapes=[pltpu.VMEM((1, LANES), F32)],
        compiler_params=_cparams(("arbitrary",)),
        name="route",
    )(x1, g_ffn, w_router_pad, b_router_pad)


def _scatter_body(seg_ref, e_ref, rank_ref, hp_ref, init_ref, xs_ref, sem):
    del init_ref
    n = e_ref.shape[0]

    def row_copy(a, pos):
        t = lax.shift_right_logical(a, 2)
        return pltpu.make_async_copy(hp_ref.at[pl.ds(t, 1), :], xs_ref.at[pl.ds(pos, 1), :], sem)

    def issue(a, c):
        pos = seg_ref[e_ref[a]] + rank_ref[a]
        row_copy(a, pos).start()
        return c

    lax.fori_loop(0, n, issue, 0)

    def drain(a, c):
        row_copy(a, 0).wait()
        return c

    lax.fori_loop(0, n, drain, 0)


def _scatter(seg_start, e_flat, rank_flat, hpack, n_slots):
    t = hpack.shape[0]
    tm = SCATTER_TM
    init = jnp.zeros((n_slots, D_MODEL // 2), U32)
    grid_spec = pltpu.PrefetchScalarGridSpec(
        num_scalar_prefetch=1,
        grid=(t // tm,),
        in_specs=[
            pl.BlockSpec((tm * TOP_K,), lambda i, seg: (i,), memory_space=pltpu.SMEM),
            pl.BlockSpec((tm * TOP_K,), lambda i, seg: (i,), memory_space=pltpu.SMEM),
            pl.BlockSpec((tm, D_MODEL // 2), lambda i, seg: (i, 0)),
            pl.BlockSpec(memory_space=pl.ANY),
        ],
        out_specs=pl.BlockSpec(memory_space=pl.ANY),
        scratch_shapes=[pltpu.SemaphoreType.DMA(())],
    )
    return pl.pallas_call(
        _scatter_body,
        grid_spec=grid_spec,
        out_shape=jax.ShapeDtypeStruct((n_slots, D_MODEL // 2), U32),
        input_output_aliases={4: 0},
        compiler_params=_cparams(("arbitrary",)),
        name="scatter_rows",
    )(seg_start, e_flat, rank_flat, hpack, init)


def _experts_body(sbe_ref, sbs_ref, sbn_ref, used_ref, xs_ref, wg_ref, wu_ref, bg_ref, bu_ref, wd_ref, bd_ref,
                  ys_ref, xbuf_ref, acc_ref, zero_ref, in_sem, out_sem):
    del sbe_ref
    sb = pl.program_id(0)
    j = pl.program_id(1)
    nj = pl.num_programs(1)
    nsub = sbn_ref[sb]
    start = sbs_ref[sb]
    bm = EXPERT_BM

    def in_copy(r):
        src = pl.multiple_of(start + r * bm, bm)
        dst = pl.multiple_of(r * bm, bm)
        return pltpu.make_async_copy(xs_ref.at[pl.ds(src, bm), :], xbuf_ref.at[pl.ds(dst, bm), :], in_sem)

    def out_copy(r):
        src = pl.multiple_of(r * bm, bm)
        dst = pl.multiple_of(start + r * bm, bm)
        return pltpu.make_async_copy(acc_ref.at[pl.ds(src, bm), :], ys_ref.at[pl.ds(dst, bm), :], out_sem)

    def zero_copy(blk):
        dst = pl.multiple_of(blk * bm, bm)
        return pltpu.make_async_copy(zero_ref, ys_ref.at[pl.ds(dst, bm), :], out_sem)

    def for_each(lo, hi, fn):
        def step(r, c):
            fn(r)
            return c
        lax.fori_loop(lo, hi, step, 0)

    @pl.when(j == 0)
    def _():
        for_each(0, nsub, lambda r: in_copy(r).start())
        for_each(0, nsub, lambda r: in_copy(r).wait())

    @pl.when(nsub > 0)
    def _():
        wg = wg_ref[...].astype(BF16)
        wu = wu_ref[...].astype(BF16)
        wd = wd_ref[...].astype(BF16)

        def sub(r):
            r0 = pl.multiple_of(r * bm, bm)
            packed = xbuf_ref[pl.ds(r0, bm), :]
            lo = lax.bitcast_convert_type(lax.shift_left(packed, jnp.uint32(16)), F32)
            hi = lax.bitcast_convert_type(packed & jnp.uint32(0xFFFF0000), F32)
            x = jnp.concatenate([lo, hi], axis=1).astype(BF16)
            gate = jnp.dot(x, wg, preferred_element_type=F32) + bg_ref[...]
            up = jnp.dot(x, wu, preferred_element_type=F32) + bu_ref[...]
            gate = jnp.minimum(gate, SWIGLU_LIMIT)
            up = jnp.clip(up, -SWIGLU_LIMIT, SWIGLU_LIMIT)
            glu = gate * _sigmoid(SWIGLU_ALPHA * gate)
            act = ((up + 1.0) * glu).astype(BF16)
            part = jnp.dot(act, wd, preferred_element_type=F32)

            @pl.when(j == 0)
            def _():
                acc_ref[pl.ds(r0, bm), :] = part + bd_ref[...]

            @pl.when(j > 0)
            def _():
                acc_ref[pl.ds(r0, bm), :] = acc_ref[pl.ds(r0, bm), :] + part

        for_each(0, nsub, sub)

    @pl.when(j == nj - 1)
    def _():
        for_each(0, nsub, lambda r: out_copy(r).start())
        for_each(0, nsub, lambda r: out_copy(r).wait())

    @pl.when((sb == pl.num_programs(0) - 1) & (j == nj - 1))
    def _():
        zero_ref[...] = jnp.zeros_like(zero_ref)
        n_blocks = ys_ref.shape[0] // bm
        for_each(used_ref[0], n_blocks, lambda blk: zero_copy(blk).start())
        for_each(used_ref[0], n_blocks, lambda blk: zero_copy(blk).wait())


def _experts(sb_expert, sb_start, sb_nsub, used_blocks, xsorted, w_gu, b_gu, w_down, b_down):
    n_sb = sb_expert.shape[0]
    n_slots = xsorted.shape[0]
    tf = EXPERT_TF
    nj = D_FF // tf

    def spec(shape, fn):
        def index_map(sb, j, sbe, sbs, sbn, used):
            jj = jnp.where(sbn[sb] > 0, j, nj - 1)
            return fn(sbe[sb], jj)
        return pl.BlockSpec(shape, index_map)

    grid_spec = pltpu.PrefetchScalarGridSpec(
        num_scalar_prefetch=4,
        grid=(n_sb, nj),
        in_specs=[
            pl.BlockSpec(memory_space=pl.ANY),
            spec((None, D_MODEL, tf), lambda e, jj: (e, 0, jj)),
            spec((None, D_MODEL, tf), lambda e, jj: (e, 0, nj + jj)),
            spec((None, 1, tf), lambda e, jj: (e, 0, jj)),
            spec((None, 1, tf), lambda e, jj: (e, 0, nj + jj)),
            spec((None, tf, D_MODEL), lambda e, jj: (e, jj, 0)),
            spec((None, 1, D_MODEL), lambda e, jj: (e, 0, 0)),
        ],
        out_specs=pl.BlockSpec(memory_space=pl.ANY),
        scratch_shapes=[
            pltpu.VMEM((EXPERT_R, D_MODEL // 2), U32),
            pltpu.VMEM((EXPERT_R, D_MODEL), F32),
            pltpu.VMEM((EXPERT_BM, D_MODEL), F32),
            pltpu.SemaphoreType.DMA(()),
            pltpu.SemaphoreType.DMA(()),
        ],
    )
    return pl.pallas_call(
        _experts_body,
        grid_spec=grid_spec,
        out_shape=jax.ShapeDtypeStruct((n_slots, D_MODEL), F32),
        compiler_params=_cparams(("arbitrary", "arbitrary")),
        name="experts",
    )(sb_expert, sb_start, sb_nsub, used_blocks, xsorted, w_gu, w_gu, b_gu, b_gu, w_down, b_down)


def _combine_body(seg_ref, e_ref, rank_ref, x1_ref, gate_ref, gf_ref, ys_ref, yp_ref, ysm_ref, buf_ref, sem,
                  *, n_prompt_tiles):
    i = pl.program_id(0)
    tm = x1_ref.shape[0]
    n = e_ref.shape[0]

    def row_copy(a, pos):
        t = lax.shift_right_logical(a, 2)
        k = a & (TOP_K - 1)
        return pltpu.make_async_copy(ys_ref.at[pl.ds(pos, 1), :], buf_ref.at[k, pl.ds(t, 1), :], sem)

    def issue(a, c):
        pos = seg_ref[e_ref[a]] + rank_ref[a]
        row_copy(a, pos).start()
        return c

    lax.fori_loop(0, n, issue, 0)

    def drain(a, c):
        row_copy(a, 0).wait()
        return c

    lax.fori_loop(0, n, drain, 0)

    g = gate_ref[...]
    acc = buf_ref[0] * g[:, 0:1]
    for k in range(1, TOP_K):
        acc = acc + buf_ref[k] * g[:, k:k + 1]
    x = x1_ref[...] + acc
    y = x * lax.rsqrt(jnp.mean(x * x, axis=-1, keepdims=True) + EPS) * gf_ref[...]

    @pl.when(i < n_prompt_tiles)
    def _():
        yp_ref[...] = y

    @pl.when(i >= n_prompt_tiles)
    def _():
        ysm_ref[...] = y


def _combine(seg_start, e_flat, rank_flat, x1, gates, g_final, ysorted, t_prompt):
    t = x1.shape[0]
    tm = COMBINE_TM
    npt = t_prompt // tm
    grid_spec = pltpu.PrefetchScalarGridSpec(
        num_scalar_prefetch=1,
        grid=(t // tm,),
        in_specs=[
            pl.BlockSpec((tm * TOP_K,), lambda i, seg: (i,), memory_space=pltpu.SMEM),
            pl.BlockSpec((tm * TOP_K,), lambda i, seg: (i,), memory_space=pltpu.SMEM),
            pl.BlockSpec((tm, D_MODEL), lambda i, seg: (i, 0)),
            pl.BlockSpec((tm, TOP_K), lambda i, seg: (i, 0)),
            pl.BlockSpec((1, D_MODEL), lambda i, seg: (0, 0)),
            pl.BlockSpec(memory_space=pl.ANY),
        ],
        out_specs=[
            pl.BlockSpec((tm, D_MODEL), lambda i, seg: (jnp.minimum(i, npt - 1), 0)),
            pl.BlockSpec((tm, D_MODEL), lambda i, seg: (jnp.maximum(i - npt, 0), 0)),
        ],
        scratch_shapes=[pltpu.VMEM((TOP_K, tm, D_MODEL), F32), pltpu.SemaphoreType.DMA(())],
    )
    return pl.pallas_call(
        functools.partial(_combine_body, n_prompt_tiles=npt),
        grid_spec=grid_spec,
        out_shape=[
            jax.ShapeDtypeStruct((t_prompt, D_MODEL), F32),
            jax.ShapeDtypeStruct((t - t_prompt, D_MODEL), F32),
        ],
        compiler_params=_cparams(("arbitrary",)),
        name="combine",
    )(seg_start, e_flat, rank_flat, x1, gates, g_final, ysorted)


def _expert_tables(counts, n_assign):
    pad = EXPERT_PAD
    padded = ((counts + pad - 1) // pad) * pad
    seg_start = jnp.cumsum(padded) - padded
    n_sb_e = (counts + EXPERT_R - 1) // EXPERT_R
    sb_cum = jnp.cumsum(n_sb_e)
    n_sb = n_assign // EXPERT_R + N_EXPERTS
    s = jnp.arange(n_sb, dtype=I32)
    total = sb_cum[-1]
    s_eff = jnp.minimum(s, total - 1)
    e = jnp.searchsorted(sb_cum, s_eff, side="right").astype(I32)
    k = s_eff - (sb_cum[e] - n_sb_e[e])
    rows = jnp.clip(counts[e] - k * EXPERT_R, 0, EXPERT_R)
    valid = s < total
    nsub = jnp.where(valid, (rows + EXPERT_BM - 1) // EXPERT_BM, 0)
    start = seg_start[e] + k * EXPERT_R
    used_blocks = (jnp.sum(padded) // EXPERT_BM).reshape(1)
    return seg_start.astype(I32), e, start.astype(I32), nsub.astype(I32), used_blocks.astype(I32)


def kernel(x_prompt, x_sample, state_rglru_h, state_rglru_conv, state_ssd, state_ssd_conv, g_mix, w_in, conv_rg_w, conv_rg_b, rg_wa, rg_ba, rg_wi, rg_bi, rg_lambda, conv_ssd_w, conv_ssd_b, ssd_dt_bias, ssd_a_log, ssd_d, ssd_norm_g, w_out, g_ffn, w_router, b_router, w_gu, b_gu, w_down, b_down, g_final):
    depth = g_mix.shape[0]
    assert depth == 1
    bp, lp, _ = x_prompt.shape
    bs, ls, _ = x_sample.shape
    tp, ts = bp * lp, bs * ls
    t = tp + ts
    hp = SSD_HEADS * SSD_HEAD_DIM
    l = 0

    xp = x_prompt.reshape(tp, D_MODEL)
    xs = x_sample.reshape(ts, D_MODEL)
    w_in_l = w_in[l]
    wdt = jnp.pad(w_in_l[:, D_MAIN:], ((0, 0), (0, LANES - SSD_HEADS)))
    row = lambda v: v.reshape(1, -1)
    pad_heads = lambda v: jnp.pad(v.reshape(1, -1), ((0, 0), (0, LANES - SSD_HEADS)))

    hb, dt_raw = _norm_dt(xp, xs, row(g_mix[l]), wdt)
    proj = _in_proj(hb, w_in_l)

    rg_params = (conv_rg_w[l], row(conv_rg_b[l]), rg_wa[l], row(rg_ba[l]), rg_wi[l], row(rg_bi[l]),
                 row(rg_lambda[l]))
    d_exp = row(jnp.repeat(ssd_d[l], SSD_HEAD_DIM))
    ssd_params = (conv_ssd_w[l], row(conv_ssd_b[l]), pad_heads(ssd_dt_bias[l]), pad_heads(ssd_a_log[l]),
                  d_exp, row(ssd_norm_g[l]))

    y_rg, rgh_p, rgc_p = _rglru_prompt(proj, bp, lp, *rg_params)
    y_ssd, ssd_p, ssdc_p = _ssd_prompt(proj, dt_raw, bp, lp, *ssd_params)

    proj4 = proj.reshape(t // ls, ls * D_MAIN)
    dt4 = dt_raw.reshape(t // ls, ls * LANES)
    mix_s4, rgh_s, rgc_s, ssd_s, ssdc_s = _mix_sample(
        proj4, dt4, tp // ls, bs,
        state_rglru_h[l], state_rglru_conv[l].reshape(bs, -1),
        state_ssd[l].reshape(bs, hp, SSD_STATE), state_ssd_conv[l].reshape(bs, -1),
        rg_params, ssd_params)
    mix_s = mix_s4.reshape(ts, 2 * D_MODEL)

    x1 = _out_proj(y_rg, y_ssd, mix_s, w_out[l], xp, xs)

    wr = jnp.pad(w_router[l], ((0, 0), (0, LANES - N_EXPERTS)))
    br = jnp.pad(b_router[l].reshape(1, -1), ((0, 0), (0, LANES - N_EXPERTS)))
    hpack, e_idx, gates, rank, counts = _route(x1, row(g_ffn[l]), wr, br)

    n_assign = t * TOP_K
    n_slots = n_assign + N_EXPERTS * EXPERT_PAD
    seg_start, sb_expert, sb_start, sb_nsub, used_blocks = _expert_tables(counts[0, :N_EXPERTS], n_assign)
    e_flat = e_idx.reshape(-1)
    rank_flat = rank.reshape(-1)
    xsorted = _scatter(seg_start, e_flat, rank_flat, hpack, n_slots)
    ysorted = _experts(sb_expert, sb_start, sb_nsub, used_blocks, xsorted, w_gu[l], b_gu[l].reshape(N_EXPERTS, 1, -1),
                       w_down[l], b_down[l].reshape(N_EXPERTS, 1, -1))
    y_p, y_s = _combine(seg_start, e_flat, rank_flat, x1, gates, row(g_final), ysorted, tp)

    return (y_p.reshape(x_prompt.shape).astype(x_prompt.dtype),
            y_s.reshape(x_sample.shape).astype(x_sample.dtype),
            rgh_p.reshape(depth, bp, D_RG),
            rgc_p.reshape(depth, bp, CONV_W - 1, D_RG),
            ssd_p.reshape(depth, bp, SSD_HEADS, SSD_HEAD_DIM, SSD_STATE),
            ssdc_p.reshape(depth, bp, CONV_W - 1, D_XBC),
            rgh_s.reshape(depth, bs, D_RG),
            rgc_s.reshape(depth, bs, CONV_W - 1, D_RG),
            ssd_s.reshape(depth, bs, SSD_HEADS, SSD_HEAD_DIM, SSD_STATE),
            ssdc_s.reshape(depth, bs, CONV_W - 1, D_XBC))
```

```python
import functools

import jax
import jax.numpy as jnp
from jax import lax
from jax.experimental import pallas as pl
from jax.experimental.pallas import tpu as pltpu

F32 = jnp.float32
BF16 = jnp.bfloat16
I32 = jnp.int32
HIGHEST = lax.Precision.HIGHEST

EPS = 1e-6
D_MODEL = 2048
D_RG = 2048
RG_BLOCKS = 16
RG_BLOCK_W = 128
RG_C = 8.0
D_SSD = 2048
SSD_HEAD_DIM = 64
SSD_HEADS = 32
SSD_GROUPS = 4
SSD_STATE = 128
SSD_CHUNK = 128
CONV_W = 4
D_BC = SSD_GROUPS * SSD_STATE
D_XBC = D_SSD + 2 * D_BC
D_MAIN = 2 * D_RG + D_SSD + D_XBC
N_EXPERTS = 32
TOP_K = 4
D_FF = 2048
SWIGLU_ALPHA = 1.702
SWIGLU_LIMIT = 7.0

LANES = 128
VMEM_LIMIT = 56 * 1024 * 1024

ROW_TILE = 512
PROJ_TM = 1088
PROJ_TN = 1024
OUT_TN = 512
RG_TL = 256
SAMPLE_BB = 8
ROUTE_TM = 512
SCATTER_TM = 256
COMBINE_TM = 256
EXPERT_PAD = 256
EXPERT_BM = 512
EXPERT_R = 1536
EXPERT_TF = 256
DMA_UNROLL = 8


def _cparams(sem, vmem=VMEM_LIMIT):
    return pltpu.CompilerParams(dimension_semantics=sem, vmem_limit_bytes=vmem)


def _softplus(x):
    return jnp.maximum(x, 0.0) + jnp.log1p(jnp.exp(-jnp.abs(x)))


_sigmoid = jax.nn.sigmoid
_silu = jax.nn.silu


def _gelu_tanh(x):
    return jax.nn.gelu(x, approximate=True)


def _head_expand_matrix(dtype):
    r = lax.broadcasted_iota(I32, (LANES, D_SSD), 0)
    c = lax.broadcasted_iota(I32, (LANES, D_SSD), 1)
    return jnp.where(lax.shift_right_logical(c, 6) == r, 1.0, 0.0).astype(dtype)


def _expand_heads(v, e_f32):
    return jnp.dot(v, e_f32, precision=HIGHEST, preferred_element_type=F32)


def _norm_dt_body(xp_ref, xs_ref, g_ref, wdt_ref, hb_ref, dt_ref, *, n_prompt_tiles):
    i = pl.program_id(0)

    def run(x_ref):
        x = x_ref[...]
        h = x * lax.rsqrt(jnp.mean(x * x, axis=-1, keepdims=True) + EPS) * g_ref[...]
        hb = h.astype(BF16)
        hb_ref[...] = hb
        dt_ref[...] = jnp.dot(hb, wdt_ref[...].astype(BF16), preferred_element_type=F32)

    @pl.when(i < n_prompt_tiles)
    def _():
        run(xp_ref)

    @pl.when(i >= n_prompt_tiles)
    def _():
        run(xs_ref)


def _norm_dt(xp, xs, g, wdt):
    tp, ts = xp.shape[0], xs.shape[0]
    npt, nst = tp // ROW_TILE, ts // ROW_TILE
    t = tp + ts
    return pl.pallas_call(
        functools.partial(_norm_dt_body, n_prompt_tiles=npt),
        grid=(npt + nst,),
        in_specs=[
            pl.BlockSpec((ROW_TILE, D_MODEL), lambda i: (jnp.minimum(i, npt - 1), 0)),
            pl.BlockSpec((ROW_TILE, D_MODEL), lambda i: (jnp.maximum(i - npt, 0), 0)),
            pl.BlockSpec((1, D_MODEL), lambda i: (0, 0)),
            pl.BlockSpec((D_MODEL, LANES), lambda i: (0, 0)),
        ],
        out_specs=[
            pl.BlockSpec((ROW_TILE, D_MODEL), lambda i: (i, 0)),
            pl.BlockSpec((ROW_TILE, LANES), lambda i: (i, 0)),
        ],
        out_shape=[jax.ShapeDtypeStruct((t, D_MODEL), BF16), jax.ShapeDtypeStruct((t, LANES), F32)],
        compiler_params=_cparams(("parallel",)),
        name="norm_dt",
    )(xp, xs, g, wdt)


def _cast_weight_tile(w_ref, wb_ref, rows_per_step=256):
    k = w_ref.shape[0]

    def step(s, c):
        r = pl.multiple_of(s * rows_per_step, rows_per_step)
        wb_ref[pl.ds(r, rows_per_step), :] = w_ref[pl.ds(r, rows_per_step), :].astype(BF16)
        return c

    lax.fori_loop(0, k // rows_per_step, step, 0)


def _in_proj_body(x_ref, w_ref, o_ref, wb_ref):
    @pl.when(pl.program_id(1) == 0)
    def _():
        _cast_weight_tile(w_ref, wb_ref)

    o_ref[...] = jnp.dot(x_ref[...], wb_ref[...], preferred_element_type=F32)


def _in_proj(hb, w_in):
    t = hb.shape[0]
    return pl.pallas_call(
        _in_proj_body,
        grid=(D_MAIN // PROJ_TN, t // PROJ_TM),
        in_specs=[
            pl.BlockSpec((PROJ_TM, D_MODEL), lambda j, i: (i, 0)),
            pl.BlockSpec((D_MODEL, PROJ_TN), lambda j, i: (0, j)),
        ],
        out_specs=pl.BlockSpec((PROJ_TM, PROJ_TN), lambda j, i: (i, j)),
        out_shape=jax.ShapeDtypeStruct((t, D_MAIN), F32),
        scratch_shapes=[pltpu.VMEM((D_MODEL, PROJ_TN), BF16)],
        compiler_params=_cparams(("arbitrary", "arbitrary")),
        name="in_proj",
    )(hb, w_in)


def _out_proj_body(rg_ref, ssd_ref, ms_ref, w_ref, xp_ref, xs_ref, o_ref, wb_ref, *, n_prompt_tiles):
    i = pl.program_id(1)

    @pl.when(i == 0)
    def _():
        _cast_weight_tile(w_ref, wb_ref)

    @pl.when(i < n_prompt_tiles)
    def _():
        m = jnp.concatenate([rg_ref[...], ssd_ref[...]], axis=1)
        o_ref[...] = xp_ref[...] + jnp.dot(m, wb_ref[...], preferred_element_type=F32)

    @pl.when(i >= n_prompt_tiles)
    def _():
        o_ref[...] = xs_ref[...] + jnp.dot(ms_ref[...].astype(BF16), wb_ref[...], preferred_element_type=F32)


def _out_proj(y_rg, y_ssd, mix_s, w_out, xp, xs):
    tp, ts = y_rg.shape[0], mix_s.shape[0]
    npt, nst = tp // ROW_TILE, ts // ROW_TILE
    prompt_rows = lambda j, i: (jnp.minimum(i, npt - 1), 0)
    return pl.pallas_call(
        functools.partial(_out_proj_body, n_prompt_tiles=npt),
        grid=(D_MODEL // OUT_TN, npt + nst),
        in_specs=[
            pl.BlockSpec((ROW_TILE, D_MODEL), prompt_rows),
            pl.BlockSpec((ROW_TILE, D_MODEL), prompt_rows),
            pl.BlockSpec((ROW_TILE, 2 * D_MODEL), lambda j, i: (jnp.maximum(i - npt, 0), 0)),
            pl.BlockSpec((2 * D_MODEL, OUT_TN), lambda j, i: (0, j)),
            pl.BlockSpec((ROW_TILE, OUT_TN), lambda j, i: (jnp.minimum(i, npt - 1), j)),
            pl.BlockSpec((ROW_TILE, OUT_TN), lambda j, i: (jnp.maximum(i - npt, 0), j)),
        ],
        out_specs=pl.BlockSpec((ROW_TILE, OUT_TN), lambda j, i: (i, j)),
        out_shape=jax.ShapeDtypeStruct((tp + ts, D_MODEL), F32),
        scratch_shapes=[pltpu.VMEM((2 * D_MODEL, OUT_TN), BF16)],
        compiler_params=_cparams(("arbitrary", "arbitrary")),
        name="out_proj",
    )(y_rg, y_ssd, mix_s, w_out, xp, xs)


def _rg_gates(xc, wa_ref, ba_ref, wi_ref, bi_ref, lam_ref):
    xcb = xc.astype(BF16)
    rs, is_ = [], []
    for h in range(RG_BLOCKS):
        xh = xcb[:, h * RG_BLOCK_W:(h + 1) * RG_BLOCK_W]
        rs.append(jnp.dot(xh, wa_ref[h].astype(BF16), preferred_element_type=F32))
        is_.append(jnp.dot(xh, wi_ref[h].astype(BF16), preferred_element_type=F32))
    r = _sigmoid(jnp.concatenate(rs, axis=1) + ba_ref[...])
    ig = _sigmoid(jnp.concatenate(is_, axis=1) + bi_ref[...])
    log_a = (-RG_C * r) * _softplus(-lam_ref[...])
    a = jnp.exp(log_a)
    u = jnp.sqrt(-jnp.tanh(log_a) * (a * a + 1.0)) * (ig * xc)
    return a, u


def _rglru_prompt_body(x_ref, gate_ref, cw_ref, cb_ref, wa_ref, ba_ref, wi_ref, bi_ref, lam_ref,
                       y_ref, h_ref, cs_ref, ext_ref, a_ref, u_ref, car_ref):
    c = pl.program_id(1)
    tl = x_ref.shape[0]

    @pl.when(c == 0)
    def _():
        ext_ref[0:8, :] = jnp.zeros((8, D_RG), F32)
        car_ref[...] = jnp.zeros((1, D_RG), F32)

    ext_ref[8:8 + tl, :] = x_ref[...]
    xc = cb_ref[...] + ext_ref[5:5 + tl, :] * cw_ref[0:1, :]
    xc = xc + ext_ref[6:6 + tl, :] * cw_ref[1:2, :]
    xc = xc + ext_ref[7:7 + tl, :] * cw_ref[2:3, :]
    xc = xc + ext_ref[8:8 + tl, :] * cw_ref[3:4, :]
    tail = ext_ref[tl:tl + 8, :]
    ext_ref[0:8, :] = tail

    a, u = _rg_gates(xc, wa_ref, ba_ref, wi_ref, bi_ref, lam_ref)
    a_ref[...] = a
    u_ref[...] = u

    row = lax.broadcasted_iota(I32, (8, D_RG), 0)

    def group(g, carry):
        r0 = pl.multiple_of(g * 8, 8)
        a8 = a_ref[pl.ds(r0, 8), :]
        u8 = u_ref[pl.ds(r0, 8), :]
        for s in (1, 2, 4):
            keep = row >= s
            a_sh = jnp.where(keep, pltpu.roll(a8, s, 0), 1.0)
            u_sh = jnp.where(keep, pltpu.roll(u8, s, 0), 0.0)
            u8 = a8 * u_sh + u8
            a8 = a8 * a_sh
        h8 = a8 * carry + u8
        u_ref[pl.ds(r0, 8), :] = h8
        return h8[7:8, :]

    carry = lax.fori_loop(0, tl // 8, group, car_ref[...])
    car_ref[...] = carry
    y_ref[...] = (u_ref[...] * _gelu_tanh(gate_ref[...])).astype(BF16)

    @pl.when(c == pl.num_programs(1) - 1)
    def _():
        h_ref[...] = carry
        cs_ref[...] = tail[5:8, :]


def _rglru_prompt(proj, batch, seq, conv_w, conv_b, wa, ba, wi, bi, lam):
    nc = seq // RG_TL
    t = batch * seq
    vec =pl.BlockSpec((1, D_RG), lambda b, c: (0, 0))
    blk = pl.BlockSpec((RG_BLOCKS, RG_BLOCK_W, RG_BLOCK_W), lambda b, c: (0, 0, 0))
    return pl.pallas_call(
        _rglru_prompt_body,
        grid=(batch, nc),
        in_specs=[
            pl.BlockSpec((RG_TL, D_RG), lambda b, c: (b * nc + c, 0)),
            pl.BlockSpec((RG_TL, D_RG), lambda b, c: (b * nc + c, 1)),
            pl.BlockSpec((CONV_W, D_RG), lambda b, c: (0, 0)),
            vec, blk, vec, blk, vec, vec,
        ],
        out_specs=[
            pl.BlockSpec((RG_TL, D_RG), lambda b, c: (b * nc + c, 0)),
            pl.BlockSpec((None, 1, D_RG), lambda b, c: (b, 0, 0)),
            pl.BlockSpec((None, CONV_W - 1, D_RG), lambda b, c: (b, 0, 0)),
        ],
        out_shape=[
            jax.ShapeDtypeStruct((t, D_RG), BF16),
            jax.ShapeDtypeStruct((batch, 1, D_RG), F32),
            jax.ShapeDtypeStruct((batch, CONV_W - 1, D_RG), F32),
        ],
        scratch_shapes=[
            pltpu.VMEM((RG_TL + 8, D_RG), F32),
            pltpu.VMEM((RG_TL, D_RG), F32),
            pltpu.VMEM((RG_TL, D_RG), F32),
            pltpu.VMEM((1, D_RG), F32),
        ],
        compiler_params=_cparams(("arbitrary", "arbitrary")),
        name="rglru_prompt",
    )(proj, proj, conv_w, conv_b, wa, ba, wi, bi, lam)


def _group_rmsnorm(y, g_row):
    outs = []
    for g in range(SSD_GROUPS):
        yg = y[:, g * D_BC:(g + 1) * D_BC]
        outs.append(yg * lax.rsqrt(jnp.mean(yg * yg, axis=-1, keepdims=True) + EPS))
    return jnp.concatenate(outs, axis=1) * g_row


def _ssd_prompt_body(xbc_ref, z_ref, dt_ref, cw_ref, cb_ref, dtb_ref, alog_ref, dexp_ref, ng_ref,
                     y_ref, s_out_ref, cs_ref, ext_ref, s_ref):
    c = pl.program_id(1)
    L = SSD_CHUNK

    @pl.when(c == 0)
    def _():
        ext_ref[0:8, :] = jnp.zeros((8, D_XBC), F32)
        s_ref[...] = jnp.zeros_like(s_ref)

    ext_ref[8:8 + L, :] = xbc_ref[...]
    xc = cb_ref[...] + ext_ref[5:5 + L, :] * cw_ref[0:1, :]
    xc = xc + ext_ref[6:6 + L, :] * cw_ref[1:2, :]
    xc = xc + ext_ref[7:7 + L, :] * cw_ref[2:3, :]
    xc = xc + ext_ref[8:8 + L, :] * cw_ref[3:4, :]
    tail = ext_ref[L:L + 8, :]
    ext_ref[0:8, :] = tail
    xc = _silu(xc)
    xs = xc[:, :D_SSD]
    bm = xc[:, D_SSD:D_SSD + D_BC].astype(BF16)
    cm = xc[:, D_SSD + D_BC:].astype(BF16)

    lane = lax.broadcasted_iota(I32, (1, LANES), 1)
    a_neg = jnp.where(lane < SSD_HEADS, -jnp.exp(alog_ref[...]), 0.0)
    dt = _softplus(dt_ref[...] + dtb_ref[...])
    da = dt * a_neg
    ri = lax.broadcasted_iota(I32, (L, L), 0)
    ci = lax.broadcasted_iota(I32, (L, L), 1)
    causal = ci <= ri
    tri = jnp.where(causal, 1.0, 0.0).astype(F32)
    acum = jnp.dot(tri, da, precision=HIGHEST, preferred_element_type=F32)
    acum_t = acum.T
    a_last = acum[L - 1:L, :]

    e_f32 = _head_expand_matrix(F32)
    stacked = jnp.concatenate([dt, jnp.exp(a_last - acum), jnp.exp(acum)], axis=0)
    expd = _expand_heads(stacked, e_f32)
    dt_e, dend_e, ea_e = expd[0:L], expd[L:2 * L], expd[2 * L:3 * L]
    xdt = xs * dt_e
    xdt_b = xdt.astype(BF16)
    w_b = (xdt * dend_e)

    lane_l = lax.broadcasted_iota(I32, (L, LANES), 1)
    lo = lane_l < SSD_HEAD_DIM
    y_parts = []
    for g in range(SSD_GROUPS):
        cg = cm[:, g * SSD_STATE:(g + 1) * SSD_STATE]
        bg = bm[:, g * SSD_STATE:(g + 1) * SSD_STATE]
        cb = lax.dot_general(cg, bg, (((1,), (1,)), ((), ())), preferred_element_type=F32)
        hpg = SSD_HEADS // SSD_GROUPS
        for jp in range(hpg // 2):
            ms = []
            for h in (g * hpg + 2 * jp, g * hpg + 2 * jp + 1):
                seg = acum[:, h:h + 1] - acum_t[h:h + 1, :]
                decay = jnp.exp(jnp.where(causal, seg, -jnp.inf))
                ms.append((cb * decay).astype(BF16))
            col = (g * hpg + 2 * jp) * SSD_HEAD_DIM
            xp = xdt_b[:, col:col + LANES]
            zero = jnp.zeros_like(xp)
            rhs = jnp.concatenate([jnp.where(lo, xp, zero), jnp.where(lo, zero, xp)], axis=0)
            y_parts.append(jnp.dot(jnp.concatenate(ms, axis=1), rhs, preferred_element_type=F32))
    y_diag = jnp.concatenate(y_parts, axis=1)

    y_off_parts = []
    for g in range(SSD_GROUPS):
        cg = cm[:, g * SSD_STATE:(g + 1) * SSD_STATE]
        sg = s_ref[g * D_BC:(g + 1) * D_BC, :].astype(BF16)
        y_off_parts.append(lax.dot_general(cg, sg, (((1,), (1,)), ((), ())), preferred_element_type=F32))
    y_off = jnp.concatenate(y_off_parts, axis=1) * ea_e

    cd_col = jnp.exp(acum_t[:, L - 1:L])
    for g in range(SSD_GROUPS):
        bg = bm[:, g * SSD_STATE:(g + 1) * SSD_STATE]
        wg_t = w_b[:, g * D_BC:(g + 1) * D_BC].T.astype(BF16)
        upd = jnp.dot(wg_t, bg, preferred_element_type=F32)
        hpg = SSD_HEADS // SSD_GROUPS
        for e in range(hpg):
            h = g * hpg + e
            r0 = h * SSD_HEAD_DIM
            cd = jnp.broadcast_to(cd_col[h:h + 1, :], (SSD_HEAD_DIM, SSD_STATE))
            s_ref[r0:r0 + SSD_HEAD_DIM, :] = (cd * s_ref[r0:r0 + SSD_HEAD_DIM, :]
                                              + upd[e * SSD_HEAD_DIM:(e + 1) * SSD_HEAD_DIM, :])

    y = (y_diag + y_off + dexp_ref[...] * xs) * _silu(z_ref[...])
    y_ref[...] = _group_rmsnorm(y, ng_ref[...]).astype(BF16)

    @pl.when(c == pl.num_programs(1) - 1)
    def _():
        s_out_ref[...] = s_ref[...]
        cs_ref[...] = tail[5:8, :]


def _ssd_prompt(proj, dt_raw, batch, seq, conv_w, conv_b, dt_bias, a_log, d_exp, norm_g):
    nc = seq // SSD_CHUNK
    L = SSD_CHUNK
    hp = SSD_HEADS * SSD_HEAD_DIM
    const = lambda b, c: (0, 0)
    return pl.pallas_call(
        _ssd_prompt_body,
        grid=(batch, nc),
        in_specs=[
            pl.BlockSpec((L, D_XBC), lambda b, c: (b * nc + c, 2)),
            pl.BlockSpec((L, D_SSD), lambda b, c: (b * nc + c, 2)),
            pl.BlockSpec((L, LANES), lambda b, c: (b * nc + c, 0)),
            pl.BlockSpec((CONV_W, D_XBC), const),
            pl.BlockSpec((1, D_XBC), const),
            pl.BlockSpec((1, LANES), const),
            pl.BlockSpec((1, LANES), const),
            pl.BlockSpec((1, D_SSD), const),
            pl.BlockSpec((1, D_SSD), const),
        ],
        out_specs=[
            pl.BlockSpec((L, D_SSD), lambda b, c: (b * nc + c, 0)),
            pl.BlockSpec((None, hp, SSD_STATE), lambda b, c: (b, 0, 0)),
            pl.BlockSpec((None, CONV_W - 1, D_XBC), lambda b, c: (b, 0, 0)),
        ],
        out_shape=[
            jax.ShapeDtypeStruct((batch * seq, D_SSD), BF16),
            jax.ShapeDtypeStruct((batch, hp, SSD_STATE), F32),
            jax.ShapeDtypeStruct((batch, CONV_W - 1, D_XBC), F32),
        ],
        scratch_shapes=[
            pltpu.VMEM((L + 8, D_XBC), F32),
            pltpu.VMEM((hp, SSD_STATE), F32),
        ],
        compiler_params=_cparams(("arbitrary", "arbitrary")),
        name="ssd_prompt",
    )(proj, proj, dt_raw, conv_w, conv_b, dt_bias, a_log, d_exp, norm_g)


def _mix_sample_body(p_ref, dt_ref, h0_ref, rgc0_ref, s0_ref, sc0_ref,
                     rcw_ref, rcb_ref, wa_ref, ba_ref, wi_ref, bi_ref, lam_ref,
                     scw_ref, scb_ref, dtb_ref, alog_ref, dexp_ref, ng_ref,
                     y_ref, h_ref, rgc_ref, s_ref, sc_ref):
    nl = p_ref.shape[1] // D_MAIN
    bb = p_ref.shape[0]
    o_gate, o_z, o_xbc = D_RG, 2 * D_RG, 2 * D_RG + D_SSD

    def col(l, off, width):
        return p_ref[:, l * D_MAIN + off:l * D_MAIN + off + width]

    hist = [rgc0_ref[:, k * D_RG:(k + 1) * D_RG] for k in range(CONV_W - 1)]
    raw = hist + [col(l, 0, D_RG) for l in range(nl)]
    xcs = []
    for l in range(nl):
        acc = rcb_ref[...] + raw[l] * rcw_ref[0:1, :]
        for k in range(1, CONV_W):
            acc = acc + raw[l + k] * rcw_ref[k:k + 1, :]
        xcs.append(acc)
    a, u = _rg_gates(jnp.concatenate(xcs, axis=0), wa_ref, ba_ref, wi_ref, bi_ref, lam_ref)
    h = h0_ref[...]
    for l in range(nl):
        h = a[l * bb:(l + 1) * bb] * h + u[l * bb:(l + 1) * bb]
        y_ref[:, l * 2 * D_MODEL:l * 2 * D_MODEL + D_RG] = h * _gelu_tanh(col(l, o_gate, D_RG))
    h_ref[...] = h
    for k in range(CONV_W - 1):
        rgc_ref[:, k * D_RG:(k + 1) * D_RG] = raw[nl + k]

    hist = [sc0_ref[:, k * D_XBC:(k + 1) * D_XBC] for k in range(CONV_W - 1)]
    raw = hist + [col(l, o_xbc, D_XBC) for l in range(nl)]
    for k in range(CONV_W - 1):
        sc_ref[:, k * D_XBC:(k + 1) * D_XBC] = raw[nl + k]
    xcs = []
    for l in range(nl):
        acc = scb_ref[...] + raw[l] * scw_ref[0:1, :]
        for k in range(1, CONV_W):
            acc = acc + raw[l + k] * scw_ref[k:k + 1, :]
        xcs.append(_silu(acc))
    xs = [x[:, :D_SSD] for x in xcs]
    bms = [x[:, D_SSD:D_SSD + D_BC].astype(BF16) for x in xcs]
    cms = [x[:, D_SSD + D_BC:].astype(BF16) for x in xcs]

    lane = lax.broadcasted_iota(I32, (1, LANES), 1)
    a_neg = jnp.where(lane < SSD_HEADS, -jnp.exp(alog_ref[...]), 0.0)
    dts = [_softplus(dt_ref[:, l * LANES:(l + 1) * LANES] + dtb_ref[...]) for l in range(nl)]
    acums = []
    run = jnp.zeros((bb, LANES), F32)
    for l in range(nl):
        run = run + dts[l] * a_neg
        acums.append(run)
    a_last = acums[nl - 1]

    e_f32 = _head_expand_matrix(F32)
    e_b16 = _head_expand_matrix(BF16)
    stacked = jnp.concatenate(dts + [jnp.exp(a_last - ac) for ac in acums] + [jnp.exp(ac) for ac in acums], axis=0)
    expd = _expand_heads(stacked, e_f32)
    dt_e = [expd[l * bb:(l + 1) * bb] for l in range(nl)]
    dend_e = [expd[(nl + l) * bb:(nl + l + 1) * bb] for l in range(nl)]
    ea_e = [expd[(2 * nl + l) * bb:(2 * nl + l + 1) * bb] for l in range(nl)]
    xdt = [xs[l] * dt_e[l] for l in range(nl)]
    xdt_r = [x.astype(BF16).astype(F32) for x in xdt]
    rows = nl * bb
    pad_rows = LANES - rows
    w_all = jnp.concatenate([(xdt[l] * dend_e[l]) for l in range(nl)]
                            + [jnp.zeros((pad_rows, D_SSD), F32)], axis=0)
    c_all = jnp.concatenate(cms, axis=0)
    b_all = jnp.concatenate(bms + [jnp.zeros((pad_rows, D_BC), BF16)], axis=0)

    r = lax.broadcasted_iota(I32, (D_BC, LANES), 0)
    cidx = lax.broadcasted_iota(I32, (D_BC, LANES), 1)
    gsum = jnp.where((lax.shift_right_logical(r, 7) == lax.shift_right_logical(cidx, 3)) & (cidx < SSD_HEADS),
                     1.0, 0.0).astype(F32)
    pairs = [(l, s) for l in range(nl) for s in range(l + 1)]
    prods = jnp.concatenate([cms[l].astype(F32) * bms[s].astype(F32) for (l, s) in pairs], axis=0)
    cbh = jnp.dot(prods, gsum, precision=HIGHEST, preferred_element_type=F32)
    m_list = []
    for idx, (l, s) in enumerate(pairs):
        decay = jnp.exp(acums[l] - acums[s])
        m_list.append((cbh[idx * bb:(idx + 1) * bb] * decay).astype(BF16))
    m_e = jnp.dot(jnp.concatenate(m_list, axis=0), e_b16, preferred_element_type=F32)
    y_diag = []
    for l in range(nl):
        acc = None
        for idx, (ll, s) in enumerate(pairs):
            if ll != l:
                continue
            term = m_e[idx * bb:(idx + 1) * bb] * xdt_r[s]
            acc = term if acc is None else acc + term
        y_diag.append(acc)

    cd_t = jnp.concatenate([jnp.exp(a_last), jnp.zeros((LANES - bb, LANES), F32)], axis=0).T
    row_seq = lax.broadcasted_iota(I32, (rows, 1), 0) & (bb - 1)
    row_pad = lax.broadcasted_iota(I32, (LANES, 1), 0)
    lane_q = lax.broadcasted_iota(I32, (LANES, LANES), 1)
    hpg = SSD_HEADS // SSD_GROUPS

    def seq_step(q, y_off):
        mine = row_seq == q
        mine_pad = ((row_pad & (bb - 1)) == q) & (row_pad < rows)
        cd_q = jnp.sum(jnp.where(lane_q == q, cd_t, 0.0), axis=1, keepdims=True)
        cd_q = jnp.broadcast_to(cd_q, (LANES, SSD_STATE))
        parts = []
        for g in range(SSD_GROUPS):
            s0 = s0_ref[q, g * D_BC:(g + 1) * D_BC, :]
            cg = c_all[:, g * SSD_STATE:(g + 1) * SSD_STATE]
            yq = lax.dot_general(cg, s0.astype(BF16), (((1,), (1,)), ((), ())), preferred_element_type=F32)
            parts.append(yq)
            bg = b_all[:, g * SSD_STATE:(g + 1) * SSD_STATE]
            wq = jnp.where(mine_pad, w_all[:, g * D_BC:(g + 1) * D_BC], 0.0)
            upd = jnp.dot(wq.T.astype(BF16), bg, preferred_element_type=F32)
            for e in range(hpg):
                hh = g * hpg + e
                cd = jnp.broadcast_to(cd_q[hh:hh + 1, :], (SSD_HEAD_DIM, SSD_STATE))
                s_ref[q, hh * SSD_HEAD_DIM:(hh + 1) * SSD_HEAD_DIM, :] = (
                    cd * s0[e * SSD_HEAD_DIM:(e + 1) * SSD_HEAD_DIM, :]
                    + upd[e * SSD_HEAD_DIM:(e + 1) * SSD_HEAD_DIM, :])
        yq_all = jnp.concatenate(parts, axis=1)
        return jnp.where(mine, yq_all, y_off)

    y_off = lax.fori_loop(0, bb, seq_step, jnp.zeros((rows, D_SSD), F32))

    for l in range(nl):
        y = (y_diag[l] + y_off[l * bb:(l + 1) * bb] * ea_e[l] + dexp_ref[...] * xs[l]) * _silu(col(l, o_z, D_SSD))
        y_ref[:, l * 2 * D_MODEL + D_RG:(l + 1) * 2 * D_MODEL] = _group_rmsnorm(y, ng_ref[...])


def _mix_sample(proj4, dt4, row0, n_seq, h0, rgc0, s0, sc0, rg_params, ssd_params):
    bb = SAMPLE_BB
    nl = proj4.shape[1] // D_MAIN
    hp = SSD_HEADS * SSD_HEAD_DIM
    blk0 = row0 // bb
    const2 = lambda i: (0, 0)
    rg_specs = [
        pl.BlockSpec((CONV_W, D_RG), const2), pl.BlockSpec((1, D_RG), const2),
        pl.BlockSpec((RG_BLOCKS, RG_BLOCK_W, RG_BLOCK_W), lambda i: (0, 0, 0)), pl.BlockSpec((1, D_RG), const2),
        pl.BlockSpec((RG_BLOCKS, RG_BLOCK_W, RG_BLOCK_W), lambda i: (0, 0, 0)), pl.BlockSpec((1, D_RG), const2),
        pl.BlockSpec((1, D_RG), const2),
    ]
    ssd_specs = [
        pl.BlockSpec((CONV_W, D_XBC), const2), pl.BlockSpec((1, D_XBC), const2),
        pl.BlockSpec((1, LANES), const2), pl.BlockSpec((1, LANES), const2),
        pl.BlockSpec((1, D_SSD), const2), pl.BlockSpec((1, D_SSD), const2),
    ]
    return pl.pallas_call(
        _mix_sample_body,
        grid=(n_seq // bb,),
        in_specs=[
            pl.BlockSpec((bb, nl * D_MAIN), lambda i: (blk0 + i, 0)),
            pl.BlockSpec((bb, nl * LANES), lambda i: (blk0 + i, 0)),
            pl.BlockSpec((bb, D_RG), lambda i: (i, 0)),
            pl.BlockSpec((bb, (CONV_W - 1) * D_RG), lambda i: (i, 0)),
            pl.BlockSpec((bb, hp, SSD_STATE), lambda i: (i, 0, 0)),
            pl.BlockSpec((bb, (CONV_W - 1) * D_XBC), lambda i: (i, 0)),
        ] + rg_specs + ssd_specs,
        out_specs=[
            pl.BlockSpec((bb, nl * 2 * D_MODEL), lambda i: (i, 0)),
            pl.BlockSpec((bb, D_RG), lambda i: (i, 0)),
            pl.BlockSpec((bb, (CONV_W - 1) * D_RG), lambda i: (i, 0)),
            pl.BlockSpec((bb, hp, SSD_STATE), lambda i: (i, 0, 0)),
            pl.BlockSpec((bb, (CONV_W - 1) * D_XBC), lambda i: (i, 0)),
        ],
        out_shape=[
            jax.ShapeDtypeStruct((n_seq, nl * 2 * D_MODEL), F32),
            jax.ShapeDtypeStruct((n_seq, D_RG), F32),
            jax.ShapeDtypeStruct((n_seq, (CONV_W - 1) * D_RG), F32),
            jax.ShapeDtypeStruct((n_seq, hp, SSD_STATE), F32),
            jax.ShapeDtypeStruct((n_seq, (CONV_W - 1) * D_XBC), F32),
        ],
        compiler_params=_cparams(("arbitrary",)),
        name="mix_sample",
    )(proj4, dt4, h0, rgc0, s0, sc0, *rg_params, *ssd_params)


def _route_body(x_ref, g_ref, wr_ref, br_ref, hp_ref, e_ref, gate_ref, rank_ref, cnt_ref, run_ref):
    i = pl.program_id(0)
    tm = x_ref.shape[0]

    @pl.when(i == 0)
    def _():
        run_ref[...] = jnp.zeros_like(run_ref)

    x = x_ref[...]
    h = x * lax.rsqrt(jnp.mean(x * x, axis=-1, keepdims=True) + EPS) * g_ref[...]
    hp_ref[...] = h

    lane = lax.broadcasted_iota(I32, (tm, LANES), 1).astype(F32)
    logits = jnp.dot(h, wr_ref[...], precision=HIGHEST, preferred_element_type=F32) + br_ref[...]
    work = jnp.where(lane < N_EXPERTS, logits, -jnp.inf)
    vals, idxs = [], []
    multi = jnp.zeros((tm, LANES), F32)
    for _ in range(TOP_K):
        m = jnp.max(work, axis=-1, keepdims=True)
        idx = jnp.min(jnp.where(work == m, lane, float(LANES)), axis=-1, keepdims=True)
        hit = lane == idx
        vals.append(m)
        idxs.append(idx)
        multi = jnp.where(hit, 1.0, multi)
        work = jnp.where(hit, -jnp.inf, work)
    ex = [jnp.exp(v - vals[0]) for v in vals]
    den = ex[0] + ex[1] + ex[2] + ex[3]
    gates = [e / den for e in ex]

    ri = lax.broadcasted_iota(I32, (tm, tm), 0)
    ci = lax.broadcasted_iota(I32, (tm, tm), 1)
    strict = jnp.where(ci < ri, 1.0, 0.0).astype(BF16)
    before = jnp.dot(strict, multi.astype(BF16), preferred_element_type=F32) + run_ref[...]
    ranks = [jnp.sum(jnp.where(lane == idx, before, 0.0), axis=-1, keepdims=True) for idx in idxs]
    run_ref[...] = run_ref[...] + jnp.sum(multi, axis=0, keepdims=True)

    lane4 = lax.broadcasted_iota(I32, (tm, TOP_K), 1)
    e_out = jnp.zeros((tm, TOP_K), I32)
    g_out = jnp.zeros((tm, TOP_K), F32)
    r_out = jnp.zeros((tm, TOP_K), I32)
    for k in range(TOP_K):
        e_out = jnp.where(lane4 == k, idxs[k].astype(I32), e_out)
        g_out = jnp.where(lane4 == k, gates[k], g_out)
        r_out = jnp.where(lane4 == k, ranks[k].astype(I32), r_out)
    e_ref[...] = e_out
    gate_ref[...] = g_out
    rank_ref[...] = r_out
    cnt_ref[...] = run_ref[...].astype(I32)


def _route(x1, g_ffn, w_router_pad, b_router_pad):
    t = x1.shape[0]
    tm = ROUTE_TM
    const = lambda i: (0, 0)
    return pl.pallas_call(
        _route_body,
        grid=(t // tm,),
        in_specs=[
            pl.BlockSpec((tm, D_MODEL), lambda i: (i, 0)),
            pl.BlockSpec((1, D_MODEL), const),
            pl.BlockSpec((D_MODEL, LANES), const),
            pl.BlockSpec((1, LANES), const),
        ],
        out_specs=[
            pl.BlockSpec((tm, D_MODEL), lambda i: (i, 0)),
            pl.BlockSpec((tm, TOP_K), lambda i: (i, 0)),
            pl.BlockSpec((tm, TOP_K), lambda i: (i, 0)),
            pl.BlockSpec((tm, TOP_K), lambda i: (i, 0)),
            pl.BlockSpec((1, LANES), const),
        ],
        out_shape=[
            jax.ShapeDtypeStruct((t, D_MODEL), F32),
            jax.ShapeDtypeStruct((t, TOP_K), I32),
            jax.ShapeDtypeStruct((t, TOP_K), F32),
            jax.ShapeDtypeStruct((t, TOP_K), I32),
            jax.ShapeDtypeStruct((1, LANES), I32),
        ],
        scratch_shapes=[pltpu.VMEM((1, LANES), F32)],
        compiler_params=_cparams(("arbitrary",)),
        name="route",
    )(x1, g_ffn, w_router_pad, b_router_pad)


def _scatter_body(seg_ref, cnt_ref, e_ref, rank_ref, h_ref, xs_ref, zero_ref, sem, zsem):
    i = pl.program_id(0)
    tm = h_ref.shape[0]
    n = e_ref.shape[0]

    def issue(blk, c):
        for u in range(DMA_UNROLL):
            a = blk * DMA_UNROLL + u
            pos = seg_ref[e_ref[a]] + rank_ref[a]
            t = lax.shift_right_logical(a, 2)
            pltpu.make_async_copy(h_ref.at[pl.ds(t, 1), :], xs_ref.at[pl.ds(pos, 1), :], sem).start()
        return c

    lax.fori_loop(0, n // DMA_UNROLL, issue, 0)
    for _ in range(TOP_K):
        pltpu.make_async_copy(h_ref, xs_ref.at[pl.ds(0, tm), :], sem).wait()

    @pl.when(i == pl.num_programs(0) - 1)
    def _():
        zero_ref[...] = jnp.zeros_like(zero_ref)
        pad = EXPERT_PAD

        def row_fill(row):
            return pltpu.make_async_copy(zero_ref.at[pl.ds(0, 1), :], xs_ref.at[pl.ds(row, 1), :], zsem)

        def block_fill(blk):
            dst = pl.multiple_of(blk * pad, pad)
            return pltpu.make_async_copy(zero_ref, xs_ref.at[pl.ds(dst, pad), :], zsem)

        def for_each(lo, hi, fn):
            def step(r, c):
                fn(r)
                return c
            lax.fori_loop(lo, hi, step, 0)

        def per_expert(e, used):
            cnt = cnt_ref[e]
            hi = ((cnt + pad - 1) // pad) * pad
            base = seg_ref[e]
            for_each(cnt, hi, lambda r: row_fill(base + r).start())
            for_each(cnt, hi, lambda r: row_fill(base + r).wait())
            return used + hi // pad

        used = lax.fori_loop(0, N_EXPERTS, per_expert, jnp.int32(0))
        n_blocks = xs_ref.shape[0] // pad
        for_each(used, n_blocks, lambda blk: block_fill(blk).start())
        for_each(used, n_blocks, lambda blk: block_fill(blk).wait())


def _scatter(seg_start, counts, e_flat, rank_flat, h2, n_slots):
    t = h2.shape[0]
    tm = SCATTER_TM
    grid_spec = pltpu.PrefetchScalarGridSpec(
        num_scalar_prefetch=2,
        grid=(t // tm,),
        in_specs=[
            pl.BlockSpec((tm * TOP_K,), lambda i, seg, cnt: (i,), memory_space=pltpu.SMEM),
            pl.BlockSpec((tm * TOP_K,), lambda i, seg, cnt: (i,), memory_space=pltpu.SMEM),
            pl.BlockSpec((tm, D_MODEL), lambda i, seg, cnt: (i, 0)),
        ],
        out_specs=pl.BlockSpec(memory_space=pl.ANY),
        scratch_shapes=[pltpu.VMEM((EXPERT_PAD, D_MODEL), F32), pltpu.SemaphoreType.DMA(()),
                        pltpu.SemaphoreType.DMA(())],
    )
    return pl.pallas_call(
        _scatter_body,
        grid_spec=grid_spec,
        out_shape=jax.ShapeDtypeStruct((n_slots, D_MODEL), F32),
        compiler_params=_cparams(("arbitrary",)),
        name="scatter_rows",
    )(seg_start, counts, e_flat, rank_flat, h2)


def _experts_body(sbe_ref, sbs_ref, sbn_ref, used_ref, xs_ref, wg_ref, wu_ref, bg_ref, bu_ref, wd_ref, bd_ref,
                  ys_ref, xbuf_ref, acc_ref, wgu_ref, wdb_ref, in_sem, out_sem):
    del sbe_ref
    sb = pl.program_id(0)
    j = pl.program_id(1)
    nj = pl.num_programs(1)
    nblk = sbn_ref[sb]
    start = sbs_ref[sb]
    cb = EXPERT_PAD
    bm = EXPERT_BM
    tf = wd_ref.shape[0]

    def in_copy(r):
        src = pl.multiple_of(start + r * cb, cb)
        dst = pl.multiple_of(r * cb, cb)
        return pltpu.make_async_copy(xs_ref.at[pl.ds(src, cb), :], acc_ref.at[pl.ds(dst, cb), :], in_sem)

    def out_copy(r):
        src = pl.multiple_of(r * cb, cb)
        dst = pl.multiple_of(start + r * cb, cb)
        return pltpu.make_async_copy(acc_ref.at[pl.ds(src, cb), :], ys_ref.at[pl.ds(dst, cb), :], out_sem)

    def zero_copy(blk):
        dst = pl.multiple_of(blk * cb, cb)
        return pltpu.make_async_copy(acc_ref.at[pl.ds(0, cb), :], ys_ref.at[pl.ds(dst, cb), :], out_sem)

    def for_each(lo, hi, fn):
        def step(r, c):
            fn(r)
            return c
        lax.fori_loop(lo, hi, step, 0)

    @pl.when(j == 0)
    def _():
        for_each(0, nblk, lambda r: in_copy(r).start())
        for_each(0, nblk, lambda r: in_copy(r).wait())

        def stage(r):
            r0 = pl.multiple_of(r * cb, cb)
            xbuf_ref[pl.ds(r0, cb), :] = acc_ref[pl.ds(r0, cb), :].astype(BF16)
            acc_ref[pl.ds(r0, cb), :] = jnp.broadcast_to(bd_ref[...], (cb, D_MODEL))

        for_each(0, nblk, stage)

    @pl.when(nblk > 0)
    def _():
        def cast_rows(s):
            r = pl.multiple_of(s * cb, cb)
            wgu_ref[pl.ds(r, cb), 0:tf] = wg_ref[pl.ds(r, cb), :].astype(BF16)
            wgu_ref[pl.ds(r, cb), tf:2 * tf] = wu_ref[pl.ds(r, cb), :].astype(BF16)

        for_each(0, D_MODEL // cb, cast_rows)
        wdb_ref[...] = wd_ref[...].astype(BF16)
        bias_gu = jnp.concatenate([bg_ref[...], bu_ref[...]], axis=1)

        def sub(r0, rows):
            x = xbuf_ref[pl.ds(r0, rows), :]
            gu = jnp.dot(x, wgu_ref[...], preferred_element_type=F32) + bias_gu
            gate = jnp.minimum(gu[:, :tf], SWIGLU_LIMIT)
            up = jnp.clip(gu[:, tf:], -SWIGLU_LIMIT, SWIGLU_LIMIT)
            glu = gate * _sigmoid(SWIGLU_ALPHA * gate)
            act = ((up + 1.0) * glu).astype(BF16)
            acc_ref[pl.ds(r0, rows), :] += jnp.dot(act, wdb_ref[...], preferred_element_type=F32)

        per_sub = bm // cb
        n_full = nblk // per_sub
        for_each(0, n_full, lambda i: sub(pl.multiple_of(i * bm, bm), bm))

        @pl.when(nblk % per_sub == 1)
        def _():
            sub(pl.multiple_of(n_full * bm, bm), cb)

    @pl.when(j == nj - 1)
    def _():
        for_each(0, nblk, lambda r: out_copy(r).start())
        for_each(0, nblk, lambda r: out_copy(r).wait())

    @pl.when((sb == pl.num_programs(0) - 1) & (j == nj - 1))
    def _():
        acc_ref[0:cb, :] = jnp.zeros((cb, D_MODEL), F32)
        n_blocks = ys_ref.shape[0] // cb
        for_each(used_ref[0], n_blocks, lambda blk: zero_copy(blk).start())
        for_each(used_ref[0], n_blocks, lambda blk: zero_copy(blk).wait())


def _experts(sb_expert, sb_start, sb_nsub, used_blocks, xsorted, w_gu, b_gu, w_down, b_down):
    n_sb = sb_expert.shape[0]
    n_slots = xsorted.shape[0]
    tf = EXPERT_TF
    nj = D_FF // tf

    def spec(shape, fn):
        def index_map(sb, j, sbe, sbs, sbn, used):
            jj = jnp.where(sbn[sb] > 0, j, nj - 1)
            return fn(sbe[sb], jj)
        return pl.BlockSpec(shape, index_map)

    grid_spec = pltpu.PrefetchScalarGridSpec(
        num_scalar_prefetch=4,
        grid=(n_sb, nj),
        in_specs=[
            pl.BlockSpec(memory_space=pl.ANY),
            spec((None, D_MODEL, tf), lambda e, jj: (e, 0, jj)),
            spec((None, D_MODEL, tf), lambda e, jj: (e, 0, nj + jj)),
            spec((None, 1, tf), lambda e, jj: (e, 0, jj)),
            spec((None, 1, tf), lambda e, jj: (e, 0, nj + jj)),
            spec((None, tf, D_MODEL), lambda e, jj: (e, jj, 0)),
            spec((None, 1, D_MODEL), lambda e, jj: (e, 0, 0)),
        ],
        out_specs=pl.BlockSpec(memory_space=pl.ANY),
        scratch_shapes=[
            pltpu.VMEM((EXPERT_R, D_MODEL), BF16),
            pltpu.VMEM((EXPERT_R, D_MODEL), F32),
            pltpu.VMEM((D_MODEL, 2 * tf), BF16),
            pltpu.VMEM((tf, D_MODEL), BF16),
            pltpu.SemaphoreType.DMA(()),
            pltpu.SemaphoreType.DMA(()),
        ],
    )
    return pl.pallas_call(
        _experts_body,
        grid_spec=grid_spec,
        out_shape=jax.ShapeDtypeStruct((n_slots, D_MODEL), F32),
        compiler_params=_cparams(("arbitrary", "arbitrary")),
        name="experts",
    )(sb_expert, sb_start, sb_nsub, used_blocks, xsorted, w_gu, w_gu, b_gu, b_gu, w_down, b_down)


def _combine_body(seg_ref, e_ref, rank_ref, x1_ref, gate_ref, gf_ref, ys_ref, yp_ref, ysm_ref, buf_ref, sem,
                  *, n_prompt_tiles):
    i = pl.program_id(0)
    tm = x1_ref.shape[0]
    n = e_ref.shape[0]

    def issue(blk, c):
        for u in range(DMA_UNROLL):
            a = blk * DMA_UNROLL + u
            pos = seg_ref[e_ref[a]] + rank_ref[a]
            t = lax.shift_right_logical(a, 2)
            pltpu.make_async_copy(ys_ref.at[pl.ds(pos, 1), :], buf_ref.at[u % TOP_K, pl.ds(t, 1), :], sem).start()
        return c

    lax.fori_loop(0, n // DMA_UNROLL, issue, 0)
    for k in range(TOP_K):
        pltpu.make_async_copy(ys_ref.at[pl.ds(0, tm), :], buf_ref.at[k], sem).wait()

    g = gate_ref[...]
    acc = buf_ref[0] * g[:, 0:1]
    for k in range(1, TOP_K):
        acc = acc + buf_ref[k] * g[:, k:k + 1]
    x = x1_ref[...] + acc
    y = x * lax.rsqrt(jnp.mean(x * x, axis=-1, keepdims=True) + EPS) * gf_ref[...]

    @pl.when(i < n_prompt_tiles)
    def _():
        yp_ref[...] = y

    @pl.when(i >= n_prompt_tiles)
    def _():
        ysm_ref[...] = y


def _combine(seg_start, e_flat, rank_flat, x1, gates, g_final, ysorted, t_prompt):
    t = x1.shape[0]
    tm = COMBINE_TM
    npt = t_prompt // tm
    grid_spec = pltpu.PrefetchScalarGridSpec(
        num_scalar_prefetch=1,
        grid=(t // tm,),
        in_specs=[
            pl.BlockSpec((tm * TOP_K,), lambda i, seg: (i,), memory_space=pltpu.SMEM),
            pl.BlockSpec((tm * TOP_K,), lambda i, seg: (i,), memory_space=pltpu.SMEM),
            pl.BlockSpec((tm, D_MODEL), lambda i, seg: (i, 0)),
            pl.BlockSpec((tm, TOP_K), lambda i, seg: (i, 0)),
            pl.BlockSpec((1, D_MODEL), lambda i, seg: (0, 0)),
            pl.BlockSpec(memory_space=pl.ANY),
        ],
        out_specs=[
            pl.BlockSpec((tm, D_MODEL), lambda i, seg: (jnp.minimum(i, npt - 1), 0)),
            pl.BlockSpec((tm, D_MODEL), lambda i, seg: (jnp.maximum(i - npt, 0), 0)),
        ],
        scratch_shapes=[pltpu.VMEM((TOP_K, tm, D_MODEL), F32), pltpu.SemaphoreType.DMA(())],
    )
    return pl.pallas_call(
        functools.partial(_combine_body, n_prompt_tiles=npt),
        grid_spec=grid_spec,
        out_shape=[
            jax.ShapeDtypeStruct((t_prompt, D_MODEL), F32),
            jax.ShapeDtypeStruct((t - t_prompt, D_MODEL), F32),
        ],
        compiler_params=_cparams(("arbitrary",)),
        name="combine",
    )(seg_start, e_flat, rank_flat, x1, gates, g_final, ysorted)


def _expert_tables(counts, n_assign):
    pad = EXPERT_PAD
    padded = ((counts + pad - 1) // pad) * pad
    seg_start = jnp.cumsum(padded) - padded
    n_sb_e = (counts + EXPERT_R - 1) // EXPERT_R
    sb_cum = jnp.cumsum(n_sb_e)
    n_sb = n_assign // EXPERT_R + N_EXPERTS
    s = jnp.arange(n_sb, dtype=I32)
    total = sb_cum[-1]
    s_eff = jnp.minimum(s, total - 1)
    e = jnp.sum((sb_cum[None, :] <= s_eff[:, None]).astype(I32), axis=1)
    k = s_eff - (sb_cum[e] - n_sb_e[e])
    rows = jnp.clip(counts[e] - k * EXPERT_R, 0, EXPERT_R)
    valid = s < total
    nsub = jnp.where(valid, (rows + pad - 1) // pad, 0)
    start = seg_start[e] + k * EXPERT_R
    used_blocks = (jnp.sum(padded) // pad).reshape(1)
    return seg_start.astype(I32), e, start.astype(I32), nsub.astype(I32), used_blocks.astype(I32)


def kernel(x_prompt, x_sample, state_rglru_h, state_rglru_conv, state_ssd, state_ssd_conv, g_mix, w_in, conv_rg_w, conv_rg_b, rg_wa, rg_ba, rg_wi, rg_bi, rg_lambda, conv_ssd_w, conv_ssd_b, ssd_dt_bias, ssd_a_log, ssd_d, ssd_norm_g, w_out, g_ffn, w_router, b_router, w_gu, b_gu, w_down, b_down, g_final):
    depth = g_mix.shape[0]
    assert depth == 1
    bp, lp, _ = x_prompt.shape
    bs, ls, _ = x_sample.shape
    tp, ts = bp * lp, bs * ls
    t = tp + ts
    hp = SSD_HEADS * SSD_HEAD_DIM
    l = 0

    xp = x_prompt.reshape(tp, D_MODEL)
    xs = x_sample.reshape(ts, D_MODEL)
    w_in_l = w_in[l]
    wdt = jnp.pad(w_in_l[:, D_MAIN:], ((0, 0), (0, LANES - SSD_HEADS)))
    row = lambda v: v.reshape(1, -1)
    pad_heads = lambda v: jnp.pad(v.reshape(1, -1), ((0, 0), (0, LANES - SSD_HEADS)))

    hb, dt_raw = _norm_dt(xp, xs, row(g_mix[l]), wdt)
    proj = _in_proj(hb, w_in_l)

    rg_params = (conv_rg_w[l], row(conv_rg_b[l]), rg_wa[l], row(rg_ba[l]), rg_wi[l], row(rg_bi[l]),
                 row(rg_lambda[l]))
    d_exp = row(jnp.repeat(ssd_d[l], SSD_HEAD_DIM))
    ssd_params = (conv_ssd_w[l], row(conv_ssd_b[l]), pad_heads(ssd_dt_bias[l]), pad_heads(ssd_a_log[l]),
                  d_exp, row(ssd_norm_g[l]))

    y_rg, rgh_p, rgc_p = _rglru_prompt(proj, bp, lp, *rg_params)
    y_ssd, ssd_p, ssdc_p = _ssd_prompt(proj, dt_raw, bp, lp, *ssd_params)

    proj4 = proj[tp:].reshape(bs, ls * D_MAIN)
    dt4 = dt_raw[tp:].reshape(bs, ls * LANES)
    mix_s4, rgh_s, rgc_s, ssd_s, ssdc_s = _mix_sample(
        proj4, dt4, 0, bs,
        state_rglru_h[l], state_rglru_conv[l].reshape(bs, -1),
        state_ssd[l].reshape(bs, hp, SSD_STATE), state_ssd_conv[l].reshape(bs, -1),
        rg_params, ssd_params)
    mix_s = mix_s4.reshape(ts, 2 * D_MODEL)

    x1 = _out_proj(y_rg, y_ssd, mix_s, w_out[l], xp, xs)

    wr = jnp.pad(w_router[l], ((0, 0), (0, LANES - N_EXPERTS)))
    br = jnp.pad(b_router[l].reshape(1, -1), ((0, 0), (0, LANES - N_EXPERTS)))
    h2, e_idx, gates, rank, counts = _route(x1, row(g_ffn[l]), wr, br)

    n_assign = t * TOP_K
    n_slots = n_assign + N_EXPERTS * EXPERT_PAD
    counts = counts[0, :N_EXPERTS]
    seg_start, sb_expert, sb_start, sb_nsub, used_blocks = _expert_tables(counts, n_assign)
    e_flat = e_idx.reshape(-1)
    rank_flat = rank.reshape(-1)
    xsorted = _scatter(seg_start, counts, e_flat, rank_flat, h2, n_slots)
    ysorted = _experts(sb_expert, sb_start, sb_nsub, used_blocks, xsorted, w_gu[l], b_gu[l].reshape(N_EXPERTS, 1, -1),
                       w_down[l], b_down[l].reshape(N_EXPERTS, 1, -1))
    y_p, y_s = _combine(seg_start, e_flat, rank_flat, x1, gates, row(g_final), ysorted, tp)

    return (y_p.reshape(x_prompt.shape).astype(x_prompt.dtype),
            y_s.reshape(x_sample.shape).astype(x_sample.dtype),
            rgh_p.reshape(depth, bp, D_RG),
            rgc_p.reshape(depth, bp, CONV_W - 1, D_RG),
            ssd_p.reshape(depth, bp, SSD_HEADS, SSD_HEAD_DIM, SSD_STATE),
            ssdc_p.reshape(depth, bp, CONV_W - 1, D_XBC),
            rgh_s.reshape(depth, bs, D_RG),
            rgc_s.reshape(depth, bs, CONV_W - 1, D_RG),
            ssd_s.reshape(depth, bs, SSD_HEADS, SSD_HEAD_DIM, SSD_STATE),
            ssdc_s.reshape(depth, bs, CONV_W - 1, D_XBC))
```

```python
import functools

import jax
import jax.numpy as jnp
from jax import lax
from jax.experimental import pallas as pl
from jax.experimental.pallas import tpu as pltpu

F32 = jnp.float32
BF16 = jnp.bfloat16
I32 = jnp.int32
HIGHEST = lax.Precision.HIGHEST

EPS = 1e-6
D_MODEL = 2048
D_RG = 2048
RG_BLOCKS = 16
RG_BLOCK_W = 128
RG_C = 8.0
D_SSD = 2048
SSD_HEAD_DIM = 64
SSD_HEADS = 32
SSD_GROUPS = 4
SSD_STATE = 128
SSD_CHUNK = 128
CONV_W = 4
D_BC = SSD_GROUPS * SSD_STATE
D_XBC = D_SSD + 2 * D_BC
D_MAIN = 2 * D_RG + D_SSD + D_XBC
N_EXPERTS = 32
TOP_K = 4
D_FF = 2048
SWIGLU_ALPHA = 1.702
SWIGLU_LIMIT = 7.0

LANES = 128
VMEM_LIMIT = 56 * 1024 * 1024

ROW_TILE = 512
PROJ_TM = 1088
PROJ_TN = 1024
OUT_TN = 512
RG_TL = 256
SAMPLE_BB = 8
ROUTE_TM = 512
SCATTER_TM = 256
COMBINE_TM = 256
EXPERT_PAD = 256
EXPERT_BM = 512
EXPERT_R = 1536
EXPERT_TF = 512
DMA_UNROLL = 8


def _cparams(sem, vmem=VMEM_LIMIT):
    return pltpu.CompilerParams(dimension_semantics=sem, vmem_limit_bytes=vmem)


def _softplus(x):
    return jnp.maximum(x, 0.0) + jnp.log1p(jnp.exp(-jnp.abs(x)))


_sigmoid = jax.nn.sigmoid
_silu = jax.nn.silu


def _gelu_tanh(x):
    return jax.nn.gelu(x, approximate=True)


def _head_expand_matrix(dtype):
    r = lax.broadcasted_iota(I32, (LANES, D_SSD), 0)
    c = lax.broadcasted_iota(I32, (LANES, D_SSD), 1)
    return jnp.where(lax.shift_right_logical(c, 6) == r, 1.0, 0.0).astype(dtype)


def _expand_heads(v, e_b16):
    rows = v.shape[0]
    hi = v.astype(BF16)
    r1 = v - hi.astype(F32)
    mid = r1.astype(BF16)
    lo = (r1 - mid.astype(F32)).astype(BF16)
    o = jnp.dot(jnp.concatenate([hi, mid, lo], axis=0), e_b16, preferred_element_type=F32)
    return (o[0:rows] + o[rows:2 * rows]) + o[2 * rows:3 * rows]


def _norm_dt_body(xp_ref, xs_ref, g_ref, wdt_ref, hb_ref, dt_ref, *, n_prompt_tiles):
    i = pl.program_id(0)

    def run(x_ref):
        x = x_ref[...]
        h = x * lax.rsqrt(jnp.mean(x * x, axis=-1, keepdims=True) + EPS) * g_ref[...]
        hb = h.astype(BF16)
        hb_ref[...] = hb
        lane = lax.broadcasted_iota(I32, (1, LANES), 1)
        wdt = jnp.where(lane < SSD_HEADS, wdt_ref[...], 0.0).astype(BF16)
        dt_ref[...] = jnp.dot(hb, wdt, preferred_element_type=F32)

    @pl.when(i < n_prompt_tiles)
    def _():
        run(xp_ref)

    @pl.when(i >= n_prompt_tiles)
    def _():
        run(xs_ref)


def _norm_dt(xp, xs, g, w_in):
    tp, ts = xp.shape[0], xs.shape[0]
    npt, nst = tp // ROW_TILE, ts // ROW_TILE
    t = tp + ts
    return pl.pallas_call(
        functools.partial(_norm_dt_body, n_prompt_tiles=npt),
        grid=(npt + nst,),
        in_specs=[
            pl.BlockSpec((ROW_TILE, D_MODEL), lambda i: (jnp.minimum(i, npt - 1), 0)),
            pl.BlockSpec((ROW_TILE, D_MODEL), lambda i: (jnp.maximum(i - npt, 0), 0)),
            pl.BlockSpec((1, D_MODEL), lambda i: (0, 0)),
            pl.BlockSpec((D_MODEL, LANES), lambda i: (0, D_MAIN // LANES)),
        ],
        out_specs=[
            pl.BlockSpec((ROW_TILE, D_MODEL), lambda i: (i, 0)),
            pl.BlockSpec((ROW_TILE, LANES), lambda i: (i, 0)),
        ],
        out_shape=[jax.ShapeDtypeStruct((t, D_MODEL), BF16), jax.ShapeDtypeStruct((t, LANES), F32)],
        compiler_params=_cparams(("parallel",)),
        name="norm_dt",
    )(xp, xs, g, w_in)


def _cast_weight_tile(w_ref, wb_ref, rows_per_step=256):
    k = w_ref.shape[0]

    def step(s, c):
        r = pl.multiple_of(s * rows_per_step, rows_per_step)
        wb_ref[pl.ds(r, rows_per_step), :] = w_ref[pl.ds(r, rows_per_step), :].astype(BF16)
        return c

    lax.fori_loop(0, k // rows_per_step, step, 0)


def _in_proj_body(x_ref, w_ref, o_ref, wb_ref):
    @pl.when(pl.program_id(1) == 0)
    def _():
        _cast_weight_tile(w_ref, wb_ref)

    o_ref[...] = jnp.dot(x_ref[...], wb_ref[...], preferred_element_type=F32)


def _in_proj(hb, w_in):
    t = hb.shape[0]
    return pl.pallas_call(
        _in_proj_body,
        grid=(D_MAIN // PROJ_TN, t // PROJ_TM),
        in_specs=[
            pl.BlockSpec((PROJ_TM, D_MODEL), lambda j, i: (i, 0)),
            pl.BlockSpec((D_MODEL, PROJ_TN), lambda j, i: (0, j)),
        ],
        out_specs=pl.BlockSpec((PROJ_TM, PROJ_TN), lambda j, i: (i, j)),
        out_shape=jax.ShapeDtypeStruct((t, D_MAIN), F32),
        scratch_shapes=[pltpu.VMEM((D_MODEL, PROJ_TN), BF16)],
        compiler_params=_cparams(("arbitrary", "arbitrary")),
        name="in_proj",
    )(hb, w_in)


def _out_proj_body(rg_ref, ssd_ref, ms_ref, w_ref, xp_ref, xs_ref, o_ref, wb_ref, *, n_prompt_tiles):
    i = pl.program_id(1)

    @pl.when(i == 0)
    def _():
        _cast_weight_tile(w_ref, wb_ref)

    @pl.when(i < n_prompt_tiles)
    def _():
        m = jnp.concatenate([rg_ref[...], ssd_ref[...]], axis=1)
        o_ref[...] = xp_ref[...] + jnp.dot(m, wb_ref[...], preferred_element_type=F32)

    @pl.when(i >= n_prompt_tiles)
    def _():
        o_ref[...] = xs_ref[...] + jnp.dot(ms_ref[...].astype(BF16), wb_ref[...], preferred_element_type=F32)


def _out_proj(y_rg, y_ssd, mix_s, w_out, xp, xs):
    tp, ts = y_rg.shape[0], mix_s.shape[0]
    npt, nst = tp // ROW_TILE, ts // ROW_TILE
    prompt_rows = lambda j, i: (jnp.minimum(i, npt - 1), 0)
    return pl.pallas_call(
        functools.partial(_out_proj_body, n_prompt_tiles=npt),
        grid=(D_MODEL // OUT_TN, npt + nst),
        in_specs=[
            pl.BlockSpec((ROW_TILE, D_MODEL), prompt_rows),
            pl.BlockSpec((ROW_TILE, D_MODEL), prompt_rows),
            pl.BlockSpec((ROW_TILE, 2 * D_MODEL), lambda j, i: (jnp.maximum(i - npt, 0), 0)),
            pl.BlockSpec((2 * D_MODEL, OUT_TN), lambda j, i: (0, j)),
            pl.BlockSpec((ROW_TILE, OUT_TN), lambda j, i: (jnp.minimum(i, npt - 1), j)),
            pl.BlockSpec((ROW_TILE, OUT_TN), lambda j, i: (jnp.maximum(i - npt, 0), j)),
        ],
        out_specs=pl.BlockSpec((ROW_TILE, OUT_TN), lambda j, i: (i, j)),
        out_shape=jax.ShapeDtypeStruct((tp + ts, D_MODEL), F32),
        scratch_shapes=[pltpu.VMEM((2 * D_MODEL, OUT_TN), BF16)],
        compiler_params=_cparams(("arbitrary", "arbitrary")),
        name="out_proj",
    )(y_rg, y_ssd, mix_s, w_out, xp, xs)


def _rg_gates(xc, wa_ref, ba_ref, wi_ref, bi_ref, lam_ref):
    xcb = xc.astype(BF16)
    rs, is_ = [], []
    for h in range(RG_BLOCKS):
        xh = xcb[:, h * RG_BLOCK_W:(h + 1) * RG_BLOCK_W]
        rs.append(jnp.dot(xh, wa_ref[h].astype(BF16), preferred_element_type=F32))
        is_.append(jnp.dot(xh, wi_ref[h].astype(BF16), preferred_element_type=F32))
    r = _sigmoid(jnp.concatenate(rs, axis=1) + ba_ref[...])
    ig = _sigmoid(jnp.concatenate(is_, axis=1) + bi_ref[...])
    log_a = (-RG_C * r) * _softplus(-lam_ref[...])
    a = jnp.exp(log_a)
    u = jnp.sqrt(-jnp.tanh(log_a) * (a * a + 1.0)) * (ig * xc)
    return a, u


def _rglru_prompt_body(x_ref, gate_ref, cw_ref, cb_ref, wa_ref, ba_ref, wi_ref, bi_ref, lam_ref,
                       y_ref, h_ref, cs_ref, ext_ref, a_ref, u_ref, car_ref):
    c = pl.program_id(1)
    tl = x_ref.shape[0]

    @pl.when(c == 0)
    def _():
        ext_ref[0:8, :] = jnp.zeros((8, D_RG), F32)
        car_ref[...] = jnp.zeros((1, D_RG), F32)

    ext_ref[8:8 + tl, :] = x_ref[...]
    xc = cb_ref[...] + ext_ref[5:5 + tl, :] * cw_ref[0:1, :]
    xc = xc + ext_ref[6:6 + tl, :] * cw_ref[1:2, :]
    xc = xc + ext_ref[7:7 + tl, :] * cw_ref[2:3, :]
    xc = xc + ext_ref[8:8 + tl, :] * cw_ref[3:4, :]
    tail = ext_ref[tl:tl + 8, :]
    ext_ref[0:8, :] = tail

    a, u = _rg_gates(xc, wa_ref, ba_ref, wi_ref, bi_ref, lam_ref)
    a_ref[...] = a
    u_ref[...] = u

    row = lax.broadcasted_iota(I32, (8, D_RG), 0)

    def group(g, carry):
        r0 = pl.multiple_of(g * 8, 8)
        a8 = a_ref[pl.ds(r0, 8), :]
        u8 = u_ref[pl.ds(r0, 8), :]
        for s in (1, 2, 4):
            keep = row >= s
            a_sh = jnp.where(keep, pltpu.roll(a8, s, 0), 1.0)
            u_sh = jnp.where(keep, pltpu.roll(u8, s, 0), 0.0)
            u8 = a8 * u_sh + u8
            a8 = a8 * a_sh
        h8 = a8 * carry + u8
        u_ref[pl.ds(r0, 8), :] = h8
        return h8[7:8, :]

    carry = lax.fori_loop(0, tl // 8, group, car_ref[...])
    car_ref[...] = carry
    y_ref[...] = (u_ref[...] * _gelu_tanh(gate_ref[...])).astype(BF16)

    @pl.when(c == pl.num_programs(1) - 1)
    def _():
        h_ref[...] = carry
        cs_ref[...] = tail[5:8, :]


def _rglru_prompt(proj, batch, seq, conv_w, conv_b, wa, ba, wi, bi, lam):
    nc = seq // RG_TL
    t = batch * seq
    vec =pl.BlockSpec((1, D_RG), lambda b, c: (0, 0))
    blk = pl.BlockSpec((RG_BLOCKS, RG_BLOCK_W, RG_BLOCK_W), lambda b, c: (0, 0, 0))
    return pl.pallas_call(
        _rglru_prompt_body,
        grid=(batch, nc),
        in_specs=[
            pl.BlockSpec((RG_TL, D_RG), lambda b, c: (b * nc + c, 0)),
            pl.BlockSpec((RG_TL, D_RG), lambda b, c: (b * nc + c, 1)),
            pl.BlockSpec((CONV_W, D_RG), lambda b, c: (0, 0)),
            vec, blk, vec, blk, vec, vec,
        ],
        out_specs=[
            pl.BlockSpec((RG_TL, D_RG), lambda b, c: (b * nc + c, 0)),
            pl.BlockSpec((None, 1, D_RG), lambda b, c: (b, 0, 0)),
            pl.BlockSpec((None, CONV_W - 1, D_RG), lambda b, c: (b, 0, 0)),
        ],
        out_shape=[
            jax.ShapeDtypeStruct((t, D_RG), BF16),
            jax.ShapeDtypeStruct((batch, 1, D_RG), F32),
            jax.ShapeDtypeStruct((batch, CONV_W - 1, D_RG), F32),
        ],
        scratch_shapes=[
            pltpu.VMEM((RG_TL + 8, D_RG), F32),
            pltpu.VMEM((RG_TL, D_RG), F32),
            pltpu.VMEM((RG_TL, D_RG), F32),
            pltpu.VMEM((1, D_RG), F32),
        ],
        compiler_params=_cparams(("arbitrary", "arbitrary")),
        name="rglru_prompt",
    )(proj, proj, conv_w, conv_b, wa, ba, wi, bi, lam)


def _group_rmsnorm(y, g_row):
    outs = []
    for g in range(SSD_GROUPS):
        yg = y[:, g * D_BC:(g + 1) * D_BC]
        outs.append(yg * lax.rsqrt(jnp.mean(yg * yg, axis=-1, keepdims=True) + EPS))
    return jnp.concatenate(outs, axis=1) * g_row


def _ssd_prompt_body(xbc_ref, z_ref, dt_ref, cw_ref, cb_ref, dtb_ref, alog_ref, dexp_ref, ng_ref,
                     y_ref, s_out_ref, cs_ref, ext_ref, s_ref):
    c = pl.program_id(1)
    L = SSD_CHUNK

    @pl.when(c == 0)
    def _():
        ext_ref[0:8, :] = jnp.zeros((8, D_XBC), F32)
        s_ref[...] = jnp.zeros_like(s_ref)

    ext_ref[8:8 + L, :] = xbc_ref[...]
    xc = cb_ref[...] + ext_ref[5:5 + L, :] * cw_ref[0:1, :]
    xc = xc + ext_ref[6:6 + L, :] * cw_ref[1:2, :]
    xc = xc + ext_ref[7:7 + L, :] * cw_ref[2:3, :]
    xc = xc + ext_ref[8:8 + L, :] * cw_ref[3:4, :]
    tail = ext_ref[L:L + 8, :]
    ext_ref[0:8, :] = tail
    xc = _silu(xc)
    xs = xc[:, :D_SSD]
    bm = xc[:, D_SSD:D_SSD + D_BC].astype(BF16)
    cm = xc[:, D_SSD + D_BC:].astype(BF16)

    lane = lax.broadcasted_iota(I32, (1, LANES), 1)
    a_neg = jnp.where(lane < SSD_HEADS, -jnp.exp(alog_ref[...]), 0.0)
    dt = _softplus(dt_ref[...] + dtb_ref[...])
    da = dt * a_neg
    ri = lax.broadcasted_iota(I32, (L, L), 0)
    ci = lax.broadcasted_iota(I32, (L, L), 1)
    causal = ci <= ri
    tri = jnp.where(causal, 1.0, 0.0).astype(F32)
    acum = jnp.dot(tri, da, precision=HIGHEST, preferred_element_type=F32)
    acum_t = acum.T
    a_last = acum[L - 1:L, :]

    e_b16 = _head_expand_matrix(BF16)
    stacked = jnp.concatenate([dt, jnp.exp(a_last - acum), jnp.exp(acum)], axis=0)
    expd = _expand_heads(stacked, e_b16)
    dt_e, dend_e, ea_e = expd[0:L], expd[L:2 * L], expd[2 * L:3 * L]
    xdt = xs * dt_e
    xdt_b = xdt.astype(BF16)
    w_b = (xdt * dend_e)

    lane_l = lax.broadcasted_iota(I32, (L, LANES), 1)
    lo = lane_l < SSD_HEAD_DIM
    y_parts = []
    for g in range(SSD_GROUPS):
        cg = cm[:, g * SSD_STATE:(g + 1) * SSD_STATE]
        bg = bm[:, g * SSD_STATE:(g + 1) * SSD_STATE]
        cb = lax.dot_general(cg, bg, (((1,), (1,)), ((), ())), preferred_element_type=F32)
        hpg = SSD_HEADS // SSD_GROUPS
        for jp in range(hpg // 2):
            ms = []
            for h in (g * hpg + 2 * jp, g * hpg + 2 * jp + 1):
                seg = acum[:, h:h + 1] - acum_t[h:h + 1, :]
                decay = jnp.exp(jnp.where(causal, seg, -jnp.inf))
                ms.append((cb * decay).astype(BF16))
            col = (g * hpg + 2 * jp) * SSD_HEAD_DIM
            xp = xdt_b[:, col:col + LANES]
            zero = jnp.zeros_like(xp)
            rhs = jnp.concatenate([jnp.where(lo, xp, zero), jnp.where(lo, zero, xp)], axis=0)
            y_parts.append(jnp.dot(jnp.concatenate(ms, axis=1), rhs, preferred_element_type=F32))
    y_diag = jnp.concatenate(y_parts, axis=1)

    y_off_parts = []
    for g in range(SSD_GROUPS):
        cg = cm[:, g * SSD_STATE:(g + 1) * SSD_STATE]
        sg = s_ref[g * D_BC:(g + 1) * D_BC, :].astype(BF16)
        y_off_parts.append(lax.dot_general(cg, sg, (((1,), (1,)), ((), ())), preferred_element_type=F32))
    y_off = jnp.concatenate(y_off_parts, axis=1) * ea_e

    cd_col = jnp.exp(acum_t[:, L - 1:L])
    for g in range(SSD_GROUPS):
        bg = bm[:, g * SSD_STATE:(g + 1) * SSD_STATE]
        wg_t = w_b[:, g * D_BC:(g + 1) * D_BC].T.astype(BF16)
        upd = jnp.dot(wg_t, bg, preferred_element_type=F32)
        hpg = SSD_HEADS // SSD_GROUPS
        for e in range(hpg):
            h = g * hpg + e
            r0 = h * SSD_HEAD_DIM
            cd = jnp.broadcast_to(cd_col[h:h + 1, :], (SSD_HEAD_DIM, SSD_STATE))
            s_ref[r0:r0 + SSD_HEAD_DIM, :] = (cd * s_ref[r0:r0 + SSD_HEAD_DIM, :]
                                              + upd[e * SSD_HEAD_DIM:(e + 1) * SSD_HEAD_DIM, :])

    y = (y_diag + y_off + dexp_ref[...] * xs) * _silu(z_ref[...])
    y_ref[...] = _group_rmsnorm(y, ng_ref[...]).astype(BF16)

    @pl.when(c == pl.num_programs(1) - 1)
    def _():
        s_out_ref[...] = s_ref[...]
        cs_ref[...] = tail[5:8, :]


def _ssd_prompt(proj, dt_raw, batch, seq, conv_w, conv_b, dt_bias, a_log, d_exp, norm_g):
    nc = seq // SSD_CHUNK
    L = SSD_CHUNK
    hp = SSD_HEADS * SSD_HEAD_DIM
    const = lambda b, c: (0, 0)
    return pl.pallas_call(
        _ssd_prompt_body,
        grid=(batch, nc),
        in_specs=[
            pl.BlockSpec((L, D_XBC), lambda b, c: (b * nc + c, 2)),
            pl.BlockSpec((L, D_SSD), lambda b, c: (b * nc + c, 2)),
            pl.BlockSpec((L, LANES), lambda b, c: (b * nc + c, 0)),
            pl.BlockSpec((CONV_W, D_XBC), const),
            pl.BlockSpec((1, D_XBC), const),
            pl.BlockSpec((1, LANES), const),
            pl.BlockSpec((1, LANES), const),
            pl.BlockSpec((1, D_SSD), const),
            pl.BlockSpec((1, D_SSD), const),
        ],
        out_specs=[
            pl.BlockSpec((L, D_SSD), lambda b, c: (b * nc + c, 0)),
            pl.BlockSpec((None, hp, SSD_STATE), lambda b, c: (b, 0, 0)),
            pl.BlockSpec((None, CONV_W - 1, D_XBC), lambda b, c: (b, 0, 0)),
        ],
        out_shape=[
            jax.ShapeDtypeStruct((batch * seq, D_SSD), BF16),
            jax.ShapeDtypeStruct((batch, hp, SSD_STATE), F32),
            jax.ShapeDtypeStruct((batch, CONV_W - 1, D_XBC), F32),
        ],
        scratch_shapes=[
            pltpu.VMEM((L + 8, D_XBC), F32),
            pltpu.VMEM((hp, SSD_STATE), F32),
        ],
        compiler_params=_cparams(("arbitrary", "arbitrary")),
        name="ssd_prompt",
    )(proj, proj, dt_raw, conv_w, conv_b, dt_bias, a_log, d_exp, norm_g)


def _mix_sample_body(p_ref, dt_ref, h0_ref, rgc0_ref, s0_ref, sc0_ref,
                     rcw_ref, rcb_ref, wa_ref, ba_ref, wi_ref, bi_ref, lam_ref,
                     scw_ref, scb_ref, dtb_ref, alog_ref, dexp_ref, ng_ref,
                     y_ref, h_ref, rgc_ref, s_ref, sc_ref):
    nl = p_ref.shape[1] // D_MAIN
    bb = p_ref.shape[0]
    o_gate, o_z, o_xbc = D_RG, 2 * D_RG, 2 * D_RG + D_SSD

    def col(l, off, width):
        return p_ref[:, l * D_MAIN + off:l * D_MAIN + off + width]

    hist = [rgc0_ref[:, k * D_RG:(k + 1) * D_RG] for k in range(CONV_W - 1)]
    raw = hist + [col(l, 0, D_RG) for l in range(nl)]
    xcs = []
    for l in range(nl):
        acc = rcb_ref[...] + raw[l] * rcw_ref[0:1, :]
        for k in range(1, CONV_W):
            acc = acc + raw[l + k] * rcw_ref[k:k + 1, :]
        xcs.append(acc)
    a, u = _rg_gates(jnp.concatenate(xcs, axis=0), wa_ref, ba_ref, wi_ref, bi_ref, lam_ref)
    h = h0_ref[...]
    for l in range(nl):
        h = a[l * bb:(l + 1) * bb] * h + u[l * bb:(l + 1) * bb]
        y_ref[:, l * 2 * D_MODEL:l * 2 * D_MODEL + D_RG] = h * _gelu_tanh(col(l, o_gate, D_RG))
    h_ref[...] = h
    for k in range(CONV_W - 1):
        rgc_ref[:, k * D_RG:(k + 1) * D_RG] = raw[nl + k]

    hist = [sc0_ref[:, k * D_XBC:(k + 1) * D_XBC] for k in range(CONV_W - 1)]
    raw = hist + [col(l, o_xbc, D_XBC) for l in range(nl)]
    for k in range(CONV_W - 1):
        sc_ref[:, k * D_XBC:(k + 1) * D_XBC] = raw[nl + k]
    xcs = []
    for l in range(nl):
        acc = scb_ref[...] + raw[l] * scw_ref[0:1, :]
        for k in range(1, CONV_W):
            acc = acc + raw[l + k] * scw_ref[k:k + 1, :]
        xcs.append(_silu(acc))
    xs = [x[:, :D_SSD] for x in xcs]
    bms = [x[:, D_SSD:D_SSD + D_BC].astype(BF16) for x in xcs]
    cms = [x[:, D_SSD + D_BC:].astype(BF16) for x in xcs]

    lane = lax.broadcasted_iota(I32, (1, LANES), 1)
    a_neg = jnp.where(lane < SSD_HEADS, -jnp.exp(alog_ref[...]), 0.0)
    dts = [_softplus(dt_ref[:, l * LANES:(l + 1) * LANES] + dtb_ref[...]) for l in range(nl)]
    acums = []
    run = jnp.zeros((bb, LANES), F32)
    for l in range(nl):
        run = run + dts[l] * a_neg
        acums.append(run)
    a_last = acums[nl - 1]

    e_b16 = _head_expand_matrix(BF16)
    stacked = jnp.concatenate(dts + [jnp.exp(a_last - ac) for ac in acums] + [jnp.exp(ac) for ac in acums], axis=0)
    expd = _expand_heads(stacked, e_b16)
    dt_e = [expd[l * bb:(l + 1) * bb] for l in range(nl)]
    dend_e = [expd[(nl + l) * bb:(nl + l + 1) * bb] for l in range(nl)]
    ea_e = [expd[(2 * nl + l) * bb:(2 * nl + l + 1) * bb] for l in range(nl)]
    xdt = [xs[l] * dt_e[l] for l in range(nl)]
    xdt_r = [x.astype(BF16).astype(F32) for x in xdt]
    rows = nl * bb
    pad_rows = LANES - rows
    w_all = jnp.concatenate([(xdt[l] * dend_e[l]) for l in range(nl)]
                            + [jnp.zeros((pad_rows, D_SSD), F32)], axis=0)
    c_all = jnp.concatenate(cms, axis=0)
    b_all = jnp.concatenate(bms + [jnp.zeros((pad_rows, D_BC), BF16)], axis=0)

    r = lax.broadcasted_iota(I32, (D_BC, LANES), 0)
    cidx = lax.broadcasted_iota(I32, (D_BC, LANES), 1)
    gsum = jnp.where((lax.shift_right_logical(r, 7) == lax.shift_right_logical(cidx, 3)) & (cidx < SSD_HEADS),
                     1.0, 0.0).astype(F32)
    pairs = [(l, s) for l in range(nl) for s in range(l + 1)]
    prods = jnp.concatenate([cms[l].astype(F32) * bms[s].astype(F32) for (l, s) in pairs], axis=0)
    cbh = jnp.dot(prods, gsum, precision=HIGHEST, preferred_element_type=F32)
    m_list = []
    for idx, (l, s) in enumerate(pairs):
        decay = jnp.exp(acums[l] - acums[s])
        m_list.append((cbh[idx * bb:(idx + 1) * bb] * decay).astype(BF16))
    m_e = jnp.dot(jnp.concatenate(m_list, axis=0), e_b16, preferred_element_type=F32)
    y_diag = []
    for l in range(nl):
        acc = None
        for idx, (ll, s) in enumerate(pairs):
            if ll != l:
                continue
            term = m_e[idx * bb:(idx + 1) * bb] * xdt_r[s]
            acc = term if acc is None else acc + term
        y_diag.append(acc)

    cd_t = jnp.concatenate([jnp.exp(a_last), jnp.zeros((LANES - bb, LANES), F32)], axis=0).T
    row_seq = lax.broadcasted_iota(I32, (rows, 1), 0) & (bb - 1)
    row_pad = lax.broadcasted_iota(I32, (LANES, 1), 0)
    lane_q = lax.broadcasted_iota(I32, (LANES, LANES), 1)
    hpg = SSD_HEADS // SSD_GROUPS

    def seq_step(q, y_off):
        mine = row_seq == q
        mine_pad = ((row_pad & (bb - 1)) == q) & (row_pad < rows)
        cd_q = jnp.sum(jnp.where(lane_q == q, cd_t, 0.0), axis=1, keepdims=True)
        cd_q = jnp.broadcast_to(cd_q, (LANES, SSD_STATE))
        parts = []
        for g in range(SSD_GROUPS):
            s0 = s0_ref[q, g * D_BC:(g + 1) * D_BC, :]
            cg = c_all[:, g * SSD_STATE:(g + 1) * SSD_STATE]
            yq = lax.dot_general(cg, s0.astype(BF16), (((1,), (1,)), ((), ())), preferred_element_type=F32)
            parts.append(yq)
            bg = b_all[:, g * SSD_STATE:(g + 1) * SSD_STATE]
            wq = jnp.where(mine_pad, w_all[:, g * D_BC:(g + 1) * D_BC], 0.0)
            upd = jnp.dot(wq.T.astype(BF16), bg, preferred_element_type=F32)
            for e in range(hpg):
                hh = g * hpg + e
                cd = jnp.broadcast_to(cd_q[hh:hh + 1, :], (SSD_HEAD_DIM, SSD_STATE))
                s_ref[q, hh * SSD_HEAD_DIM:(hh + 1) * SSD_HEAD_DIM, :] = (
                    cd * s0[e * SSD_HEAD_DIM:(e + 1) * SSD_HEAD_DIM, :]
                    + upd[e * SSD_HEAD_DIM:(e + 1) * SSD_HEAD_DIM, :])
        yq_all = jnp.concatenate(parts, axis=1)
        return jnp.where(mine, yq_all, y_off)

    y_off = lax.fori_loop(0, bb, seq_step, jnp.zeros((rows, D_SSD), F32))

    for l in range(nl):
        y = (y_diag[l] + y_off[l * bb:(l + 1) * bb] * ea_e[l] + dexp_ref[...] * xs[l]) * _silu(col(l, o_z, D_SSD))
        y_ref[:, l * 2 * D_MODEL + D_RG:(l + 1) * 2 * D_MODEL] = _group_rmsnorm(y, ng_ref[...])


def _mix_sample(proj4, dt4, row0, n_seq, h0, rgc0, s0, sc0, rg_params, ssd_params):
    bb = SAMPLE_BB
    nl = proj4.shape[1] // D_MAIN
    hp = SSD_HEADS * SSD_HEAD_DIM
    blk0 = row0 // bb
    const2 = lambda i: (0, 0)
    rg_specs = [
        pl.BlockSpec((CONV_W, D_RG), const2), pl.BlockSpec((1, D_RG), const2),
        pl.BlockSpec((RG_BLOCKS, RG_BLOCK_W, RG_BLOCK_W), lambda i: (0, 0, 0)), pl.BlockSpec((1, D_RG), const2),
        pl.BlockSpec((RG_BLOCKS, RG_BLOCK_W, RG_BLOCK_W), lambda i: (0, 0, 0)), pl.BlockSpec((1, D_RG), const2),
        pl.BlockSpec((1, D_RG), const2),
    ]
    ssd_specs = [
        pl.BlockSpec((CONV_W, D_XBC), const2), pl.BlockSpec((1, D_XBC), const2),
        pl.BlockSpec((1, LANES), const2), pl.BlockSpec((1, LANES), const2),
        pl.BlockSpec((1, D_SSD), const2), pl.BlockSpec((1, D_SSD), const2),
    ]
    return pl.pallas_call(
        _mix_sample_body,
        grid=(n_seq // bb,),
        in_specs=[
            pl.BlockSpec((bb, nl * D_MAIN), lambda i: (blk0 + i, 0)),
            pl.BlockSpec((bb, nl * LANES), lambda i: (blk0 + i, 0)),
            pl.BlockSpec((bb, D_RG), lambda i: (i, 0)),
            pl.BlockSpec((bb, (CONV_W - 1) * D_RG), lambda i: (i, 0)),
            pl.BlockSpec((bb, hp, SSD_STATE), lambda i: (i, 0, 0)),
            pl.BlockSpec((bb, (CONV_W - 1) * D_XBC), lambda i: (i, 0)),
        ] + rg_specs + ssd_specs,
        out_specs=[
            pl.BlockSpec((bb, nl * 2 * D_MODEL), lambda i: (i, 0)),
            pl.BlockSpec((bb, D_RG), lambda i: (i, 0)),
            pl.BlockSpec((bb, (CONV_W - 1) * D_RG), lambda i: (i, 0)),
            pl.BlockSpec((bb, hp, SSD_STATE), lambda i: (i, 0, 0)),
            pl.BlockSpec((bb, (CONV_W - 1) * D_XBC), lambda i: (i, 0)),
        ],
        out_shape=[
            jax.ShapeDtypeStruct((n_seq, nl * 2 * D_MODEL), F32),
            jax.ShapeDtypeStruct((n_seq, D_RG), F32),
            jax.ShapeDtypeStruct((n_seq, (CONV_W - 1) * D_RG), F32),
            jax.ShapeDtypeStruct((n_seq, hp, SSD_STATE), F32),
            jax.ShapeDtypeStruct((n_seq, (CONV_W - 1) * D_XBC), F32),
        ],
        compiler_params=_cparams(("arbitrary",)),
        name="mix_sample",
    )(proj4, dt4, h0, rgc0, s0, sc0, *rg_params, *ssd_params)


def _route_body(x_ref, g_ref, wr_ref, br_ref, hp_ref, e_ref, gate_ref, rank_ref, cnt_ref, run_ref):
    i = pl.program_id(0)
    tm = x_ref.shape[0]

    @pl.when(i == 0)
    def _():
        run_ref[...] = jnp.zeros_like(run_ref)

    x = x_ref[...]
    h = x * lax.rsqrt(jnp.mean(x * x, axis=-1, keepdims=True) + EPS) * g_ref[...]
    hp_ref[...] = h

    lane = lax.broadcasted_iota(I32, (tm, LANES), 1).astype(F32)
    logits = jnp.dot(h, wr_ref[...], precision=HIGHEST, preferred_element_type=F32) + br_ref[...]
    work = jnp.where(lane < N_EXPERTS, logits, -jnp.inf)
    vals, idxs = [], []
    multi = jnp.zeros((tm, LANES), F32)
    for _ in range(TOP_K):
        m = jnp.max(work, axis=-1, keepdims=True)
        idx = jnp.min(jnp.where(work == m, lane, float(LANES)), axis=-1, keepdims=True)
        hit = lane == idx
        vals.append(m)
        idxs.append(idx)
        multi = jnp.where(hit, 1.0, multi)
        work = jnp.where(hit, -jnp.inf, work)
    ex = [jnp.exp(v - vals[0]) for v in vals]
    den = ex[0] + ex[1] + ex[2] + ex[3]
    gates = [e / den for e in ex]

    ri = lax.broadcasted_iota(I32, (tm, tm), 0)
    ci = lax.broadcasted_iota(I32, (tm, tm), 1)
    strict = jnp.where(ci < ri, 1.0, 0.0).astype(BF16)
    before = jnp.dot(strict, multi.astype(BF16), preferred_element_type=F32) + run_ref[...]
    ranks = [jnp.sum(jnp.where(lane == idx, before, 0.0), axis=-1, keepdims=True) for idx in idxs]
    run_ref[...] = run_ref[...] + jnp.sum(multi, axis=0, keepdims=True)

    lane4 = lax.broadcasted_iota(I32, (tm, TOP_K), 1)
    e_out = jnp.zeros((tm, TOP_K), I32)
    g_out = jnp.zeros((tm, TOP_K), F32)
    r_out = jnp.zeros((tm, TOP_K), I32)
    for k in range(TOP_K):
        e_out = jnp.where(lane4 == k, idxs[k].astype(I32), e_out)
        g_out = jnp.where(lane4 == k, gates[k], g_out)
        r_out = jnp.where(lane4 == k, ranks[k].astype(I32), r_out)
    e_ref[...] = e_out
    gate_ref[...] = g_out
    rank_ref[...] = r_out
    cnt_ref[...] = run_ref[...].astype(I32)


def _route(x1, g_ffn, w_router_pad, b_router_pad):
    t = x1.shape[0]
    tm = ROUTE_TM
    const = lambda i: (0, 0)
    return pl.pallas_call(
        _route_body,
        grid=(t // tm,),
        in_specs=[
            pl.BlockSpec((tm, D_MODEL), lambda i: (i, 0)),
            pl.BlockSpec((1, D_MODEL), const),
            pl.BlockSpec((D_MODEL, LANES), const),
            pl.BlockSpec((1, LANES), const),
        ],
        out_specs=[
            pl.BlockSpec((tm, D_MODEL), lambda i: (i, 0)),
            pl.BlockSpec((tm, TOP_K), lambda i: (i, 0)),
            pl.BlockSpec((tm, TOP_K), lambda i: (i, 0)),
            pl.BlockSpec((tm, TOP_K), lambda i: (i, 0)),
            pl.BlockSpec((1, LANES), const),
        ],
        out_shape=[
            jax.ShapeDtypeStruct((t, D_MODEL), F32),
            jax.ShapeDtypeStruct((t, TOP_K), I32),
            jax.ShapeDtypeStruct((t, TOP_K), F32),
            jax.ShapeDtypeStruct((t, TOP_K), I32),
            jax.ShapeDtypeStruct((1, LANES), I32),
        ],
        scratch_shapes=[pltpu.VMEM((1, LANES), F32)],
        compiler_params=_cparams(("arbitrary",)),
        name="route",
    )(x1, g_ffn, w_router_pad, b_router_pad)


def _scatter_body(seg_ref, cnt_ref, e_ref, rank_ref, h_ref, xs_ref, zero_ref, sem, zsem):
    i = pl.program_id(0)
    tm = h_ref.shape[0]
    n = e_ref.shape[0]

    def issue(blk, c):
        for u in range(DMA_UNROLL):
            a = blk * DMA_UNROLL + u
            pos = seg_ref[e_ref[a]] + rank_ref[a]
            t = lax.shift_right_logical(a, 2)
            pltpu.make_async_copy(h_ref.at[pl.ds(t, 1), :], xs_ref.at[pl.ds(pos, 1), :], sem).start()
        return c

    lax.fori_loop(0, n // DMA_UNROLL, issue, 0)
    for _ in range(TOP_K):
        pltpu.make_async_copy(h_ref, xs_ref.at[pl.ds(0, tm), :], sem).wait()

    @pl.when(i == pl.num_programs(0) - 1)
    def _():
        zero_ref[...] = jnp.zeros_like(zero_ref)
        pad = EXPERT_PAD

        def row_fill(row):
            return pltpu.make_async_copy(zero_ref.at[pl.ds(0, 1), :], xs_ref.at[pl.ds(row, 1), :], zsem)

        def block_fill(blk):
            dst = pl.multiple_of(blk * pad, pad)
            return pltpu.make_async_copy(zero_ref, xs_ref.at[pl.ds(dst, pad), :], zsem)

        def for_each(lo, hi, fn):
            def step(r, c):
                fn(r)
                return c
            lax.fori_loop(lo, hi, step, 0)

        def per_expert(e, used):
            cnt = cnt_ref[e]
            hi = ((cnt + pad - 1) // pad) * pad
            base = seg_ref[e]
            for_each(cnt, hi, lambda r: row_fill(base + r).start())
            for_each(cnt, hi, lambda r: row_fill(base + r).wait())
            return used + hi // pad

        used = lax.fori_loop(0, N_EXPERTS, per_expert, jnp.int32(0))
        n_blocks = xs_ref.shape[0] // pad
        for_each(used, n_blocks, lambda blk: block_fill(blk).start())
        for_each(used, n_blocks, lambda blk: block_fill(blk).wait())


def _scatter(seg_start, counts, e_flat, rank_flat, h2, n_slots):
    t = h2.shape[0]
    tm = SCATTER_TM
    grid_spec = pltpu.PrefetchScalarGridSpec(
        num_scalar_prefetch=2,
        grid=(t // tm,),
        in_specs=[
            pl.BlockSpec((tm * TOP_K,), lambda i, seg, cnt: (i,), memory_space=pltpu.SMEM),
            pl.BlockSpec((tm * TOP_K,), lambda i, seg, cnt: (i,), memory_space=pltpu.SMEM),
            pl.BlockSpec((tm, D_MODEL), lambda i, seg, cnt: (i, 0)),
        ],
        out_specs=pl.BlockSpec(memory_space=pl.ANY),
        scratch_shapes=[pltpu.VMEM((EXPERT_PAD, D_MODEL), F32), pltpu.SemaphoreType.DMA(()),
                        pltpu.SemaphoreType.DMA(())],
    )
    return pl.pallas_call(
        _scatter_body,
        grid_spec=grid_spec,
        out_shape=jax.ShapeDtypeStruct((n_slots, D_MODEL), F32),
        compiler_params=_cparams(("arbitrary",)),
        name="scatter_rows",
    )(seg_start, counts, e_flat, rank_flat, h2)


def _experts_body(sbe_ref, sbs_ref, sbn_ref, used_ref, xs_ref, wg_ref, wu_ref, bg_ref, bu_ref, wd_ref, bd_ref,
                  ys_ref, xbuf_ref, acc_ref, wgu_ref, wdb_ref, in_sem, out_sem):
    del sbe_ref
    sb = pl.program_id(0)
    j = pl.program_id(1)
    nj = pl.num_programs(1)
    nblk = sbn_ref[sb]
    start = sbs_ref[sb]
    cb = EXPERT_PAD
    bm = EXPERT_BM
    tf = wd_ref.shape[0]

    def in_copy(r):
        src = pl.multiple_of(start + r * cb, cb)
        dst = pl.multiple_of(r * cb, cb)
        return pltpu.make_async_copy(xs_ref.at[pl.ds(src, cb), :], acc_ref.at[pl.ds(dst, cb), :], in_sem.at[r])

    def out_copy(r):
        src = pl.multiple_of(r * cb, cb)
        dst = pl.multiple_of(start + r * cb, cb)
        return pltpu.make_async_copy(acc_ref.at[pl.ds(src, cb), :], ys_ref.at[pl.ds(dst, cb), :], out_sem)

    def zero_copy(blk):
        dst = pl.multiple_of(blk * cb, cb)
        return pltpu.make_async_copy(acc_ref.at[pl.ds(0, cb), :], ys_ref.at[pl.ds(dst, cb), :], out_sem)

    def for_each(lo, hi, fn):
        def step(r, c):
            fn(r)
            return c
        lax.fori_loop(lo, hi, step, 0)

    @pl.when(j == 0)
    def _():
        for_each(0, nblk, lambda r: in_copy(r).start())

        def stage(r):
            in_copy(r).wait()
            r0 = pl.multiple_of(r * cb, cb)
            xbuf_ref[pl.ds(r0, cb), :] = acc_ref[pl.ds(r0, cb), :].astype(BF16)
            acc_ref[pl.ds(r0, cb), :] = jnp.broadcast_to(bd_ref[...], (cb, D_MODEL))

        for_each(0, nblk, stage)

    @pl.when(nblk > 0)
    def _():
        def cast_rows(s):
            r = pl.multiple_of(s * cb, cb)
            wgu_ref[pl.ds(r, cb), 0:tf] = wg_ref[pl.ds(r, cb), :].astype(BF16)
            wgu_ref[pl.ds(r, cb), tf:2 * tf] = wu_ref[pl.ds(r, cb), :].astype(BF16)

        for_each(0, D_MODEL // cb, cast_rows)
        wdb_ref[...] = wd_ref[...].astype(BF16)
        bias_gu = jnp.concatenate([bg_ref[...], bu_ref[...]], axis=1)

        def sub(r0, rows):
            x = xbuf_ref[pl.ds(r0, rows), :]
            gu = jnp.dot(x, wgu_ref[...], preferred_element_type=F32) + bias_gu
            gate = jnp.minimum(gu[:, :tf], SWIGLU_LIMIT)
            up = jnp.clip(gu[:, tf:], -SWIGLU_LIMIT, SWIGLU_LIMIT)
            glu = gate * _sigmoid(SWIGLU_ALPHA * gate)
            act = ((up + 1.0) * glu).astype(BF16)
            acc_ref[pl.ds(r0, rows), :] += jnp.dot(act, wdb_ref[...], preferred_element_type=F32)

            @pl.when(j == nj - 1)
            def _():
                for b in range(rows // cb):
                    out_copy(r0 // cb + b).start()

        per_sub = bm // cb
        n_full = nblk // per_sub
        for_each(0, n_full, lambda i: sub(pl.multiple_of(i * bm, bm), bm))

        @pl.when(nblk % per_sub == 1)
        def _():
            sub(pl.multiple_of(n_full * bm, bm), cb)

    @pl.when(j == nj - 1)
    def _():
        for_each(0, nblk, lambda r: out_copy(r).wait())

    @pl.when((sb == pl.num_programs(0) - 1) & (j == nj - 1))
    def _():
        acc_ref[0:cb, :] = jnp.zeros((cb, D_MODEL), F32)
        n_blocks = ys_ref.shape[0] // cb
        for_each(used_ref[0], n_blocks, lambda blk: zero_copy(blk).start())
        for_each(used_ref[0], n_blocks, lambda blk: zero_copy(blk).wait())


def _experts(sb_expert, sb_start, sb_nsub, used_blocks, xsorted, w_gu, b_gu, w_down, b_down):
    n_sb = sb_expert.shape[0]
    n_slots = xsorted.shape[0]
    tf = EXPERT_TF
    nj = D_FF // tf

    def spec(shape, fn):
        def index_map(sb, j, sbe, sbs, sbn, used):
            jj = jnp.where(sbn[sb] > 0, j, nj - 1)
            return fn(sbe[sb], jj)
        return pl.BlockSpec(shape, index_map)

    grid_spec = pltpu.PrefetchScalarGridSpec(
        num_scalar_prefetch=4,
        grid=(n_sb, nj),
        in_specs=[
            pl.BlockSpec(memory_space=pl.ANY),
            spec((None, D_MODEL, tf), lambda e, jj: (e, 0, jj)),
            spec((None, D_MODEL, tf), lambda e, jj: (e, 0, nj + jj)),
            spec((None, 1, tf), lambda e, jj: (e, 0, jj)),
            spec((None, 1, tf), lambda e, jj: (e, 0, nj + jj)),
            spec((None, tf, D_MODEL), lambda e, jj: (e, jj, 0)),
            spec((None, 1, D_MODEL), lambda e, jj: (e, 0, 0)),
        ],
        out_specs=pl.BlockSpec(memory_space=pl.ANY),
        scratch_shapes=[
            pltpu.VMEM((EXPERT_R, D_MODEL), BF16),
            pltpu.VMEM((EXPERT_R, D_MODEL), F32),
            pltpu.VMEM((D_MODEL, 2 * tf), BF16),
            pltpu.VMEM((tf, D_MODEL), BF16),
            pltpu.SemaphoreType.DMA((EXPERT_R // EXPERT_PAD,)),
            pltpu.SemaphoreType.DMA(()),
        ],
    )
    return pl.pallas_call(
        _experts_body,
        grid_spec=grid_spec,
        out_shape=jax.ShapeDtypeStruct((n_slots, D_MODEL), F32),
        compiler_params=_cparams(("arbitrary", "arbitrary")),
        name="experts",
    )(sb_expert, sb_start, sb_nsub, used_blocks, xsorted, w_gu, w_gu, b_gu, b_gu, w_down, b_down)


def _combine_body(seg_ref, e_ref, rank_ref, x1_ref, gate_ref, gf_ref, ys_ref, yp_ref, ysm_ref, buf_ref, sem,
                  *, n_prompt_tiles):
    i = pl.program_id(0)
    tm = x1_ref.shape[0]
    n = e_ref.shape[0]

    def issue(blk, c):
        for u in range(DMA_UNROLL):
            a = blk * DMA_UNROLL + u
            pos = seg_ref[e_ref[a]] + rank_ref[a]
            t = lax.shift_right_logical(a, 2)
            pltpu.make_async_copy(ys_ref.at[pl.ds(pos, 1), :], buf_ref.at[u % TOP_K, pl.ds(t, 1), :], sem).start()
        return c

    lax.fori_loop(0, n // DMA_UNROLL, issue, 0)
    for k in range(TOP_K):
        pltpu.make_async_copy(ys_ref.at[pl.ds(0, tm), :], buf_ref.at[k], sem).wait()

    g = gate_ref[...]
    acc = buf_ref[0] * g[:, 0:1]
    for k in range(1, TOP_K):
        acc = acc + buf_ref[k] * g[:, k:k + 1]
    x = x1_ref[...] + acc
    y = x * lax.rsqrt(jnp.mean(x * x, axis=-1, keepdims=True) + EPS) * gf_ref[...]

    @pl.when(i < n_prompt_tiles)
    def _():
        yp_ref[...] = y

    @pl.when(i >= n_prompt_tiles)
    def _():
        ysm_ref[...] = y


def _combine(seg_start, e_flat, rank_flat, x1, gates, g_final, ysorted, t_prompt):
    t = x1.shape[0]
    tm = COMBINE_TM
    npt = t_prompt // tm
    grid_spec = pltpu.PrefetchScalarGridSpec(
        num_scalar_prefetch=1,
        grid=(t // tm,),
        in_specs=[
            pl.BlockSpec((tm * TOP_K,), lambda i, seg: (i,), memory_space=pltpu.SMEM),
            pl.BlockSpec((tm * TOP_K,), lambda i, seg: (i,), memory_space=pltpu.SMEM),
            pl.BlockSpec((tm, D_MODEL), lambda i, seg: (i, 0)),
            pl.BlockSpec((tm, TOP_K), lambda i, seg: (i, 0)),
            pl.BlockSpec((1, D_MODEL), lambda i, seg: (0, 0)),
            pl.BlockSpec(memory_space=pl.ANY),
        ],
        out_specs=[
            pl.BlockSpec((tm, D_MODEL), lambda i, seg: (jnp.minimum(i, npt - 1), 0)),
            pl.BlockSpec((tm, D_MODEL), lambda i, seg: (jnp.maximum(i - npt, 0), 0)),
        ],
        scratch_shapes=[pltpu.VMEM((TOP_K, tm, D_MODEL), F32), pltpu.SemaphoreType.DMA(())],
    )
    return pl.pallas_call(
        functools.partial(_combine_body, n_prompt_tiles=npt),
        grid_spec=grid_spec,
        out_shape=[
            jax.ShapeDtypeStruct((t_prompt, D_MODEL), F32),
            jax.ShapeDtypeStruct((t - t_prompt, D_MODEL), F32),
        ],
        compiler_params=_cparams(("arbitrary",)),
        name="combine",
    )(seg_start, e_flat, rank_flat, x1, gates, g_final, ysorted)


def _expert_tables(counts, n_assign):
    pad = EXPERT_PAD
    padded = ((counts + pad - 1) // pad) * pad
    seg_start = jnp.cumsum(padded) - padded
    n_sb_e = (counts + EXPERT_R - 1) // EXPERT_R
    sb_cum = jnp.cumsum(n_sb_e)
    n_sb = n_assign // EXPERT_R + N_EXPERTS
    s = jnp.arange(n_sb, dtype=I32)
    total = sb_cum[-1]
    s_eff = jnp.minimum(s, total - 1)
    e = jnp.sum((sb_cum[None, :] <= s_eff[:, None]).astype(I32), axis=1)
    k = s_eff - (sb_cum[e] - n_sb_e[e])
    rows = jnp.clip(counts[e] - k * EXPERT_R, 0, EXPERT_R)
    valid = s < total
    nsub = jnp.where(valid, (rows + pad - 1) // pad, 0)
    start = seg_start[e] + k * EXPERT_R
    used_blocks = (jnp.sum(padded) // pad).reshape(1)
    return seg_start.astype(I32), e, start.astype(I32), nsub.astype(I32), used_blocks.astype(I32)


def kernel(x_prompt, x_sample, state_rglru_h, state_rglru_conv, state_ssd, state_ssd_conv, g_mix, w_in, conv_rg_w, conv_rg_b, rg_wa, rg_ba, rg_wi, rg_bi, rg_lambda, conv_ssd_w, conv_ssd_b, ssd_dt_bias, ssd_a_log, ssd_d, ssd_norm_g, w_out, g_ffn, w_router, b_router, w_gu, b_gu, w_down, b_down, g_final):
    depth = g_mix.shape[0]
    assert depth == 1
    bp, lp, _ = x_prompt.shape
    bs, ls, _ = x_sample.shape
    tp, ts = bp * lp, bs * ls
    t = tp + ts
    hp = SSD_HEADS * SSD_HEAD_DIM
    l = 0

    xp = x_prompt.reshape(tp, D_MODEL)
    xs = x_sample.reshape(ts, D_MODEL)
    w_in_l = w_in[l]
    row = lambda v: v.reshape(1, -1)
    pad_heads = lambda v: jnp.pad(v.reshape(1, -1), ((0, 0), (0, LANES - SSD_HEADS)))

    hb, dt_raw = _norm_dt(xp, xs, row(g_mix[l]), w_in_l)
    proj = _in_proj(hb, w_in_l)

    rg_params = (conv_rg_w[l], row(conv_rg_b[l]), rg_wa[l], row(rg_ba[l]), rg_wi[l], row(rg_bi[l]),
                 row(rg_lambda[l]))
    d_exp = row(jnp.repeat(ssd_d[l], SSD_HEAD_DIM))
    ssd_params = (conv_ssd_w[l], row(conv_ssd_b[l]), pad_heads(ssd_dt_bias[l]), pad_heads(ssd_a_log[l]),
                  d_exp, row(ssd_norm_g[l]))

    y_rg, rgh_p, rgc_p = _rglru_prompt(proj, bp, lp, *rg_params)
    y_ssd, ssd_p, ssdc_p = _ssd_prompt(proj, dt_raw, bp, lp, *ssd_params)

    proj4 = proj[tp:].reshape(bs, ls * D_MAIN)
    dt4 = dt_raw[tp:].reshape(bs, ls * LANES)
    mix_s4, rgh_s, rgc_s, ssd_s, ssdc_s = _mix_sample(
        proj4, dt4, 0, bs,
        state_rglru_h[l], state_rglru_conv[l].reshape(bs, -1),
        state_ssd[l].reshape(bs, hp, SSD_STATE), state_ssd_conv[l].reshape(bs, -1),
        rg_params, ssd_params)
    mix_s = mix_s4.reshape(ts, 2 * D_MODEL)

    x1 = _out_proj(y_rg, y_ssd, mix_s, w_out[l], xp, xs)

    wr = jnp.pad(w_router[l], ((0, 0), (0, LANES - N_EXPERTS)))
    br = jnp.pad(b_router[l].reshape(1, -1), ((0, 0), (0, LANES - N_EXPERTS)))
    h2, e_idx, gates, rank, counts = _route(x1, row(g_ffn[l]), wr, br)

    n_assign = t * TOP_K
    n_slots = n_assign + N_EXPERTS * EXPERT_PAD
    counts = counts[0, :N_EXPERTS]
    seg_start, sb_expert, sb_start, sb_nsub, used_blocks = _expert_tables(counts, n_assign)
    e_flat = e_idx.reshape(-1)
    rank_flat = rank.reshape(-1)
    xsorted = _scatter(seg_start, counts, e_flat, rank_flat, h2, n_slots)
    ysorted = _experts(sb_expert, sb_start, sb_nsub, used_blocks, xsorted, w_gu[l], b_gu[l].reshape(N_EXPERTS, 1, -1),
                       w_down[l], b_down[l].reshape(N_EXPERTS, 1, -1))
    y_p, y_s = _combine(seg_start, e_flat, rank_flat, x1, gates, row(g_final), ysorted, tp)

    return (y_p.reshape(x_prompt.shape).astype(x_prompt.dtype),
            y_s.reshape(x_sample.shape).astype(x_sample.dtype),
            rgh_p.reshape(depth, bp, D_RG),
            rgc_p.reshape(depth, bp, CONV_W - 1, D_RG),
            ssd_p.reshape(depth, bp, SSD_HEADS, SSD_HEAD_DIM, SSD_STATE),
            ssdc_p.reshape(depth, bp, CONV_W - 1, D_XBC),
            rgh_s.reshape(depth, bs, D_RG),
            rgc_s.reshape(depth, bs, CONV_W - 1, D_RG),
            ssd_s.reshape(depth, bs, SSD_HEADS, SSD_HEAD_DIM, SSD_STATE),
            ssdc_s.reshape(depth, bs, CONV_W - 1, D_XBC))
```

```python
import functools

import jax
import jax.numpy as jnp
from jax import lax
from jax.experimental import pallas as pl
from jax.experimental.pallas import tpu as pltpu

F32 = jnp.float32
BF16 = jnp.bfloat16
I32 = jnp.int32
HIGHEST = lax.Precision.HIGHEST

EPS = 1e-6
D_MODEL = 2048
D_RG = 2048
RG_BLOCKS = 16
RG_BLOCK_W = 128
RG_C = 8.0
D_SSD = 2048
SSD_HEAD_DIM = 64
SSD_HEADS = 32
SSD_GROUPS = 4
SSD_STATE = 128
SSD_CHUNK = 128
CONV_W = 4
D_BC = SSD_GROUPS * SSD_STATE
D_XBC = D_SSD + 2 * D_BC
D_MAIN = 2 * D_RG + D_SSD + D_XBC
N_EXPERTS = 32
TOP_K = 4
D_FF = 2048
SWIGLU_ALPHA = 1.702
SWIGLU_LIMIT = 7.0

LANES = 128
VMEM_LIMIT = 56 * 1024 * 1024

ROW_TILE = 512
PROJ_TM = 1088
PROJ_TN = 1024
OUT_TN = 512
RG_TL = 256
SAMPLE_BB = 8
ROUTE_TM = 512
SCATTER_TM = 256
COMBINE_TM = 256
EXPERT_PAD = 128
EXPERT_BM = 512
EXPERT_R = 1536
EXPERT_TF = 512
DMA_UNROLL = 8


def _cparams(sem, vmem=VMEM_LIMIT):
    return pltpu.CompilerParams(dimension_semantics=sem, vmem_limit_bytes=vmem)


def _softplus(x):
    return jnp.maximum(x, 0.0) + jnp.log1p(jnp.exp(-jnp.abs(x)))


_sigmoid = jax.nn.sigmoid
_silu = jax.nn.silu


def _gelu_tanh(x):
    return jax.nn.gelu(x, approximate=True)


def _head_expand_matrix(dtype):
    r = lax.broadcasted_iota(I32, (LANES, D_SSD), 0)
    c = lax.broadcasted_iota(I32, (LANES, D_SSD), 1)
    return jnp.where(lax.shift_right_logical(c, 6) == r, 1.0, 0.0).astype(dtype)


def _expand_heads(v, e_b16):
    rows = v.shape[0]
    hi = v.astype(BF16)
    r1 = v - hi.astype(F32)
    mid = r1.astype(BF16)
    lo = (r1 - mid.astype(F32)).astype(BF16)
    o = jnp.dot(jnp.concatenate([hi, mid, lo], axis=0), e_b16, preferred_element_type=F32)
    return (o[0:rows] + o[rows:2 * rows]) + o[2 * rows:3 * rows]


def _norm_dt_body(xp_ref, xs_ref, g_ref, wdt_ref, hb_ref, dt_ref, *, n_prompt_tiles):
    i = pl.program_id(0)

    def run(x_ref):
        x = x_ref[...]
        h = x * lax.rsqrt(jnp.mean(x * x, axis=-1, keepdims=True) + EPS) * g_ref[...]
        hb = h.astype(BF16)
        hb_ref[...] = hb
        r = lax.broadcasted_iota(I32, (LANES, 1), 0)
        wdt = jnp.where(r < SSD_HEADS, wdt_ref[...], 0.0).astype(BF16)
        dt_ref[...] = lax.dot_general(hb, wdt, (((1,), (1,)), ((), ())), preferred_element_type=F32)

    @pl.when(i < n_prompt_tiles)
    def _():
        run(xp_ref)

    @pl.when(i >= n_prompt_tiles)
    def _():
        run(xs_ref)


def _norm_dt(xp, xs, g, w_in_t):
    tp, ts = xp.shape[0], xs.shape[0]
    npt, nst = tp // ROW_TILE, ts // ROW_TILE
    t = tp + ts
    return pl.pallas_call(
        functools.partial(_norm_dt_body, n_prompt_tiles=npt),
        grid=(npt + nst,),
        in_specs=[
            pl.BlockSpec((ROW_TILE, D_MODEL), lambda i: (jnp.minimum(i, npt - 1), 0)),
            pl.BlockSpec((ROW_TILE, D_MODEL), lambda i: (jnp.maximum(i - npt, 0), 0)),
            pl.BlockSpec((1, D_MODEL), lambda i: (0, 0)),
            pl.BlockSpec((LANES, D_MODEL), lambda i: (D_MAIN // LANES, 0)),
        ],
        out_specs=[
            pl.BlockSpec((ROW_TILE, D_MODEL), lambda i: (i, 0)),
            pl.BlockSpec((ROW_TILE, LANES), lambda i: (i, 0)),
        ],
        out_shape=[jax.ShapeDtypeStruct((t, D_MODEL), BF16), jax.ShapeDtypeStruct((t, LANES), F32)],
        compiler_params=_cparams(("parallel",)),
        name="norm_dt",
    )(xp, xs, g, w_in_t)


def _cast_weight_tile(w_ref, wb_ref, rows_per_step=256):
    k = w_ref.shape[0]

    def step(s, c):
        r = pl.multiple_of(s * rows_per_step, rows_per_step)
        wb_ref[pl.ds(r, rows_per_step), :] = w_ref[pl.ds(r, rows_per_step), :].astype(BF16)
        return c

    lax.fori_loop(0, k // rows_per_step, step, 0)


def _in_proj_body(x_ref, w_ref, o_ref, wb_ref):
    @pl.when(pl.program_id(1) == 0)
    def _():
        _cast_weight_tile(w_ref, wb_ref)

    o_ref[...] = lax.dot_general(x_ref[...], wb_ref[...], (((1,), (1,)), ((), ())), preferred_element_type=F32)


def _in_proj(hb, w_in_t):
    t = hb.shape[0]
    return pl.pallas_call(
        _in_proj_body,
        grid=(D_MAIN // PROJ_TN, t // PROJ_TM),
        in_specs=[
            pl.BlockSpec((PROJ_TM, D_MODEL), lambda j, i: (i, 0)),
            pl.BlockSpec((PROJ_TN, D_MODEL), lambda j, i: (j, 0)),
        ],
        out_specs=pl.BlockSpec((PROJ_TM, PROJ_TN), lambda j, i: (i, j)),
        out_shape=jax.ShapeDtypeStruct((t, D_MAIN), F32),
        scratch_shapes=[pltpu.VMEM((PROJ_TN, D_MODEL), BF16)],
        compiler_params=_cparams(("arbitrary", "arbitrary")),
        name="in_proj",
    )(hb, w_in_t)


def _out_proj_body(rg_ref, ssd_ref, ms_ref, w_ref, xp_ref, xs_ref, o_ref, wb_ref, *, n_prompt_tiles):
    i = pl.program_id(1)

    @pl.when(i == 0)
    def _():
        _cast_weight_tile(w_ref, wb_ref)

    @pl.when(i < n_prompt_tiles)
    def _():
        m = jnp.concatenate([rg_ref[...], ssd_ref[...]], axis=1)
        o_ref[...] = xp_ref[...] + jnp.dot(m, wb_ref[...], preferred_element_type=F32)

    @pl.when(i >= n_prompt_tiles)
    def _():
        o_ref[...] = xs_ref[...] + jnp.dot(ms_ref[...].astype(BF16), wb_ref[...], preferred_element_type=F32)


def _out_proj(y_rg, y_ssd, mix_s, w_out, xp, xs):
    tp, ts = y_rg.shape[0], mix_s.shape[0]
    npt, nst = tp // ROW_TILE, ts // ROW_TILE
    prompt_rows = lambda j, i: (jnp.minimum(i, npt - 1), 0)
    return pl.pallas_call(
        functools.partial(_out_proj_body, n_prompt_tiles=npt),
        grid=(D_MODEL // OUT_TN, npt + nst),
        in_specs=[
            pl.BlockSpec((ROW_TILE, D_MODEL), prompt_rows),
            pl.BlockSpec((ROW_TILE, D_MODEL), prompt_rows),
            pl.BlockSpec((ROW_TILE, 2 * D_MODEL), lambda j, i: (jnp.maximum(i - npt, 0), 0)),
            pl.BlockSpec((2 * D_MODEL, OUT_TN), lambda j, i: (0, j)),
            pl.BlockSpec((ROW_TILE, OUT_TN), lambda j, i: (jnp.minimum(i, npt - 1), j)),
            pl.BlockSpec((ROW_TILE, OUT_TN), lambda j, i: (jnp.maximum(i - npt, 0), j)),
        ],
        out_specs=pl.BlockSpec((ROW_TILE, OUT_TN), lambda j, i: (i, j)),
        out_shape=jax.ShapeDtypeStruct((tp + ts, D_MODEL), F32),
        scratch_shapes=[pltpu.VMEM((2 * D_MODEL, OUT_TN), BF16)],
        compiler_params=_cparams(("arbitrary", "arbitrary")),
        name="out_proj",
    )(y_rg, y_ssd, mix_s, w_out, xp, xs)


def _rg_gates(xc, wa_ref, ba_ref, wi_ref, bi_ref, lam_ref):
    xcb = xc.astype(BF16)
    rs, is_ = [], []
    for h in range(RG_BLOCKS):
        xh = xcb[:, h * RG_BLOCK_W:(h + 1) * RG_BLOCK_W]
        rs.append(jnp.dot(xh, wa_ref[h].astype(BF16), preferred_element_type=F32))
        is_.append(jnp.dot(xh, wi_ref[h].astype(BF16), preferred_element_type=F32))
    r = _sigmoid(jnp.concatenate(rs, axis=1) + ba_ref[...])
    ig = _sigmoid(jnp.concatenate(is_, axis=1) + bi_ref[...])
    log_a = (-RG_C * r) * _softplus(-lam_ref[...])
    a = jnp.exp(log_a)
    u = jnp.sqrt(-jnp.tanh(log_a) * (a * a + 1.0)) * (ig * xc)
    return a, u


def _rglru_prompt_body(x_ref, gate_ref, cw_ref, cb_ref, wa_ref, ba_ref, wi_ref, bi_ref, lam_ref,
                       y_ref, h_ref, cs_ref, ext_ref, a_ref, u_ref, car_ref):
    c = pl.program_id(1)
    tl = x_ref.shape[0]

    @pl.when(c == 0)
    def _():
        ext_ref[0:8, :] = jnp.zeros((8, D_RG), F32)
        car_ref[...] = jnp.zeros((1, D_RG), F32)

    ext_ref[8:8 + tl, :] = x_ref[...]
    xc = cb_ref[...] + ext_ref[5:5 + tl, :] * cw_ref[0:1, :]
    xc = xc + ext_ref[6:6 + tl, :] * cw_ref[1:2, :]
    xc = xc + ext_ref[7:7 + tl, :] * cw_ref[2:3, :]
    xc = xc + ext_ref[8:8 + tl, :] * cw_ref[3:4, :]
    tail = ext_ref[tl:tl + 8, :]
    ext_ref[0:8, :] = tail

    a, u = _rg_gates(xc, wa_ref, ba_ref, wi_ref, bi_ref, lam_ref)
    a_ref[...] = a
    u_ref[...] = u

    row = lax.broadcasted_iota(I32, (8, D_RG), 0)

    def group(g, carry):
        r0 = pl.multiple_of(g * 8, 8)
        a8 = a_ref[pl.ds(r0, 8), :]
        u8 = u_ref[pl.ds(r0, 8), :]
        for s in (1, 2, 4):
            keep = row >= s
            a_sh = jnp.where(keep, pltpu.roll(a8, s, 0), 1.0)
            u_sh = jnp.where(keep, pltpu.roll(u8, s, 0), 0.0)
            u8 = a8 * u_sh + u8
            a8 = a8 * a_sh
        h8 = a8 * carry + u8
        u_ref[pl.ds(r0, 8), :] = h8
        return h8[7:8, :]

    carry = lax.fori_loop(0, tl // 8, group, car_ref[...])
    car_ref[...] = carry
    y_ref[...] = (u_ref[...] * _gelu_tanh(gate_ref[...])).astype(BF16)

    @pl.when(c == pl.num_programs(1) - 1)
    def _():
        h_ref[...] = carry
        cs_ref[...] = tail[5:8, :]


def _rglru_prompt(proj, batch, seq, conv_w, conv_b, wa, ba, wi, bi, lam):
    nc = seq // RG_TL
    t = batch * seq
    vec =pl.BlockSpec((1, D_RG), lambda b, c: (0, 0))
    blk = pl.BlockSpec((RG_BLOCKS, RG_BLOCK_W, RG_BLOCK_W), lambda b, c: (0, 0, 0))
    return pl.pallas_call(
        _rglru_prompt_body,
        grid=(batch, nc),
        in_specs=[
            pl.BlockSpec((RG_TL, D_RG), lambda b, c: (b * nc + c, 0)),
            pl.BlockSpec((RG_TL, D_RG), lambda b, c: (b * nc + c, 1)),
            pl.BlockSpec((CONV_W, D_RG), lambda b, c: (0, 0)),
            vec, blk, vec, blk, vec, vec,
        ],
        out_specs=[
            pl.BlockSpec((RG_TL, D_RG), lambda b, c: (b * nc + c, 0)),
            pl.BlockSpec((None, 1, D_RG), lambda b, c: (b, 0, 0)),
            pl.BlockSpec((None, CONV_W - 1, D_RG), lambda b, c: (b, 0, 0)),
        ],
        out_shape=[
            jax.ShapeDtypeStruct((t, D_RG), BF16),
            jax.ShapeDtypeStruct((batch, 1, D_RG), F32),
            jax.ShapeDtypeStruct((batch, CONV_W - 1, D_RG), F32),
        ],
        scratch_shapes=[
            pltpu.VMEM((RG_TL + 8, D_RG), F32),
            pltpu.VMEM((RG_TL, D_RG), F32),
            pltpu.VMEM((RG_TL, D_RG), F32),
            pltpu.VMEM((1, D_RG), F32),
        ],
        compiler_params=_cparams(("arbitrary", "arbitrary")),
        name="rglru_prompt",
    )(proj, proj, conv_w, conv_b, wa, ba, wi, bi, lam)


def _group_rmsnorm(y, g_row):
    outs = []
    for g in range(SSD_GROUPS):
        yg = y[:, g * D_BC:(g + 1) * D_BC]
        outs.append(yg * lax.rsqrt(jnp.mean(yg * yg, axis=-1, keepdims=True) + EPS))
    return jnp.concatenate(outs, axis=1) * g_row


def _ssd_prompt_body(xbc_ref, z_ref, dt_ref, cw_ref, cb_ref, dtb_ref, alog_ref, dexp_ref, ng_ref,
                     y_ref, s_out_ref, cs_ref, ext_ref, s_ref):
    c = pl.program_id(1)
    L = SSD_CHUNK

    @pl.when(c == 0)
    def _():
        ext_ref[0:8, :] = jnp.zeros((8, D_XBC), F32)
        s_ref[...] = jnp.zeros_like(s_ref)

    ext_ref[8:8 + L, :] = xbc_ref[...]
    xc = cb_ref[...] + ext_ref[5:5 + L, :] * cw_ref[0:1, :]
    xc = xc + ext_ref[6:6 + L, :] * cw_ref[1:2, :]
    xc = xc + ext_ref[7:7 + L, :] * cw_ref[2:3, :]
    xc = xc + ext_ref[8:8 + L, :] * cw_ref[3:4, :]
    tail = ext_ref[L:L + 8, :]
    ext_ref[0:8, :] = tail
    xc = _silu(xc)
    xs = xc[:, :D_SSD]
    bm = xc[:, D_SSD:D_SSD + D_BC].astype(BF16)
    cm = xc[:, D_SSD + D_BC:].astype(BF16)

    lane = lax.broadcasted_iota(I32, (1, LANES), 1)
    a_neg = jnp.where(lane < SSD_HEADS, -jnp.exp(alog_ref[...]), 0.0)
    dt = _softplus(dt_ref[...] + dtb_ref[...])
    da = dt * a_neg
    ri = lax.broadcasted_iota(I32, (L, L), 0)
    ci = lax.broadcasted_iota(I32, (L, L), 1)
    causal = ci <= ri
    tri = jnp.where(causal, 1.0, 0.0).astype(F32)
    acum = jnp.dot(tri, da, precision=HIGHEST, preferred_element_type=F32)
    acum_t = acum.T
    a_last = acum[L - 1:L, :]

    e_b16 = _head_expand_matrix(BF16)
    stacked = jnp.concatenate([dt, jnp.exp(a_last - acum), jnp.exp(acum)], axis=0)
    expd = _expand_heads(stacked, e_b16)
    dt_e, dend_e, ea_e = expd[0:L], expd[L:2 * L], expd[2 * L:3 * L]
    xdt = xs * dt_e
    xdt_b = xdt.astype(BF16)
    w_b = (xdt * dend_e)

    lane_l = lax.broadcasted_iota(I32, (L, LANES), 1)
    lo = lane_l < SSD_HEAD_DIM
    y_parts = []
    for g in range(SSD_GROUPS):
        cg = cm[:, g * SSD_STATE:(g + 1) * SSD_STATE]
        bg = bm[:, g * SSD_STATE:(g + 1) * SSD_STATE]
        cb = lax.dot_general(cg, bg, (((1,), (1,)), ((), ())), preferred_element_type=F32)
        hpg = SSD_HEADS // SSD_GROUPS
        for jp in range(hpg // 2):
            ms = []
            for h in (g * hpg + 2 * jp, g * hpg + 2 * jp + 1):
                seg = acum[:, h:h + 1] - acum_t[h:h + 1, :]
                decay = jnp.exp(jnp.where(causal, seg, -jnp.inf))
                ms.append((cb * decay).astype(BF16))
            col = (g * hpg + 2 * jp) * SSD_HEAD_DIM
            xp = xdt_b[:, col:col + LANES]
            zero = jnp.zeros_like(xp)
            rhs = jnp.concatenate([jnp.where(lo, xp, zero), jnp.where(lo, zero, xp)], axis=0)
            y_parts.append(jnp.dot(jnp.concatenate(ms, axis=1), rhs, preferred_element_type=F32))
    y_diag = jnp.concatenate(y_parts, axis=1)

    y_off_parts = []
    for g in range(SSD_GROUPS):
        cg = cm[:, g * SSD_STATE:(g + 1) * SSD_STATE]
        sg = s_ref[g * D_BC:(g + 1) * D_BC, :].astype(BF16)
        y_off_parts.append(lax.dot_general(cg, sg, (((1,), (1,)), ((), ())), preferred_element_type=F32))
    y_off = jnp.concatenate(y_off_parts, axis=1) * ea_e

    cd_col = jnp.exp(acum_t[:, L - 1:L])
    for g in range(SSD_GROUPS):
        bg = bm[:, g * SSD_STATE:(g + 1) * SSD_STATE]
        wg_t = w_b[:, g * D_BC:(g + 1) * D_BC].T.astype(BF16)
        upd = jnp.dot(wg_t, bg, preferred_element_type=F32)
        hpg = SSD_HEADS // SSD_GROUPS
        for e in range(hpg):
            h = g * hpg + e
            r0 = h * SSD_HEAD_DIM
            cd = jnp.broadcast_to(cd_col[h:h + 1, :], (SSD_HEAD_DIM, SSD_STATE))
            s_ref[r0:r0 + SSD_HEAD_DIM, :] = (cd * s_ref[r0:r0 + SSD_HEAD_DIM, :]
                                              + upd[e * SSD_HEAD_DIM:(e + 1) * SSD_HEAD_DIM, :])

    y = (y_diag + y_off + dexp_ref[...] * xs) * _silu(z_ref[...])
    y_ref[...] = _group_rmsnorm(y, ng_ref[...]).astype(BF16)

    @pl.when(c == pl.num_programs(1) - 1)
    def _():
        s_out_ref[...] = s_ref[...]
        cs_ref[...] = tail[5:8, :]


def _ssd_prompt(proj, dt_raw, batch, seq, conv_w, conv_b, dt_bias, a_log, d_exp, norm_g):
    nc = seq // SSD_CHUNK
    L = SSD_CHUNK
    hp = SSD_HEADS * SSD_HEAD_DIM
    const = lambda b, c: (0, 0)
    return pl.pallas_call(
        _ssd_prompt_body,
        grid=(batch, nc),
        in_specs=[
            pl.BlockSpec((L, D_XBC), lambda b, c: (b * nc + c, 2)),
            pl.BlockSpec((L, D_SSD), lambda b, c: (b * nc + c, 2)),
            pl.BlockSpec((L, LANES), lambda b, c: (b * nc + c, 0)),
            pl.BlockSpec((CONV_W, D_XBC), const),
            pl.BlockSpec((1, D_XBC), const),
            pl.BlockSpec((1, LANES), const),
            pl.BlockSpec((1, LANES), const),
            pl.BlockSpec((1, D_SSD), const),
            pl.BlockSpec((1, D_SSD), const),
        ],
        out_specs=[
            pl.BlockSpec((L, D_SSD), lambda b, c: (b * nc + c, 0)),
            pl.BlockSpec((None, hp, SSD_STATE), lambda b, c: (b, 0, 0)),
            pl.BlockSpec((None, CONV_W - 1, D_XBC), lambda b, c: (b, 0, 0)),
        ],
        out_shape=[
            jax.ShapeDtypeStruct((batch * seq, D_SSD), BF16),
            jax.ShapeDtypeStruct((batch, hp, SSD_STATE), F32),
            jax.ShapeDtypeStruct((batch, CONV_W - 1, D_XBC), F32),
        ],
        scratch_shapes=[
            pltpu.VMEM((L + 8, D_XBC), F32),
            pltpu.VMEM((hp, SSD_STATE), F32),
        ],
        compiler_params=_cparams(("arbitrary", "arbitrary")),
        name="ssd_prompt",
    )(proj, proj, dt_raw, conv_w, conv_b, dt_bias, a_log, d_exp, norm_g)


def _mix_sample_body(p_ref, dt_ref, h0_ref, rgc0_ref, s0_ref, sc0_ref,
                     rcw_ref, rcb_ref, wa_ref, ba_ref, wi_ref, bi_ref, lam_ref,
                     scw_ref, scb_ref, dtb_ref, alog_ref, dexp_ref, ng_ref,
                     y_ref, h_ref, rgc_ref, s_ref, sc_ref):
    nl = p_ref.shape[1] // D_MAIN
    bb = p_ref.shape[0]
    o_gate, o_z, o_xbc = D_RG, 2 * D_RG, 2 * D_RG + D_SSD

    def col(l, off, width):
        return p_ref[:, l * D_MAIN + off:l * D_MAIN + off + width]

    hist = [rgc0_ref[k] for k in range(CONV_W - 1)]
    raw = hist + [col(l, 0, D_RG) for l in range(nl)]
    xcs = []
    for l in range(nl):
        acc = rcb_ref[...] + raw[l] * rcw_ref[0:1, :]
        for k in range(1, CONV_W):
            acc = acc + raw[l + k] * rcw_ref[k:k + 1, :]
        xcs.append(acc)
    a, u = _rg_gates(jnp.concatenate(xcs, axis=0), wa_ref, ba_ref, wi_ref, bi_ref, lam_ref)
    h = h0_ref[...]
    for l in range(nl):
        h = a[l * bb:(l + 1) * bb] * h + u[l * bb:(l + 1) * bb]
        y_ref[:, l * 2 * D_MODEL:l * 2 * D_MODEL + D_RG] = h * _gelu_tanh(col(l, o_gate, D_RG))
    h_ref[...] = h
    for k in range(CONV_W - 1):
        rgc_ref[k] = raw[nl + k]

    hist = [sc0_ref[k] for k in range(CONV_W - 1)]
    raw = hist + [col(l, o_xbc, D_XBC) for l in range(nl)]
    for k in range(CONV_W - 1):
        sc_ref[k] = raw[nl + k]
    xcs = []
    for l in range(nl):
        acc = scb_ref[...] + raw[l] * scw_ref[0:1, :]
        for k in range(1, CONV_W):
            acc = acc + raw[l + k] * scw_ref[k:k + 1, :]
        xcs.append(_silu(acc))
    xs = [x[:, :D_SSD] for x in xcs]
    bms = [x[:, D_SSD:D_SSD + D_BC].astype(BF16) for x in xcs]
    cms = [x[:, D_SSD + D_BC:].astype(BF16) for x in xcs]

    lane = lax.broadcasted_iota(I32, (1, LANES), 1)
    a_neg = jnp.where(lane < SSD_HEADS, -jnp.exp(alog_ref[...]), 0.0)
    dts = [_softplus(dt_ref[:, l * LANES:(l + 1) * LANES] + dtb_ref[...]) for l in range(nl)]
    acums = []
    run = jnp.zeros((bb, LANES), F32)
    for l in range(nl):
        run = run + dts[l] * a_neg
        acums.append(run)
    a_last = acums[nl - 1]

    e_b16 = _head_expand_matrix(BF16)
    stacked = jnp.concatenate(dts + [jnp.exp(a_last - ac) for ac in acums] + [jnp.exp(ac) for ac in acums], axis=0)
    expd = _expand_heads(stacked, e_b16)
    dt_e = [expd[l * bb:(l + 1) * bb] for l in range(nl)]
    dend_e = [expd[(nl + l) * bb:(nl + l + 1) * bb] for l in range(nl)]
    ea_e = [expd[(2 * nl + l) * bb:(2 * nl + l + 1) * bb] for l in range(nl)]
    xdt = [xs[l] * dt_e[l] for l in range(nl)]
    xdt_r = [x.astype(BF16).astype(F32) for x in xdt]
    rows = nl * bb
    pad_rows = LANES - rows
    w_all = jnp.concatenate([(xdt[l] * dend_e[l]) for l in range(nl)]
                            + [jnp.zeros((pad_rows, D_SSD), F32)], axis=0)
    c_all = jnp.concatenate(cms, axis=0)
    b_all = jnp.concatenate(bms + [jnp.zeros((pad_rows, D_BC), BF16)], axis=0)

    r = lax.broadcasted_iota(I32, (D_BC, LANES), 0)
    cidx = lax.broadcasted_iota(I32, (D_BC, LANES), 1)
    gsum = jnp.where((lax.shift_right_logical(r, 7) == lax.shift_right_logical(cidx, 3)) & (cidx < SSD_HEADS),
                     1.0, 0.0).astype(F32)
    pairs = [(l, s) for l in range(nl) for s in range(l + 1)]
    prods = jnp.concatenate([cms[l].astype(F32) * bms[s].astype(F32) for (l, s) in pairs], axis=0)
    cbh = jnp.dot(prods, gsum, precision=HIGHEST, preferred_element_type=F32)
    m_list = []
    for idx, (l, s) in enumerate(pairs):
        decay = jnp.exp(acums[l] - acums[s])
        m_list.append((cbh[idx * bb:(idx + 1) * bb] * decay).astype(BF16))
    m_e = jnp.dot(jnp.concatenate(m_list, axis=0), e_b16, preferred_element_type=F32)
    y_diag = []
    for l in range(nl):
        acc = None
        for idx, (ll, s) in enumerate(pairs):
            if ll != l:
                continue
            term = m_e[idx * bb:(idx + 1) * bb] * xdt_r[s]
            acc = term if acc is None else acc + term
        y_diag.append(acc)

    cd_t = jnp.concatenate([jnp.exp(a_last), jnp.zeros((LANES - bb, LANES), F32)], axis=0).T
    row_seq = lax.broadcasted_iota(I32, (rows, 1), 0) & (bb - 1)
    row_pad = lax.broadcasted_iota(I32, (LANES, 1), 0)
    lane_q = lax.broadcasted_iota(I32, (LANES, LANES), 1)
    hpg = SSD_HEADS // SSD_GROUPS

    def seq_step(q, y_off):
        mine = row_seq == q
        mine_pad = ((row_pad & (bb - 1)) == q) & (row_pad < rows)
        cd_q = jnp.sum(jnp.where(lane_q == q, cd_t, 0.0), axis=1, keepdims=True)
        cd_q = jnp.broadcast_to(cd_q, (LANES, SSD_STATE))
        parts = []
        for g in range(SSD_GROUPS):
            s0 = s0_ref[q, g * D_BC:(g + 1) * D_BC, :]
            cg = c_all[:, g * SSD_STATE:(g + 1) * SSD_STATE]
            yq = lax.dot_general(cg, s0.astype(BF16), (((1,), (1,)), ((), ())), preferred_element_type=F32)
            parts.append(yq)
            bg = b_all[:, g * SSD_STATE:(g + 1) * SSD_STATE]
            wq = jnp.where(mine_pad, w_all[:, g * D_BC:(g + 1) * D_BC], 0.0)
            upd = jnp.dot(wq.T.astype(BF16), bg, preferred_element_type=F32)
            for e in range(hpg):
                hh = g * hpg + e
                cd = jnp.broadcast_to(cd_q[hh:hh + 1, :], (SSD_HEAD_DIM, SSD_STATE))
                s_ref[q, hh * SSD_HEAD_DIM:(hh + 1) * SSD_HEAD_DIM, :] = (
                    cd * s0[e * SSD_HEAD_DIM:(e + 1) * SSD_HEAD_DIM, :]
                    + upd[e * SSD_HEAD_DIM:(e + 1) * SSD_HEAD_DIM, :])
        yq_all = jnp.concatenate(parts, axis=1)
        return jnp.where(mine, yq_all, y_off)

    y_off = lax.fori_loop(0, bb, seq_step, jnp.zeros((rows, D_SSD), F32))

    for l in range(nl):
        y = (y_diag[l] + y_off[l * bb:(l + 1) * bb] * ea_e[l] + dexp_ref[...] * xs[l]) * _silu(col(l, o_z, D_SSD))
        y_ref[:, l * 2 * D_MODEL + D_RG:(l + 1) * 2 * D_MODEL] = _group_rmsnorm(y, ng_ref[...])


def _mix_sample(proj4, dt4, row0, n_seq, h0, rgc0, s0, sc0, rg_params, ssd_params):
    bb = SAMPLE_BB
    nl = proj4.shape[1] // D_MAIN
    hp = SSD_HEADS * SSD_HEAD_DIM
    blk0 = row0 // bb
    const2 = lambda i: (0, 0)
    rg_specs = [
        pl.BlockSpec((CONV_W, D_RG), const2), pl.BlockSpec((1, D_RG), const2),
        pl.BlockSpec((RG_BLOCKS, RG_BLOCK_W, RG_BLOCK_W), lambda i: (0, 0, 0)), pl.BlockSpec((1, D_RG), const2),
        pl.BlockSpec((RG_BLOCKS, RG_BLOCK_W, RG_BLOCK_W), lambda i: (0, 0, 0)), pl.BlockSpec((1, D_RG), const2),
        pl.BlockSpec((1, D_RG), const2),
    ]
    ssd_specs = [
        pl.BlockSpec((CONV_W, D_XBC), const2), pl.BlockSpec((1, D_XBC), const2),
        pl.BlockSpec((1, LANES), const2), pl.BlockSpec((1, LANES), const2),
        pl.BlockSpec((1, D_SSD), const2), pl.BlockSpec((1, D_SSD), const2),
    ]
    return pl.pallas_call(
        _mix_sample_body,
        grid=(n_seq // bb,),
        in_specs=[
            pl.BlockSpec((bb, nl * D_MAIN), lambda i: (blk0 + i, 0)),
            pl.BlockSpec((bb, nl * LANES), lambda i: (blk0 + i, 0)),
            pl.BlockSpec((bb, D_RG), lambda i: (i, 0)),
            pl.BlockSpec((CONV_W - 1, bb, D_RG), lambda i: (0, i, 0)),
            pl.BlockSpec((bb, hp, SSD_STATE), lambda i: (i, 0, 0)),
            pl.BlockSpec((CONV_W - 1, bb, D_XBC), lambda i: (0, i, 0)),
        ] + rg_specs + ssd_specs,
        out_specs=[
            pl.BlockSpec((bb, nl * 2 * D_MODEL), lambda i: (i, 0)),
            pl.BlockSpec((bb, D_RG), lambda i: (i, 0)),
            pl.BlockSpec((CONV_W - 1, bb, D_RG), lambda i: (0, i, 0)),
            pl.BlockSpec((bb, hp, SSD_STATE), lambda i: (i, 0, 0)),
            pl.BlockSpec((CONV_W - 1, bb, D_XBC), lambda i: (0, i, 0)),
        ],
        out_shape=[
            jax.ShapeDtypeStruct((n_seq, nl * 2 * D_MODEL), F32),
            jax.ShapeDtypeStruct((n_seq, D_RG), F32),
            jax.ShapeDtypeStruct((CONV_W - 1, n_seq, D_RG), F32),
            jax.ShapeDtypeStruct((n_seq, hp, SSD_STATE), F32),
            jax.ShapeDtypeStruct((CONV_W - 1, n_seq, D_XBC), F32),
        ],
        compiler_params=_cparams(("arbitrary",)),
        name="mix_sample",
    )(proj4, dt4, h0, rgc0, s0, sc0, *rg_params, *ssd_params)


def _route_body(x_ref, g_ref, wr_ref, br_ref, hp_ref, e_ref, gate_ref, rank_ref, cnt_ref, run_ref):
    i = pl.program_id(0)
    tm = x_ref.shape[0]

    @pl.when(i == 0)
    def _():
        run_ref[...] = jnp.zeros_like(run_ref)

    x = x_ref[...]
    h = x * lax.rsqrt(jnp.mean(x * x, axis=-1, keepdims=True) + EPS) * g_ref[...]
    hp_ref[...] = h

    lane = lax.broadcasted_iota(I32, (tm, LANES), 1).astype(F32)
    logits = jnp.dot(h, wr_ref[...], precision=HIGHEST, preferred_element_type=F32) + br_ref[...]
    work = jnp.where(lane < N_EXPERTS, logits, -jnp.inf)
    vals, idxs = [], []
    multi = jnp.zeros((tm, LANES), F32)
    for _ in range(TOP_K):
        m = jnp.max(work, axis=-1, keepdims=True)
        idx = jnp.min(jnp.where(work == m, lane, float(LANES)), axis=-1, keepdims=True)
        hit = lane == idx
        vals.append(m)
        idxs.append(idx)
        multi = jnp.where(hit, 1.0, multi)
        work = jnp.where(hit, -jnp.inf, work)
    ex = [jnp.exp(v - vals[0]) for v in vals]
    den = ex[0] + ex[1] + ex[2] + ex[3]
    gates = [e / den for e in ex]

    ri = lax.broadcasted_iota(I32, (tm, tm), 0)
    ci = lax.broadcasted_iota(I32, (tm, tm), 1)
    strict = jnp.where(ci < ri, 1.0, 0.0).astype(BF16)
    before = jnp.dot(strict, multi.astype(BF16), preferred_element_type=F32) + run_ref[...]
    ranks = [jnp.sum(jnp.where(lane == idx, before, 0.0), axis=-1, keepdims=True) for idx in idxs]
    run_ref[...] = run_ref[...] + jnp.sum(multi, axis=0, keepdims=True)

    lane4 = lax.broadcasted_iota(I32, (tm, TOP_K), 1)
    e_out = jnp.zeros((tm, TOP_K), I32)
    g_out = jnp.zeros((tm, TOP_K), F32)
    r_out = jnp.zeros((tm, TOP_K), I32)
    for k in range(TOP_K):
        e_out = jnp.where(lane4 == k, idxs[k].astype(I32), e_out)
        g_out = jnp.where(lane4 == k, gates[k], g_out)
        r_out = jnp.where(lane4 == k, ranks[k].astype(I32), r_out)
    e_ref[...] = e_out
    gate_ref[...] = g_out
    rank_ref[...] = r_out
    cnt_ref[...] = run_ref[...].astype(I32)


def _route(x1, g_ffn, w_router_pad, b_router_pad):
    t = x1.shape[0]
    tm = ROUTE_TM
    const = lambda i: (0, 0)
    return pl.pallas_call(
        _route_body,
        grid=(t // tm,),
        in_specs=[
            pl.BlockSpec((tm, D_MODEL), lambda i: (i, 0)),
            pl.BlockSpec((1, D_MODEL), const),
            pl.BlockSpec((D_MODEL, LANES), const),
            pl.BlockSpec((1, LANES), const),
        ],
        out_specs=[
            pl.BlockSpec((tm, D_MODEL), lambda i: (i, 0)),
            pl.BlockSpec((tm, TOP_K), lambda i: (i, 0)),
            pl.BlockSpec((tm, TOP_K), lambda i: (i, 0)),
            pl.BlockSpec((tm, TOP_K), lambda i: (i, 0)),
            pl.BlockSpec((1, LANES), const),
        ],
        out_shape=[
            jax.ShapeDtypeStruct((t, D_MODEL), F32),
            jax.ShapeDtypeStruct((t, TOP_K), I32),
            jax.ShapeDtypeStruct((t, TOP_K), F32),
            jax.ShapeDtypeStruct((t, TOP_K), I32),
            jax.ShapeDtypeStruct((1, LANES), I32),
        ],
        scratch_shapes=[pltpu.VMEM((1, LANES), F32)],
        compiler_params=_cparams(("arbitrary",)),
        name="route",
    )(x1, g_ffn, w_router_pad, b_router_pad)


def _scatter_body(seg_ref, cnt_ref, e_ref, rank_ref, h_ref, xs_ref, zero_ref, sem, zsem):
    i = pl.program_id(0)
    tm = h_ref.shape[0]
    n = e_ref.shape[0]

    def issue(blk, c):
        for u in range(DMA_UNROLL):
            a = blk * DMA_UNROLL + u
            pos = seg_ref[e_ref[a]] + rank_ref[a]
            t = lax.shift_right_logical(a, 2)
            pltpu.make_async_copy(h_ref.at[pl.ds(t, 1), :], xs_ref.at[pl.ds(pos, 1), :], sem).start()
        return c

    lax.fori_loop(0, n // DMA_UNROLL, issue, 0)
    for _ in range(TOP_K):
        pltpu.make_async_copy(h_ref, xs_ref.at[pl.ds(0, tm), :], sem).wait()

    @pl.when(i == pl.num_programs(0) - 1)
    def _():
        zero_ref[...] = jnp.zeros_like(zero_ref)
        pad = EXPERT_PAD

        def row_fill(row):
            return pltpu.make_async_copy(zero_ref.at[pl.ds(0, 1), :], xs_ref.at[pl.ds(row, 1), :], zsem)

        def block_fill(blk):
            dst = pl.multiple_of(blk * pad, pad)
            return pltpu.make_async_copy(zero_ref, xs_ref.at[pl.ds(dst, pad), :], zsem)

        def for_each(lo, hi, fn):
            def step(r, c):
                fn(r)
                return c
            lax.fori_loop(lo, hi, step, 0)

        def per_expert(e, used):
            cnt = cnt_ref[e]
            hi = ((cnt + pad - 1) // pad) * pad
            base = seg_ref[e]
            for_each(cnt, hi, lambda r: row_fill(base + r).start())
            for_each(cnt, hi, lambda r: row_fill(base + r).wait())
            return used + hi // pad

        used = lax.fori_loop(0, N_EXPERTS, per_expert, jnp.int32(0))
        n_blocks = xs_ref.shape[0] // pad
        for_each(used, n_blocks, lambda blk: block_fill(blk).start())
        for_each(used, n_blocks, lambda blk: block_fill(blk).wait())


def _scatter(seg_start, counts, e_flat, rank_flat, h2, n_slots):
    t = h2.shape[0]
    tm = SCATTER_TM
    grid_spec = pltpu.PrefetchScalarGridSpec(
        num_scalar_prefetch=2,
        grid=(t // tm,),
        in_specs=[
            pl.BlockSpec((tm * TOP_K,), lambda i, seg, cnt: (i,), memory_space=pltpu.SMEM),
            pl.BlockSpec((tm * TOP_K,), lambda i, seg, cnt: (i,), memory_space=pltpu.SMEM),
            pl.BlockSpec((tm, D_MODEL), lambda i, seg, cnt: (i, 0)),
        ],
        out_specs=pl.BlockSpec(memory_space=pl.ANY),
        scratch_shapes=[pltpu.VMEM((EXPERT_PAD, D_MODEL), F32), pltpu.SemaphoreType.DMA(()),
                        pltpu.SemaphoreType.DMA(())],
    )
    return pl.pallas_call(
        _scatter_body,
        grid_spec=grid_spec,
        out_shape=jax.ShapeDtypeStruct((n_slots, D_MODEL), F32),
        compiler_params=_cparams(("arbitrary",)),
        name="scatter_rows",
    )(seg_start, counts, e_flat, rank_flat, h2)


def _experts_body(sbe_ref, sbs_ref, sbn_ref, used_ref, xs_ref, wg_ref, wu_ref, bg_ref, bu_ref, wd_ref, bd_ref,
                  ys_ref, xbuf_ref, acc_ref, wgu_ref, wdb_ref, in_sem, out_sem):
    del sbe_ref
    sb = pl.program_id(0)
    j = pl.program_id(1)
    nj = pl.num_programs(1)
    nblk = sbn_ref[sb]
    start = sbs_ref[sb]
    cb = EXPERT_PAD
    bm = EXPERT_BM
    tf = wd_ref.shape[0]

    def in_copy(r):
        src = pl.multiple_of(start + r * cb, cb)
        dst = pl.multiple_of(r * cb, cb)
        return pltpu.make_async_copy(xs_ref.at[pl.ds(src, cb), :], acc_ref.at[pl.ds(dst, cb), :], in_sem.at[r])

    def out_copy(r):
        src = pl.multiple_of(r * cb, cb)
        dst = pl.multiple_of(start + r * cb, cb)
        return pltpu.make_async_copy(acc_ref.at[pl.ds(src, cb), :], ys_ref.at[pl.ds(dst, cb), :], out_sem)

    def zero_copy(blk):
        dst = pl.multiple_of(blk * cb, cb)
        return pltpu.make_async_copy(acc_ref.at[pl.ds(0, cb), :], ys_ref.at[pl.ds(dst, cb), :], out_sem)

    def for_each(lo, hi, fn):
        def step(r, c):
            fn(r)
            return c
        lax.fori_loop(lo, hi, step, 0)

    @pl.when(j == 0)
    def _():
        for_each(0, nblk, lambda r: in_copy(r).start())

        def stage(r):
            in_copy(r).wait()
            r0 = pl.multiple_of(r * cb, cb)
            xbuf_ref[pl.ds(r0, cb), :] = acc_ref[pl.ds(r0, cb), :].astype(BF16)
            acc_ref[pl.ds(r0, cb), :] = jnp.broadcast_to(bd_ref[...], (cb, D_MODEL))

        for_each(0, nblk, stage)

    @pl.when(nblk > 0)
    def _():
        def cast_rows(s):
            r = pl.multiple_of(s * cb, cb)
            wgu_ref[pl.ds(r, cb), 0:tf] = wg_ref[pl.ds(r, cb), :].astype(BF16)
            wgu_ref[pl.ds(r, cb), tf:2 * tf] = wu_ref[pl.ds(r, cb), :].astype(BF16)

        for_each(0, D_MODEL // cb, cast_rows)
        wdb_ref[...] = wd_ref[...].astype(BF16)
        bias_gu = jnp.concatenate([bg_ref[...], bu_ref[...]], axis=1)

        def sub(r0, rows):
            x = xbuf_ref[pl.ds(r0, rows), :]
            gu = jnp.dot(x, wgu_ref[...], preferred_element_type=F32) + bias_gu
            gate = jnp.minimum(gu[:, :tf], SWIGLU_LIMIT)
            up = jnp.clip(gu[:, tf:], -SWIGLU_LIMIT, SWIGLU_LIMIT)
            glu = gate * _sigmoid(SWIGLU_ALPHA * gate)
            act = ((up + 1.0) * glu).astype(BF16)
            acc_ref[pl.ds(r0, rows), :] += jnp.dot(act, wdb_ref[...], preferred_element_type=F32)

            @pl.when(j == nj - 1)
            def _():
                for b in range(rows // cb):
                    out_copy(r0 // cb + b).start()

        per_sub = bm // cb
        n_full = nblk // per_sub
        for_each(0, n_full, lambda i: sub(pl.multiple_of(i * bm, bm), bm))
        done = n_full * per_sub
        piece = per_sub // 2
        while piece >= 1:
            here = done

            @pl.when((nblk & piece) != 0)
            def _():
                sub(pl.multiple_of(here * cb, cb), piece * cb)

            done = done + (nblk & piece)
            piece //= 2

    @pl.when(j == nj - 1)
    def _():
        for_each(0, nblk, lambda r: out_copy(r).wait())

    @pl.when((sb == pl.num_programs(0) - 1) & (j == nj - 1))
    def _():
        acc_ref[0:cb, :] = jnp.zeros((cb, D_MODEL), F32)
        n_blocks = ys_ref.shape[0] // cb
        for_each(used_ref[0], n_blocks, lambda blk: zero_copy(blk).start())
        for_each(used_ref[0], n_blocks, lambda blk: zero_copy(blk).wait())


def _experts(sb_expert, sb_start, sb_nsub, used_blocks, xsorted, w_gu, b_gu, w_down, b_down):
    n_sb = sb_expert.shape[0]
    n_slots = xsorted.shape[0]
    tf = EXPERT_TF
    nj = D_FF // tf

    def spec(shape, fn):
        def index_map(sb, j, sbe, sbs, sbn, used):
            jj = jnp.where(sbn[sb] > 0, j, nj - 1)
            return fn(sbe[sb], jj)
        return pl.BlockSpec(shape, index_map)

    grid_spec = pltpu.PrefetchScalarGridSpec(
        num_scalar_prefetch=4,
        grid=(n_sb, nj),
        in_specs=[
            pl.BlockSpec(memory_space=pl.ANY),
            spec((None, D_MODEL, tf), lambda e, jj: (e, 0, jj)),
            spec((None, D_MODEL, tf), lambda e, jj: (e, 0, nj + jj)),
            spec((None, 1, tf), lambda e, jj: (e, 0, jj)),
            spec((None, 1, tf), lambda e, jj: (e, 0, nj + jj)),
            spec((None, tf, D_MODEL), lambda e, jj: (e, jj, 0)),
            spec((None, 1, D_MODEL), lambda e, jj: (e, 0, 0)),
        ],
        out_specs=pl.BlockSpec(memory_space=pl.ANY),
        scratch_shapes=[
            pltpu.VMEM((EXPERT_R, D_MODEL), BF16),
            pltpu.VMEM((EXPERT_R, D_MODEL), F32),
            pltpu.VMEM((D_MODEL, 2 * tf), BF16),
            pltpu.VMEM((tf, D_MODEL), BF16),
            pltpu.SemaphoreType.DMA((EXPERT_R // EXPERT_PAD,)),
            pltpu.SemaphoreType.DMA(()),
        ],
    )
    return pl.pallas_call(
        _experts_body,
        grid_spec=grid_spec,
        out_shape=jax.ShapeDtypeStruct((n_slots, D_MODEL), F32),
        compiler_params=_cparams(("arbitrary", "arbitrary")),
        name="experts",
    )(sb_expert, sb_start, sb_nsub, used_blocks, xsorted, w_gu, w_gu, b_gu, b_gu, w_down, b_down)


def _combine_body(seg_ref, e_ref, rank_ref, x1_ref, gate_ref, gf_ref, ys_ref, yp_ref, ysm_ref, buf_ref, sem,
                  *, n_prompt_tiles):
    i = pl.program_id(0)
    tm = x1_ref.shape[0]
    n = e_ref.shape[0]

    def issue(blk, c):
        for u in range(DMA_UNROLL):
            a = blk * DMA_UNROLL + u
            pos = seg_ref[e_ref[a]] + rank_ref[a]
            t = lax.shift_right_logical(a, 2)
            pltpu.make_async_copy(ys_ref.at[pl.ds(pos, 1), :], buf_ref.at[u % TOP_K, pl.ds(t, 1), :], sem).start()
        return c

    lax.fori_loop(0, n // DMA_UNROLL, issue, 0)
    for k in range(TOP_K):
        pltpu.make_async_copy(ys_ref.at[pl.ds(0, tm), :], buf_ref.at[k], sem).wait()

    g = gate_ref[...]
    acc = buf_ref[0] * g[:, 0:1]
    for k in range(1, TOP_K):
        acc = acc + buf_ref[k] * g[:, k:k + 1]
    x = x1_ref[...] + acc
    y = x * lax.rsqrt(jnp.mean(x * x, axis=-1, keepdims=True) + EPS) * gf_ref[...]

    @pl.when(i < n_prompt_tiles)
    def _():
        yp_ref[...] = y

    @pl.when(i >= n_prompt_tiles)
    def _():
        ysm_ref[...] = y


def _combine(seg_start, e_flat, rank_flat, x1, gates, g_final, ysorted, t_prompt):
    t = x1.shape[0]
    tm = COMBINE_TM
    npt = t_prompt // tm
    grid_spec = pltpu.PrefetchScalarGridSpec(
        num_scalar_prefetch=1,
        grid=(t // tm,),
        in_specs=[
            pl.BlockSpec((tm * TOP_K,), lambda i, seg: (i,), memory_space=pltpu.SMEM),
            pl.BlockSpec((tm * TOP_K,), lambda i, seg: (i,), memory_space=pltpu.SMEM),
            pl.BlockSpec((tm, D_MODEL), lambda i, seg: (i, 0)),
            pl.BlockSpec((tm, TOP_K), lambda i, seg: (i, 0)),
            pl.BlockSpec((1, D_MODEL), lambda i, seg: (0, 0)),
            pl.BlockSpec(memory_space=pl.ANY),
        ],
        out_specs=[
            pl.BlockSpec((tm, D_MODEL), lambda i, seg: (jnp.minimum(i, npt - 1), 0)),
            pl.BlockSpec((tm, D_MODEL), lambda i, seg: (jnp.maximum(i - npt, 0), 0)),
        ],
        scratch_shapes=[pltpu.VMEM((TOP_K, tm, D_MODEL), F32), pltpu.SemaphoreType.DMA(())],
    )
    return pl.pallas_call(
        functools.partial(_combine_body, n_prompt_tiles=npt),
        grid_spec=grid_spec,
        out_shape=[
            jax.ShapeDtypeStruct((t_prompt, D_MODEL), F32),
            jax.ShapeDtypeStruct((t - t_prompt, D_MODEL), F32),
        ],
        compiler_params=_cparams(("arbitrary",)),
        name="combine",
    )(seg_start, e_flat, rank_flat, x1, gates, g_final, ysorted)


def _expert_tables(counts, n_assign):
    pad = EXPERT_PAD
    padded = ((counts + pad - 1) // pad) * pad
    seg_start = jnp.cumsum(padded) - padded
    n_sb_e = (counts + EXPERT_R - 1) // EXPERT_R
    sb_cum = jnp.cumsum(n_sb_e)
    n_sb = n_assign // EXPERT_R + N_EXPERTS
    s = jnp.arange(n_sb, dtype=I32)
    total = sb_cum[-1]
    s_eff = jnp.minimum(s, total - 1)
    e = jnp.sum((sb_cum[None, :] <= s_eff[:, None]).astype(I32), axis=1)
    k = s_eff - (sb_cum[e] - n_sb_e[e])
    rows = jnp.clip(counts[e] - k * EXPERT_R, 0, EXPERT_R)
    valid = s < total
    nsub = jnp.where(valid, (rows + pad - 1) // pad, 0)
    start = seg_start[e] + k * EXPERT_R
    used_blocks = (jnp.sum(padded) // pad).reshape(1)
    return seg_start.astype(I32), e, start.astype(I32), nsub.astype(I32), used_blocks.astype(I32)


def kernel(x_prompt, x_sample, state_rglru_h, state_rglru_conv, state_ssd, state_ssd_conv, g_mix, w_in, conv_rg_w, conv_rg_b, rg_wa, rg_ba, rg_wi, rg_bi, rg_lambda, conv_ssd_w, conv_ssd_b, ssd_dt_bias, ssd_a_log, ssd_d, ssd_norm_g, w_out, g_ffn, w_router, b_router, w_gu, b_gu, w_down, b_down, g_final):
    depth = g_mix.shape[0]
    assert depth == 1
    bp, lp, _ = x_prompt.shape
    bs, ls, _ = x_sample.shape
    tp, ts = bp * lp, bs * ls
    t = tp + ts
    hp = SSD_HEADS * SSD_HEAD_DIM
    l = 0

    xp = x_prompt.reshape(tp, D_MODEL)
    xs = x_sample.reshape(ts, D_MODEL)
    w_in_t = jnp.transpose(w_in[l])
    row = lambda v: v.reshape(1, -1)
    pad_heads = lambda v: jnp.pad(v.reshape(1, -1), ((0, 0), (0, LANES - SSD_HEADS)))

    hb, dt_raw = _norm_dt(xp, xs, row(g_mix[l]), w_in_t)
    proj = _in_proj(hb, w_in_t)

    rg_params = (conv_rg_w[l], row(conv_rg_b[l]), rg_wa[l], row(rg_ba[l]), rg_wi[l], row(rg_bi[l]),
                 row(rg_lambda[l]))
    d_exp = row(jnp.repeat(ssd_d[l], SSD_HEAD_DIM))
    ssd_params = (conv_ssd_w[l], row(conv_ssd_b[l]), pad_heads(ssd_dt_bias[l]), pad_heads(ssd_a_log[l]),
                  d_exp, row(ssd_norm_g[l]))

    y_rg, rgh_p, rgc_p = _rglru_prompt(proj, bp, lp, *rg_params)
    y_ssd, ssd_p, ssdc_p = _ssd_prompt(proj, dt_raw, bp, lp, *ssd_params)

    proj4 = proj[tp:].reshape(bs, ls * D_MAIN)
    dt4 = dt_raw[tp:].reshape(bs, ls * LANES)
    mix_s4, rgh_s, rgc_s, ssd_s, ssdc_s = _mix_sample(
        proj4, dt4, 0, bs,
        state_rglru_h[l], jnp.transpose(state_rglru_conv[l], (1, 0, 2)),
        state_ssd[l].reshape(bs, hp, SSD_STATE), jnp.transpose(state_ssd_conv[l], (1, 0, 2)),
        rg_params, ssd_params)
    mix_s = mix_s4.reshape(ts, 2 * D_MODEL)

    x1 = _out_proj(y_rg, y_ssd, mix_s, w_out[l], xp, xs)

    wr = jnp.pad(w_router[l], ((0, 0), (0, LANES - N_EXPERTS)))
    br = jnp.pad(b_router[l].reshape(1, -1), ((0, 0), (0, LANES - N_EXPERTS)))
    h2, e_idx, gates, rank, counts = _route(x1, row(g_ffn[l]), wr, br)

    n_assign = t * TOP_K
    n_slots = n_assign + N_EXPERTS * EXPERT_PAD
    counts = counts[0, :N_EXPERTS]
    seg_start, sb_expert, sb_start, sb_nsub, used_blocks = _expert_tables(counts, n_assign)
    e_flat = e_idx.reshape(-1)
    rank_flat = rank.reshape(-1)
    xsorted = _scatter(seg_start, counts, e_flat, rank_flat, h2, n_slots)
    ysorted = _experts(sb_expert, sb_start, sb_nsub, used_blocks, xsorted, w_gu[l], b_gu[l].reshape(N_EXPERTS, 1, -1),
                       w_down[l], b_down[l].reshape(N_EXPERTS, 1, -1))
    y_p, y_s = _combine(seg_start, e_flat, rank_flat, x1, gates, row(g_final), ysorted, tp)

    return (y_p.reshape(x_prompt.shape).astype(x_prompt.dtype),
            y_s.reshape(x_sample.shape).astype(x_sample.dtype),
            rgh_p.reshape(depth, bp, D_RG),
            rgc_p.reshape(depth, bp, CONV_W - 1, D_RG),
            ssd_p.reshape(depth, bp, SSD_HEADS, SSD_HEAD_DIM, SSD_STATE),
            ssdc_p.reshape(depth, bp, CONV_W - 1, D_XBC),
            rgh_s.reshape(depth, bs, D_RG),
            jnp.transpose(rgc_s, (1, 0, 2)).reshape(depth, bs, CONV_W - 1, D_RG),
            ssd_s.reshape(depth, bs, SSD_HEADS, SSD_HEAD_DIM, SSD_STATE),
            jnp.transpose(ssdc_s, (1, 0, 2)).reshape(depth, bs, CONV_W - 1, D_XBC))
```

```python
import functools

import jax
import jax.numpy as jnp
from jax import lax
from jax.experimental import pallas as pl
from jax.experimental.pallas import tpu as pltpu

F32 = jnp.float32
BF16 = jnp.bfloat16
I32 = jnp.int32
HIGHEST = lax.Precision.HIGHEST

EPS = 1e-6
D_MODEL = 2048
D_RG = 2048
RG_BLOCKS = 16
RG_BLOCK_W = 128
RG_C = 8.0
D_SSD = 2048
SSD_HEAD_DIM = 64
SSD_HEADS = 32
SSD_GROUPS = 4
SSD_STATE = 128
SSD_CHUNK = 128
CONV_W = 4
D_BC = SSD_GROUPS * SSD_STATE
D_XBC = D_SSD + 2 * D_BC
D_MAIN = 2 * D_RG + D_SSD + D_XBC
N_EXPERTS = 32
TOP_K = 4
D_FF = 2048
SWIGLU_ALPHA = 1.702
SWIGLU_LIMIT = 7.0

LANES = 128
VMEM_LIMIT = 56 * 1024 * 1024

ROW_TILE = 512
PROJ_TM = 1088
PROJ_TN = 1024
OUT_TN = 512
RG_TL = 256
SAMPLE_BB = 8
ROUTE_TM = 512
SCATTER_TM = 256
COMBINE_TM = 256
EXPERT_PAD = 128
EXPERT_BM = 512
EXPERT_R = 1152
EXPERT_TF = 512
DMA_UNROLL = 8


def _cparams(sem, vmem=VMEM_LIMIT):
    return pltpu.CompilerParams(dimension_semantics=sem, vmem_limit_bytes=vmem)


def _softplus(x):
    return jnp.maximum(x, 0.0) + jnp.log1p(jnp.exp(-jnp.abs(x)))


_sigmoid = jax.nn.sigmoid
_silu = jax.nn.silu


def _gelu_tanh(x):
    return jax.nn.gelu(x, approximate=True)


def _head_expand_matrix(dtype):
    r = lax.broadcasted_iota(I32, (LANES, D_SSD), 0)
    c = lax.broadcasted_iota(I32, (LANES, D_SSD), 1)
    return jnp.where(lax.shift_right_logical(c, 6) == r, 1.0, 0.0).astype(dtype)


def _expand_heads(v, e_b16):
    rows = v.shape[0]
    hi = v.astype(BF16)
    r1 = v - hi.astype(F32)
    mid = r1.astype(BF16)
    lo = (r1 - mid.astype(F32)).astype(BF16)
    o = jnp.dot(jnp.concatenate([hi, mid, lo], axis=0), e_b16, preferred_element_type=F32)
    return (o[0:rows] + o[rows:2 * rows]) + o[2 * rows:3 * rows]


def _norm_dt_body(xp_ref, xs_ref, g_ref, wdt_ref, hb_ref, dt_ref, *, n_prompt_tiles):
    i = pl.program_id(0)

    def run(x_ref):
        x = x_ref[...]
        h = x * lax.rsqrt(jnp.mean(x * x, axis=-1, keepdims=True) + EPS) * g_ref[...]
        hb = h.astype(BF16)
        hb_ref[...] = hb
        r = lax.broadcasted_iota(I32, (LANES, 1), 0)
        wdt = jnp.where(r < SSD_HEADS, wdt_ref[...], 0.0).astype(BF16)
        dt_ref[...] = lax.dot_general(hb, wdt, (((1,), (1,)), ((), ())), preferred_element_type=F32)

    @pl.when(i < n_prompt_tiles)
    def _():
        run(xp_ref)

    @pl.when(i >= n_prompt_tiles)
    def _():
        run(xs_ref)


def _norm_dt(xp, xs, g, w_in_t):
    tp, ts = xp.shape[0], xs.shape[0]
    npt, nst = tp // ROW_TILE, ts // ROW_TILE
    t = tp + ts
    return pl.pallas_call(
        functools.partial(_norm_dt_body, n_prompt_tiles=npt),
        grid=(npt + nst,),
        in_specs=[
            pl.BlockSpec((ROW_TILE, D_MODEL), lambda i: (jnp.minimum(i, npt - 1), 0)),
            pl.BlockSpec((ROW_TILE, D_MODEL), lambda i: (jnp.maximum(i - npt, 0), 0)),
            pl.BlockSpec((1, D_MODEL), lambda i: (0, 0)),
            pl.BlockSpec((LANES, D_MODEL), lambda i: (D_MAIN // LANES, 0)),
        ],
        out_specs=[
            pl.BlockSpec((ROW_TILE, D_MODEL), lambda i: (i, 0)),
            pl.BlockSpec((ROW_TILE, LANES), lambda i: (i, 0)),
        ],
        out_shape=[jax.ShapeDtypeStruct((t, D_MODEL), BF16), jax.ShapeDtypeStruct((t, LANES), F32)],
        compiler_params=_cparams(("parallel",)),
        name="norm_dt",
    )(xp, xs, g, w_in_t)


def _cast_weight_tile(w_ref, wb_ref, rows_per_step=256):
    k = w_ref.shape[0]

    def step(s, c):
        r = pl.multiple_of(s * rows_per_step, rows_per_step)
        wb_ref[pl.ds(r, rows_per_step), :] = w_ref[pl.ds(r, rows_per_step), :].astype(BF16)
        return c

    lax.fori_loop(0, k // rows_per_step, step, 0)


def _in_proj_body(x_ref, w_ref, o_ref, wb_ref):
    @pl.when(pl.program_id(1) == 0)
    def _():
        _cast_weight_tile(w_ref, wb_ref)

    o_ref[...] = lax.dot_general(x_ref[...], wb_ref[...], (((1,), (1,)), ((), ())), preferred_element_type=F32)


def _in_proj(hb, w_in_t):
    t = hb.shape[0]
    return pl.pallas_call(
        _in_proj_body,
        grid=(D_MAIN // PROJ_TN, t // PROJ_TM),
        in_specs=[
            pl.BlockSpec((PROJ_TM, D_MODEL), lambda j, i: (i, 0)),
            pl.BlockSpec((PROJ_TN, D_MODEL), lambda j, i: (j, 0)),
        ],
        out_specs=pl.BlockSpec((PROJ_TM, PROJ_TN), lambda j, i: (i, j)),
        out_shape=jax.ShapeDtypeStruct((t, D_MAIN), F32),
        scratch_shapes=[pltpu.VMEM((PROJ_TN, D_MODEL), BF16)],
        compiler_params=_cparams(("arbitrary", "arbitrary")),
        name="in_proj",
    )(hb, w_in_t)


def _out_proj_body(rg_ref, ssd_ref, ms_ref, w_ref, xp_ref, xs_ref, o_ref, wb_ref, *, n_prompt_tiles):
    i = pl.program_id(1)

    @pl.when(i == 0)
    def _():
        _cast_weight_tile(w_ref, wb_ref)

    @pl.when(i < n_prompt_tiles)
    def _():
        m = jnp.concatenate([rg_ref[...], ssd_ref[...]], axis=1)
        o_ref[...] = xp_ref[...] + jnp.dot(m, wb_ref[...], preferred_element_type=F32)

    @pl.when(i >= n_prompt_tiles)
    def _():
        o_ref[...] = xs_ref[...] + jnp.dot(ms_ref[...].astype(BF16), wb_ref[...], preferred_element_type=F32)


def _out_proj(y_rg, y_ssd, mix_s, w_out, xp, xs):
    tp, ts = y_rg.shape[0], mix_s.shape[0]
    npt, nst = tp // ROW_TILE, ts // ROW_TILE
    prompt_rows = lambda j, i: (jnp.minimum(i, npt - 1), 0)
    return pl.pallas_call(
        functools.partial(_out_proj_body, n_prompt_tiles=npt),
        grid=(D_MODEL // OUT_TN, npt + nst),
        in_specs=[
            pl.BlockSpec((ROW_TILE, D_MODEL), prompt_rows),
            pl.BlockSpec((ROW_TILE, D_MODEL), prompt_rows),
            pl.BlockSpec((ROW_TILE, 2 * D_MODEL), lambda j, i: (jnp.maximum(i - npt, 0), 0)),
            pl.BlockSpec((2 * D_MODEL, OUT_TN), lambda j, i: (0, j)),
            pl.BlockSpec((ROW_TILE, OUT_TN), lambda j, i: (jnp.minimum(i, npt - 1), j)),
            pl.BlockSpec((ROW_TILE, OUT_TN), lambda j, i: (jnp.maximum(i - npt, 0), j)),
        ],
        out_specs=pl.BlockSpec((ROW_TILE, OUT_TN), lambda j, i: (i, j)),
        out_shape=jax.ShapeDtypeStruct((tp + ts, D_MODEL), F32),
        scratch_shapes=[pltpu.VMEM((2 * D_MODEL, OUT_TN), BF16)],
        compiler_params=_cparams(("arbitrary", "arbitrary")),
        name="out_proj",
    )(y_rg, y_ssd, mix_s, w_out, xp, xs)


def _rg_gates(xc, wa_ref, ba_ref, wi_ref, bi_ref, lam_ref):
    xcb = xc.astype(BF16)
    rs, is_ = [], []
    for h in range(RG_BLOCKS):
        xh = xcb[:, h * RG_BLOCK_W:(h + 1) * RG_BLOCK_W]
        rs.append(jnp.dot(xh, wa_ref[h].astype(BF16), preferred_element_type=F32))
        is_.append(jnp.dot(xh, wi_ref[h].astype(BF16), preferred_element_type=F32))
    r = _sigmoid(jnp.concatenate(rs, axis=1) + ba_ref[...])
    ig = _sigmoid(jnp.concatenate(is_, axis=1) + bi_ref[...])
    log_a = (-RG_C * r) * _softplus(-lam_ref[...])
    a = jnp.exp(log_a)
    u = jnp.sqrt(-jnp.tanh(log_a) * (a * a + 1.0)) * (ig * xc)
    return a, u


def _rglru_prompt_body(x_ref, gate_ref, cw_ref, cb_ref, wa_ref, ba_ref, wi_ref, bi_ref, lam_ref,
                       y_ref, h_ref, cs_ref, ext_ref, a_ref, u_ref, car_ref):
    c = pl.program_id(1)
    tl = x_ref.shape[0]

    @pl.when(c == 0)
    def _():
        ext_ref[0:8, :] = jnp.zeros((8, D_RG), F32)
        car_ref[...] = jnp.zeros((1, D_RG), F32)

    ext_ref[8:8 + tl, :] = x_ref[...]
    xc = cb_ref[...] + ext_ref[5:5 + tl, :] * cw_ref[0:1, :]
    xc = xc + ext_ref[6:6 + tl, :] * cw_ref[1:2, :]
    xc = xc + ext_ref[7:7 + tl, :] * cw_ref[2:3, :]
    xc = xc + ext_ref[8:8 + tl, :] * cw_ref[3:4, :]
    tail = ext_ref[tl:tl + 8, :]
    ext_ref[0:8, :] = tail

    a, u = _rg_gates(xc, wa_ref, ba_ref, wi_ref, bi_ref, lam_ref)
    a_ref[...] = a
    u_ref[...] = u

    row = lax.broadcasted_iota(I32, (8, D_RG), 0)

    def group(g, carry):
        r0 = pl.multiple_of(g * 8, 8)
        a8 = a_ref[pl.ds(r0, 8), :]
        u8 = u_ref[pl.ds(r0, 8), :]
        for s in (1, 2, 4):
            keep = row >= s
            a_sh = jnp.where(keep, pltpu.roll(a8, s, 0), 1.0)
            u_sh = jnp.where(keep, pltpu.roll(u8, s, 0), 0.0)
            u8 = a8 * u_sh + u8
            a8 = a8 * a_sh
        h8 = a8 * carry + u8
        u_ref[pl.ds(r0, 8), :] = h8
        return h8[7:8, :]

    carry = lax.fori_loop(0, tl // 8, group, car_ref[...])
    car_ref[...] = carry
    y_ref[...] = (u_ref[...] * _gelu_tanh(gate_ref[...])).astype(BF16)

    @pl.when(c == pl.num_programs(1) - 1)
    def _():
        h_ref[...] = carry
        cs_ref[...] = tail[5:8, :]


def _rglru_prompt(proj, batch, seq, conv_w, conv_b, wa, ba, wi, bi, lam):
    nc = seq // RG_TL
    t = batch * seq
    vec =pl.BlockSpec((1, D_RG), lambda b, c: (0, 0))
    blk = pl.BlockSpec((RG_BLOCKS, RG_BLOCK_W, RG_BLOCK_W), lambda b, c: (0, 0, 0))
    return pl.pallas_call(
        _rglru_prompt_body,
        grid=(batch, nc),
        in_specs=[
            pl.BlockSpec((RG_TL, D_RG), lambda b, c: (b * nc + c, 0)),
            pl.BlockSpec((RG_TL, D_RG), lambda b, c: (b * nc + c, 1)),
            pl.BlockSpec((CONV_W, D_RG), lambda b, c: (0, 0)),
            vec, blk, vec, blk, vec, vec,
        ],
        out_specs=[
            pl.BlockSpec((RG_TL, D_RG), lambda b, c: (b * nc + c, 0)),
            pl.BlockSpec((None, 1, D_RG), lambda b, c: (b, 0, 0)),
            pl.BlockSpec((None, CONV_W - 1, D_RG), lambda b, c: (b, 0, 0)),
        ],
        out_shape=[
            jax.ShapeDtypeStruct((t, D_RG), BF16),
            jax.ShapeDtypeStruct((batch, 1, D_RG), F32),
            jax.ShapeDtypeStruct((batch, CONV_W - 1, D_RG), F32),
        ],
        scratch_shapes=[
            pltpu.VMEM((RG_TL + 8, D_RG), F32),
            pltpu.VMEM((RG_TL, D_RG), F32),
            pltpu.VMEM((RG_TL, D_RG), F32),
            pltpu.VMEM((1, D_RG), F32),
        ],
        compiler_params=_cparams(("arbitrary", "arbitrary")),
        name="rglru_prompt",
    )(proj, proj, conv_w, conv_b, wa, ba, wi, bi, lam)


def _group_rmsnorm(y, g_row):
    outs = []
    for g in range(SSD_GROUPS):
        yg = y[:, g * D_BC:(g + 1) * D_BC]
        outs.append(yg * lax.rsqrt(jnp.mean(yg * yg, axis=-1, keepdims=True) + EPS))
    return jnp.concatenate(outs, axis=1) * g_row


def _ssd_prompt_body(xbc_ref, z_ref, dt_ref, cw_ref, cb_ref, dtb_ref, alog_ref, dexp_ref, ng_ref,
                     y_ref, s_out_ref, cs_ref, ext_ref, s_ref):
    c = pl.program_id(1)
    L = SSD_CHUNK

    @pl.when(c == 0)
    def _():
        ext_ref[0:8, :] = jnp.zeros((8, D_XBC), F32)
        s_ref[...] = jnp.zeros_like(s_ref)

    ext_ref[8:8 + L, :] = xbc_ref[...]
    xc = cb_ref[...] + ext_ref[5:5 + L, :] * cw_ref[0:1, :]
    xc = xc + ext_ref[6:6 + L, :] * cw_ref[1:2, :]
    xc = xc + ext_ref[7:7 + L, :] * cw_ref[2:3, :]
    xc = xc + ext_ref[8:8 + L, :] * cw_ref[3:4, :]
    tail = ext_ref[L:L + 8, :]
    ext_ref[0:8, :] = tail
    xc = _silu(xc)
    xs = xc[:, :D_SSD]
    bm = xc[:, D_SSD:D_SSD + D_BC].astype(BF16)
    cm = xc[:, D_SSD + D_BC:].astype(BF16)

    lane = lax.broadcasted_iota(I32, (1, LANES), 1)
    a_neg = jnp.where(lane < SSD_HEADS, -jnp.exp(alog_ref[...]), 0.0)
    dt = _softplus(dt_ref[...] + dtb_ref[...])
    da = dt * a_neg
    ri = lax.broadcasted_iota(I32, (L, L), 0)
    ci = lax.broadcasted_iota(I32, (L, L), 1)
    causal = ci <= ri
    tri = jnp.where(causal, 1.0, 0.0).astype(F32)
    acum = jnp.dot(tri, da, precision=HIGHEST, preferred_element_type=F32)
    acum_t = acum.T
    a_last = acum[L - 1:L, :]

    e_b16 = _head_expand_matrix(BF16)
    stacked = jnp.concatenate([dt, jnp.exp(a_last - acum), jnp.exp(acum)], axis=0)
    expd = _expand_heads(stacked, e_b16)
    dt_e, dend_e, ea_e = expd[0:L], expd[L:2 * L], expd[2 * L:3 * L]
    xdt = xs * dt_e
    xdt_b = xdt.astype(BF16)
    w_b = (xdt * dend_e)

    lane_l = lax.broadcasted_iota(I32, (L, LANES), 1)
    lo = lane_l < SSD_HEAD_DIM
    y_parts = []
    for g in range(SSD_GROUPS):
        cg = cm[:, g * SSD_STATE:(g + 1) * SSD_STATE]
        bg = bm[:, g * SSD_STATE:(g + 1) * SSD_STATE]
        cb = lax.dot_general(cg, bg, (((1,), (1,)), ((), ())), preferred_element_type=F32)
        hpg = SSD_HEADS // SSD_GROUPS
        for jp in range(hpg // 2):
            ms = []
            for h in (g * hpg + 2 * jp, g * hpg + 2 * jp + 1):
                seg = acum[:, h:h + 1] - acum_t[h:h + 1, :]
                decay = jnp.exp(jnp.where(causal, seg, -jnp.inf))
                ms.append((cb * decay).astype(BF16))
            col = (g * hpg + 2 * jp) * SSD_HEAD_DIM
            xp = xdt_b[:, col:col + LANES]
            zero = jnp.zeros_like(xp)
            rhs = jnp.concatenate([jnp.where(lo, xp, zero), jnp.where(lo, zero, xp)], axis=0)
            y_parts.append(jnp.dot(jnp.concatenate(ms, axis=1), rhs, preferred_element_type=F32))
    y_diag = jnp.concatenate(y_parts, axis=1)

    y_off_parts = []
    for g in range(SSD_GROUPS):
        cg = cm[:, g * SSD_STATE:(g + 1) * SSD_STATE]
        sg = s_ref[g * D_BC:(g + 1) * D_BC, :].astype(BF16)
        y_off_parts.append(lax.dot_general(cg, sg, (((1,), (1,)), ((), ())), preferred_element_type=F32))
    y_off = jnp.concatenate(y_off_parts, axis=1) * ea_e

    cd_col = jnp.exp(acum_t[:, L - 1:L])
    for g in range(SSD_GROUPS):
        bg = bm[:, g * SSD_STATE:(g + 1) * SSD_STATE]
        wg_t = w_b[:, g * D_BC:(g + 1) * D_BC].T.astype(BF16)
        upd = jnp.dot(wg_t, bg, preferred_element_type=F32)
        hpg = SSD_HEADS // SSD_GROUPS
        for e in range(hpg):
            h = g * hpg + e
            r0 = h * SSD_HEAD_DIM
            cd = jnp.broadcast_to(cd_col[h:h + 1, :], (SSD_HEAD_DIM, SSD_STATE))
            s_ref[r0:r0 + SSD_HEAD_DIM, :] = (cd * s_ref[r0:r0 + SSD_HEAD_DIM, :]
                                              + upd[e * SSD_HEAD_DIM:(e + 1) * SSD_HEAD_DIM, :])

    y = (y_diag + y_off + dexp_ref[...] * xs) * _silu(z_ref[...])
    y_ref[...] = _group_rmsnorm(y, ng_ref[...]).astype(BF16)

    @pl.when(c == pl.num_programs(1) - 1)
    def _():
        s_out_ref[...] = s_ref[...]
        cs_ref[...] = tail[5:8, :]


def _ssd_prompt(proj, dt_raw, batch, seq, conv_w, conv_b, dt_bias, a_log, d_exp, norm_g):
    nc = seq // SSD_CHUNK
    L = SSD_CHUNK
    hp = SSD_HEADS * SSD_HEAD_DIM
    const = lambda b, c: (0, 0)
    return pl.pallas_call(
        _ssd_prompt_body,
        grid=(batch, nc),
        in_specs=[
            pl.BlockSpec((L, D_XBC), lambda b, c: (b * nc + c, 2)),
            pl.BlockSpec((L, D_SSD), lambda b, c: (b * nc + c, 2)),
            pl.BlockSpec((L, LANES), lambda b, c: (b * nc + c, 0)),
            pl.BlockSpec((CONV_W, D_XBC), const),
            pl.BlockSpec((1, D_XBC), const),
            pl.BlockSpec((1, LANES), const),
            pl.BlockSpec((1, LANES), const),
            pl.BlockSpec((1, D_SSD), const),
            pl.BlockSpec((1, D_SSD), const),
        ],
        out_specs=[
            pl.BlockSpec((L, D_SSD), lambda b, c: (b * nc + c, 0)),
            pl.BlockSpec((None, hp, SSD_STATE), lambda b, c: (b, 0, 0)),
            pl.BlockSpec((None, CONV_W - 1, D_XBC), lambda b, c: (b, 0, 0)),
        ],
        out_shape=[
            jax.ShapeDtypeStruct((batch * seq, D_SSD), BF16),
            jax.ShapeDtypeStruct((batch, hp, SSD_STATE), F32),
            jax.ShapeDtypeStruct((batch, CONV_W - 1, D_XBC), F32),
        ],
        scratch_shapes=[
            pltpu.VMEM((L + 8, D_XBC), F32),
            pltpu.VMEM((hp, SSD_STATE), F32),
        ],
        compiler_params=_cparams(("arbitrary", "arbitrary")),
        name="ssd_prompt",
    )(proj, proj, dt_raw, conv_w, conv_b, dt_bias, a_log, d_exp, norm_g)


def _mix_sample_body(p_ref, dt_ref, h0_ref, rgc0_ref, s0_ref, sc0_ref,
                     rcw_ref, rcb_ref, wa_ref, ba_ref, wi_ref, bi_ref, lam_ref,
                     scw_ref, scb_ref, dtb_ref, alog_ref, dexp_ref, ng_ref,
                     y_ref, h_ref, rgc_ref, s_ref, sc_ref):
    nl = p_ref.shape[1] // D_MAIN
    bb = p_ref.shape[0]
    o_gate, o_z, o_xbc = D_RG, 2 * D_RG, 2 * D_RG + D_SSD

    def col(l, off, width):
        return p_ref[:, l * D_MAIN + off:l * D_MAIN + off + width]

    hist = [rgc0_ref[k] for k in range(CONV_W - 1)]
    raw = hist + [col(l, 0, D_RG) for l in range(nl)]
    xcs = []
    for l in range(nl):
        acc = rcb_ref[...] + raw[l] * rcw_ref[0:1, :]
        for k in range(1, CONV_W):
            acc = acc + raw[l + k] * rcw_ref[k:k + 1, :]
        xcs.append(acc)
    a, u = _rg_gates(jnp.concatenate(xcs, axis=0), wa_ref, ba_ref, wi_ref, bi_ref, lam_ref)
    h = h0_ref[...]
    for l in range(nl):
        h = a[l * bb:(l + 1) * bb] * h + u[l * bb:(l + 1) * bb]
        y_ref[:, l * 2 * D_MODEL:l * 2 * D_MODEL + D_RG] = h * _gelu_tanh(col(l, o_gate, D_RG))
    h_ref[...] = h
    for k in range(CONV_W - 1):
        rgc_ref[k] = raw[nl + k]

    hist = [sc0_ref[k] for k in range(CONV_W - 1)]
    raw = hist + [col(l, o_xbc, D_XBC) for l in range(nl)]
    for k in range(CONV_W - 1):
        sc_ref[k] = raw[nl + k]
    xcs = []
    for l in range(nl):
        acc = scb_ref[...] + raw[l] * scw_ref[0:1, :]
        for k in range(1, CONV_W):
            acc = acc + raw[l + k] * scw_ref[k:k + 1, :]
        xcs.append(_silu(acc))
    xs = [x[:, :D_SSD] for x in xcs]
    bms = [x[:, D_SSD:D_SSD + D_BC].astype(BF16) for x in xcs]
    cms = [x[:, D_SSD + D_BC:].astype(BF16) for x in xcs]

    lane = lax.broadcasted_iota(I32, (1, LANES), 1)
    a_neg = jnp.where(lane < SSD_HEADS, -jnp.exp(alog_ref[...]), 0.0)
    dts = [_softplus(dt_ref[:, l * LANES:(l + 1) * LANES] + dtb_ref[...]) for l in range(nl)]
    acums = []
    run = jnp.zeros((bb, LANES), F32)
    for l in range(nl):
        run = run + dts[l] * a_neg
        acums.append(run)
    a_last = acums[nl - 1]

    e_b16 = _head_expand_matrix(BF16)
    stacked = jnp.concatenate(dts + [jnp.exp(a_last - ac) for ac in acums] + [jnp.exp(ac) for ac in acums], axis=0)
    expd = _expand_heads(stacked, e_b16)
    dt_e = [expd[l * bb:(l + 1) * bb] for l in range(nl)]
    dend_e = [expd[(nl + l) * bb:(nl + l + 1) * bb] for l in range(nl)]
    ea_e = [expd[(2 * nl + l) * bb:(2 * nl + l + 1) * bb] for l in range(nl)]
    xdt = [xs[l] * dt_e[l] for l in range(nl)]
    xdt_r = [x.astype(BF16).astype(F32) for x in xdt]
    rows = nl * bb
    pad_rows = LANES - rows
    w_all = jnp.concatenate([(xdt[l] * dend_e[l]) for l in range(nl)]
                            + [jnp.zeros((pad_rows, D_SSD), F32)], axis=0)
    c_all = jnp.concatenate(cms, axis=0)
    b_all = jnp.concatenate(bms + [jnp.zeros((pad_rows, D_BC), BF16)], axis=0)

    r = lax.broadcasted_iota(I32, (D_BC, LANES), 0)
    cidx = lax.broadcasted_iota(I32, (D_BC, LANES), 1)
    gsum = jnp.where((lax.shift_right_logical(r, 7) == lax.shift_right_logical(cidx, 3)) & (cidx < SSD_HEADS),
                     1.0, 0.0).astype(F32)
    pairs = [(l, s) for l in range(nl) for s in range(l + 1)]
    prods = jnp.concatenate([cms[l].astype(F32) * bms[s].astype(F32) for (l, s) in pairs], axis=0)
    cbh = jnp.dot(prods, gsum, precision=HIGHEST, preferred_element_type=F32)
    m_list = []
    for idx, (l, s) in enumerate(pairs):
        decay = jnp.exp(acums[l] - acums[s])
        m_list.append((cbh[idx * bb:(idx + 1) * bb] * decay).astype(BF16))
    m_e = jnp.dot(jnp.concatenate(m_list, axis=0), e_b16, preferred_element_type=F32)
    y_diag = []
    for l in range(nl):
        acc = None
        for idx, (ll, s) in enumerate(pairs):
            if ll != l:
                continue
            term = m_e[idx * bb:(idx + 1) * bb] * xdt_r[s]
            acc = term if acc is None else acc + term
        y_diag.append(acc)

    cd_t = jnp.concatenate([jnp.exp(a_last), jnp.zeros((LANES - bb, LANES), F32)], axis=0).T
    row_seq = lax.broadcasted_iota(I32, (rows, 1), 0) & (bb - 1)
    row_pad = lax.broadcasted_iota(I32, (LANES, 1), 0)
    lane_q = lax.broadcasted_iota(I32, (LANES, LANES), 1)
    hpg = SSD_HEADS // SSD_GROUPS

    def seq_step(q, y_off):
        mine = row_seq == q
        mine_pad = ((row_pad & (bb - 1)) == q) & (row_pad < rows)
        cd_q = jnp.sum(jnp.where(lane_q == q, cd_t, 0.0), axis=1, keepdims=True)
        cd_q = jnp.broadcast_to(cd_q, (LANES, SSD_STATE))
        parts = []
        for g in range(SSD_GROUPS):
            s0 = s0_ref[q, g * D_BC:(g + 1) * D_BC, :]
            cg = c_all[:, g * SSD_STATE:(g + 1) * SSD_STATE]
            yq = lax.dot_general(cg, s0.astype(BF16), (((1,), (1,)), ((), ())), preferred_element_type=F32)
            parts.append(yq)
            bg = b_all[:, g * SSD_STATE:(g + 1) * SSD_STATE]
            wq = jnp.where(mine_pad, w_all[:, g * D_BC:(g + 1) * D_BC], 0.0)
            upd = jnp.dot(wq.T.astype(BF16), bg, preferred_element_type=F32)
            for e in range(hpg):
                hh = g * hpg + e
                cd = jnp.broadcast_to(cd_q[hh:hh + 1, :], (SSD_HEAD_DIM, SSD_STATE))
                s_ref[q, hh * SSD_HEAD_DIM:(hh + 1) * SSD_HEAD_DIM, :] = (
                    cd * s0[e * SSD_HEAD_DIM:(e + 1) * SSD_HEAD_DIM, :]
                    + upd[e * SSD_HEAD_DIM:(e + 1) * SSD_HEAD_DIM, :])
        yq_all = jnp.concatenate(parts, axis=1)
        return jnp.where(mine, yq_all, y_off)

    y_off = lax.fori_loop(0, bb, seq_step, jnp.zeros((rows, D_SSD), F32))

    for l in range(nl):
        y = (y_diag[l] + y_off[l * bb:(l + 1) * bb] * ea_e[l] + dexp_ref[...] * xs[l]) * _silu(col(l, o_z, D_SSD))
        y_ref[:, l * 2 * D_MODEL + D_RG:(l + 1) * 2 * D_MODEL] = _group_rmsnorm(y, ng_ref[...])


def _mix_sample(proj4, dt4, row0, n_seq, h0, rgc0, s0, sc0, rg_params, ssd_params):
    bb = SAMPLE_BB
    nl = proj4.shape[1] // D_MAIN
    hp = SSD_HEADS * SSD_HEAD_DIM
    blk0 = row0 // bb
    const2 = lambda i: (0, 0)
    rg_specs = [
        pl.BlockSpec((CONV_W, D_RG), const2), pl.BlockSpec((1, D_RG), const2),
        pl.BlockSpec((RG_BLOCKS, RG_BLOCK_W, RG_BLOCK_W), lambda i: (0, 0, 0)), pl.BlockSpec((1, D_RG), const2),
        pl.BlockSpec((RG_BLOCKS, RG_BLOCK_W, RG_BLOCK_W), lambda i: (0, 0, 0)), pl.BlockSpec((1, D_RG), const2),
        pl.BlockSpec((1, D_RG), const2),
    ]
    ssd_specs = [
        pl.BlockSpec((CONV_W, D_XBC), const2), pl.BlockSpec((1, D_XBC), const2),
        pl.BlockSpec((1, LANES), const2), pl.BlockSpec((1, LANES), const2),
        pl.BlockSpec((1, D_SSD), const2), pl.BlockSpec((1, D_SSD), const2),
    ]
    return pl.pallas_call(
        _mix_sample_body,
        grid=(n_seq // bb,),
        in_specs=[
            pl.BlockSpec((bb, nl * D_MAIN), lambda i: (blk0 + i, 0)),
            pl.BlockSpec((bb, nl * LANES), lambda i: (blk0 + i, 0)),
            pl.BlockSpec((bb, D_RG), lambda i: (i, 0)),
            pl.BlockSpec((CONV_W - 1, bb, D_RG), lambda i: (0, i, 0)),
            pl.BlockSpec((bb, hp, SSD_STATE), lambda i: (i, 0, 0)),
            pl.BlockSpec((CONV_W - 1, bb, D_XBC), lambda i: (0, i, 0)),
        ] + rg_specs + ssd_specs,
        out_specs=[
            pl.BlockSpec((bb, nl * 2 * D_MODEL), lambda i: (i, 0)),
            pl.BlockSpec((bb, D_RG), lambda i: (i, 0)),
            pl.BlockSpec((CONV_W - 1, bb, D_RG), lambda i: (0, i, 0)),
            pl.BlockSpec((bb, hp, SSD_STATE), lambda i: (i, 0, 0)),
            pl.BlockSpec((CONV_W - 1, bb, D_XBC), lambda i: (0, i, 0)),
        ],
        out_shape=[
            jax.ShapeDtypeStruct((n_seq, nl * 2 * D_MODEL), F32),
            jax.ShapeDtypeStruct((n_seq, D_RG), F32),
            jax.ShapeDtypeStruct((CONV_W - 1, n_seq, D_RG), F32),
            jax.ShapeDtypeStruct((n_seq, hp, SSD_STATE), F32),
            jax.ShapeDtypeStruct((CONV_W - 1, n_seq, D_XBC), F32),
        ],
        compiler_params=_cparams(("arbitrary",)),
        name="mix_sample",
    )(proj4, dt4, h0, rgc0, s0, sc0, *rg_params, *ssd_params)


def _route_body(x_ref, g_ref, wr_ref, br_ref, hp_ref, e_ref, gate_ref, rank_ref, cnt_ref, run_ref):
    i = pl.program_id(0)
    tm = x_ref.shape[0]

    @pl.when(i == 0)
    def _():
        run_ref[...] = jnp.zeros_like(run_ref)

    x = x_ref[...]
    h = x * lax.rsqrt(jnp.mean(x * x, axis=-1, keepdims=True) + EPS) * g_ref[...]
    hp_ref[...] = h

    lane = lax.broadcasted_iota(I32, (tm, LANES), 1).astype(F32)
    logits = jnp.dot(h, wr_ref[...], precision=HIGHEST, preferred_element_type=F32) + br_ref[...]
    work = jnp.where(lane < N_EXPERTS, logits, -jnp.inf)
    vals, idxs = [], []
    multi = jnp.zeros((tm, LANES), F32)
    for _ in range(TOP_K):
        m = jnp.max(work, axis=-1, keepdims=True)
        idx = jnp.min(jnp.where(work == m, lane, float(LANES)), axis=-1, keepdims=True)
        hit = lane == idx
        vals.append(m)
        idxs.append(idx)
        multi = jnp.where(hit, 1.0, multi)
        work = jnp.where(hit, -jnp.inf, work)
    ex = [jnp.exp(v - vals[0]) for v in vals]
    den = ex[0] + ex[1] + ex[2] + ex[3]
    gates = [e / den for e in ex]

    ri = lax.broadcasted_iota(I32, (tm, tm), 0)
    ci = lax.broadcasted_iota(I32, (tm, tm), 1)
    strict = jnp.where(ci < ri, 1.0, 0.0).astype(BF16)
    before = jnp.dot(strict, multi.astype(BF16), preferred_element_type=F32) + run_ref[...]
    ranks = [jnp.sum(jnp.where(lane == idx, before, 0.0), axis=-1, keepdims=True) for idx in idxs]
    run_ref[...] = run_ref[...] + jnp.sum(multi, axis=0, keepdims=True)

    lane4 = lax.broadcasted_iota(I32, (tm, TOP_K), 1)
    e_out = jnp.zeros((tm, TOP_K), I32)
    g_out = jnp.zeros((tm, TOP_K), F32)
    r_out = jnp.zeros((tm, TOP_K), I32)
    for k in range(TOP_K):
        e_out = jnp.where(lane4 == k, idxs[k].astype(I32), e_out)
        g_out = jnp.where(lane4 == k, gates[k], g_out)
        r_out = jnp.where(lane4 == k, ranks[k].astype(I32), r_out)
    e_ref[...] = e_out
    gate_ref[...] = g_out
    rank_ref[...] = r_out
    cnt_ref[...] = run_ref[...].astype(I32)


def _route(x1, g_ffn, w_router_pad, b_router_pad):
    t = x1.shape[0]
    tm = ROUTE_TM
    const = lambda i: (0, 0)
    return pl.pallas_call(
        _route_body,
        grid=(t // tm,),
        in_specs=[
            pl.BlockSpec((tm, D_MODEL), lambda i: (i, 0)),
            pl.BlockSpec((1, D_MODEL), const),
            pl.BlockSpec((D_MODEL, LANES), const),
            pl.BlockSpec((1, LANES), const),
        ],
        out_specs=[
            pl.BlockSpec((tm, D_MODEL), lambda i: (i, 0)),
            pl.BlockSpec((tm, TOP_K), lambda i: (i, 0)),
            pl.BlockSpec((tm, TOP_K), lambda i: (i, 0)),
            pl.BlockSpec((tm, TOP_K), lambda i: (i, 0)),
            pl.BlockSpec((1, LANES), const),
        ],
        out_shape=[
            jax.ShapeDtypeStruct((t, D_MODEL), F32),
            jax.ShapeDtypeStruct((t, TOP_K), I32),
            jax.ShapeDtypeStruct((t, TOP_K), F32),
            jax.ShapeDtypeStruct((t, TOP_K), I32),
            jax.ShapeDtypeStruct((1, LANES), I32),
        ],
        scratch_shapes=[pltpu.VMEM((1, LANES), F32)],
        compiler_params=_cparams(("arbitrary",)),
        name="route",
    )(x1, g_ffn, w_router_pad, b_router_pad)


def _scatter_body(seg_ref, cnt_ref, e_ref, rank_ref, h_ref, xs_ref, zero_ref, sem, zsem):
    i = pl.program_id(0)
    tm = h_ref.shape[0]
    n = e_ref.shape[0]

    def issue(blk, c):
        for u in range(DMA_UNROLL):
            a = blk * DMA_UNROLL + u
            pos = seg_ref[e_ref[a]] + rank_ref[a]
            t = lax.shift_right_logical(a, 2)
            pltpu.make_async_copy(h_ref.at[pl.ds(t, 1), :], xs_ref.at[pl.ds(pos, 1), :], sem).start()
        return c

    lax.fori_loop(0, n // DMA_UNROLL, issue, 0)
    for _ in range(TOP_K):
        pltpu.make_async_copy(h_ref, xs_ref.at[pl.ds(0, tm), :], sem).wait()

    @pl.when(i == pl.num_programs(0) - 1)
    def _():
        zero_ref[...] = jnp.zeros_like(zero_ref)
        pad = EXPERT_PAD

        def row_fill(row):
            return pltpu.make_async_copy(zero_ref.at[pl.ds(0, 1), :], xs_ref.at[pl.ds(row, 1), :], zsem)

        def block_fill(blk):
            dst = pl.multiple_of(blk * pad, pad)
            return pltpu.make_async_copy(zero_ref, xs_ref.at[pl.ds(dst, pad), :], zsem)

        def for_each(lo, hi, fn):
            def step(r, c):
                fn(r)
                return c
            lax.fori_loop(lo, hi, step, 0)

        def per_expert(e, used):
            cnt = cnt_ref[e]
            hi = ((cnt + pad - 1) // pad) * pad
            base = seg_ref[e]
            for_each(cnt, hi, lambda r: row_fill(base + r).start())
            for_each(cnt, hi, lambda r: row_fill(base + r).wait())
            return used + hi // pad

        used = lax.fori_loop(0, N_EXPERTS, per_expert, jnp.int32(0))
        n_blocks = xs_ref.shape[0] // pad
        for_each(used, n_blocks, lambda blk: block_fill(blk).start())
        for_each(used, n_blocks, lambda blk: block_fill(blk).wait())


def _scatter(seg_start, counts, e_flat, rank_flat, h2, n_slots):
    t = h2.shape[0]
    tm = SCATTER_TM
    grid_spec = pltpu.PrefetchScalarGridSpec(
        num_scalar_prefetch=2,
        grid=(t // tm,),
        in_specs=[
            pl.BlockSpec((tm * TOP_K,), lambda i, seg, cnt: (i,), memory_space=pltpu.SMEM),
            pl.BlockSpec((tm * TOP_K,), lambda i, seg, cnt: (i,), memory_space=pltpu.SMEM),
            pl.BlockSpec((tm, D_MODEL), lambda i, seg, cnt: (i, 0)),
        ],
        out_specs=pl.BlockSpec(memory_space=pl.ANY),
        scratch_shapes=[pltpu.VMEM((EXPERT_PAD, D_MODEL), F32), pltpu.SemaphoreType.DMA(()),
                        pltpu.SemaphoreType.DMA(())],
    )
    return pl.pallas_call(
        _scatter_body,
        grid_spec=grid_spec,
        out_shape=jax.ShapeDtypeStruct((n_slots, D_MODEL), F32),
        compiler_params=_cparams(("arbitrary",)),
        name="scatter_rows",
    )(seg_start, counts, e_flat, rank_flat, h2)


def _experts_body(sbe_ref, sbs_ref, sbn_ref, used_ref, xs_ref, wg_ref, wu_ref, bg_ref, bu_ref, wd_ref, bd_ref,
                  ys_ref, xin_ref, xbuf_ref, acc_ref, wgu_ref, wdb_ref, in_sem, out_sem):
    del sbe_ref
    sb = pl.program_id(0)
    j = pl.program_id(1)
    nj = pl.num_programs(1)
    n_sb = pl.num_programs(0)
    nblk = sbn_ref[sb]
    start = sbs_ref[sb]
    cb = EXPERT_PAD
    bm = EXPERT_BM
    tf = wd_ref.shape[0]

    def in_copy(which, r):
        src = pl.multiple_of(sbs_ref[which] + r * cb, cb)
        dst = pl.multiple_of(r * cb, cb)
        return pltpu.make_async_copy(xs_ref.at[pl.ds(src, cb), :], xin_ref.at[pl.ds(dst, cb), :], in_sem.at[r])

    def out_copy(r):
        src = pl.multiple_of(r * cb, cb)
        dst = pl.multiple_of(start + r * cb, cb)
        return pltpu.make_async_copy(acc_ref.at[pl.ds(src, cb), :], ys_ref.at[pl.ds(dst, cb), :], out_sem)

    def zero_copy(blk):
        dst = pl.multiple_of(blk * cb, cb)
        return pltpu.make_async_copy(acc_ref.at[pl.ds(0, cb), :], ys_ref.at[pl.ds(dst, cb), :], out_sem)

    def for_each(lo, hi, fn):
        def step(r, c):
            fn(r)
            return c
        lax.fori_loop(lo, hi, step, 0)

    @pl.when((j == 0) & (sb == 0))
    def _():
        for_each(0, nblk, lambda r: in_copy(0, r).start())

    @pl.when(j == 0)
    def _():
        def stage(r):
            in_copy(sb, r).wait()
            r0 = pl.multiple_of(r * cb, cb)
            xbuf_ref[pl.ds(r0, cb), :] = xin_ref[pl.ds(r0, cb), :].astype(BF16)
            acc_ref[pl.ds(r0, cb), :] = jnp.broadcast_to(bd_ref[...], (cb, D_MODEL))

        for_each(0, nblk, stage)

        @pl.when(sb + 1 < n_sb)
        def _():
            nxt = jnp.minimum(sb + 1, n_sb - 1)
            for_each(0, sbn_ref[nxt], lambda r: in_copy(nxt, r).start())

    @pl.when(nblk > 0)
    def _():
        def cast_rows(s):
            r = pl.multiple_of(s * cb, cb)
            wgu_ref[pl.ds(r, cb), 0:tf] = wg_ref[pl.ds(r, cb), :].astype(BF16)
            wgu_ref[pl.ds(r, cb), tf:2 * tf] = wu_ref[pl.ds(r, cb), :].astype(BF16)

        for_each(0, D_MODEL // cb, cast_rows)
        wdb_ref[...] = wd_ref[...].astype(BF16)
        bias_gu = jnp.concatenate([bg_ref[...], bu_ref[...]], axis=1)

        def sub(r0, rows):
            x = xbuf_ref[pl.ds(r0, rows), :]
            gu = jnp.dot(x, wgu_ref[...], preferred_element_type=F32) + bias_gu
            gate = jnp.minimum(gu[:, :tf], SWIGLU_LIMIT)
            up = jnp.clip(gu[:, tf:], -SWIGLU_LIMIT, SWIGLU_LIMIT)
            glu = gate * _sigmoid(SWIGLU_ALPHA * gate)
            act = ((up + 1.0) * glu).astype(BF16)
            acc_ref[pl.ds(r0, rows), :] += jnp.dot(act, wdb_ref[...], preferred_element_type=F32)

            @pl.when(j == nj - 1)
            def _():
                for b in range(rows // cb):
                    out_copy(r0 // cb + b).start()

        per_sub = bm // cb
        n_full = nblk // per_sub
        for_each(0, n_full, lambda i: sub(pl.multiple_of(i * bm, bm), bm))
        done = n_full * per_sub
        piece = per_sub // 2
        while piece >= 1:
            here = done

            @pl.when((nblk & piece) != 0)
            def _():
                sub(pl.multiple_of(here * cb, cb), piece * cb)

            done = done + (nblk & piece)
            piece //= 2

    @pl.when(j == nj - 1)
    def _():
        for_each(0, nblk, lambda r: out_copy(r).wait())

    @pl.when((sb == pl.num_programs(0) - 1) & (j == nj - 1))
    def _():
        acc_ref[0:cb, :] = jnp.zeros((cb, D_MODEL), F32)
        n_blocks = ys_ref.shape[0] // cb
        for_each(used_ref[0], n_blocks, lambda blk: zero_copy(blk).start())
        for_each(used_ref[0], n_blocks, lambda blk: zero_copy(blk).wait())


def _experts(sb_expert, sb_start, sb_nsub, used_blocks, xsorted, w_gu, b_gu, w_down, b_down):
    n_sb = sb_expert.shape[0]
    n_slots = xsorted.shape[0]
    tf = EXPERT_TF
    nj = D_FF // tf

    def spec(shape, fn):
        def index_map(sb, j, sbe, sbs, sbn, used):
            jj = jnp.where(sbn[sb] > 0, j, nj - 1)
            return fn(sbe[sb], jj)
        return pl.BlockSpec(shape, index_map)

    grid_spec = pltpu.PrefetchScalarGridSpec(
        num_scalar_prefetch=4,
        grid=(n_sb, nj),
        in_specs=[
            pl.BlockSpec(memory_space=pl.ANY),
            spec((None, D_MODEL, tf), lambda e, jj: (e, 0, jj)),
            spec((None, D_MODEL, tf), lambda e, jj: (e, 0, nj + jj)),
            spec((None, 1, tf), lambda e, jj: (e, 0, jj)),
            spec((None, 1, tf), lambda e, jj: (e, 0, nj + jj)),
            spec((None, tf, D_MODEL), lambda e, jj: (e, jj, 0)),
            spec((None, 1, D_MODEL), lambda e, jj: (e, 0, 0)),
        ],
        out_specs=pl.BlockSpec(memory_space=pl.ANY),
        scratch_shapes=[
            pltpu.VMEM((EXPERT_R, D_MODEL), F32),
            pltpu.VMEM((EXPERT_R, D_MODEL), BF16),
            pltpu.VMEM((EXPERT_R, D_MODEL), F32),
            pltpu.VMEM((D_MODEL, 2 * tf), BF16),
            pltpu.VMEM((tf, D_MODEL), BF16),
            pltpu.SemaphoreType.DMA((EXPERT_R // EXPERT_PAD,)),
            pltpu.SemaphoreType.DMA(()),
        ],
    )
    return pl.pallas_call(
        _experts_body,
        grid_spec=grid_spec,
        out_shape=jax.ShapeDtypeStruct((n_slots, D_MODEL), F32),
        compiler_params=_cparams(("arbitrary", "arbitrary")),
        name="experts",
    )(sb_expert, sb_start, sb_nsub, used_blocks, xsorted, w_gu, w_gu, b_gu, b_gu, w_down, b_down)


def _combine_body(seg_ref, e_ref, rank_ref, x1_ref, gate_ref, gf_ref, ys_ref, yp_ref, ysm_ref, buf_ref, sem,
                  *, n_prompt_tiles):
    i = pl.program_id(0)
    tm = x1_ref.shape[0]
    n = e_ref.shape[0]

    def issue(blk, c):
        for u in range(DMA_UNROLL):
            a = blk * DMA_UNROLL + u
            pos = seg_ref[e_ref[a]] + rank_ref[a]
            t = lax.shift_right_logical(a, 2)
            pltpu.make_async_copy(ys_ref.at[pl.ds(pos, 1), :], buf_ref.at[u % TOP_K, pl.ds(t, 1), :], sem).start()
        return c

    lax.fori_loop(0, n // DMA_UNROLL, issue, 0)
    for k in range(TOP_K):
        pltpu.make_async_copy(ys_ref.at[pl.ds(0, tm), :], buf_ref.at[k], sem).wait()

    g = gate_ref[...]
    acc = buf_ref[0] * g[:, 0:1]
    for k in range(1, TOP_K):
        acc = acc + buf_ref[k] * g[:, k:k + 1]
    x = x1_ref[...] + acc
    y = x * lax.rsqrt(jnp.mean(x * x, axis=-1, keepdims=True) + EPS) * gf_ref[...]

    @pl.when(i < n_prompt_tiles)
    def _():
        yp_ref[...] = y

    @pl.when(i >= n_prompt_tiles)
    def _():
        ysm_ref[...] = y


def _combine(seg_start, e_flat, rank_flat, x1, gates, g_final, ysorted, t_prompt):
    t = x1.shape[0]
    tm = COMBINE_TM
    npt = t_prompt // tm
    grid_spec = pltpu.PrefetchScalarGridSpec(
        num_scalar_prefetch=1,
        grid=(t // tm,),
        in_specs=[
            pl.BlockSpec((tm * TOP_K,), lambda i, seg: (i,), memory_space=pltpu.SMEM),
            pl.BlockSpec((tm * TOP_K,), lambda i, seg: (i,), memory_space=pltpu.SMEM),
            pl.BlockSpec((tm, D_MODEL), lambda i, seg: (i, 0)),
            pl.BlockSpec((tm, TOP_K), lambda i, seg: (i, 0)),
            pl.BlockSpec((1, D_MODEL), lambda i, seg: (0, 0)),
            pl.BlockSpec(memory_space=pl.ANY),
        ],
        out_specs=[
            pl.BlockSpec((tm, D_MODEL), lambda i, seg: (jnp.minimum(i, npt - 1), 0)),
            pl.BlockSpec((tm, D_MODEL), lambda i, seg: (jnp.maximum(i - npt, 0), 0)),
        ],
        scratch_shapes=[pltpu.VMEM((TOP_K, tm, D_MODEL), F32), pltpu.SemaphoreType.DMA(())],
    )
    return pl.pallas_call(
        functools.partial(_combine_body, n_prompt_tiles=npt),
        grid_spec=grid_spec,
        out_shape=[
            jax.ShapeDtypeStruct((t_prompt, D_MODEL), F32),
            jax.ShapeDtypeStruct((t - t_prompt, D_MODEL), F32),
        ],
        compiler_params=_cparams(("arbitrary",)),
        name="combine",
    )(seg_start, e_flat, rank_flat, x1, gates, g_final, ysorted)


def _expert_tables(counts, n_assign):
    pad = EXPERT_PAD
    padded = ((counts + pad - 1) // pad) * pad
    seg_start = jnp.cumsum(padded) - padded
    n_sb_e = (counts + EXPERT_R - 1) // EXPERT_R
    sb_cum = jnp.cumsum(n_sb_e)
    n_sb = n_assign // EXPERT_R + N_EXPERTS
    s = jnp.arange(n_sb, dtype=I32)
    total = sb_cum[-1]
    s_eff = jnp.minimum(s, total - 1)
    e = jnp.sum((sb_cum[None, :] <= s_eff[:, None]).astype(I32), axis=1)
    k = s_eff - (sb_cum[e] - n_sb_e[e])
    rows = jnp.clip(counts[e] - k * EXPERT_R, 0, EXPERT_R)
    valid = s < total
    nsub = jnp.where(valid, (rows + pad - 1) // pad, 0)
    start = seg_start[e] + k * EXPERT_R
    used_blocks = (jnp.sum(padded) // pad).reshape(1)
    return seg_start.astype(I32), e, start.astype(I32), nsub.astype(I32), used_blocks.astype(I32)


def kernel(x_prompt, x_sample, state_rglru_h, state_rglru_conv, state_ssd, state_ssd_conv, g_mix, w_in, conv_rg_w, conv_rg_b, rg_wa, rg_ba, rg_wi, rg_bi, rg_lambda, conv_ssd_w, conv_ssd_b, ssd_dt_bias, ssd_a_log, ssd_d, ssd_norm_g, w_out, g_ffn, w_router, b_router, w_gu, b_gu, w_down, b_down, g_final):
    depth = g_mix.shape[0]
    assert depth == 1
    bp, lp, _ = x_prompt.shape
    bs, ls, _ = x_sample.shape
    tp, ts = bp * lp, bs * ls
    t = tp + ts
    hp = SSD_HEADS * SSD_HEAD_DIM
    l = 0

    xp = x_prompt.reshape(tp, D_MODEL)
    xs = x_sample.reshape(ts, D_MODEL)
    w_in_t = jnp.transpose(w_in[l])
    row = lambda v: v.reshape(1, -1)
    pad_heads = lambda v: jnp.pad(v.reshape(1, -1), ((0, 0), (0, LANES - SSD_HEADS)))

    hb, dt_raw = _norm_dt(xp, xs, row(g_mix[l]), w_in_t)
    proj = _in_proj(hb, w_in_t)

    rg_params = (conv_rg_w[l], row(conv_rg_b[l]), rg_wa[l], row(rg_ba[l]), rg_wi[l], row(rg_bi[l]),
                 row(rg_lambda[l]))
    d_exp = row(jnp.repeat(ssd_d[l], SSD_HEAD_DIM))
    ssd_params = (conv_ssd_w[l], row(conv_ssd_b[l]), pad_heads(ssd_dt_bias[l]), pad_heads(ssd_a_log[l]),
                  d_exp, row(ssd_norm_g[l]))

    y_rg, rgh_p, rgc_p = _rglru_prompt(proj, bp, lp, *rg_params)
    y_ssd, ssd_p, ssdc_p = _ssd_prompt(proj, dt_raw, bp, lp, *ssd_params)

    proj4 = proj[tp:].reshape(bs, ls * D_MAIN)
    dt4 = dt_raw[tp:].reshape(bs, ls * LANES)
    mix_s4, rgh_s, rgc_s, ssd_s, ssdc_s = _mix_sample(
        proj4, dt4, 0, bs,
        state_rglru_h[l], jnp.transpose(state_rglru_conv[l], (1, 0, 2)),
        state_ssd[l].reshape(bs, hp, SSD_STATE), jnp.transpose(state_ssd_conv[l], (1, 0, 2)),
        rg_params, ssd_params)
    mix_s = mix_s4.reshape(ts, 2 * D_MODEL)

    x1 = _out_proj(y_rg, y_ssd, mix_s, w_out[l], xp, xs)

    wr = jnp.pad(w_router[l], ((0, 0), (0, LANES - N_EXPERTS)))
    br = jnp.pad(b_router[l].reshape(1, -1), ((0, 0), (0, LANES - N_EXPERTS)))
    h2, e_idx, gates, rank, counts = _route(x1, row(g_ffn[l]), wr, br)

    n_assign = t * TOP_K
    n_slots = n_assign + N_EXPERTS * EXPERT_PAD
    counts = counts[0, :N_EXPERTS]
    seg_start, sb_expert, sb_start, sb_nsub, used_blocks = _expert_tables(counts, n_assign)
    e_flat = e_idx.reshape(-1)
    rank_flat = rank.reshape(-1)
    xsorted = _scatter(seg_start, counts, e_flat, rank_flat, h2, n_slots)
    ysorted = _experts(sb_expert, sb_start, sb_nsub, used_blocks, xsorted, w_gu[l], b_gu[l].reshape(N_EXPERTS, 1, -1),
                       w_down[l], b_down[l].reshape(N_EXPERTS, 1, -1))
    y_p, y_s = _combine(seg_start, e_flat, rank_flat, x1, gates, row(g_final), ysorted, tp)

    return (y_p.reshape(x_prompt.shape).astype(x_prompt.dtype),
            y_s.reshape(x_sample.shape).astype(x_sample.dtype),
            rgh_p.reshape(depth, bp, D_RG),
            rgc_p.reshape(depth, bp, CONV_W - 1, D_RG),
            ssd_p.reshape(depth, bp, SSD_HEADS, SSD_HEAD_DIM, SSD_STATE),
            ssdc_p.reshape(depth, bp, CONV_W - 1, D_XBC),
            rgh_s.reshape(depth, bs, D_RG),
            jnp.transpose(rgc_s, (1, 0, 2)).reshape(depth, bs, CONV_W - 1, D_RG),
            ssd_s.reshape(depth, bs, SSD_HEADS, SSD_HEAD_DIM, SSD_STATE),
            jnp.transpose(ssdc_s, (1, 0, 2)).reshape(depth, bs, CONV_W - 1, D_XBC))
```

```python
import functools

import jax
import jax.numpy as jnp
from jax import lax
from jax.experimental import pallas as pl
from jax.experimental.pallas import tpu as pltpu

F32 = jnp.float32
BF16 = jnp.bfloat16
I32 = jnp.int32
HIGHEST = lax.Precision.HIGHEST

EPS = 1e-6
D_MODEL = 2048
D_RG = 2048
RG_BLOCKS = 16
RG_BLOCK_W = 128
RG_C = 8.0
D_SSD = 2048
SSD_HEAD_DIM = 64
SSD_HEADS = 32
SSD_GROUPS = 4
SSD_STATE = 128
SSD_CHUNK = 128
CONV_W = 4
D_BC = SSD_GROUPS * SSD_STATE
D_XBC = D_SSD + 2 * D_BC
D_MAIN = 2 * D_RG + D_SSD + D_XBC
N_EXPERTS = 32
TOP_K = 4
D_FF = 2048
SWIGLU_ALPHA = 1.702
SWIGLU_LIMIT = 7.0

LANES = 128
VMEM_LIMIT = 56 * 1024 * 1024

ROW_TILE = 512
PROJ_TM = 1088
PROJ_TN = 1024
OUT_TN = 512
RG_TL = 256
SAMPLE_BB = 8
ROUTE_TM = 512
SCATTER_TM = 256
COMBINE_TM = 256
EXPERT_PAD = 128
EXPERT_BM = 512
EXPERT_R = 1152
EXPERT_TF = 512
DMA_UNROLL = 8


def _cparams(sem, vmem=VMEM_LIMIT):
    return pltpu.CompilerParams(dimension_semantics=sem, vmem_limit_bytes=vmem)


def _softplus(x):
    return jnp.maximum(x, 0.0) + jnp.log1p(jnp.exp(-jnp.abs(x)))


_sigmoid = jax.nn.sigmoid
_silu = jax.nn.silu


def _gelu_tanh(x):
    return jax.nn.gelu(x, approximate=True)


def _head_expand_matrix(dtype):
    r = lax.broadcasted_iota(I32, (LANES, D_SSD), 0)
    c = lax.broadcasted_iota(I32, (LANES, D_SSD), 1)
    return jnp.where(lax.shift_right_logical(c, 6) == r, 1.0, 0.0).astype(dtype)


def _expand_heads(v, e_b16):
    rows = v.shape[0]
    hi = v.astype(BF16)
    r1 = v - hi.astype(F32)
    mid = r1.astype(BF16)
    lo = (r1 - mid.astype(F32)).astype(BF16)
    o = jnp.dot(jnp.concatenate([hi, mid, lo], axis=0), e_b16, preferred_element_type=F32)
    return (o[0:rows] + o[rows:2 * rows]) + o[2 * rows:3 * rows]


def _norm_dt_body(xp_ref, xs_ref, g_ref, wdt_ref, hb_ref, dt_ref, *, n_prompt_tiles):
    i = pl.program_id(0)

    def run(x_ref):
        x = x_ref[...]
        h = x * lax.rsqrt(jnp.mean(x * x, axis=-1, keepdims=True) + EPS) * g_ref[...]
        hb = h.astype(BF16)
        hb_ref[...] = hb
        r = lax.broadcasted_iota(I32, (LANES, 1), 0)
        wdt = jnp.where(r < SSD_HEADS, wdt_ref[...], 0.0).astype(BF16)
        dt_ref[...] = lax.dot_general(hb, wdt, (((1,), (1,)), ((), ())), preferred_element_type=F32)

    @pl.when(i < n_prompt_tiles)
    def _():
        run(xp_ref)

    @pl.when(i >= n_prompt_tiles)
    def _():
        run(xs_ref)


def _norm_dt(xp, xs, g, w_in_t):
    tp, ts = xp.shape[0], xs.shape[0]
    npt, nst = tp // ROW_TILE, ts // ROW_TILE
    t = tp + ts
    return pl.pallas_call(
        functools.partial(_norm_dt_body, n_prompt_tiles=npt),
        grid=(npt + nst,),
        in_specs=[
            pl.BlockSpec((ROW_TILE, D_MODEL), lambda i: (jnp.minimum(i, npt - 1), 0)),
            pl.BlockSpec((ROW_TILE, D_MODEL), lambda i: (jnp.maximum(i - npt, 0), 0)),
            pl.BlockSpec((1, D_MODEL), lambda i: (0, 0)),
            pl.BlockSpec((LANES, D_MODEL), lambda i: (D_MAIN // LANES, 0)),
        ],
        out_specs=[
            pl.BlockSpec((ROW_TILE, D_MODEL), lambda i: (i, 0)),
            pl.BlockSpec((ROW_TILE, LANES), lambda i: (i, 0)),
        ],
        out_shape=[jax.ShapeDtypeStruct((t, D_MODEL), BF16), jax.ShapeDtypeStruct((t, LANES), F32)],
        compiler_params=_cparams(("parallel",)),
        name="norm_dt",
    )(xp, xs, g, w_in_t)


def _cast_weight_tile(w_ref, wb_ref, rows_per_step=256):
    k = w_ref.shape[0]

    def step(s, c):
        r = pl.multiple_of(s * rows_per_step, rows_per_step)
        wb_ref[pl.ds(r, rows_per_step), :] = w_ref[pl.ds(r, rows_per_step), :].astype(BF16)
        return c

    lax.fori_loop(0, k // rows_per_step, step, 0)


def _in_proj_body(x_ref, w_ref, o_ref, wb_ref):
    @pl.when(pl.program_id(1) == 0)
    def _():
        _cast_weight_tile(w_ref, wb_ref)

    o_ref[...] = lax.dot_general(x_ref[...], wb_ref[...], (((1,), (1,)), ((), ())), preferred_element_type=F32)


def _in_proj(hb, w_in_t):
    t = hb.shape[0]
    return pl.pallas_call(
        _in_proj_body,
        grid=(D_MAIN // PROJ_TN, t // PROJ_TM),
        in_specs=[
            pl.BlockSpec((PROJ_TM, D_MODEL), lambda j, i: (i, 0)),
            pl.BlockSpec((PROJ_TN, D_MODEL), lambda j, i: (j, 0)),
        ],
        out_specs=pl.BlockSpec((PROJ_TM, PROJ_TN), lambda j, i: (i, j)),
        out_shape=jax.ShapeDtypeStruct((t, D_MAIN), F32),
        scratch_shapes=[pltpu.VMEM((PROJ_TN, D_MODEL), BF16)],
        compiler_params=_cparams(("arbitrary", "arbitrary")),
        name="in_proj",
    )(hb, w_in_t)


def _out_proj_body(rg_ref, ssd_ref, ms_ref, w_ref, xp_ref, xs_ref, o_ref, wb_ref, *, n_prompt_tiles):
    i = pl.program_id(1)

    @pl.when(i == 0)
    def _():
        _cast_weight_tile(w_ref, wb_ref)

    @pl.when(i < n_prompt_tiles)
    def _():
        m = jnp.concatenate([rg_ref[...], ssd_ref[...]], axis=1)
        o_ref[...] = xp_ref[...] + jnp.dot(m, wb_ref[...], preferred_element_type=F32)

    @pl.when(i >= n_prompt_tiles)
    def _():
        o_ref[...] = xs_ref[...] + jnp.dot(ms_ref[...].astype(BF16), wb_ref[...], preferred_element_type=F32)


def _out_proj(y_rg, y_ssd, mix_s, w_out, xp, xs):
    tp, ts = y_rg.shape[0], mix_s.shape[0]
    npt, nst = tp // ROW_TILE, ts // ROW_TILE
    prompt_rows = lambda j, i: (jnp.minimum(i, npt - 1), 0)
    return pl.pallas_call(
        functools.partial(_out_proj_body, n_prompt_tiles=npt),
        grid=(D_MODEL // OUT_TN, npt + nst),
        in_specs=[
            pl.BlockSpec((ROW_TILE, D_MODEL), prompt_rows),
            pl.BlockSpec((ROW_TILE, D_MODEL), prompt_rows),
            pl.BlockSpec((ROW_TILE, 2 * D_MODEL), lambda j, i: (jnp.maximum(i - npt, 0), 0)),
            pl.BlockSpec((2 * D_MODEL, OUT_TN), lambda j, i: (0, j)),
            pl.BlockSpec((ROW_TILE, OUT_TN), lambda j, i: (jnp.minimum(i, npt - 1), j)),
            pl.BlockSpec((ROW_TILE, OUT_TN), lambda j, i: (jnp.maximum(i - npt, 0), j)),
        ],
        out_specs=pl.BlockSpec((ROW_TILE, OUT_TN), lambda j, i: (i, j)),
        out_shape=jax.ShapeDtypeStruct((tp + ts, D_MODEL), F32),
        scratch_shapes=[pltpu.VMEM((2 * D_MODEL, OUT_TN), BF16)],
        compiler_params=_cparams(("arbitrary", "arbitrary")),
        name="out_proj",
    )(y_rg, y_ssd, mix_s, w_out, xp, xs)


def _rg_gates(xc, wa_ref, ba_ref, wi_ref, bi_ref, lam_ref):
    xcb = xc.astype(BF16)
    rs, is_ = [], []
    for h in range(RG_BLOCKS):
        xh = xcb[:, h * RG_BLOCK_W:(h + 1) * RG_BLOCK_W]
        rs.append(jnp.dot(xh, wa_ref[h].astype(BF16), preferred_element_type=F32))
        is_.append(jnp.dot(xh, wi_ref[h].astype(BF16), preferred_element_type=F32))
    r = _sigmoid(jnp.concatenate(rs, axis=1) + ba_ref[...])
    ig = _sigmoid(jnp.concatenate(is_, axis=1) + bi_ref[...])
    log_a = (-RG_C * r) * _softplus(-lam_ref[...])
    a = jnp.exp(log_a)
    u = jnp.sqrt(-jnp.tanh(log_a) * (a * a + 1.0)) * (ig * xc)
    return a, u


def _rglru_prompt_body(x_ref, gate_ref, cw_ref, cb_ref, wa_ref, ba_ref, wi_ref, bi_ref, lam_ref,
                       y_ref, h_ref, cs_ref, ext_ref, a_ref, u_ref, car_ref):
    c = pl.program_id(1)
    tl = x_ref.shape[0]

    @pl.when(c == 0)
    def _():
        ext_ref[0:8, :] = jnp.zeros((8, D_RG), F32)
        car_ref[...] = jnp.zeros((1, D_RG), F32)

    ext_ref[8:8 + tl, :] = x_ref[...]
    xc = cb_ref[...] + ext_ref[5:5 + tl, :] * cw_ref[0:1, :]
    xc = xc + ext_ref[6:6 + tl, :] * cw_ref[1:2, :]
    xc = xc + ext_ref[7:7 + tl, :] * cw_ref[2:3, :]
    xc = xc + ext_ref[8:8 + tl, :] * cw_ref[3:4, :]
    tail = ext_ref[tl:tl + 8, :]
    ext_ref[0:8, :] = tail

    a, u = _rg_gates(xc, wa_ref, ba_ref, wi_ref, bi_ref, lam_ref)
    a_ref[...] = a
    u_ref[...] = u

    row = lax.broadcasted_iota(I32, (8, D_RG), 0)

    def group(g, carry):
        r0 = pl.multiple_of(g * 8, 8)
        a8 = a_ref[pl.ds(r0, 8), :]
        u8 = u_ref[pl.ds(r0, 8), :]
        for s in (1, 2, 4):
            keep = row >= s
            a_sh = jnp.where(keep, pltpu.roll(a8, s, 0), 1.0)
            u_sh = jnp.where(keep, pltpu.roll(u8, s, 0), 0.0)
            u8 = a8 * u_sh + u8
            a8 = a8 * a_sh
        h8 = a8 * carry + u8
        u_ref[pl.ds(r0, 8), :] = h8
        return h8[7:8, :]

    carry = lax.fori_loop(0, tl // 8, group, car_ref[...])
    car_ref[...] = carry
    y_ref[...] = (u_ref[...] * _gelu_tanh(gate_ref[...])).astype(BF16)

    @pl.when(c == pl.num_programs(1) - 1)
    def _():
        h_ref[...] = carry
        cs_ref[...] = tail[5:8, :]


def _rglru_prompt(proj, batch, seq, conv_w, conv_b, wa, ba, wi, bi, lam):
    nc = seq // RG_TL
    t = batch * seq
    vec =pl.BlockSpec((1, D_RG), lambda b, c: (0, 0))
    blk = pl.BlockSpec((RG_BLOCKS, RG_BLOCK_W, RG_BLOCK_W), lambda b, c: (0, 0, 0))
    return pl.pallas_call(
        _rglru_prompt_body,
        grid=(batch, nc),
        in_specs=[
            pl.BlockSpec((RG_TL, D_RG), lambda b, c: (b * nc + c, 0)),
            pl.BlockSpec((RG_TL, D_RG), lambda b, c: (b * nc + c, 1)),
            pl.BlockSpec((CONV_W, D_RG), lambda b, c: (0, 0)),
            vec, blk, vec, blk, vec, vec,
        ],
        out_specs=[
            pl.BlockSpec((RG_TL, D_RG), lambda b, c: (b * nc + c, 0)),
            pl.BlockSpec((None, 1, D_RG), lambda b, c: (b, 0, 0)),
            pl.BlockSpec((None, CONV_W - 1, D_RG), lambda b, c: (b, 0, 0)),
        ],
        out_shape=[
            jax.ShapeDtypeStruct((t, D_RG), BF16),
            jax.ShapeDtypeStruct((batch, 1, D_RG), F32),
            jax.ShapeDtypeStruct((batch, CONV_W - 1, D_RG), F32),
        ],
        scratch_shapes=[
            pltpu.VMEM((RG_TL + 8, D_RG), F32),
            pltpu.VMEM((RG_TL, D_RG), F32),
            pltpu.VMEM((RG_TL, D_RG), F32),
            pltpu.VMEM((1, D_RG), F32),
        ],
        compiler_params=_cparams(("arbitrary", "arbitrary")),
        name="rglru_prompt",
    )(proj, proj, conv_w, conv_b, wa, ba, wi, bi, lam)


def _group_rmsnorm(y, g_row):
    outs = []
    for g in range(SSD_GROUPS):
        yg = y[:, g * D_BC:(g + 1) * D_BC]
        outs.append(yg * lax.rsqrt(jnp.mean(yg * yg, axis=-1, keepdims=True) + EPS))
    return jnp.concatenate(outs, axis=1) * g_row


def _ssd_prompt_body(xbc_ref, z_ref, dt_ref, cw_ref, cb_ref, dtb_ref, alog_ref, dexp_ref, ng_ref,
                     y_ref, s_out_ref, cs_ref, ext_ref, s_ref):
    c = pl.program_id(1)
    L = SSD_CHUNK

    @pl.when(c == 0)
    def _():
        ext_ref[0:8, :] = jnp.zeros((8, D_XBC), F32)
        s_ref[...] = jnp.zeros_like(s_ref)

    ext_ref[8:8 + L, :] = xbc_ref[...]
    xc = cb_ref[...] + ext_ref[5:5 + L, :] * cw_ref[0:1, :]
    xc = xc + ext_ref[6:6 + L, :] * cw_ref[1:2, :]
    xc = xc + ext_ref[7:7 + L, :] * cw_ref[2:3, :]
    xc = xc + ext_ref[8:8 + L, :] * cw_ref[3:4, :]
    tail = ext_ref[L:L + 8, :]
    ext_ref[0:8, :] = tail
    xc = _silu(xc)
    xs = xc[:, :D_SSD]
    bm = xc[:, D_SSD:D_SSD + D_BC].astype(BF16)
    cm = xc[:, D_SSD + D_BC:].astype(BF16)

    lane = lax.broadcasted_iota(I32, (1, LANES), 1)
    a_neg = jnp.where(lane < SSD_HEADS, -jnp.exp(alog_ref[...]), 0.0)
    dt = _softplus(dt_ref[...] + dtb_ref[...])
    da = dt * a_neg
    ri = lax.broadcasted_iota(I32, (L, L), 0)
    ci = lax.broadcasted_iota(I32, (L, L), 1)
    causal = ci <= ri
    tri = jnp.where(causal, 1.0, 0.0).astype(F32)
    acum = jnp.dot(tri, da, precision=HIGHEST, preferred_element_type=F32)
    acum_t = acum.T
    a_last = acum[L - 1:L, :]

    e_b16 = _head_expand_matrix(BF16)
    stacked = jnp.concatenate([dt, jnp.exp(a_last - acum), jnp.exp(acum)], axis=0)
    expd = _expand_heads(stacked, e_b16)
    dt_e, dend_e, ea_e = expd[0:L], expd[L:2 * L], expd[2 * L:3 * L]
    xdt = xs * dt_e
    xdt_b = xdt.astype(BF16)
    w_b = (xdt * dend_e)

    lane_l = lax.broadcasted_iota(I32, (L, LANES), 1)
    lo = lane_l < SSD_HEAD_DIM
    y_parts = []
    for g in range(SSD_GROUPS):
        cg = cm[:, g * SSD_STATE:(g + 1) * SSD_STATE]
        bg = bm[:, g * SSD_STATE:(g + 1) * SSD_STATE]
        cb = lax.dot_general(cg, bg, (((1,), (1,)), ((), ())), preferred_element_type=F32)
        hpg = SSD_HEADS // SSD_GROUPS
        for jp in range(hpg // 2):
            ms = []
            for h in (g * hpg + 2 * jp, g * hpg + 2 * jp + 1):
                seg = acum[:, h:h + 1] - acum_t[h:h + 1, :]
                decay = jnp.exp(jnp.where(causal, seg, -jnp.inf))
                ms.append((cb * decay).astype(BF16))
            col = (g * hpg + 2 * jp) * SSD_HEAD_DIM
            xp = xdt_b[:, col:col + LANES]
            zero = jnp.zeros_like(xp)
            rhs = jnp.concatenate([jnp.where(lo, xp, zero), jnp.where(lo, zero, xp)], axis=0)
            y_parts.append(jnp.dot(jnp.concatenate(ms, axis=1), rhs, preferred_element_type=F32))
    y_diag = jnp.concatenate(y_parts, axis=1)

    y_off_parts = []
    for g in range(SSD_GROUPS):
        cg = cm[:, g * SSD_STATE:(g + 1) * SSD_STATE]
        sg = s_ref[g * D_BC:(g + 1) * D_BC, :].astype(BF16)
        y_off_parts.append(lax.dot_general(cg, sg, (((1,), (1,)), ((), ())), preferred_element_type=F32))
    y_off = jnp.concatenate(y_off_parts, axis=1) * ea_e

    cd_col = jnp.exp(acum_t[:, L - 1:L])
    for g in range(SSD_GROUPS):
        bg = bm[:, g * SSD_STATE:(g + 1) * SSD_STATE]
        wg_t = w_b[:, g * D_BC:(g + 1) * D_BC].T.astype(BF16)
        upd = jnp.dot(wg_t, bg, preferred_element_type=F32)
        hpg = SSD_HEADS // SSD_GROUPS
        for e in range(hpg):
            h = g * hpg + e
            r0 = h * SSD_HEAD_DIM
            cd = jnp.broadcast_to(cd_col[h:h + 1, :], (SSD_HEAD_DIM, SSD_STATE))
            s_ref[r0:r0 + SSD_HEAD_DIM, :] = (cd * s_ref[r0:r0 + SSD_HEAD_DIM, :]
                                              + upd[e * SSD_HEAD_DIM:(e + 1) * SSD_HEAD_DIM, :])

    y = (y_diag + y_off + dexp_ref[...] * xs) * _silu(z_ref[...])
    y_ref[...] = _group_rmsnorm(y, ng_ref[...]).astype(BF16)

    @pl.when(c == pl.num_programs(1) - 1)
    def _():
        s_out_ref[...] = s_ref[...]
        cs_ref[...] = tail[5:8, :]


def _ssd_prompt(proj, dt_raw, batch, seq, conv_w, conv_b, dt_bias, a_log, d_exp, norm_g):
    nc = seq // SSD_CHUNK
    L = SSD_CHUNK
    hp = SSD_HEADS * SSD_HEAD_DIM
    const = lambda b, c: (0, 0)
    return pl.pallas_call(
        _ssd_prompt_body,
        grid=(batch, nc),
        in_specs=[
            pl.BlockSpec((L, D_XBC), lambda b, c: (b * nc + c, 2)),
            pl.BlockSpec((L, D_SSD), lambda b, c: (b * nc + c, 2)),
            pl.BlockSpec((L, LANES), lambda b, c: (b * nc + c, 0)),
            pl.BlockSpec((CONV_W, D_XBC), const),
            pl.BlockSpec((1, D_XBC), const),
            pl.BlockSpec((1, LANES), const),
            pl.BlockSpec((1, LANES), const),
            pl.BlockSpec((1, D_SSD), const),
            pl.BlockSpec((1, D_SSD), const),
        ],
        out_specs=[
            pl.BlockSpec((L, D_SSD), lambda b, c: (b * nc + c, 0)),
            pl.BlockSpec((None, hp, SSD_STATE), lambda b, c: (b, 0, 0)),
            pl.BlockSpec((None, CONV_W - 1, D_XBC), lambda b, c: (b, 0, 0)),
        ],
        out_shape=[
            jax.ShapeDtypeStruct((batch * seq, D_SSD), BF16),
            jax.ShapeDtypeStruct((batch, hp, SSD_STATE), F32),
            jax.ShapeDtypeStruct((batch, CONV_W - 1, D_XBC), F32),
        ],
        scratch_shapes=[
            pltpu.VMEM((L + 8, D_XBC), F32),
            pltpu.VMEM((hp, SSD_STATE), F32),
        ],
        compiler_params=_cparams(("arbitrary", "arbitrary")),
        name="ssd_prompt",
    )(proj, proj, dt_raw, conv_w, conv_b, dt_bias, a_log, d_exp, norm_g)


def _mix_sample_body(p_ref, dt_ref, h0_ref, rgc0_ref, s0_ref, sc0_ref,
                     rcw_ref, rcb_ref, wa_ref, ba_ref, wi_ref, bi_ref, lam_ref,
                     scw_ref, scb_ref, dtb_ref, alog_ref, dexp_ref, ng_ref,
                     y_ref, h_ref, rgc_ref, s_ref, sc_ref):
    nl = p_ref.shape[1] // D_MAIN
    bb = p_ref.shape[0]
    o_gate, o_z, o_xbc = D_RG, 2 * D_RG, 2 * D_RG + D_SSD

    def col(l, off, width):
        return p_ref[:, l * D_MAIN + off:l * D_MAIN + off + width]

    hist = [rgc0_ref[k] for k in range(CONV_W - 1)]
    raw = hist + [col(l, 0, D_RG) for l in range(nl)]
    xcs = []
    for l in range(nl):
        acc = rcb_ref[...] + raw[l] * rcw_ref[0:1, :]
        for k in range(1, CONV_W):
            acc = acc + raw[l + k] * rcw_ref[k:k + 1, :]
        xcs.append(acc)
    a, u = _rg_gates(jnp.concatenate(xcs, axis=0), wa_ref, ba_ref, wi_ref, bi_ref, lam_ref)
    h = h0_ref[...]
    for l in range(nl):
        h = a[l * bb:(l + 1) * bb] * h + u[l * bb:(l + 1) * bb]
        y_ref[:, l * 2 * D_MODEL:l * 2 * D_MODEL + D_RG] = h * _gelu_tanh(col(l, o_gate, D_RG))
    h_ref[...] = h
    for k in range(CONV_W - 1):
        rgc_ref[k] = raw[nl + k]

    hist = [sc0_ref[k] for k in range(CONV_W - 1)]
    raw = hist + [col(l, o_xbc, D_XBC) for l in range(nl)]
    for k in range(CONV_W - 1):
        sc_ref[k] = raw[nl + k]
    xcs = []
    for l in range(nl):
        acc = scb_ref[...] + raw[l] * scw_ref[0:1, :]
        for k in range(1, CONV_W):
            acc = acc + raw[l + k] * scw_ref[k:k + 1, :]
        xcs.append(_silu(acc))
    xs = [x[:, :D_SSD] for x in xcs]
    bms = [x[:, D_SSD:D_SSD + D_BC].astype(BF16) for x in xcs]
    cms = [x[:, D_SSD + D_BC:].astype(BF16) for x in xcs]

    lane = lax.broadcasted_iota(I32, (1, LANES), 1)
    a_neg = jnp.where(lane < SSD_HEADS, -jnp.exp(alog_ref[...]), 0.0)
    dts = [_softplus(dt_ref[:, l * LANES:(l + 1) * LANES] + dtb_ref[...]) for l in range(nl)]
    acums = []
    run = jnp.zeros((bb, LANES), F32)
    for l in range(nl):
        run = run + dts[l] * a_neg
        acums.append(run)
    a_last = acums[nl - 1]

    e_b16 = _head_expand_matrix(BF16)
    stacked = jnp.concatenate(dts + [jnp.exp(a_last - ac) for ac in acums] + [jnp.exp(ac) for ac in acums], axis=0)
    expd = _expand_heads(stacked, e_b16)
    dt_e = [expd[l * bb:(l + 1) * bb] for l in range(nl)]
    dend_e = [expd[(nl + l) * bb:(nl + l + 1) * bb] for l in range(nl)]
    ea_e = [expd[(2 * nl + l) * bb:(2 * nl + l + 1) * bb] for l in range(nl)]
    xdt = [xs[l] * dt_e[l] for l in range(nl)]
    xdt_r = [x.astype(BF16).astype(F32) for x in xdt]
    rows = nl * bb
    pad_rows = LANES - rows
    w_all = jnp.concatenate([(xdt[l] * dend_e[l]) for l in range(nl)]
                            + [jnp.zeros((pad_rows, D_SSD), F32)], axis=0)
    c_all = jnp.concatenate(cms, axis=0)
    b_all = jnp.concatenate(bms + [jnp.zeros((pad_rows, D_BC), BF16)], axis=0)

    r = lax.broadcasted_iota(I32, (D_BC, LANES), 0)
    cidx = lax.broadcasted_iota(I32, (D_BC, LANES), 1)
    gsum = jnp.where((lax.shift_right_logical(r, 7) == lax.shift_right_logical(cidx, 3)) & (cidx < SSD_HEADS),
                     1.0, 0.0).astype(F32)
    pairs = [(l, s) for l in range(nl) for s in range(l + 1)]
    prods = jnp.concatenate([cms[l].astype(F32) * bms[s].astype(F32) for (l, s) in pairs], axis=0)
    cbh = jnp.dot(prods, gsum, precision=HIGHEST, preferred_element_type=F32)
    m_list = []
    for idx, (l, s) in enumerate(pairs):
        decay = jnp.exp(acums[l] - acums[s])
        m_list.append((cbh[idx * bb:(idx + 1) * bb] * decay).astype(BF16))
    m_e = jnp.dot(jnp.concatenate(m_list, axis=0), e_b16, preferred_element_type=F32)
    y_diag = []
    for l in range(nl):
        acc = None
        for idx, (ll, s) in enumerate(pairs):
            if ll != l:
                continue
            term = m_e[idx * bb:(idx + 1) * bb] * xdt_r[s]
            acc = term if acc is None else acc + term
        y_diag.append(acc)

    cd_t = jnp.concatenate([jnp.exp(a_last), jnp.zeros((LANES - bb, LANES), F32)], axis=0).T
    row_seq = lax.broadcasted_iota(I32, (rows, 1), 0) & (bb - 1)
    row_pad = lax.broadcasted_iota(I32, (LANES, 1), 0)
    lane_q = lax.broadcasted_iota(I32, (LANES, LANES), 1)
    hpg = SSD_HEADS // SSD_GROUPS

    def seq_step(q, y_off):
        mine = row_seq == q
        mine_pad = ((row_pad & (bb - 1)) == q) & (row_pad < rows)
        cd_q = jnp.sum(jnp.where(lane_q == q, cd_t, 0.0), axis=1, keepdims=True)
        cd_q = jnp.broadcast_to(cd_q, (LANES, SSD_STATE))
        parts = []
        for g in range(SSD_GROUPS):
            s0 = s0_ref[q, g * D_BC:(g + 1) * D_BC, :]
            cg = c_all[:, g * SSD_STATE:(g + 1) * SSD_STATE]
            yq = lax.dot_general(cg, s0.astype(BF16), (((1,), (1,)), ((), ())), preferred_element_type=F32)
            parts.append(yq)
            bg = b_all[:, g * SSD_STATE:(g + 1) * SSD_STATE]
            wq = jnp.where(mine_pad, w_all[:, g * D_BC:(g + 1) * D_BC], 0.0)
            upd = jnp.dot(wq.T.astype(BF16), bg, preferred_element_type=F32)
            for e in range(hpg):
                hh = g * hpg + e
                cd = jnp.broadcast_to(cd_q[hh:hh + 1, :], (SSD_HEAD_DIM, SSD_STATE))
                s_ref[q, hh * SSD_HEAD_DIM:(hh + 1) * SSD_HEAD_DIM, :] = (
                    cd * s0[e * SSD_HEAD_DIM:(e + 1) * SSD_HEAD_DIM, :]
                    + upd[e * SSD_HEAD_DIM:(e + 1) * SSD_HEAD_DIM, :])
        yq_all = jnp.concatenate(parts, axis=1)
        return jnp.where(mine, yq_all, y_off)

    y_off = lax.fori_loop(0, bb, seq_step, jnp.zeros((rows, D_SSD), F32))

    for l in range(nl):
        y = (y_diag[l] + y_off[l * bb:(l + 1) * bb] * ea_e[l] + dexp_ref[...] * xs[l]) * _silu(col(l, o_z, D_SSD))
        y_ref[:, l * 2 * D_MODEL + D_RG:(l + 1) * 2 * D_MODEL] = _group_rmsnorm(y, ng_ref[...])


def _mix_sample(proj4, dt4, row0, n_seq, h0, rgc0, s0, sc0, rg_params, ssd_params):
    bb = SAMPLE_BB
    nl = proj4.shape[1] // D_MAIN
    hp = SSD_HEADS * SSD_HEAD_DIM
    blk0 = row0 // bb
    const2 = lambda i: (0, 0)
    rg_specs = [
        pl.BlockSpec((CONV_W, D_RG), const2), pl.BlockSpec((1, D_RG), const2),
        pl.BlockSpec((RG_BLOCKS, RG_BLOCK_W, RG_BLOCK_W), lambda i: (0, 0, 0)), pl.BlockSpec((1, D_RG), const2),
        pl.BlockSpec((RG_BLOCKS, RG_BLOCK_W, RG_BLOCK_W), lambda i: (0, 0, 0)), pl.BlockSpec((1, D_RG), const2),
        pl.BlockSpec((1, D_RG), const2),
    ]
    ssd_specs = [
        pl.BlockSpec((CONV_W, D_XBC), const2), pl.BlockSpec((1, D_XBC), const2),
        pl.BlockSpec((1, LANES), const2), pl.BlockSpec((1, LANES), const2),
        pl.BlockSpec((1, D_SSD), const2), pl.BlockSpec((1, D_SSD), const2),
    ]
    return pl.pallas_call(
        _mix_sample_body,
        grid=(n_seq // bb,),
        in_specs=[
            pl.BlockSpec((bb, nl * D_MAIN), lambda i: (blk0 + i, 0)),
            pl.BlockSpec((bb, nl * LANES), lambda i: (blk0 + i, 0)),
            pl.BlockSpec((bb, D_RG), lambda i: (i, 0)),
            pl.BlockSpec((CONV_W - 1, bb, D_RG), lambda i: (0, i, 0)),
            pl.BlockSpec((bb, hp, SSD_STATE), lambda i: (i, 0, 0)),
            pl.BlockSpec((CONV_W - 1, bb, D_XBC), lambda i: (0, i, 0)),
        ] + rg_specs + ssd_specs,
        out_specs=[
            pl.BlockSpec((bb, nl * 2 * D_MODEL), lambda i: (i, 0)),
            pl.BlockSpec((bb, D_RG), lambda i: (i, 0)),
            pl.BlockSpec((CONV_W - 1, bb, D_RG), lambda i: (0, i, 0)),
            pl.BlockSpec((bb, hp, SSD_STATE), lambda i: (i, 0, 0)),
            pl.BlockSpec((CONV_W - 1, bb, D_XBC), lambda i: (0, i, 0)),
        ],
        out_shape=[
            jax.ShapeDtypeStruct((n_seq, nl * 2 * D_MODEL), F32),
            jax.ShapeDtypeStruct((n_seq, D_RG), F32),
            jax.ShapeDtypeStruct((CONV_W - 1, n_seq, D_RG), F32),
            jax.ShapeDtypeStruct((n_seq, hp, SSD_STATE), F32),
            jax.ShapeDtypeStruct((CONV_W - 1, n_seq, D_XBC), F32),
        ],
        compiler_params=_cparams(("arbitrary",)),
        name="mix_sample",
    )(proj4, dt4, h0, rgc0, s0, sc0, *rg_params, *ssd_params)


def _route_body(x_ref, g_ref, wr_ref, br_ref, hp_ref, e_ref, gate_ref, rank_ref, cnt_ref, run_ref):
    i = pl.program_id(0)
    tm = x_ref.shape[0]

    @pl.when(i == 0)
    def _():
        run_ref[...] = jnp.zeros_like(run_ref)

    x = x_ref[...]
    h = x * lax.rsqrt(jnp.mean(x * x, axis=-1, keepdims=True) + EPS) * g_ref[...]
    hp_ref[...] = h

    lane = lax.broadcasted_iota(I32, (tm, LANES), 1).astype(F32)
    logits = jnp.dot(h, wr_ref[...], precision=HIGHEST, preferred_element_type=F32) + br_ref[...]
    work = jnp.where(lane < N_EXPERTS, logits, -jnp.inf)
    vals, idxs = [], []
    multi = jnp.zeros((tm, LANES), F32)
    for _ in range(TOP_K):
        m = jnp.max(work, axis=-1, keepdims=True)
        idx = jnp.min(jnp.where(work == m, lane, float(LANES)), axis=-1, keepdims=True)
        hit = lane == idx
        vals.append(m)
        idxs.append(idx)
        multi = jnp.where(hit, 1.0, multi)
        work = jnp.where(hit, -jnp.inf, work)
    ex = [jnp.exp(v - vals[0]) for v in vals]
    den = ex[0] + ex[1] + ex[2] + ex[3]
    gates = [e / den for e in ex]

    ri = lax.broadcasted_iota(I32, (tm, tm), 0)
    ci = lax.broadcasted_iota(I32, (tm, tm), 1)
    strict = jnp.where(ci < ri, 1.0, 0.0).astype(BF16)
    before = jnp.dot(strict, multi.astype(BF16), preferred_element_type=F32) + run_ref[...]
    ranks = [jnp.sum(jnp.where(lane == idx, before, 0.0), axis=-1, keepdims=True) for idx in idxs]
    run_ref[...] = run_ref[...] + jnp.sum(multi, axis=0, keepdims=True)

    lane4 = lax.broadcasted_iota(I32, (tm, TOP_K), 1)
    e_out = jnp.zeros((tm, TOP_K), I32)
    g_out = jnp.zeros((tm, TOP_K), F32)
    r_out = jnp.zeros((tm, TOP_K), I32)
    for k in range(TOP_K):
        e_out = jnp.where(lane4 == k, idxs[k].astype(I32), e_out)
        g_out = jnp.where(lane4 == k, gates[k], g_out)
        r_out = jnp.where(lane4 == k, ranks[k].astype(I32), r_out)
    e_ref[...] = e_out
    gate_ref[...] = g_out
    rank_ref[...] = r_out
    cnt_ref[...] = run_ref[...].astype(I32)


def _route(x1, g_ffn, w_router_pad, b_router_pad):
    t = x1.shape[0]
    tm = ROUTE_TM
    const = lambda i: (0, 0)
    return pl.pallas_call(
        _route_body,
        grid=(t // tm,),
        in_specs=[
            pl.BlockSpec((tm, D_MODEL), lambda i: (i, 0)),
            pl.BlockSpec((1, D_MODEL), const),
            pl.BlockSpec((D_MODEL, LANES), const),
            pl.BlockSpec((1, LANES), const),
        ],
        out_specs=[
            pl.BlockSpec((tm, D_MODEL), lambda i: (i, 0)),
            pl.BlockSpec((tm, TOP_K), lambda i: (i, 0)),
            pl.BlockSpec((tm, TOP_K), lambda i: (i, 0)),
            pl.BlockSpec((tm, TOP_K), lambda i: (i, 0)),
            pl.BlockSpec((1, LANES), const),
        ],
        out_shape=[
            jax.ShapeDtypeStruct((t, D_MODEL), F32),
            jax.ShapeDtypeStruct((t, TOP_K), I32),
            jax.ShapeDtypeStruct((t, TOP_K), F32),
            jax.ShapeDtypeStruct((t, TOP_K), I32),
            jax.ShapeDtypeStruct((1, LANES), I32),
        ],
        scratch_shapes=[pltpu.VMEM((1, LANES), F32)],
        compiler_params=_cparams(("arbitrary",)),
        name="route",
    )(x1, g_ffn, w_router_pad, b_router_pad)


def _slots_body(seg_ref, cnt_ref, e_ref, rank_ref, slot_ref):
    i = pl.program_id(0)
    n = e_ref.shape[0]

    def put(blk, c):
        for u in range(DMA_UNROLL):
            a = blk * DMA_UNROLL + u
            slot_ref[seg_ref[e_ref[a]] + rank_ref[a]] = i * n + a
        return c

    lax.fori_loop(0, n // DMA_UNROLL, put, 0)

    @pl.when(i == pl.num_programs(0) - 1)
    def _():
        pad = EXPERT_PAD

        def fill(lo, hi):
            def step(p, c):
                slot_ref[p] = 0
                return c
            lax.fori_loop(lo, hi, step, 0)

        def per_expert(e, end):
            cnt = cnt_ref[e]
            hi = ((cnt + pad - 1) // pad) * pad
            fill(seg_ref[e] + cnt, seg_ref[e] + hi)
            return seg_ref[e] + hi

        end = lax.fori_loop(0, N_EXPERTS, per_expert, jnp.int32(0))
        fill(end, slot_ref.shape[0])


def _slot_assignments(seg_start, counts, e_flat, rank_flat, n_slots):
    n = e_flat.shape[0]
    tm = SCATTER_TM * TOP_K
    grid_spec = pltpu.PrefetchScalarGridSpec(
        num_scalar_prefetch=2,
        grid=(n // tm,),
        in_specs=[
            pl.BlockSpec((tm,), lambda i, seg, cnt: (i,), memory_space=pltpu.SMEM),
            pl.BlockSpec((tm,), lambda i, seg, cnt: (i,), memory_space=pltpu.SMEM),
        ],
        out_specs=pl.BlockSpec((n_slots,), lambda i, seg, cnt: (0,), memory_space=pltpu.SMEM),
    )
    return pl.pallas_call(
        _slots_body,
        grid_spec=grid_spec,
        out_shape=jax.ShapeDtypeStruct((n_slots,), I32),
        compiler_params=_cparams(("arbitrary",)),
        name="slot_assignments",
    )(seg_start, counts, e_flat, rank_flat)


def _experts_body(sbe_ref, sbs_ref, sbn_ref, sbr_ref, used_ref, slot_ref, h_ref, wg_ref, wu_ref, bg_ref, bu_ref,
                  wd_ref, bd_ref, ys_ref, xin_ref, xbuf_ref, acc_ref, wgu_ref, wdb_ref, in_sem, out_sem):
    del sbe_ref
    sb = pl.program_id(0)
    j = pl.program_id(1)
    nj = pl.num_programs(1)
    n_sb = pl.num_programs(0)
    nblk = sbn_ref[sb]
    start = sbs_ref[sb]
    cb = EXPERT_PAD
    bm = EXPERT_BM
    tf = wd_ref.shape[0]

    n_rows = xin_ref.shape[0]
    rows_per_step = n_rows // (nj * (D_MODEL // cb))

    def gather_row(which, i):
        slot = sbs_ref[which] + jnp.minimum(i, sbr_ref[which] - 1)
        t = lax.shift_right_logical(slot_ref[slot], 2)
        return pltpu.make_async_copy(h_ref.at[pl.ds(t, 1), :], xin_ref.at[pl.ds(i, 1), :], in_sem)

    def out_copy(r):
        src = pl.multiple_of(r * cb, cb)
        dst = pl.multiple_of(start + r * cb, cb)
        return pltpu.make_async_copy(acc_ref.at[pl.ds(src, cb), :], ys_ref.at[pl.ds(dst, cb), :], out_sem)

    def zero_copy(blk):
        dst = pl.multiple_of(blk * cb, cb)
        return pltpu.make_async_copy(acc_ref.at[pl.ds(0, cb), :], ys_ref.at[pl.ds(dst, cb), :], out_sem)

    def for_each(lo, hi, fn):
        def step(r, c):
            fn(r)
            return c
        lax.fori_loop(lo, hi, step, 0)

    @pl.when((j == 0) & (sb == 0) & (nblk > 0))
    def _():
        def first(blk):
            for u in range(DMA_UNROLL):
                gather_row(0, blk * DMA_UNROLL + u).start()

        for_each(0, n_rows // DMA_UNROLL, first)

    @pl.when((j == 0) & (nblk > 0))
    def _():
        pltpu.make_async_copy(h_ref.at[pl.ds(0, n_rows), :], xin_ref, in_sem).wait()

        def stage(r):
            r0 = pl.multiple_of(r * cb, cb)
            xbuf_ref[pl.ds(r0, cb), :] = xin_ref[pl.ds(r0, cb), :].astype(BF16)
            acc_ref[pl.ds(r0, cb), :] = jnp.broadcast_to(bd_ref[...], (cb, D_MODEL))

        for_each(0, nblk, stage)

    @pl.when(nblk > 0)
    def _():
        nxt = jnp.minimum(sb + 1, n_sb - 1)
        fetch_next = (sb + 1 < n_sb) & (sbn_ref[nxt] > 0)

        def cast_rows(s, fetch):
            r = pl.multiple_of(s * cb, cb)
            wgu_ref[pl.ds(r, cb), 0:tf] = wg_ref[pl.ds(r, cb), :].astype(BF16)
            wgu_ref[pl.ds(r, cb), tf:2 * tf] = wu_ref[pl.ds(r, cb), :].astype(BF16)
            if fetch:
                base = (j * (D_MODEL // cb) + s) * rows_per_step
                for u in range(rows_per_step):
                    gather_row(nxt, base + u).start()

        @pl.when(fetch_next)
        def _():
            for_each(0, D_MODEL // cb, lambda s: cast_rows(s, True))

        @pl.when(jnp.logical_not(fetch_next))
        def _():
            for_each(0, D_MODEL // cb, lambda s: cast_rows(s, False))
        wdb_ref[...] = wd_ref[...].astype(BF16)
        bias_gu = jnp.concatenate([bg_ref[...], bu_ref[...]], axis=1)

        def sub(r0, rows):
            x = xbuf_ref[pl.ds(r0, rows), :]
            gu = jnp.dot(x, wgu_ref[...], preferred_element_type=F32) + bias_gu
            gate = jnp.minimum(gu[:, :tf], SWIGLU_LIMIT)
            up = jnp.clip(gu[:, tf:], -SWIGLU_LIMIT, SWIGLU_LIMIT)
            glu = gate * _sigmoid(SWIGLU_ALPHA * gate)
            act = ((up + 1.0) * glu).astype(BF16)
            acc_ref[pl.ds(r0, rows), :] += jnp.dot(act, wdb_ref[...], preferred_element_type=F32)

            @pl.when(j == nj - 1)
            def _():
                for b in range(rows // cb):
                    out_copy(r0 // cb + b).start()

        per_sub = bm // cb
        n_full = nblk // per_sub
        for_each(0, n_full, lambda i: sub(pl.multiple_of(i * bm, bm), bm))
        done = n_full * per_sub
        piece = per_sub // 2
        while piece >= 1:
            here = done

            @pl.when((nblk & piece) != 0)
            def _():
                sub(pl.multiple_of(here * cb, cb), piece * cb)

            done = done + (nblk & piece)
            piece //= 2

    @pl.when(j == nj - 1)
    def _():
        for_each(0, nblk, lambda r: out_copy(r).wait())

    @pl.when((sb == pl.num_programs(0) - 1) & (j == nj - 1))
    def _():
        acc_ref[0:cb, :] = jnp.zeros((cb, D_MODEL), F32)
        n_blocks = ys_ref.shape[0] // cb
        for_each(used_ref[0], n_blocks, lambda blk: zero_copy(blk).start())
        for_each(used_ref[0], n_blocks, lambda blk: zero_copy(blk).wait())


def _experts(sb_expert, sb_start, sb_nsub, sb_rows, used_blocks, slot_assign, h2, w_gu, b_gu, w_down, b_down):
    n_sb = sb_expert.shape[0]
    n_slots = slot_assign.shape[0]
    tf = EXPERT_TF
    nj = D_FF // tf
    assert EXPERT_R % (nj * (D_MODEL // EXPERT_PAD)) == 0 and EXPERT_R % DMA_UNROLL == 0

    def spec(shape, fn):
        def index_map(sb, j, sbe, sbs, sbn, sbr, used, slot):
            jj = jnp.where(sbn[sb] > 0, j, nj - 1)
            return fn(sbe[sb], jj)
        return pl.BlockSpec(shape, index_map)

    grid_spec = pltpu.PrefetchScalarGridSpec(
        num_scalar_prefetch=6,
        grid=(n_sb, nj),
        in_specs=[
            pl.BlockSpec(memory_space=pl.ANY),
            spec((None, D_MODEL, tf), lambda e, jj: (e, 0, jj)),
            spec((None, D_MODEL, tf), lambda e, jj: (e, 0, nj + jj)),
            spec((None, 1, tf), lambda e, jj: (e, 0, jj)),
            spec((None, 1, tf), lambda e, jj: (e, 0, nj + jj)),
            spec((None, tf, D_MODEL), lambda e, jj: (e, jj, 0)),
            spec((None, 1, D_MODEL), lambda e, jj: (e, 0, 0)),
        ],
        out_specs=pl.BlockSpec(memory_space=pl.ANY),
        scratch_shapes=[
            pltpu.VMEM((EXPERT_R, D_MODEL), F32),
            pltpu.VMEM((EXPERT_R, D_MODEL), BF16),
            pltpu.VMEM((EXPERT_R, D_MODEL), F32),
            pltpu.VMEM((D_MODEL, 2 * tf), BF16),
            pltpu.VMEM((tf, D_MODEL), BF16),
            pltpu.SemaphoreType.DMA(()),
            pltpu.SemaphoreType.DMA(()),
        ],
    )
    return pl.pallas_call(
        _experts_body,
        grid_spec=grid_spec,
        out_shape=jax.ShapeDtypeStruct((n_slots, D_MODEL), F32),
        compiler_params=_cparams(("arbitrary", "arbitrary")),
        name="experts",
    )(sb_expert, sb_start, sb_nsub, sb_rows, used_blocks, slot_assign, h2, w_gu, w_gu, b_gu, b_gu, w_down, b_down)


def _combine_body(seg_ref, e_ref, rank_ref, x1_ref, gate_ref, gf_ref, ys_ref, yp_ref, ysm_ref, buf_ref, sem,
                  *, n_prompt_tiles):
    i = pl.program_id(0)
    tm = x1_ref.shape[0]
    n = e_ref.shape[0]

    def issue(blk, c):
        for u in range(DMA_UNROLL):
            a = blk * DMA_UNROLL + u
            pos = seg_ref[e_ref[a]] + rank_ref[a]
            t = lax.shift_right_logical(a, 2)
            pltpu.make_async_copy(ys_ref.at[pl.ds(pos, 1), :], buf_ref.at[u % TOP_K, pl.ds(t, 1), :], sem).start()
        return c

    lax.fori_loop(0, n // DMA_UNROLL, issue, 0)
    for k in range(TOP_K):
        pltpu.make_async_copy(ys_ref.at[pl.ds(0, tm), :], buf_ref.at[k], sem).wait()

    g = gate_ref[...]
    acc = buf_ref[0] * g[:, 0:1]
    for k in range(1, TOP_K):
        acc = acc + buf_ref[k] * g[:, k:k + 1]
    x = x1_ref[...] + acc
    y = x * lax.rsqrt(jnp.mean(x * x, axis=-1, keepdims=True) + EPS) * gf_ref[...]

    @pl.when(i < n_prompt_tiles)
    def _():
        yp_ref[...] = y

    @pl.when(i >= n_prompt_tiles)
    def _():
        ysm_ref[...] = y


def _combine(seg_start, e_flat, rank_flat, x1, gates, g_final, ysorted, t_prompt):
    t = x1.shape[0]
    tm = COMBINE_TM
    npt = t_prompt // tm
    grid_spec = pltpu.PrefetchScalarGridSpec(
        num_scalar_prefetch=1,
        grid=(t // tm,),
        in_specs=[
            pl.BlockSpec((tm * TOP_K,), lambda i, seg: (i,), memory_space=pltpu.SMEM),
            pl.BlockSpec((tm * TOP_K,), lambda i, seg: (i,), memory_space=pltpu.SMEM),
            pl.BlockSpec((tm, D_MODEL), lambda i, seg: (i, 0)),
            pl.BlockSpec((tm, TOP_K), lambda i, seg: (i, 0)),
            pl.BlockSpec((1, D_MODEL), lambda i, seg: (0, 0)),
            pl.BlockSpec(memory_space=pl.ANY),
        ],
        out_specs=[
            pl.BlockSpec((tm, D_MODEL), lambda i, seg: (jnp.minimum(i, npt - 1), 0)),
            pl.BlockSpec((tm, D_MODEL), lambda i, seg: (jnp.maximum(i - npt, 0), 0)),
        ],
        scratch_shapes=[pltpu.VMEM((TOP_K, tm, D_MODEL), F32), pltpu.SemaphoreType.DMA(())],
    )
    return pl.pallas_call(
        functools.partial(_combine_body, n_prompt_tiles=npt),
        grid_spec=grid_spec,
        out_shape=[
            jax.ShapeDtypeStruct((t_prompt, D_MODEL), F32),
            jax.ShapeDtypeStruct((t - t_prompt, D_MODEL), F32),
        ],
        compiler_params=_cparams(("arbitrary",)),
        name="combine",
    )(seg_start, e_flat, rank_flat, x1, gates, g_final, ysorted)


def _expert_tables(counts, n_assign):
    pad = EXPERT_PAD
    padded = ((counts + pad - 1) // pad) * pad
    seg_start = jnp.cumsum(padded) - padded
    n_sb_e = (counts + EXPERT_R - 1) // EXPERT_R
    sb_cum = jnp.cumsum(n_sb_e)
    n_sb = n_assign // EXPERT_R + N_EXPERTS
    s = jnp.arange(n_sb, dtype=I32)
    total = sb_cum[-1]
    s_eff = jnp.minimum(s, total - 1)
    e = jnp.sum((sb_cum[None, :] <= s_eff[:, None]).astype(I32), axis=1)
    k = s_eff - (sb_cum[e] - n_sb_e[e])
    rows = jnp.clip(counts[e] - k * EXPERT_R, 0, EXPERT_R)
    valid = s < total
    nsub = jnp.where(valid, (rows + pad - 1) // pad, 0)
    start = seg_start[e] + k * EXPERT_R
    used_blocks = (jnp.sum(padded) // pad).reshape(1)
    rows = jnp.where(valid, rows, 0)
    return (seg_start.astype(I32), e, start.astype(I32), nsub.astype(I32), rows.astype(I32),
            used_blocks.astype(I32))


def kernel(x_prompt, x_sample, state_rglru_h, state_rglru_conv, state_ssd, state_ssd_conv, g_mix, w_in, conv_rg_w, conv_rg_b, rg_wa, rg_ba, rg_wi, rg_bi, rg_lambda, conv_ssd_w, conv_ssd_b, ssd_dt_bias, ssd_a_log, ssd_d, ssd_norm_g, w_out, g_ffn, w_router, b_router, w_gu, b_gu, w_down, b_down, g_final):
    depth = g_mix.shape[0]
    assert depth == 1
    bp, lp, _ = x_prompt.shape
    bs, ls, _ = x_sample.shape
    tp, ts = bp * lp, bs * ls
    t = tp + ts
    hp = SSD_HEADS * SSD_HEAD_DIM
    l = 0

    xp = x_prompt.reshape(tp, D_MODEL)
    xs = x_sample.reshape(ts, D_MODEL)
    w_in_t = jnp.transpose(w_in[l])
    row = lambda v: v.reshape(1, -1)
    pad_heads = lambda v: jnp.pad(v.reshape(1, -1), ((0, 0), (0, LANES - SSD_HEADS)))

    hb, dt_raw = _norm_dt(xp, xs, row(g_mix[l]), w_in_t)
    proj = _in_proj(hb, w_in_t)

    rg_params = (conv_rg_w[l], row(conv_rg_b[l]), rg_wa[l], row(rg_ba[l]), rg_wi[l], row(rg_bi[l]),
                 row(rg_lambda[l]))
    d_exp = row(jnp.repeat(ssd_d[l], SSD_HEAD_DIM))
    ssd_params = (conv_ssd_w[l], row(conv_ssd_b[l]), pad_heads(ssd_dt_bias[l]), pad_heads(ssd_a_log[l]),
                  d_exp, row(ssd_norm_g[l]))

    y_rg, rgh_p, rgc_p = _rglru_prompt(proj, bp, lp, *rg_params)
    y_ssd, ssd_p, ssdc_p = _ssd_prompt(proj, dt_raw, bp, lp, *ssd_params)

    proj4 = proj[tp:].reshape(bs, ls * D_MAIN)
    dt4 = dt_raw[tp:].reshape(bs, ls * LANES)
    mix_s4, rgh_s, rgc_s, ssd_s, ssdc_s = _mix_sample(
        proj4, dt4, 0, bs,
        state_rglru_h[l], jnp.transpose(state_rglru_conv[l], (1, 0, 2)),
        state_ssd[l].reshape(bs, hp, SSD_STATE), jnp.transpose(state_ssd_conv[l], (1, 0, 2)),
        rg_params, ssd_params)
    mix_s = mix_s4.reshape(ts, 2 * D_MODEL)

    x1 = _out_proj(y_rg, y_ssd, mix_s, w_out[l], xp, xs)

    wr = jnp.pad(w_router[l], ((0, 0), (0, LANES - N_EXPERTS)))
    br = jnp.pad(b_router[l].reshape(1, -1), ((0, 0), (0, LANES - N_EXPERTS)))
    h2, e_idx, gates, rank, counts = _route(x1, row(g_ffn[l]), wr, br)

    n_assign = t * TOP_K
    n_slots = n_assign + N_EXPERTS * EXPERT_PAD
    counts = counts[0, :N_EXPERTS]
    seg_start, sb_expert, sb_start, sb_nsub, sb_rows, used_blocks = _expert_tables(counts, n_assign)
    e_flat = e_idx.reshape(-1)
    rank_flat = rank.reshape(-1)
    slot_assign = _slot_assignments(seg_start, counts, e_flat, rank_flat, n_slots)
    ysorted = _experts(sb_expert, sb_start, sb_nsub, sb_rows, used_blocks, slot_assign, h2, w_gu[l],
                       b_gu[l].reshape(N_EXPERTS, 1, -1), w_down[l], b_down[l].reshape(N_EXPERTS, 1, -1))
    y_p, y_s = _combine(seg_start, e_flat, rank_flat, x1, gates, row(g_final), ysorted, tp)

    return (y_p.reshape(x_prompt.shape).astype(x_prompt.dtype),
            y_s.reshape(x_sample.shape).astype(x_sample.dtype),
            rgh_p.reshape(depth, bp, D_RG),
            rgc_p.reshape(depth, bp, CONV_W - 1, D_RG),
            ssd_p.reshape(depth, bp, SSD_HEADS, SSD_HEAD_DIM, SSD_STATE),
            ssdc_p.reshape(depth, bp, CONV_W - 1, D_XBC),
            rgh_s.reshape(depth, bs, D_RG),
            jnp.transpose(rgc_s, (1, 0, 2)).reshape(depth, bs, CONV_W - 1, D_RG),
            ssd_s.reshape(depth, bs, SSD_HEADS, SSD_HEAD_DIM, SSD_STATE),
            jnp.transpose(ssdc_s, (1, 0, 2)).reshape(depth, bs, CONV_W - 1, D_XBC))
```

```python
import functools

import jax
import jax.numpy as jnp
from jax import lax
from jax.experimental import pallas as pl
from jax.experimental.pallas import tpu as pltpu

F32 = jnp.float32
BF16 = jnp.bfloat16
I32 = jnp.int32
HIGHEST = lax.Precision.HIGHEST

EPS = 1e-6
D_MODEL = 2048
D_RG = 2048
RG_BLOCKS = 16
RG_BLOCK_W = 128
RG_C = 8.0
D_SSD = 2048
SSD_HEAD_DIM = 64
SSD_HEADS = 32
SSD_GROUPS = 4
SSD_STATE = 128
SSD_CHUNK = 128
CONV_W = 4
D_BC = SSD_GROUPS * SSD_STATE
D_XBC = D_SSD + 2 * D_BC
D_MAIN = 2 * D_RG + D_SSD + D_XBC
N_EXPERTS = 32
TOP_K = 4
D_FF = 2048
SWIGLU_ALPHA = 1.702
SWIGLU_LIMIT = 7.0

LANES = 128
VMEM_LIMIT = 56 * 1024 * 1024

ROW_TILE = 512
PROJ_TM = 1088
PROJ_TN = 1024
OUT_TN = 512
RG_TL = 256
SAMPLE_BB = 8
ROUTE_TM = 512
SCATTER_TM = 256
COMBINE_TM = 256
EXPERT_PAD = 128
EXPERT_BM = 512
EXPERT_R = 1152
EXPERT_TF = 512
DMA_UNROLL = 8


def _cparams(sem, vmem=VMEM_LIMIT):
    return pltpu.CompilerParams(dimension_semantics=sem, vmem_limit_bytes=vmem)


def _softplus(x):
    return jnp.maximum(x, 0.0) + jnp.log1p(jnp.exp(-jnp.abs(x)))


_sigmoid = jax.nn.sigmoid
_silu = jax.nn.silu


def _gelu_tanh(x):
    return jax.nn.gelu(x, approximate=True)


def _head_expand_matrix(dtype):
    r = lax.broadcasted_iota(I32, (LANES, D_SSD), 0)
    c = lax.broadcasted_iota(I32, (LANES, D_SSD), 1)
    return jnp.where(lax.shift_right_logical(c, 6) == r, 1.0, 0.0).astype(dtype)


def _expand_heads(v, e_b16):
    rows = v.shape[0]
    hi = v.astype(BF16)
    r1 = v - hi.astype(F32)
    mid = r1.astype(BF16)
    lo = (r1 - mid.astype(F32)).astype(BF16)
    o = jnp.dot(jnp.concatenate([hi, mid, lo], axis=0), e_b16, preferred_element_type=F32)
    return (o[0:rows] + o[rows:2 * rows]) + o[2 * rows:3 * rows]


def _norm_dt_body(xp_ref, xs_ref, g_ref, wdt_ref, hb_ref, dt_ref, *, n_prompt_tiles):
    i = pl.program_id(0)

    def run(x_ref):
        x = x_ref[...]
        h = x * lax.rsqrt(jnp.mean(x * x, axis=-1, keepdims=True) + EPS) * g_ref[...]
        hb = h.astype(BF16)
        hb_ref[...] = hb
        r = lax.broadcasted_iota(I32, (LANES, 1), 0)
        wdt = jnp.where(r < SSD_HEADS, wdt_ref[...], 0.0).astype(BF16)
        dt_ref[...] = lax.dot_general(hb, wdt, (((1,), (1,)), ((), ())), preferred_element_type=F32)

    @pl.when(i < n_prompt_tiles)
    def _():
        run(xp_ref)

    @pl.when(i >= n_prompt_tiles)
    def _():
        run(xs_ref)


def _norm_dt(xp, xs, g, w_in_t):
    tp, ts = xp.shape[0], xs.shape[0]
    npt, nst = tp // ROW_TILE, ts // ROW_TILE
    t = tp + ts
    return pl.pallas_call(
        functools.partial(_norm_dt_body, n_prompt_tiles=npt),
        grid=(npt + nst,),
        in_specs=[
            pl.BlockSpec((ROW_TILE, D_MODEL), lambda i: (jnp.minimum(i, npt - 1), 0)),
            pl.BlockSpec((ROW_TILE, D_MODEL), lambda i: (jnp.maximum(i - npt, 0), 0)),
            pl.BlockSpec((1, D_MODEL), lambda i: (0, 0)),
            pl.BlockSpec((LANES, D_MODEL), lambda i: (D_MAIN // LANES, 0)),
        ],
        out_specs=[
            pl.BlockSpec((ROW_TILE, D_MODEL), lambda i: (i, 0)),
            pl.BlockSpec((ROW_TILE, LANES), lambda i: (i, 0)),
        ],
        out_shape=[jax.ShapeDtypeStruct((t, D_MODEL), BF16), jax.ShapeDtypeStruct((t, LANES), F32)],
        compiler_params=_cparams(("parallel",)),
        name="norm_dt",
    )(xp, xs, g, w_in_t)


def _cast_weight_tile(w_ref, wb_ref, rows_per_step=256):
    k = w_ref.shape[0]

    def step(s, c):
        r = pl.multiple_of(s * rows_per_step, rows_per_step)
        wb_ref[pl.ds(r, rows_per_step), :] = w_ref[pl.ds(r, rows_per_step), :].astype(BF16)
        return c

    lax.fori_loop(0, k // rows_per_step, step, 0)


def _in_proj_body(x_ref, w_ref, o_ref, wb_ref):
    @pl.when(pl.program_id(1) == 0)
    def _():
        _cast_weight_tile(w_ref, wb_ref)

    o_ref[...] = lax.dot_general(x_ref[...], wb_ref[...], (((1,), (1,)), ((), ())), preferred_element_type=F32)


def _in_proj(hb, w_in_t):
    t = hb.shape[0]
    return pl.pallas_call(
        _in_proj_body,
        grid=(D_MAIN // PROJ_TN, t // PROJ_TM),
        in_specs=[
            pl.BlockSpec((PROJ_TM, D_MODEL), lambda j, i: (i, 0)),
            pl.BlockSpec((PROJ_TN, D_MODEL), lambda j, i: (j, 0)),
        ],
        out_specs=pl.BlockSpec((PROJ_TM, PROJ_TN), lambda j, i: (i, j)),
        out_shape=jax.ShapeDtypeStruct((t, D_MAIN), F32),
        scratch_shapes=[pltpu.VMEM((PROJ_TN, D_MODEL), BF16)],
        compiler_params=_cparams(("arbitrary", "arbitrary")),
        name="in_proj",
    )(hb, w_in_t)


def _out_proj_body(rg_ref, ssd_ref, ms_ref, w_ref, xp_ref, xs_ref, o_ref, wb_ref, *, n_prompt_tiles):
    i = pl.program_id(1)

    @pl.when(i == 0)
    def _():
        _cast_weight_tile(w_ref, wb_ref)

    @pl.when(i < n_prompt_tiles)
    def _():
        m = jnp.concatenate([rg_ref[...], ssd_ref[...]], axis=1)
        o_ref[...] = xp_ref[...] + jnp.dot(m, wb_ref[...], preferred_element_type=F32)

    @pl.when(i >= n_prompt_tiles)
    def _():
        o_ref[...] = xs_ref[...] + jnp.dot(ms_ref[...].astype(BF16), wb_ref[...], preferred_element_type=F32)


def _out_proj(y_rg, y_ssd, mix_s, w_out, xp, xs):
    tp, ts = y_rg.shape[0], mix_s.shape[0]
    npt, nst = tp // ROW_TILE, ts // ROW_TILE
    prompt_rows = lambda j, i: (jnp.minimum(i, npt - 1), 0)
    return pl.pallas_call(
        functools.partial(_out_proj_body, n_prompt_tiles=npt),
        grid=(D_MODEL // OUT_TN, npt + nst),
        in_specs=[
            pl.BlockSpec((ROW_TILE, D_MODEL), prompt_rows),
            pl.BlockSpec((ROW_TILE, D_MODEL), prompt_rows),
            pl.BlockSpec((ROW_TILE, 2 * D_MODEL), lambda j, i: (jnp.maximum(i - npt, 0), 0)),
            pl.BlockSpec((2 * D_MODEL, OUT_TN), lambda j, i: (0, j)),
            pl.BlockSpec((ROW_TILE, OUT_TN), lambda j, i: (jnp.minimum(i, npt - 1), j)),
            pl.BlockSpec((ROW_TILE, OUT_TN), lambda j, i: (jnp.maximum(i - npt, 0), j)),
        ],
        out_specs=pl.BlockSpec((ROW_TILE, OUT_TN), lambda j, i: (i, j)),
        out_shape=jax.ShapeDtypeStruct((tp + ts, D_MODEL), F32),
        scratch_shapes=[pltpu.VMEM((2 * D_MODEL, OUT_TN), BF16)],
        compiler_params=_cparams(("arbitrary", "arbitrary")),
        name="out_proj",
    )(y_rg, y_ssd, mix_s, w_out, xp, xs)


def _rg_gates(xc, wa_ref, ba_ref, wi_ref, bi_ref, lam_ref):
    xcb = xc.astype(BF16)
    rs, is_ = [], []
    for h in range(RG_BLOCKS):
        xh = xcb[:, h * RG_BLOCK_W:(h + 1) * RG_BLOCK_W]
        rs.append(jnp.dot(xh, wa_ref[h].astype(BF16), preferred_element_type=F32))
        is_.append(jnp.dot(xh, wi_ref[h].astype(BF16), preferred_element_type=F32))
    r = _sigmoid(jnp.concatenate(rs, axis=1) + ba_ref[...])
    ig = _sigmoid(jnp.concatenate(is_, axis=1) + bi_ref[...])
    log_a = (-RG_C * r) * _softplus(-lam_ref[...])
    a = jnp.exp(log_a)
    u = jnp.sqrt(-jnp.tanh(log_a) * (a * a + 1.0)) * (ig * xc)
    return a, u


def _rglru_prompt_body(x_ref, gate_ref, cw_ref, cb_ref, wa_ref, ba_ref, wi_ref, bi_ref, lam_ref,
                       y_ref, h_ref, cs_ref, ext_ref, a_ref, u_ref, car_ref):
    c = pl.program_id(1)
    tl = x_ref.shape[0]

    @pl.when(c == 0)
    def _():
        ext_ref[0:8, :] = jnp.zeros((8, D_RG), F32)
        car_ref[...] = jnp.zeros((1, D_RG), F32)

    ext_ref[8:8 + tl, :] = x_ref[...]
    xc = cb_ref[...] + ext_ref[5:5 + tl, :] * cw_ref[0:1, :]
    xc = xc + ext_ref[6:6 + tl, :] * cw_ref[1:2, :]
    xc = xc + ext_ref[7:7 + tl, :] * cw_ref[2:3, :]
    xc = xc + ext_ref[8:8 + tl, :] * cw_ref[3:4, :]
    tail = ext_ref[tl:tl + 8, :]
    ext_ref[0:8, :] = tail

    a, u = _rg_gates(xc, wa_ref, ba_ref, wi_ref, bi_ref, lam_ref)
    a_ref[...] = a
    u_ref[...] = u

    row = lax.broadcasted_iota(I32, (8, D_RG), 0)

    def group(g, carry):
        r0 = pl.multiple_of(g * 8, 8)
        a8 = a_ref[pl.ds(r0, 8), :]
        u8 = u_ref[pl.ds(r0, 8), :]
        for s in (1, 2, 4):
            keep = row >= s
            a_sh = jnp.where(keep, pltpu.roll(a8, s, 0), 1.0)
            u_sh = jnp.where(keep, pltpu.roll(u8, s, 0), 0.0)
            u8 = a8 * u_sh + u8
            a8 = a8 * a_sh
        h8 = a8 * carry + u8
        u_ref[pl.ds(r0, 8), :] = h8
        return h8[7:8, :]

    carry = lax.fori_loop(0, tl // 8, group, car_ref[...])
    car_ref[...] = carry
    y_ref[...] = (u_ref[...] * _gelu_tanh(gate_ref[...])).astype(BF16)

    @pl.when(c == pl.num_programs(1) - 1)
    def _():
        h_ref[...] = carry
        cs_ref[...] = tail[5:8, :]


def _rglru_prompt(proj, batch, seq, conv_w, conv_b, wa, ba, wi, bi, lam):
    nc = seq // RG_TL
    t = batch * seq
    vec =pl.BlockSpec((1, D_RG), lambda b, c: (0, 0))
    blk = pl.BlockSpec((RG_BLOCKS, RG_BLOCK_W, RG_BLOCK_W), lambda b, c: (0, 0, 0))
    return pl.pallas_call(
        _rglru_prompt_body,
        grid=(batch, nc),
        in_specs=[
            pl.BlockSpec((RG_TL, D_RG), lambda b, c: (b * nc + c, 0)),
            pl.BlockSpec((RG_TL, D_RG), lambda b, c: (b * nc + c, 1)),
            pl.BlockSpec((CONV_W, D_RG), lambda b, c: (0, 0)),
            vec, blk, vec, blk, vec, vec,
        ],
        out_specs=[
            pl.BlockSpec((RG_TL, D_RG), lambda b, c: (b * nc + c, 0)),
            pl.BlockSpec((None, 1, D_RG), lambda b, c: (b, 0, 0)),
            pl.BlockSpec((None, CONV_W - 1, D_RG), lambda b, c: (b, 0, 0)),
        ],
        out_shape=[
            jax.ShapeDtypeStruct((t, D_RG), BF16),
            jax.ShapeDtypeStruct((batch, 1, D_RG), F32),
            jax.ShapeDtypeStruct((batch, CONV_W - 1, D_RG), F32),
        ],
        scratch_shapes=[
            pltpu.VMEM((RG_TL + 8, D_RG), F32),
            pltpu.VMEM((RG_TL, D_RG), F32),
            pltpu.VMEM((RG_TL, D_RG), F32),
            pltpu.VMEM((1, D_RG), F32),
        ],
        compiler_params=_cparams(("arbitrary", "arbitrary")),
        name="rglru_prompt",
    )(proj, proj, conv_w, conv_b, wa, ba, wi, bi, lam)


def _group_rmsnorm(y, g_row):
    outs = []
    for g in range(SSD_GROUPS):
        yg = y[:, g * D_BC:(g + 1) * D_BC]
        outs.append(yg * lax.rsqrt(jnp.mean(yg * yg, axis=-1, keepdims=True) + EPS))
    return jnp.concatenate(outs, axis=1) * g_row


def _ssd_prompt_body(xbc_ref, z_ref, dt_ref, cw_ref, cb_ref, dtb_ref, alog_ref, dexp_ref, ng_ref,
                     y_ref, s_out_ref, cs_ref, ext_ref, s_ref):
    c = pl.program_id(1)
    L = SSD_CHUNK

    @pl.when(c == 0)
    def _():
        ext_ref[0:8, :] = jnp.zeros((8, D_XBC), F32)
        s_ref[...] = jnp.zeros_like(s_ref)

    ext_ref[8:8 + L, :] = xbc_ref[...]
    xc = cb_ref[...] + ext_ref[5:5 + L, :] * cw_ref[0:1, :]
    xc = xc + ext_ref[6:6 + L, :] * cw_ref[1:2, :]
    xc = xc + ext_ref[7:7 + L, :] * cw_ref[2:3, :]
    xc = xc + ext_ref[8:8 + L, :] * cw_ref[3:4, :]
    tail = ext_ref[L:L + 8, :]
    ext_ref[0:8, :] = tail
    xc = _silu(xc)
    xs = xc[:, :D_SSD]
    bm = xc[:, D_SSD:D_SSD + D_BC].astype(BF16)
    cm = xc[:, D_SSD + D_BC:].astype(BF16)

    lane = lax.broadcasted_iota(I32, (1, LANES), 1)
    a_neg = jnp.where(lane < SSD_HEADS, -jnp.exp(alog_ref[...]), 0.0)
    dt = _softplus(dt_ref[...] + dtb_ref[...])
    da = dt * a_neg
    ri = lax.broadcasted_iota(I32, (L, L), 0)
    ci = lax.broadcasted_iota(I32, (L, L), 1)
    causal = ci <= ri
    tri = jnp.where(causal, 1.0, 0.0).astype(F32)
    acum = jnp.dot(tri, da, precision=HIGHEST, preferred_element_type=F32)
    acum_t = acum.T
    a_last = acum[L - 1:L, :]

    e_b16 = _head_expand_matrix(BF16)
    stacked = jnp.concatenate([dt, jnp.exp(a_last - acum), jnp.exp(acum)], axis=0)
    expd = _expand_heads(stacked, e_b16)
    dt_e, dend_e, ea_e = expd[0:L], expd[L:2 * L], expd[2 * L:3 * L]
    xdt = xs * dt_e
    xdt_b = xdt.astype(BF16)
    w_b = (xdt * dend_e)

    lane_l = lax.broadcasted_iota(I32, (L, LANES), 1)
    lo = lane_l < SSD_HEAD_DIM
    y_parts = []
    for g in range(SSD_GROUPS):
        cg = cm[:, g * SSD_STATE:(g + 1) * SSD_STATE]
        bg = bm[:, g * SSD_STATE:(g + 1) * SSD_STATE]
        cb = lax.dot_general(cg, bg, (((1,), (1,)), ((), ())), preferred_element_type=F32)
        hpg = SSD_HEADS // SSD_GROUPS
        for jp in range(hpg // 2):
            ms = []
            for h in (g * hpg + 2 * jp, g * hpg + 2 * jp + 1):
                seg = acum[:, h:h + 1] - acum_t[h:h + 1, :]
                decay = jnp.exp(jnp.where(causal, seg, -jnp.inf))
                ms.append((cb * decay).astype(BF16))
            col = (g * hpg + 2 * jp) * SSD_HEAD_DIM
            xp = xdt_b[:, col:col + LANES]
            zero = jnp.zeros_like(xp)
            rhs = jnp.concatenate([jnp.where(lo, xp, zero), jnp.where(lo, zero, xp)], axis=0)
            y_parts.append(jnp.dot(jnp.concatenate(ms, axis=1), rhs, preferred_element_type=F32))
    y_diag = jnp.concatenate(y_parts, axis=1)

    y_off_parts = []
    for g in range(SSD_GROUPS):
        cg = cm[:, g * SSD_STATE:(g + 1) * SSD_STATE]
        sg = s_ref[g * D_BC:(g + 1) * D_BC, :].astype(BF16)
        y_off_parts.append(lax.dot_general(cg, sg, (((1,), (1,)), ((), ())), preferred_element_type=F32))
    y_off = jnp.concatenate(y_off_parts, axis=1) * ea_e

    cd_col = jnp.exp(acum_t[:, L - 1:L])
    for g in range(SSD_GROUPS):
        bg = bm[:, g * SSD_STATE:(g + 1) * SSD_STATE]
        wg_t = w_b[:, g * D_BC:(g + 1) * D_BC].T.astype(BF16)
        upd = jnp.dot(wg_t, bg, preferred_element_type=F32)
        hpg = SSD_HEADS // SSD_GROUPS
        for e in range(hpg):
            h = g * hpg + e
            r0 = h * SSD_HEAD_DIM
            cd = jnp.broadcast_to(cd_col[h:h + 1, :], (SSD_HEAD_DIM, SSD_STATE))
            s_ref[r0:r0 + SSD_HEAD_DIM, :] = (cd * s_ref[r0:r0 + SSD_HEAD_DIM, :]
                                              + upd[e * SSD_HEAD_DIM:(e + 1) * SSD_HEAD_DIM, :])

    y = (y_diag + y_off + dexp_ref[...] * xs) * _silu(z_ref[...])
    y_ref[...] = _group_rmsnorm(y, ng_ref[...]).astype(BF16)

    @pl.when(c == pl.num_programs(1) - 1)
    def _():
        s_out_ref[...] = s_ref[...]
        cs_ref[...] = tail[5:8, :]


def _ssd_prompt(proj, dt_raw, batch, seq, conv_w, conv_b, dt_bias, a_log, d_exp, norm_g):
    nc = seq // SSD_CHUNK
    L = SSD_CHUNK
    hp = SSD_HEADS * SSD_HEAD_DIM
    const = lambda b, c: (0, 0)
    return pl.pallas_call(
        _ssd_prompt_body,
        grid=(batch, nc),
        in_specs=[
            pl.BlockSpec((L, D_XBC), lambda b, c: (b * nc + c, 2)),
            pl.BlockSpec((L, D_SSD), lambda b, c: (b * nc + c, 2)),
            pl.BlockSpec((L, LANES), lambda b, c: (b * nc + c, 0)),
            pl.BlockSpec((CONV_W, D_XBC), const),
            pl.BlockSpec((1, D_XBC), const),
            pl.BlockSpec((1, LANES), const),
            pl.BlockSpec((1, LANES), const),
            pl.BlockSpec((1, D_SSD), const),
            pl.BlockSpec((1, D_SSD), const),
        ],
        out_specs=[
            pl.BlockSpec((L, D_SSD), lambda b, c: (b * nc + c, 0)),
            pl.BlockSpec((None, hp, SSD_STATE), lambda b, c: (b, 0, 0)),
            pl.BlockSpec((None, CONV_W - 1, D_XBC), lambda b, c: (b, 0, 0)),
        ],
        out_shape=[
            jax.ShapeDtypeStruct((batch * seq, D_SSD), BF16),
            jax.ShapeDtypeStruct((batch, hp, SSD_STATE), F32),
            jax.ShapeDtypeStruct((batch, CONV_W - 1, D_XBC), F32),
        ],
        scratch_shapes=[
            pltpu.VMEM((L + 8, D_XBC), F32),
            pltpu.VMEM((hp, SSD_STATE), F32),
        ],
        compiler_params=_cparams(("arbitrary", "arbitrary")),
        name="ssd_prompt",
    )(proj, proj, dt_raw, conv_w, conv_b, dt_bias, a_log, d_exp, norm_g)


def _mix_sample_body(p_ref, dt_ref, h0_ref, rgc0_ref, s0_ref, sc0_ref,
                     rcw_ref, rcb_ref, wa_ref, ba_ref, wi_ref, bi_ref, lam_ref,
                     scw_ref, scb_ref, dtb_ref, alog_ref, dexp_ref, ng_ref,
                     y_ref, h_ref, rgc_ref, s_ref, sc_ref):
    nl = p_ref.shape[0]
    bb = p_ref.shape[1]
    o_gate, o_z, o_xbc = D_RG, 2 * D_RG, 2 * D_RG + D_SSD

    def col(l, off, width):
        return p_ref[l, :, off:off + width]

    hist = [rgc0_ref[k] for k in range(CONV_W - 1)]
    raw = hist + [col(l, 0, D_RG) for l in range(nl)]
    xcs = []
    for l in range(nl):
        acc = rcb_ref[...] + raw[l] * rcw_ref[0:1, :]
        for k in range(1, CONV_W):
            acc = acc + raw[l + k] * rcw_ref[k:k + 1, :]
        xcs.append(acc)
    a, u = _rg_gates(jnp.concatenate(xcs, axis=0), wa_ref, ba_ref, wi_ref, bi_ref, lam_ref)
    h = h0_ref[...]
    for l in range(nl):
        h = a[l * bb:(l + 1) * bb] * h + u[l * bb:(l + 1) * bb]
        y_ref[l, :, 0:D_RG] = h * _gelu_tanh(col(l, o_gate, D_RG))
    h_ref[...] = h
    for k in range(CONV_W - 1):
        rgc_ref[k] = raw[nl + k]

    hist = [sc0_ref[k] for k in range(CONV_W - 1)]
    raw = hist + [col(l, o_xbc, D_XBC) for l in range(nl)]
    for k in range(CONV_W - 1):
        sc_ref[k] = raw[nl + k]
    xcs = []
    for l in range(nl):
        acc = scb_ref[...] + raw[l] * scw_ref[0:1, :]
        for k in range(1, CONV_W):
            acc = acc + raw[l + k] * scw_ref[k:k + 1, :]
        xcs.append(_silu(acc))
    xs = [x[:, :D_SSD] for x in xcs]
    bms = [x[:, D_SSD:D_SSD + D_BC].astype(BF16) for x in xcs]
    cms = [x[:, D_SSD + D_BC:].astype(BF16) for x in xcs]

    lane = lax.broadcasted_iota(I32, (1, LANES), 1)
    a_neg = jnp.where(lane < SSD_HEADS, -jnp.exp(alog_ref[...]), 0.0)
    dts = [_softplus(dt_ref[l] + dtb_ref[...]) for l in range(nl)]
    acums = []
    run = jnp.zeros((bb, LANES), F32)
    for l in range(nl):
        run = run + dts[l] * a_neg
        acums.append(run)
    a_last = acums[nl - 1]

    e_b16 = _head_expand_matrix(BF16)
    stacked = jnp.concatenate(dts + [jnp.exp(a_last - ac) for ac in acums] + [jnp.exp(ac) for ac in acums], axis=0)
    expd = _expand_heads(stacked, e_b16)
    dt_e = [expd[l * bb:(l + 1) * bb] for l in range(nl)]
    dend_e = [expd[(nl + l) * bb:(nl + l + 1) * bb] for l in range(nl)]
    ea_e = [expd[(2 * nl + l) * bb:(2 * nl + l + 1) * bb] for l in range(nl)]
    xdt = [xs[l] * dt_e[l] for l in range(nl)]
    xdt_r = [x.astype(BF16).astype(F32) for x in xdt]
    rows = nl * bb
    pad_rows = LANES - rows
    w_all = jnp.concatenate([(xdt[l] * dend_e[l]) for l in range(nl)]
                            + [jnp.zeros((pad_rows, D_SSD), F32)], axis=0)
    c_all = jnp.concatenate(cms, axis=0)
    b_all = jnp.concatenate(bms + [jnp.zeros((pad_rows, D_BC), BF16)], axis=0)

    r = lax.broadcasted_iota(I32, (D_BC, LANES), 0)
    cidx = lax.broadcasted_iota(I32, (D_BC, LANES), 1)
    gsum = jnp.where((lax.shift_right_logical(r, 7) == lax.shift_right_logical(cidx, 3)) & (cidx < SSD_HEADS),
                     1.0, 0.0).astype(F32)
    pairs = [(l, s) for l in range(nl) for s in range(l + 1)]
    prods = jnp.concatenate([cms[l].astype(F32) * bms[s].astype(F32) for (l, s) in pairs], axis=0)
    cbh = jnp.dot(prods, gsum, precision=HIGHEST, preferred_element_type=F32)
    m_list = []
    for idx, (l, s) in enumerate(pairs):
        decay = jnp.exp(acums[l] - acums[s])
        m_list.append((cbh[idx * bb:(idx + 1) * bb] * decay).astype(BF16))
    m_e = jnp.dot(jnp.concatenate(m_list, axis=0), e_b16, preferred_element_type=F32)
    y_diag = []
    for l in range(nl):
        acc = None
        for idx, (ll, s) in enumerate(pairs):
            if ll != l:
                continue
            term = m_e[idx * bb:(idx + 1) * bb] * xdt_r[s]
            acc = term if acc is None else acc + term
        y_diag.append(acc)

    cd_t = jnp.concatenate([jnp.exp(a_last), jnp.zeros((LANES - bb, LANES), F32)], axis=0).T
    row_seq = lax.broadcasted_iota(I32, (rows, 1), 0) & (bb - 1)
    row_pad = lax.broadcasted_iota(I32, (LANES, 1), 0)
    lane_q = lax.broadcasted_iota(I32, (LANES, LANES), 1)
    hpg = SSD_HEADS // SSD_GROUPS

    def seq_step(q, y_off):
        mine = row_seq == q
        mine_pad = ((row_pad & (bb - 1)) == q) & (row_pad < rows)
        cd_q = jnp.sum(jnp.where(lane_q == q, cd_t, 0.0), axis=1, keepdims=True)
        cd_q = jnp.broadcast_to(cd_q, (LANES, SSD_STATE))
        parts = []
        for g in range(SSD_GROUPS):
            s0 = s0_ref[q, g * D_BC:(g + 1) * D_BC, :]
            cg = c_all[:, g * SSD_STATE:(g + 1) * SSD_STATE]
            yq = lax.dot_general(cg, s0.astype(BF16), (((1,), (1,)), ((), ())), preferred_element_type=F32)
            parts.append(yq)
            bg = b_all[:, g * SSD_STATE:(g + 1) * SSD_STATE]
            wq = jnp.where(mine_pad, w_all[:, g * D_BC:(g + 1) * D_BC], 0.0)
            upd = jnp.dot(wq.T.astype(BF16), bg, preferred_element_type=F32)
            for e in range(hpg):
                hh = g * hpg + e
                cd = jnp.broadcast_to(cd_q[hh:hh + 1, :], (SSD_HEAD_DIM, SSD_STATE))
                s_ref[q, hh * SSD_HEAD_DIM:(hh + 1) * SSD_HEAD_DIM, :] = (
                    cd * s0[e * SSD_HEAD_DIM:(e + 1) * SSD_HEAD_DIM, :]
                    + upd[e * SSD_HEAD_DIM:(e + 1) * SSD_HEAD_DIM, :])
        yq_all = jnp.concatenate(parts, axis=1)
        return jnp.where(mine, yq_all, y_off)

    y_off = lax.fori_loop(0, bb, seq_step, jnp.zeros((rows, D_SSD), F32))

    for l in range(nl):
        y = (y_diag[l] + y_off[l * bb:(l + 1) * bb] * ea_e[l] + dexp_ref[...] * xs[l]) * _silu(col(l, o_z, D_SSD))
        y_ref[l, :, D_RG:2 * D_MODEL] = _group_rmsnorm(y, ng_ref[...])


def _mix_sample(proj3, dt3, row0, nl, n_seq, h0, rgc0, s0, sc0, rg_params, ssd_params):
    bb = SAMPLE_BB
    hp = SSD_HEADS * SSD_HEAD_DIM
    blk0 = row0 // (n_seq * nl)
    const2 = lambda i: (0, 0)
    rg_specs = [
        pl.BlockSpec((CONV_W, D_RG), const2), pl.BlockSpec((1, D_RG), const2),
        pl.BlockSpec((RG_BLOCKS, RG_BLOCK_W, RG_BLOCK_W), lambda i: (0, 0, 0)), pl.BlockSpec((1, D_RG), const2),
        pl.BlockSpec((RG_BLOCKS, RG_BLOCK_W, RG_BLOCK_W), lambda i: (0, 0, 0)), pl.BlockSpec((1, D_RG), const2),
        pl.BlockSpec((1, D_RG), const2),
    ]
    ssd_specs = [
        pl.BlockSpec((CONV_W, D_XBC), const2), pl.BlockSpec((1, D_XBC), const2),
        pl.BlockSpec((1, LANES), const2), pl.BlockSpec((1, LANES), const2),
        pl.BlockSpec((1, D_SSD), const2), pl.BlockSpec((1, D_SSD), const2),
    ]
    return pl.pallas_call(
        _mix_sample_body,
        grid=(n_seq // bb,),
        in_specs=[
            pl.BlockSpec((nl, bb, D_MAIN), lambda i: (blk0, i, 0)),
            pl.BlockSpec((nl, bb, LANES), lambda i: (blk0, i, 0)),
            pl.BlockSpec((bb, D_RG), lambda i: (i, 0)),
            pl.BlockSpec((CONV_W - 1, bb, D_RG), lambda i: (0, i, 0)),
            pl.BlockSpec((bb, hp, SSD_STATE), lambda i: (i, 0, 0)),
            pl.BlockSpec((CONV_W - 1, bb, D_XBC), lambda i: (0, i, 0)),
        ] + rg_specs + ssd_specs,
        out_specs=[
            pl.BlockSpec((nl, bb, 2 * D_MODEL), lambda i: (0, i, 0)),
            pl.BlockSpec((bb, D_RG), lambda i: (i, 0)),
            pl.BlockSpec((CONV_W - 1, bb, D_RG), lambda i: (0, i, 0)),
            pl.BlockSpec((bb, hp, SSD_STATE), lambda i: (i, 0, 0)),
            pl.BlockSpec((CONV_W - 1, bb, D_XBC), lambda i: (0, i, 0)),
        ],
        out_shape=[
            jax.ShapeDtypeStruct((nl, n_seq, 2 * D_MODEL), F32),
            jax.ShapeDtypeStruct((n_seq, D_RG), F32),
            jax.ShapeDtypeStruct((CONV_W - 1, n_seq, D_RG), F32),
            jax.ShapeDtypeStruct((n_seq, hp, SSD_STATE), F32),
            jax.ShapeDtypeStruct((CONV_W - 1, n_seq, D_XBC), F32),
        ],
        compiler_params=_cparams(("arbitrary",)),
        name="mix_sample",
    )(proj3, dt3, h0, rgc0, s0, sc0, *rg_params, *ssd_params)


def _route_body(x_ref, g_ref, wr_ref, br_ref, hp_ref, e_ref, gate_ref, rank_ref, cnt_ref, run_ref):
    i = pl.program_id(0)
    tm = x_ref.shape[0]

    @pl.when(i == 0)
    def _():
        run_ref[...] = jnp.zeros_like(run_ref)

    x = x_ref[...]
    h = x * lax.rsqrt(jnp.mean(x * x, axis=-1, keepdims=True) + EPS) * g_ref[...]
    hp_ref[...] = h

    lane = lax.broadcasted_iota(I32, (tm, LANES), 1).astype(F32)
    logits = jnp.dot(h, wr_ref[...], precision=HIGHEST, preferred_element_type=F32) + br_ref[...]
    work = jnp.where(lane < N_EXPERTS, logits, -jnp.inf)
    vals, idxs = [], []
    multi = jnp.zeros((tm, LANES), F32)
    for _ in range(TOP_K):
        m = jnp.max(work, axis=-1, keepdims=True)
        idx = jnp.min(jnp.where(work == m, lane, float(LANES)), axis=-1, keepdims=True)
        hit = lane == idx
        vals.append(m)
        idxs.append(idx)
        multi = jnp.where(hit, 1.0, multi)
        work = jnp.where(hit, -jnp.inf, work)
    ex = [jnp.exp(v - vals[0]) for v in vals]
    den = ex[0] + ex[1] + ex[2] + ex[3]
    gates = [e / den for e in ex]

    ri = lax.broadcasted_iota(I32, (tm, tm), 0)
    ci = lax.broadcasted_iota(I32, (tm, tm), 1)
    strict = jnp.where(ci < ri, 1.0, 0.0).astype(BF16)
    before = jnp.dot(strict, multi.astype(BF16), preferred_element_type=F32) + run_ref[...]
    ranks = [jnp.sum(jnp.where(lane == idx, before, 0.0), axis=-1, keepdims=True) for idx in idxs]
    run_ref[...] = run_ref[...] + jnp.sum(multi, axis=0, keepdims=True)

    lane4 = lax.broadcasted_iota(I32, (tm, TOP_K), 1)
    e_out = jnp.zeros((tm, TOP_K), I32)
    g_out = jnp.zeros((tm, TOP_K), F32)
    r_out = jnp.zeros((tm, TOP_K), I32)
    for k in range(TOP_K):
        e_out = jnp.where(lane4 == k, idxs[k].astype(I32), e_out)
        g_out = jnp.where(lane4 == k, gates[k], g_out)
        r_out = jnp.where(lane4 == k, ranks[k].astype(I32), r_out)
    e_ref[...] = e_out
    gate_ref[...] = g_out
    rank_ref[...] = r_out
    cnt_ref[...] = run_ref[...].astype(I32)


def _route(x1, g_ffn, w_router_pad, b_router_pad):
    t = x1.shape[0]
    tm = ROUTE_TM
    const = lambda i: (0, 0)
    return pl.pallas_call(
        _route_body,
        grid=(t // tm,),
        in_specs=[
            pl.BlockSpec((tm, D_MODEL), lambda i: (i, 0)),
            pl.BlockSpec((1, D_MODEL), const),
            pl.BlockSpec((D_MODEL, LANES), const),
            pl.BlockSpec((1, LANES), const),
        ],
        out_specs=[
            pl.BlockSpec((tm, D_MODEL), lambda i: (i, 0)),
            pl.BlockSpec((tm, TOP_K), lambda i: (i, 0)),
            pl.BlockSpec((tm, TOP_K), lambda i: (i, 0)),
            pl.BlockSpec((tm, TOP_K), lambda i: (i, 0)),
            pl.BlockSpec((1, LANES), const),
        ],
        out_shape=[
            jax.ShapeDtypeStruct((t, D_MODEL), F32),
            jax.ShapeDtypeStruct((t, TOP_K), I32),
            jax.ShapeDtypeStruct((t, TOP_K), F32),
            jax.ShapeDtypeStruct((t, TOP_K), I32),
            jax.ShapeDtypeStruct((1, LANES), I32),
        ],
        scratch_shapes=[pltpu.VMEM((1, LANES), F32)],
        compiler_params=_cparams(("arbitrary",)),
        name="route",
    )(x1, g_ffn, w_router_pad, b_router_pad)


def _scatter_body(seg_ref, cnt_ref, e_ref, rank_ref, h_ref, xs_ref, zero_ref, sem, zsem):
    i = pl.program_id(0)
    tm = h_ref.shape[0]
    n = e_ref.shape[0]

    def issue(blk, c):
        for u in range(DMA_UNROLL):
            a = blk * DMA_UNROLL + u
            pos = seg_ref[e_ref[a]] + rank_ref[a]
            t = lax.shift_right_logical(a, 2)
            pltpu.make_async_copy(h_ref.at[pl.ds(t, 1), :], xs_ref.at[pl.ds(pos, 1), :], sem).start()
        return c

    lax.fori_loop(0, n // DMA_UNROLL, issue, 0)
    for _ in range(TOP_K):
        pltpu.make_async_copy(h_ref, xs_ref.at[pl.ds(0, tm), :], sem).wait()

    @pl.when(i == pl.num_programs(0) - 1)
    def _():
        zero_ref[...] = jnp.zeros_like(zero_ref)
        pad = EXPERT_PAD

        def row_fill(row):
            return pltpu.make_async_copy(zero_ref.at[pl.ds(0, 1), :], xs_ref.at[pl.ds(row, 1), :], zsem)

        def block_fill(blk):
            dst = pl.multiple_of(blk * pad, pad)
            return pltpu.make_async_copy(zero_ref, xs_ref.at[pl.ds(dst, pad), :], zsem)

        def for_each(lo, hi, fn):
            def step(r, c):
                fn(r)
                return c
            lax.fori_loop(lo, hi, step, 0)

        def per_expert(e, used):
            cnt = cnt_ref[e]
            hi = ((cnt + pad - 1) // pad) * pad
            base = seg_ref[e]
            for_each(cnt, hi, lambda r: row_fill(base + r).start())
            for_each(cnt, hi, lambda r: row_fill(base + r).wait())
            return used + hi // pad

        used = lax.fori_loop(0, N_EXPERTS, per_expert, jnp.int32(0))
        n_blocks = xs_ref.shape[0] // pad
        for_each(used, n_blocks, lambda blk: block_fill(blk).start())
        for_each(used, n_blocks, lambda blk: block_fill(blk).wait())


def _scatter(seg_start, counts, e_flat, rank_flat, h2, n_slots):
    t = h2.shape[0]
    tm = SCATTER_TM
    grid_spec = pltpu.PrefetchScalarGridSpec(
        num_scalar_prefetch=2,
        grid=(t // tm,),
        in_specs=[
            pl.BlockSpec((tm * TOP_K,), lambda i, seg, cnt: (i,), memory_space=pltpu.SMEM),
            pl.BlockSpec((tm * TOP_K,), lambda i, seg, cnt: (i,), memory_space=pltpu.SMEM),
            pl.BlockSpec((tm, D_MODEL), lambda i, seg, cnt: (i, 0)),
        ],
        out_specs=pl.BlockSpec(memory_space=pl.ANY),
        scratch_shapes=[pltpu.VMEM((EXPERT_PAD, D_MODEL), F32), pltpu.SemaphoreType.DMA(()),
                        pltpu.SemaphoreType.DMA(())],
    )
    return pl.pallas_call(
        _scatter_body,
        grid_spec=grid_spec,
        out_shape=jax.ShapeDtypeStruct((n_slots, D_MODEL), F32),
        compiler_params=_cparams(("arbitrary",)),
        name="scatter_rows",
    )(seg_start, counts, e_flat, rank_flat, h2)


def _experts_body(sbe_ref, sbs_ref, sbn_ref, used_ref, xs_ref, wg_ref, wu_ref, bg_ref, bu_ref, wd_ref, bd_ref,
                  ys_ref, xin_ref, xbuf_ref, acc_ref, wgu_ref, wdb_ref, in_sem, out_sem):
    del sbe_ref
    sb = pl.program_id(0)
    j = pl.program_id(1)
    nj = pl.num_programs(1)
    n_sb = pl.num_programs(0)
    nblk = sbn_ref[sb]
    start = sbs_ref[sb]
    cb = EXPERT_PAD
    bm = EXPERT_BM
    tf = wd_ref.shape[0]

    def in_copy(which, r):
        src = pl.multiple_of(sbs_ref[which] + r * cb, cb)
        dst = pl.multiple_of(r * cb, cb)
        return pltpu.make_async_copy(xs_ref.at[pl.ds(src, cb), :], xin_ref.at[pl.ds(dst, cb), :], in_sem.at[r])

    def out_copy(r):
        src = pl.multiple_of(r * cb, cb)
        dst = pl.multiple_of(start + r * cb, cb)
        return pltpu.make_async_copy(acc_ref.at[pl.ds(src, cb), :], ys_ref.at[pl.ds(dst, cb), :], out_sem)

    def zero_copy(blk):
        dst = pl.multiple_of(blk * cb, cb)
        return pltpu.make_async_copy(acc_ref.at[pl.ds(0, cb), :], ys_ref.at[pl.ds(dst, cb), :], out_sem)

    def for_each(lo, hi, fn):
        def step(r, c):
            fn(r)
            return c
        lax.fori_loop(lo, hi, step, 0)

    @pl.when((j == 0) & (sb == 0))
    def _():
        for_each(0, nblk, lambda r: in_copy(0, r).start())

    @pl.when(j == 0)
    def _():
        def stage(r):
            in_copy(sb, r).wait()
            r0 = pl.multiple_of(r * cb, cb)
            xbuf_ref[pl.ds(r0, cb), :] = xin_ref[pl.ds(r0, cb), :].astype(BF16)
            acc_ref[pl.ds(r0, cb), :] = jnp.broadcast_to(bd_ref[...], (cb, D_MODEL))

        for_each(0, nblk, stage)

        @pl.when(sb + 1 < n_sb)
        def _():
            nxt = jnp.minimum(sb + 1, n_sb - 1)
            for_each(0, sbn_ref[nxt], lambda r: in_copy(nxt, r).start())

    @pl.when(nblk > 0)
    def _():
        def cast_rows(s):
            r = pl.multiple_of(s * cb, cb)
            wgu_ref[pl.ds(r, cb), 0:tf] = wg_ref[pl.ds(r, cb), :].astype(BF16)
            wgu_ref[pl.ds(r, cb), tf:2 * tf] = wu_ref[pl.ds(r, cb), :].astype(BF16)

        for_each(0, D_MODEL // cb, cast_rows)
        wdb_ref[...] = wd_ref[...].astype(BF16)
        bias_gu = jnp.concatenate([bg_ref[...], bu_ref[...]], axis=1)

        def sub(r0, rows):
            x = xbuf_ref[pl.ds(r0, rows), :]
            gu = jnp.dot(x, wgu_ref[...], preferred_element_type=F32) + bias_gu
            gate = jnp.minimum(gu[:, :tf], SWIGLU_LIMIT)
            up = jnp.clip(gu[:, tf:], -SWIGLU_LIMIT, SWIGLU_LIMIT)
            glu = gate * _sigmoid(SWIGLU_ALPHA * gate)
            act = ((up + 1.0) * glu).astype(BF16)
            acc_ref[pl.ds(r0, rows), :] += jnp.dot(act, wdb_ref[...], preferred_element_type=F32)

            @pl.when(j == nj - 1)
            def _():
                for b in range(rows // cb):
                    out_copy(r0 // cb + b).start()

        per_sub = bm // cb
        n_full = nblk // per_sub
        for_each(0, n_full, lambda i: sub(pl.multiple_of(i * bm, bm), bm))
        done = n_full * per_sub
        piece = per_sub // 2
        while piece >= 1:
            here = done

            @pl.when((nblk & piece) != 0)
            def _():
                sub(pl.multiple_of(here * cb, cb), piece * cb)

            done = done + (nblk & piece)
            piece //= 2

    @pl.when(j == nj - 1)
    def _():
        for_each(0, nblk, lambda r: out_copy(r).wait())

    @pl.when((sb == pl.num_programs(0) - 1) & (j == nj - 1))
    def _():
        acc_ref[0:cb, :] = jnp.zeros((cb, D_MODEL), F32)
        n_blocks = ys_ref.shape[0] // cb
        for_each(used_ref[0], n_blocks, lambda blk: zero_copy(blk).start())
        for_each(used_ref[0], n_blocks, lambda blk: zero_copy(blk).wait())


def _experts(sb_expert, sb_start, sb_nsub, used_blocks, xsorted, w_gu, b_gu, w_down, b_down):
    n_sb = sb_expert.shape[0]
    n_slots = xsorted.shape[0]
    tf = EXPERT_TF
    nj = D_FF // tf

    def spec(shape, fn):
        def index_map(sb, j, sbe, sbs, sbn, used):
            jj = jnp.where(sbn[sb] > 0, j, nj - 1)
            return fn(sbe[sb], jj)
        return pl.BlockSpec(shape, index_map)

    grid_spec = pltpu.PrefetchScalarGridSpec(
        num_scalar_prefetch=4,
        grid=(n_sb, nj),
        in_specs=[
            pl.BlockSpec(memory_space=pl.ANY),
            spec((None, D_MODEL, tf), lambda e, jj: (e, 0, jj)),
            spec((None, D_MODEL, tf), lambda e, jj: (e, 0, nj + jj)),
            spec((None, 1, tf), lambda e, jj: (e, 0, jj)),
            spec((None, 1, tf), lambda e, jj: (e, 0, nj + jj)),
            spec((None, tf, D_MODEL), lambda e, jj: (e, jj, 0)),
            spec((None, 1, D_MODEL), lambda e, jj: (e, 0, 0)),
        ],
        out_specs=pl.BlockSpec(memory_space=pl.ANY),
        scratch_shapes=[
            pltpu.VMEM((EXPERT_R, D_MODEL), F32),
            pltpu.VMEM((EXPERT_R, D_MODEL), BF16),
            pltpu.VMEM((EXPERT_R, D_MODEL), F32),
            pltpu.VMEM((D_MODEL, 2 * tf), BF16),
            pltpu.VMEM((tf, D_MODEL), BF16),
            pltpu.SemaphoreType.DMA((EXPERT_R // EXPERT_PAD,)),
            pltpu.SemaphoreType.DMA(()),
        ],
    )
    return pl.pallas_call(
        _experts_body,
        grid_spec=grid_spec,
        out_shape=jax.ShapeDtypeStruct((n_slots, D_MODEL), F32),
        compiler_params=_cparams(("arbitrary", "arbitrary")),
        name="experts",
    )(sb_expert, sb_start, sb_nsub, used_blocks, xsorted, w_gu, w_gu, b_gu, b_gu, w_down, b_down)


def _combine_body(seg_ref, e_ref, rank_ref, e_next_ref, rank_next_ref, x1_ref, gate_ref, gf_ref, ys_ref,
                  yp_ref, ysm_ref, buf_ref, sem, *, n_prompt_tiles):
    i = pl.program_id(0)
    tm = x1_ref.shape[0]
    n = e_ref.shape[0]
    half = i % 2

    def gather(e_r, rank_r, dst):
        def issue(blk, c):
            for u in range(DMA_UNROLL):
                a = blk * DMA_UNROLL + u
                pos = seg_ref[e_r[a]] + rank_r[a]
                t = lax.shift_right_logical(a, 2)
                pltpu.make_async_copy(ys_ref.at[pl.ds(pos, 1), :], buf_ref.at[dst, u % TOP_K, pl.ds(t, 1), :],
                                      sem.at[dst]).start()
            return c

        lax.fori_loop(0, n // DMA_UNROLL, issue, 0)

    @pl.when(i == 0)
    def _():
        gather(e_ref, rank_ref, 0)

    @pl.when(i + 1 < pl.num_programs(0))
    def _():
        gather(e_next_ref, rank_next_ref, 1 - half)

    for k in range(TOP_K):
        pltpu.make_async_copy(ys_ref.at[pl.ds(0, tm), :], buf_ref.at[half, k], sem.at[half]).wait()

    g = gate_ref[...]
    acc = buf_ref[half, 0] * g[:, 0:1]
    for k in range(1, TOP_K):
        acc = acc + buf_ref[half, k] * g[:, k:k + 1]
    x = x1_ref[...] + acc
    y = x * lax.rsqrt(jnp.mean(x * x, axis=-1, keepdims=True) + EPS) * gf_ref[...]

    @pl.when(i < n_prompt_tiles)
    def _():
        yp_ref[...] = y

    @pl.when(i >= n_prompt_tiles)
    def _():
        ysm_ref[...] = y


def _combine(seg_start, e_flat, rank_flat, x1, gates, g_final, ysorted, t_prompt):
    t = x1.shape[0]
    tm = COMBINE_TM
    npt = t_prompt // tm
    n_tiles = t // tm
    grid_spec = pltpu.PrefetchScalarGridSpec(
        num_scalar_prefetch=1,
        grid=(n_tiles,),
        in_specs=[
            pl.BlockSpec((tm * TOP_K,), lambda i, seg: (i,), memory_space=pltpu.SMEM),
            pl.BlockSpec((tm * TOP_K,), lambda i, seg: (i,), memory_space=pltpu.SMEM),
            pl.BlockSpec((tm * TOP_K,), lambda i, seg: (jnp.minimum(i + 1, n_tiles - 1),), memory_space=pltpu.SMEM),
            pl.BlockSpec((tm * TOP_K,), lambda i, seg: (jnp.minimum(i + 1, n_tiles - 1),), memory_space=pltpu.SMEM),
            pl.BlockSpec((tm, D_MODEL), lambda i, seg: (i, 0)),
            pl.BlockSpec((tm, TOP_K), lambda i, seg: (i, 0)),
            pl.BlockSpec((1, D_MODEL), lambda i, seg: (0, 0)),
            pl.BlockSpec(memory_space=pl.ANY),
        ],
        out_specs=[
            pl.BlockSpec((tm, D_MODEL), lambda i, seg: (jnp.minimum(i, npt - 1), 0)),
            pl.BlockSpec((tm, D_MODEL), lambda i, seg: (jnp.maximum(i - npt, 0), 0)),
        ],
        scratch_shapes=[pltpu.VMEM((2, TOP_K, tm, D_MODEL), F32), pltpu.SemaphoreType.DMA((2,))],
    )
    return pl.pallas_call(
        functools.partial(_combine_body, n_prompt_tiles=npt),
        grid_spec=grid_spec,
        out_shape=[
            jax.ShapeDtypeStruct((t_prompt, D_MODEL), F32),
            jax.ShapeDtypeStruct((t - t_prompt, D_MODEL), F32),
        ],
        compiler_params=_cparams(("arbitrary",)),
        name="combine",
    )(seg_start, e_flat, rank_flat, e_flat, rank_flat, x1, gates, g_final, ysorted)


def _expert_tables(counts, n_assign):
    pad = EXPERT_PAD
    padded = ((counts + pad - 1) // pad) * pad
    seg_start = jnp.cumsum(padded) - padded
    n_sb_e = (counts + EXPERT_R - 1) // EXPERT_R
    sb_cum = jnp.cumsum(n_sb_e)
    n_sb = n_assign // EXPERT_R + N_EXPERTS
    s = jnp.arange(n_sb, dtype=I32)
    total = sb_cum[-1]
    s_eff = jnp.minimum(s, total - 1)
    e = jnp.sum((sb_cum[None, :] <= s_eff[:, None]).astype(I32), axis=1)
    k = s_eff - (sb_cum[e] - n_sb_e[e])
    rows = jnp.clip(counts[e] - k * EXPERT_R, 0, EXPERT_R)
    valid = s < total
    nsub = jnp.where(valid, (rows + pad - 1) // pad, 0)
    start = seg_start[e] + k * EXPERT_R
    used_blocks = (jnp.sum(padded) // pad).reshape(1)
    return seg_start.astype(I32), e, start.astype(I32), nsub.astype(I32), used_blocks.astype(I32)


def kernel(x_prompt, x_sample, state_rglru_h, state_rglru_conv, state_ssd, state_ssd_conv, g_mix, w_in, conv_rg_w, conv_rg_b, rg_wa, rg_ba, rg_wi, rg_bi, rg_lambda, conv_ssd_w, conv_ssd_b, ssd_dt_bias, ssd_a_log, ssd_d, ssd_norm_g, w_out, g_ffn, w_router, b_router, w_gu, b_gu, w_down, b_down, g_final):
    depth = g_mix.shape[0]
    assert depth == 1
    bp, lp, _ = x_prompt.shape
    bs, ls, _ = x_sample.shape
    tp, ts = bp * lp, bs * ls
    t = tp + ts
    hp = SSD_HEADS * SSD_HEAD_DIM
    l = 0

    xp = x_prompt.reshape(tp, D_MODEL)
    xs = jnp.transpose(x_sample, (1, 0, 2)).reshape(ts, D_MODEL)
    w_in_t = jnp.transpose(w_in[l])
    row = lambda v: v.reshape(1, -1)
    pad_heads = lambda v: jnp.pad(v.reshape(1, -1), ((0, 0), (0, LANES - SSD_HEADS)))

    hb, dt_raw = _norm_dt(xp, xs, row(g_mix[l]), w_in_t)
    proj = _in_proj(hb, w_in_t)

    rg_params = (conv_rg_w[l], row(conv_rg_b[l]), rg_wa[l], row(rg_ba[l]), rg_wi[l], row(rg_bi[l]),
                 row(rg_lambda[l]))
    d_exp = row(jnp.repeat(ssd_d[l], SSD_HEAD_DIM))
    ssd_params = (conv_ssd_w[l], row(conv_ssd_b[l]), pad_heads(ssd_dt_bias[l]), pad_heads(ssd_a_log[l]),
                  d_exp, row(ssd_norm_g[l]))

    y_rg, rgh_p, rgc_p = _rglru_prompt(proj, bp, lp, *rg_params)
    y_ssd, ssd_p, ssdc_p = _ssd_prompt(proj, dt_raw, bp, lp, *ssd_params)

    assert tp % (bs * ls) == 0 and bs % 8 == 0
    mix_s3, rgh_s, rgc_s, ssd_s, ssdc_s = _mix_sample(
        proj.reshape(t // bs, bs, D_MAIN), dt_raw.reshape(t // bs, bs, LANES), tp, ls, bs,
        state_rglru_h[l], jnp.transpose(state_rglru_conv[l], (1, 0, 2)),
        state_ssd[l].reshape(bs, hp, SSD_STATE), jnp.transpose(state_ssd_conv[l], (1, 0, 2)),
        rg_params, ssd_params)
    mix_s = mix_s3.reshape(ts, 2 * D_MODEL)

    x1 = _out_proj(y_rg, y_ssd, mix_s, w_out[l], xp, xs)

    wr = jnp.pad(w_router[l], ((0, 0), (0, LANES - N_EXPERTS)))
    br = jnp.pad(b_router[l].reshape(1, -1), ((0, 0), (0, LANES - N_EXPERTS)))
    h2, e_idx, gates, rank, counts = _route(x1, row(g_ffn[l]), wr, br)

    n_assign = t * TOP_K
    n_slots = n_assign + N_EXPERTS * EXPERT_PAD
    counts = counts[0, :N_EXPERTS]
    seg_start, sb_expert, sb_start, sb_nsub, used_blocks = _expert_tables(counts, n_assign)
    e_flat = e_idx.reshape(-1)
    rank_flat = rank.reshape(-1)
    xsorted = _scatter(seg_start, counts, e_flat, rank_flat, h2, n_slots)
    ysorted = _experts(sb_expert, sb_start, sb_nsub, used_blocks, xsorted, w_gu[l], b_gu[l].reshape(N_EXPERTS, 1, -1),
                       w_down[l], b_down[l].reshape(N_EXPERTS, 1, -1))
    y_p, y_s = _combine(seg_start, e_flat, rank_flat, x1, gates, row(g_final), ysorted, tp)

    return (y_p.reshape(x_prompt.shape).astype(x_prompt.dtype),
            jnp.transpose(y_s.reshape(ls, bs, D_MODEL), (1, 0, 2)).astype(x_sample.dtype),
            rgh_p.reshape(depth, bp, D_RG),
            rgc_p.reshape(depth, bp, CONV_W - 1, D_RG),
            ssd_p.reshape(depth, bp, SSD_HEADS, SSD_HEAD_DIM, SSD_STATE),
            ssdc_p.reshape(depth, bp, CONV_W - 1, D_XBC),
            rgh_s.reshape(depth, bs, D_RG),
            jnp.transpose(rgc_s, (1, 0, 2)).reshape(depth, bs, CONV_W - 1, D_RG),
            ssd_s.reshape(depth, bs, SSD_HEADS, SSD_HEAD_DIM, SSD_STATE),
            jnp.transpose(ssdc_s, (1, 0, 2)).reshape(depth, bs, CONV_W - 1, D_XBC))
```

```python
import functools

import jax
import jax.numpy as jnp
from jax import lax
from jax.experimental import pallas as pl
from jax.experimental.pallas import tpu as pltpu

F32 = jnp.float32
BF16 = jnp.bfloat16
I32 = jnp.int32
HIGHEST = lax.Precision.HIGHEST

EPS = 1e-6
D_MODEL = 2048
D_RG = 2048
RG_BLOCKS = 16
RG_BLOCK_W = 128
RG_C = 8.0
D_SSD = 2048
SSD_HEAD_DIM = 64
SSD_HEADS = 32
SSD_GROUPS = 4
SSD_STATE = 128
SSD_CHUNK = 128
CONV_W = 4
D_BC = SSD_GROUPS * SSD_STATE
D_XBC = D_SSD + 2 * D_BC
D_MAIN = 2 * D_RG + D_SSD + D_XBC
N_EXPERTS = 32
TOP_K = 4
D_FF = 2048
SWIGLU_ALPHA = 1.702
SWIGLU_LIMIT = 7.0

LANES = 128
VMEM_LIMIT = 56 * 1024 * 1024

ROW_TILE = 512
PROJ_TM = 1088
PROJ_TN = 1024
OUT_TN = 512
RG_TL = 256
SAMPLE_BB = 8
ROUTE_TM = 512
SCATTER_TM = 256
COMBINE_TM = 256
EXPERT_PAD = 128
EXPERT_BM = 512
EXPERT_R = 1152
EXPERT_TF = 512
DMA_UNROLL = 8


def _cparams(sem, vmem=VMEM_LIMIT):
    return pltpu.CompilerParams(dimension_semantics=sem, vmem_limit_bytes=vmem)


def _softplus(x):
    return jnp.maximum(x, 0.0) + jnp.log1p(jnp.exp(-jnp.abs(x)))


_sigmoid = jax.nn.sigmoid
_silu = jax.nn.silu


def _gelu_tanh(x):
    return jax.nn.gelu(x, approximate=True)


def _head_expand_matrix(dtype):
    r = lax.broadcasted_iota(I32, (LANES, D_SSD), 0)
    c = lax.broadcasted_iota(I32, (LANES, D_SSD), 1)
    return jnp.where(lax.shift_right_logical(c, 6) == r, 1.0, 0.0).astype(dtype)


def _expand_heads(v, e_b16):
    rows = v.shape[0]
    hi = v.astype(BF16)
    r1 = v - hi.astype(F32)
    mid = r1.astype(BF16)
    lo = (r1 - mid.astype(F32)).astype(BF16)
    o = jnp.dot(jnp.concatenate([hi, mid, lo], axis=0), e_b16, preferred_element_type=F32)
    return (o[0:rows] + o[rows:2 * rows]) + o[2 * rows:3 * rows]


def _norm_dt_body(xp_ref, xs_ref, g_ref, wdt_ref, hb_ref, dt_ref, *, n_prompt_tiles):
    i = pl.program_id(0)

    def run(x_ref):
        x = x_ref[...]
        h = x * lax.rsqrt(jnp.mean(x * x, axis=-1, keepdims=True) + EPS) * g_ref[...]
        hb = h.astype(BF16)
        hb_ref[...] = hb
        r = lax.broadcasted_iota(I32, (LANES, 1), 0)
        wdt = jnp.where(r < SSD_HEADS, wdt_ref[...], 0.0).astype(BF16)
        dt_ref[...] = lax.dot_general(hb, wdt, (((1,), (1,)), ((), ())), preferred_element_type=F32)

    @pl.when(i < n_prompt_tiles)
    def _():
        run(xp_ref)

    @pl.when(i >= n_prompt_tiles)
    def _():
        run(xs_ref)


def _norm_dt(xp, xs, g, w_in_t):
    tp, ts = xp.shape[0], xs.shape[0]
    npt, nst = tp // ROW_TILE, ts // ROW_TILE
    t = tp + ts
    return pl.pallas_call(
        functools.partial(_norm_dt_body, n_prompt_tiles=npt),
        grid=(npt + nst,),
        in_specs=[
            pl.BlockSpec((ROW_TILE, D_MODEL), lambda i: (jnp.minimum(i, npt - 1), 0)),
            pl.BlockSpec((ROW_TILE, D_MODEL), lambda i: (jnp.maximum(i - npt, 0), 0)),
            pl.BlockSpec((1, D_MODEL), lambda i: (0, 0)),
            pl.BlockSpec((LANES, D_MODEL), lambda i: (D_MAIN // LANES, 0)),
        ],
        out_specs=[
            pl.BlockSpec((ROW_TILE, D_MODEL), lambda i: (i, 0)),
            pl.BlockSpec((ROW_TILE, LANES), lambda i: (i, 0)),
        ],
        out_shape=[jax.ShapeDtypeStruct((t, D_MODEL), BF16), jax.ShapeDtypeStruct((t, LANES), F32)],
        compiler_params=_cparams(("parallel",)),
        name="norm_dt",
    )(xp, xs, g, w_in_t)


def _cast_weight_tile(w_ref, wb_ref, rows_per_step=256):
    k = w_ref.shape[0]

    def step(s, c):
        r = pl.multiple_of(s * rows_per_step, rows_per_step)
        wb_ref[pl.ds(r, rows_per_step), :] = w_ref[pl.ds(r, rows_per_step), :].astype(BF16)
        return c

    lax.fori_loop(0, k // rows_per_step, step, 0)


def _in_proj_body(x_ref, w_ref, o_ref, wb_ref):
    @pl.when(pl.program_id(1) == 0)
    def _():
        _cast_weight_tile(w_ref, wb_ref)

    o_ref[...] = lax.dot_general(x_ref[...], wb_ref[...], (((1,), (1,)), ((), ())), preferred_element_type=F32)


def _in_proj(hb, w_in_t):
    t = hb.shape[0]
    return pl.pallas_call(
        _in_proj_body,
        grid=(D_MAIN // PROJ_TN, t // PROJ_TM),
        in_specs=[
            pl.BlockSpec((PROJ_TM, D_MODEL), lambda j, i: (i, 0)),
            pl.BlockSpec((PROJ_TN, D_MODEL), lambda j, i: (j, 0)),
        ],
        out_specs=pl.BlockSpec((PROJ_TM, PROJ_TN), lambda j, i: (i, j)),
        out_shape=jax.ShapeDtypeStruct((t, D_MAIN), F32),
        scratch_shapes=[pltpu.VMEM((PROJ_TN, D_MODEL), BF16)],
        compiler_params=_cparams(("arbitrary", "arbitrary")),
        name="in_proj",
    )(hb, w_in_t)


def _out_proj_body(rg_ref, ssd_ref, ms_ref, w_ref, xp_ref, xs_ref, o_ref, wb_ref, *, n_prompt_tiles):
    i = pl.program_id(1)

    @pl.when(i == 0)
    def _():
        _cast_weight_tile(w_ref, wb_ref)

    @pl.when(i < n_prompt_tiles)
    def _():
        m = jnp.concatenate([rg_ref[...], ssd_ref[...]], axis=1)
        o_ref[...] = xp_ref[...] + jnp.dot(m, wb_ref[...], preferred_element_type=F32)

    @pl.when(i >= n_prompt_tiles)
    def _():
        o_ref[...] = xs_ref[...] + jnp.dot(ms_ref[...].astype(BF16), wb_ref[...], preferred_element_type=F32)


def _out_proj(y_rg, y_ssd, mix_s, w_out, xp, xs):
    tp, ts = y_rg.shape[0], mix_s.shape[0]
    npt, nst = tp // ROW_TILE, ts // ROW_TILE
    prompt_rows = lambda j, i: (jnp.minimum(i, npt - 1), 0)
    return pl.pallas_call(
        functools.partial(_out_proj_body, n_prompt_tiles=npt),
        grid=(D_MODEL // OUT_TN, npt + nst),
        in_specs=[
            pl.BlockSpec((ROW_TILE, D_MODEL), prompt_rows),
            pl.BlockSpec((ROW_TILE, D_MODEL), prompt_rows),
            pl.BlockSpec((ROW_TILE, 2 * D_MODEL), lambda j, i: (jnp.maximum(i - npt, 0), 0)),
            pl.BlockSpec((2 * D_MODEL, OUT_TN), lambda j, i: (0, j)),
            pl.BlockSpec((ROW_TILE, OUT_TN), lambda j, i: (jnp.minimum(i, npt - 1), j)),
            pl.BlockSpec((ROW_TILE, OUT_TN), lambda j, i: (jnp.maximum(i - npt, 0), j)),
        ],
        out_specs=pl.BlockSpec((ROW_TILE, OUT_TN), lambda j, i: (i, j)),
        out_shape=jax.ShapeDtypeStruct((tp + ts, D_MODEL), F32),
        scratch_shapes=[pltpu.VMEM((2 * D_MODEL, OUT_TN), BF16)],
        compiler_params=_cparams(("arbitrary", "arbitrary")),
        name="out_proj",
    )(y_rg, y_ssd, mix_s, w_out, xp, xs)


def _rg_gates(xc, wa_ref, ba_ref, wi_ref, bi_ref, lam_ref):
    xcb = xc.astype(BF16)
    rs, is_ = [], []
    for h in range(RG_BLOCKS):
        xh = xcb[:, h * RG_BLOCK_W:(h + 1) * RG_BLOCK_W]
        rs.append(jnp.dot(xh, wa_ref[h].astype(BF16), preferred_element_type=F32))
        is_.append(jnp.dot(xh, wi_ref[h].astype(BF16), preferred_element_type=F32))
    r = _sigmoid(jnp.concatenate(rs, axis=1) + ba_ref[...])
    ig = _sigmoid(jnp.concatenate(is_, axis=1) + bi_ref[...])
    log_a = (-RG_C * r) * _softplus(-lam_ref[...])
    a = jnp.exp(log_a)
    u = jnp.sqrt(-jnp.tanh(log_a) * (a * a + 1.0)) * (ig * xc)
    return a, u


def _rglru_prompt_body(x_ref, gate_ref, cw_ref, cb_ref, wa_ref, ba_ref, wi_ref, bi_ref, lam_ref,
                       y_ref, h_ref, cs_ref, ext_ref, a_ref, u_ref, car_ref):
    c = pl.program_id(1)
    tl = x_ref.shape[0]

    @pl.when(c == 0)
    def _():
        ext_ref[0:8, :] = jnp.zeros((8, D_RG), F32)
        car_ref[...] = jnp.zeros((1, D_RG), F32)

    ext_ref[8:8 + tl, :] = x_ref[...]
    xc = cb_ref[...] + ext_ref[5:5 + tl, :] * cw_ref[0:1, :]
    xc = xc + ext_ref[6:6 + tl, :] * cw_ref[1:2, :]
    xc = xc + ext_ref[7:7 + tl, :] * cw_ref[2:3, :]
    xc = xc + ext_ref[8:8 + tl, :] * cw_ref[3:4, :]
    tail = ext_ref[tl:tl + 8, :]
    ext_ref[0:8, :] = tail

    a, u = _rg_gates(xc, wa_ref, ba_ref, wi_ref, bi_ref, lam_ref)
    a_ref[...] = a
    u_ref[...] = u

    row = lax.broadcasted_iota(I32, (8, D_RG), 0)

    def group(g, carry):
        r0 = pl.multiple_of(g * 8, 8)
        a8 = a_ref[pl.ds(r0, 8), :]
        u8 = u_ref[pl.ds(r0, 8), :]
        for s in (1, 2, 4):
            keep = row >= s
            a_sh = jnp.where(keep, pltpu.roll(a8, s, 0), 1.0)
            u_sh = jnp.where(keep, pltpu.roll(u8, s, 0), 0.0)
            u8 = a8 * u_sh + u8
            a8 = a8 * a_sh
        h8 = a8 * carry + u8
        u_ref[pl.ds(r0, 8), :] = h8
        return h8[7:8, :]

    carry = lax.fori_loop(0, tl // 8, group, car_ref[...])
    car_ref[...] = carry
    y_ref[...] = (u_ref[...] * _gelu_tanh(gate_ref[...])).astype(BF16)

    @pl.when(c == pl.num_programs(1) - 1)
    def _():
        h_ref[...] = carry
        cs_ref[...] = tail[5:8, :]


def _rglru_prompt(proj, batch, seq, conv_w, conv_b, wa, ba, wi, bi, lam):
    nc = seq // RG_TL
    t = batch * seq
    vec =pl.BlockSpec((1, D_RG), lambda b, c: (0, 0))
    blk = pl.BlockSpec((RG_BLOCKS, RG_BLOCK_W, RG_BLOCK_W), lambda b, c: (0, 0, 0))
    return pl.pallas_call(
        _rglru_prompt_body,
        grid=(batch, nc),
        in_specs=[
            pl.BlockSpec((RG_TL, D_RG), lambda b, c: (b * nc + c, 0)),
            pl.BlockSpec((RG_TL, D_RG), lambda b, c: (b * nc + c, 1)),
            pl.BlockSpec((CONV_W, D_RG), lambda b, c: (0, 0)),
            vec, blk, vec, blk, vec, vec,
        ],
        out_specs=[
            pl.BlockSpec((RG_TL, D_RG), lambda b, c: (b * nc + c, 0)),
            pl.BlockSpec((None, 1, D_RG), lambda b, c: (b, 0, 0)),
            pl.BlockSpec((None, CONV_W - 1, D_RG), lambda b, c: (b, 0, 0)),
        ],
        out_shape=[
            jax.ShapeDtypeStruct((t, D_RG), BF16),
            jax.ShapeDtypeStruct((batch, 1, D_RG), F32),
            jax.ShapeDtypeStruct((batch, CONV_W - 1, D_RG), F32),
        ],
        scratch_shapes=[
            pltpu.VMEM((RG_TL + 8, D_RG), F32),
            pltpu.VMEM((RG_TL, D_RG), F32),
            pltpu.VMEM((RG_TL, D_RG), F32),
            pltpu.VMEM((1, D_RG), F32),
        ],
        compiler_params=_cparams(("arbitrary", "arbitrary")),
        name="rglru_prompt",
    )(proj, proj, conv_w, conv_b, wa, ba, wi, bi, lam)


def _group_rmsnorm(y, g_row):
    outs = []
    for g in range(SSD_GROUPS):
        yg = y[:, g * D_BC:(g + 1) * D_BC]
        outs.append(yg * lax.rsqrt(jnp.mean(yg * yg, axis=-1, keepdims=True) + EPS))
    return jnp.concatenate(outs, axis=1) * g_row


def _ssd_prompt_body(xbc_ref, z_ref, dt_ref, cw_ref, cb_ref, dtb_ref, alog_ref, dexp_ref, ng_ref,
                     y_ref, s_out_ref, cs_ref, ext_ref, s_ref):
    c = pl.program_id(1)
    L = SSD_CHUNK

    @pl.when(c == 0)
    def _():
        ext_ref[0:8, :] = jnp.zeros((8, D_XBC), F32)
        s_ref[...] = jnp.zeros_like(s_ref)

    ext_ref[8:8 + L, :] = xbc_ref[...]
    xc = cb_ref[...] + ext_ref[5:5 + L, :] * cw_ref[0:1, :]
    xc = xc + ext_ref[6:6 + L, :] * cw_ref[1:2, :]
    xc = xc + ext_ref[7:7 + L, :] * cw_ref[2:3, :]
    xc = xc + ext_ref[8:8 + L, :] * cw_ref[3:4, :]
    tail = ext_ref[L:L + 8, :]
    ext_ref[0:8, :] = tail
    xc = _silu(xc)
    xs = xc[:, :D_SSD]
    bm = xc[:, D_SSD:D_SSD + D_BC].astype(BF16)
    cm = xc[:, D_SSD + D_BC:].astype(BF16)

    lane = lax.broadcasted_iota(I32, (1, LANES), 1)
    a_neg = jnp.where(lane < SSD_HEADS, -jnp.exp(alog_ref[...]), 0.0)
    dt = _softplus(dt_ref[...] + dtb_ref[...])
    da = dt * a_neg
    ri = lax.broadcasted_iota(I32, (L, L), 0)
    ci = lax.broadcasted_iota(I32, (L, L), 1)
    causal = ci <= ri
    tri = jnp.where(causal, 1.0, 0.0).astype(F32)
    acum = jnp.dot(tri, da, precision=HIGHEST, preferred_element_type=F32)
    acum_t = acum.T
    a_last = acum[L - 1:L, :]

    e_b16 = _head_expand_matrix(BF16)
    stacked = jnp.concatenate([dt, jnp.exp(a_last - acum), jnp.exp(acum)], axis=0)
    expd = _expand_heads(stacked, e_b16)
    dt_e, dend_e, ea_e = expd[0:L], expd[L:2 * L], expd[2 * L:3 * L]
    xdt = xs * dt_e
    xdt_b = xdt.astype(BF16)
    w_b = (xdt * dend_e)

    lane_l = lax.broadcasted_iota(I32, (L, LANES), 1)
    lo = lane_l < SSD_HEAD_DIM
    y_parts = []
    for g in range(SSD_GROUPS):
        cg = cm[:, g * SSD_STATE:(g + 1) * SSD_STATE]
        bg = bm[:, g * SSD_STATE:(g + 1) * SSD_STATE]
        cb = lax.dot_general(cg, bg, (((1,), (1,)), ((), ())), preferred_element_type=F32)
        hpg = SSD_HEADS // SSD_GROUPS
        for jp in range(hpg // 2):
            ms = []
            for h in (g * hpg + 2 * jp, g * hpg + 2 * jp + 1):
                seg = acum[:, h:h + 1] - acum_t[h:h + 1, :]
                decay = jnp.exp(jnp.where(causal, seg, -jnp.inf))
                ms.append((cb * decay).astype(BF16))
            col = (g * hpg + 2 * jp) * SSD_HEAD_DIM
            xp = xdt_b[:, col:col + LANES]
            zero = jnp.zeros_like(xp)
            rhs = jnp.concatenate([jnp.where(lo, xp, zero), jnp.where(lo, zero, xp)], axis=0)
            y_parts.append(jnp.dot(jnp.concatenate(ms, axis=1), rhs, preferred_element_type=F32))
    y_diag = jnp.concatenate(y_parts, axis=1)

    y_off_parts = []
    for g in range(SSD_GROUPS):
        cg = cm[:, g * SSD_STATE:(g + 1) * SSD_STATE]
        sg = s_ref[g * D_BC:(g + 1) * D_BC, :].astype(BF16)
        y_off_parts.append(lax.dot_general(cg, sg, (((1,), (1,)), ((), ())), preferred_element_type=F32))
    y_off = jnp.concatenate(y_off_parts, axis=1) * ea_e

    cd_col = jnp.exp(acum_t[:, L - 1:L])
    for g in range(SSD_GROUPS):
        bg = bm[:, g * SSD_STATE:(g + 1) * SSD_STATE]
        wg_t = w_b[:, g * D_BC:(g + 1) * D_BC].T.astype(BF16)
        upd = jnp.dot(wg_t, bg, preferred_element_type=F32)
        hpg = SSD_HEADS // SSD_GROUPS
        for e in range(hpg):
            h = g * hpg + e
            r0 = h * SSD_HEAD_DIM
            cd = jnp.broadcast_to(cd_col[h:h + 1, :], (SSD_HEAD_DIM, SSD_STATE))
            s_ref[r0:r0 + SSD_HEAD_DIM, :] = (cd * s_ref[r0:r0 + SSD_HEAD_DIM, :]
                                              + upd[e * SSD_HEAD_DIM:(e + 1) * SSD_HEAD_DIM, :])

    y = (y_diag + y_off + dexp_ref[...] * xs) * _silu(z_ref[...])
    y_ref[...] = _group_rmsnorm(y, ng_ref[...]).astype(BF16)

    @pl.when(c == pl.num_programs(1) - 1)
    def _():
        s_out_ref[...] = s_ref[...]
        cs_ref[...] = tail[5:8, :]


def _ssd_prompt(proj, dt_raw, batch, seq, conv_w, conv_b, dt_bias, a_log, d_exp, norm_g):
    nc = seq // SSD_CHUNK
    L = SSD_CHUNK
    hp = SSD_HEADS * SSD_HEAD_DIM
    const = lambda b, c: (0, 0)
    return pl.pallas_call(
        _ssd_prompt_body,
        grid=(batch, nc),
        in_specs=[
            pl.BlockSpec((L, D_XBC), lambda b, c: (b * nc + c, 2)),
            pl.BlockSpec((L, D_SSD), lambda b, c: (b * nc + c, 2)),
            pl.BlockSpec((L, LANES), lambda b, c: (b * nc + c, 0)),
            pl.BlockSpec((CONV_W, D_XBC), const),
            pl.BlockSpec((1, D_XBC), const),
            pl.BlockSpec((1, LANES), const),
            pl.BlockSpec((1, LANES), const),
            pl.BlockSpec((1, D_SSD), const),
            pl.BlockSpec((1, D_SSD), const),
        ],
        out_specs=[
            pl.BlockSpec((L, D_SSD), lambda b, c: (b * nc + c, 0)),
            pl.BlockSpec((None, hp, SSD_STATE), lambda b, c: (b, 0, 0)),
            pl.BlockSpec((None, CONV_W - 1, D_XBC), lambda b, c: (b, 0, 0)),
        ],
        out_shape=[
            jax.ShapeDtypeStruct((batch * seq, D_SSD), BF16),
            jax.ShapeDtypeStruct((batch, hp, SSD_STATE), F32),
            jax.ShapeDtypeStruct((batch, CONV_W - 1, D_XBC), F32),
        ],
        scratch_shapes=[
            pltpu.VMEM((L + 8, D_XBC), F32),
            pltpu.VMEM((hp, SSD_STATE), F32),
        ],
        compiler_params=_cparams(("arbitrary", "arbitrary")),
        name="ssd_prompt",
    )(proj, proj, dt_raw, conv_w, conv_b, dt_bias, a_log, d_exp, norm_g)


def _mix_sample_body(p_ref, dt_ref, h0_ref, rgc0_ref, s0_ref, sc0_ref,
                     rcw_ref, rcb_ref, wa_ref, ba_ref, wi_ref, bi_ref, lam_ref,
                     scw_ref, scb_ref, dtb_ref, alog_ref, dexp_ref, ng_ref,
                     y_ref, h_ref, rgc_ref, s_ref, sc_ref):
    nl = p_ref.shape[0]
    bb = p_ref.shape[1]
    o_gate, o_z, o_xbc = D_RG, 2 * D_RG, 2 * D_RG + D_SSD

    def col(l, off, width):
        return p_ref[l, :, off:off + width]

    hist = [rgc0_ref[k] for k in range(CONV_W - 1)]
    raw = hist + [col(l, 0, D_RG) for l in range(nl)]
    xcs = []
    for l in range(nl):
        acc = rcb_ref[...] + raw[l] * rcw_ref[0:1, :]
        for k in range(1, CONV_W):
            acc = acc + raw[l + k] * rcw_ref[k:k + 1, :]
        xcs.append(acc)
    a, u = _rg_gates(jnp.concatenate(xcs, axis=0), wa_ref, ba_ref, wi_ref, bi_ref, lam_ref)
    h = h0_ref[...]
    for l in range(nl):
        h = a[l * bb:(l + 1) * bb] * h + u[l * bb:(l + 1) * bb]
        y_ref[l, :, 0:D_RG] = h * _gelu_tanh(col(l, o_gate, D_RG))
    h_ref[...] = h
    for k in range(CONV_W - 1):
        rgc_ref[k] = raw[nl + k]

    hist = [sc0_ref[k] for k in range(CONV_W - 1)]
    raw = hist + [col(l, o_xbc, D_XBC) for l in range(nl)]
    for k in range(CONV_W - 1):
        sc_ref[k] = raw[nl + k]
    xcs = []
    for l in range(nl):
        acc = scb_ref[...] + raw[l] * scw_ref[0:1, :]
        for k in range(1, CONV_W):
            acc = acc + raw[l + k] * scw_ref[k:k + 1, :]
        xcs.append(_silu(acc))
    xs = [x[:, :D_SSD] for x in xcs]
    bms = [x[:, D_SSD:D_SSD + D_BC].astype(BF16) for x in xcs]
    cms = [x[:, D_SSD + D_BC:].astype(BF16) for x in xcs]

    lane = lax.broadcasted_iota(I32, (1, LANES), 1)
    a_neg = jnp.where(lane < SSD_HEADS, -jnp.exp(alog_ref[...]), 0.0)
    dts = [_softplus(dt_ref[l] + dtb_ref[...]) for l in range(nl)]
    acums = []
    run = jnp.zeros((bb, LANES), F32)
    for l in range(nl):
        run = run + dts[l] * a_neg
        acums.append(run)
    a_last = acums[nl - 1]

    e_b16 = _head_expand_matrix(BF16)
    stacked = jnp.concatenate(dts + [jnp.exp(a_last - ac) for ac in acums] + [jnp.exp(ac) for ac in acums], axis=0)
    expd = _expand_heads(stacked, e_b16)
    dt_e = [expd[l * bb:(l + 1) * bb] for l in range(nl)]
    dend_e = [expd[(nl + l) * bb:(nl + l + 1) * bb] for l in range(nl)]
    ea_e = [expd[(2 * nl + l) * bb:(2 * nl + l + 1) * bb] for l in range(nl)]
    xdt = [xs[l] * dt_e[l] for l in range(nl)]
    xdt_r = [x.astype(BF16).astype(F32) for x in xdt]
    rows = nl * bb
    pad_rows = LANES - rows
    w_all = jnp.concatenate([(xdt[l] * dend_e[l]) for l in range(nl)]
                            + [jnp.zeros((pad_rows, D_SSD), F32)], axis=0)
    c_all = jnp.concatenate(cms, axis=0)
    b_all = jnp.concatenate(bms + [jnp.zeros((pad_rows, D_BC), BF16)], axis=0)

    r = lax.broadcasted_iota(I32, (D_BC, LANES), 0)
    cidx = lax.broadcasted_iota(I32, (D_BC, LANES), 1)
    gsum = jnp.where((lax.shift_right_logical(r, 7) == lax.shift_right_logical(cidx, 3)) & (cidx < SSD_HEADS),
                     1.0, 0.0).astype(F32)
    pairs = [(l, s) for l in range(nl) for s in range(l + 1)]
    prods = jnp.concatenate([cms[l].astype(F32) * bms[s].astype(F32) for (l, s) in pairs], axis=0)
    cbh = jnp.dot(prods, gsum, precision=HIGHEST, preferred_element_type=F32)
    m_list = []
    for idx, (l, s) in enumerate(pairs):
        decay = jnp.exp(acums[l] - acums[s])
        m_list.append((cbh[idx * bb:(idx + 1) * bb] * decay).astype(BF16))
    m_e = jnp.dot(jnp.concatenate(m_list, axis=0), e_b16, preferred_element_type=F32)
    y_diag = []
    for l in range(nl):
        acc = None
        for idx, (ll, s) in enumerate(pairs):
            if ll != l:
                continue
            term = m_e[idx * bb:(idx + 1) * bb] * xdt_r[s]
            acc = term if acc is None else acc + term
        y_diag.append(acc)

    cd_t = jnp.concatenate([jnp.exp(a_last), jnp.zeros((LANES - bb, LANES), F32)], axis=0).T
    row_seq = lax.broadcasted_iota(I32, (rows, 1), 0) & (bb - 1)
    row_pad = lax.broadcasted_iota(I32, (LANES, 1), 0)
    lane_q = lax.broadcasted_iota(I32, (LANES, LANES), 1)
    hpg = SSD_HEADS // SSD_GROUPS

    def seq_step(q, y_off):
        mine = row_seq == q
        mine_pad = ((row_pad & (bb - 1)) == q) & (row_pad < rows)
        cd_q = jnp.sum(jnp.where(lane_q == q, cd_t, 0.0), axis=1, keepdims=True)
        cd_q = jnp.broadcast_to(cd_q, (LANES, SSD_STATE))
        parts = []
        for g in range(SSD_GROUPS):
            s0 = s0_ref[q, g * D_BC:(g + 1) * D_BC, :]
            cg = c_all[:, g * SSD_STATE:(g + 1) * SSD_STATE]
            yq = lax.dot_general(cg, s0.astype(BF16), (((1,), (1,)), ((), ())), preferred_element_type=F32)
            parts.append(yq)
            bg = b_all[:, g * SSD_STATE:(g + 1) * SSD_STATE]
            wq = jnp.where(mine_pad, w_all[:, g * D_BC:(g + 1) * D_BC], 0.0)
            upd = jnp.dot(wq.T.astype(BF16), bg, preferred_element_type=F32)
            for e in range(hpg):
                hh = g * hpg + e
                cd = jnp.broadcast_to(cd_q[hh:hh + 1, :], (SSD_HEAD_DIM, SSD_STATE))
                s_ref[q, hh * SSD_HEAD_DIM:(hh + 1) * SSD_HEAD_DIM, :] = (
                    cd * s0[e * SSD_HEAD_DIM:(e + 1) * SSD_HEAD_DIM, :]
                    + upd[e * SSD_HEAD_DIM:(e + 1) * SSD_HEAD_DIM, :])
        yq_all = jnp.concatenate(parts, axis=1)
        return jnp.where(mine, yq_all, y_off)

    y_off = lax.fori_loop(0, bb, seq_step, jnp.zeros((rows, D_SSD), F32))

    for l in range(nl):
        y = (y_diag[l] + y_off[l * bb:(l + 1) * bb] * ea_e[l] + dexp_ref[...] * xs[l]) * _silu(col(l, o_z, D_SSD))
        y_ref[l, :, D_RG:2 * D_MODEL] = _group_rmsnorm(y, ng_ref[...])


def _mix_sample(proj3, dt3, row0, nl, n_seq, h0, rgc0, s0, sc0, rg_params, ssd_params):
    bb = SAMPLE_BB
    hp = SSD_HEADS * SSD_HEAD_DIM
    blk0 = row0 // (n_seq * nl)
    const2 = lambda i: (0, 0)
    rg_specs = [
        pl.BlockSpec((CONV_W, D_RG), const2), pl.BlockSpec((1, D_RG), const2),
        pl.BlockSpec((RG_BLOCKS, RG_BLOCK_W, RG_BLOCK_W), lambda i: (0, 0, 0)), pl.BlockSpec((1, D_RG), const2),
        pl.BlockSpec((RG_BLOCKS, RG_BLOCK_W, RG_BLOCK_W), lambda i: (0, 0, 0)), pl.BlockSpec((1, D_RG), const2),
        pl.BlockSpec((1, D_RG), const2),
    ]
    ssd_specs = [
        pl.BlockSpec((CONV_W, D_XBC), const2), pl.BlockSpec((1, D_XBC), const2),
        pl.BlockSpec((1, LANES), const2), pl.BlockSpec((1, LANES), const2),
        pl.BlockSpec((1, D_SSD), const2), pl.BlockSpec((1, D_SSD), const2),
    ]
    return pl.pallas_call(
        _mix_sample_body,
        grid=(n_seq // bb,),
        in_specs=[
            pl.BlockSpec((nl, bb, D_MAIN), lambda i: (blk0, i, 0)),
            pl.BlockSpec((nl, bb, LANES), lambda i: (blk0, i, 0)),
            pl.BlockSpec((bb, D_RG), lambda i: (i, 0)),
            pl.BlockSpec((CONV_W - 1, bb, D_RG), lambda i: (0, i, 0)),
            pl.BlockSpec((bb, hp, SSD_STATE), lambda i: (i, 0, 0)),
            pl.BlockSpec((CONV_W - 1, bb, D_XBC), lambda i: (0, i, 0)),
        ] + rg_specs + ssd_specs,
        out_specs=[
            pl.BlockSpec((nl, bb, 2 * D_MODEL), lambda i: (0, i, 0)),
            pl.BlockSpec((bb, D_RG), lambda i: (i, 0)),
            pl.BlockSpec((CONV_W - 1, bb, D_RG), lambda i: (0, i, 0)),
            pl.BlockSpec((bb, hp, SSD_STATE), lambda i: (i, 0, 0)),
            pl.BlockSpec((CONV_W - 1, bb, D_XBC), lambda i: (0, i, 0)),
        ],
        out_shape=[
            jax.ShapeDtypeStruct((nl, n_seq, 2 * D_MODEL), F32),
            jax.ShapeDtypeStruct((n_seq, D_RG), F32),
            jax.ShapeDtypeStruct((CONV_W - 1, n_seq, D_RG), F32),
            jax.ShapeDtypeStruct((n_seq, hp, SSD_STATE), F32),
            jax.ShapeDtypeStruct((CONV_W - 1, n_seq, D_XBC), F32),
        ],
        compiler_params=_cparams(("arbitrary",)),
        name="mix_sample",
    )(proj3, dt3, h0, rgc0, s0, sc0, *rg_params, *ssd_params)


def _route_body(x_ref, g_ref, wr_ref, br_ref, hp_ref, e_ref, gate_ref, rank_ref, cnt_ref, run_ref):
    i = pl.program_id(0)
    tm = x_ref.shape[0]

    @pl.when(i == 0)
    def _():
        run_ref[...] = jnp.zeros_like(run_ref)

    x = x_ref[...]
    h = x * lax.rsqrt(jnp.mean(x * x, axis=-1, keepdims=True) + EPS) * g_ref[...]
    hp_ref[...] = h

    lane = lax.broadcasted_iota(I32, (tm, LANES), 1).astype(F32)
    h_hi = h.astype(BF16)
    h_lo = (h - h_hi.astype(F32)).astype(BF16)
    w = wr_ref[...]
    w_hi = w.astype(BF16)
    w_lo = (w - w_hi.astype(F32)).astype(BF16)
    logits = (jnp.dot(h_hi, w_hi, preferred_element_type=F32)
              + (jnp.dot(h_hi, w_lo, preferred_element_type=F32) + jnp.dot(h_lo, w_hi, preferred_element_type=F32))
              + br_ref[...])
    work = jnp.where(lane < N_EXPERTS, logits, -jnp.inf)
    vals, idxs = [], []
    multi = jnp.zeros((tm, LANES), F32)
    for _ in range(TOP_K):
        m = jnp.max(work, axis=-1, keepdims=True)
        idx = jnp.min(jnp.where(work == m, lane, float(LANES)), axis=-1, keepdims=True)
        hit = lane == idx
        vals.append(m)
        idxs.append(idx)
        multi = jnp.where(hit, 1.0, multi)
        work = jnp.where(hit, -jnp.inf, work)
    ex = [jnp.exp(v - vals[0]) for v in vals]
    den = ex[0] + ex[1] + ex[2] + ex[3]
    gates = [e / den for e in ex]

    ri = lax.broadcasted_iota(I32, (tm, tm), 0)
    ci = lax.broadcasted_iota(I32, (tm, tm), 1)
    strict = jnp.where(ci < ri, 1.0, 0.0).astype(BF16)
    before = jnp.dot(strict, multi.astype(BF16), preferred_element_type=F32) + run_ref[...]
    ranks = [jnp.sum(jnp.where(lane == idx, before, 0.0), axis=-1, keepdims=True) for idx in idxs]
    run_ref[...] = run_ref[...] + jnp.sum(multi, axis=0, keepdims=True)

    lane4 = lax.broadcasted_iota(I32, (tm, TOP_K), 1)
    e_out = jnp.zeros((tm, TOP_K), I32)
    g_out = jnp.zeros((tm, TOP_K), F32)
    r_out = jnp.zeros((tm, TOP_K), I32)
    for k in range(TOP_K):
        e_out = jnp.where(lane4 == k, idxs[k].astype(I32), e_out)
        g_out = jnp.where(lane4 == k, gates[k], g_out)
        r_out = jnp.where(lane4 == k, ranks[k].astype(I32), r_out)
    e_ref[...] = e_out
    gate_ref[...] = g_out
    rank_ref[...] = r_out
    cnt_ref[...] = run_ref[...].astype(I32)


def _route(x1, g_ffn, w_router_pad, b_router_pad):
    t = x1.shape[0]
    tm = ROUTE_TM
    const = lambda i: (0, 0)
    return pl.pallas_call(
        _route_body,
        grid=(t // tm,),
        in_specs=[
            pl.BlockSpec((tm, D_MODEL), lambda i: (i, 0)),
            pl.BlockSpec((1, D_MODEL), const),
            pl.BlockSpec((D_MODEL, LANES), const),
            pl.BlockSpec((1, LANES), const),
        ],
        out_specs=[
            pl.BlockSpec((tm, D_MODEL), lambda i: (i, 0)),
            pl.BlockSpec((tm, TOP_K), lambda i: (i, 0)),
            pl.BlockSpec((tm, TOP_K), lambda i: (i, 0)),
            pl.BlockSpec((tm, TOP_K), lambda i: (i, 0)),
            pl.BlockSpec((1, LANES), const),
        ],
        out_shape=[
            jax.ShapeDtypeStruct((t, D_MODEL), F32),
            jax.ShapeDtypeStruct((t, TOP_K), I32),
            jax.ShapeDtypeStruct((t, TOP_K), F32),
            jax.ShapeDtypeStruct((t, TOP_K), I32),
            jax.ShapeDtypeStruct((1, LANES), I32),
        ],
        scratch_shapes=[pltpu.VMEM((1, LANES), F32)],
        compiler_params=_cparams(("arbitrary",)),
        name="route",
    )(x1, g_ffn, w_router_pad, b_router_pad)


def _scatter_body(seg_ref, cnt_ref, e_ref, rank_ref, h_ref, xs_ref, zero_ref, sem, zsem):
    i = pl.program_id(0)
    tm = h_ref.shape[0]
    n = e_ref.shape[0]

    def issue(blk, c):
        for u in range(DMA_UNROLL):
            a = blk * DMA_UNROLL + u
            pos = seg_ref[e_ref[a]] + rank_ref[a]
            t = lax.shift_right_logical(a, 2)
            pltpu.make_async_copy(h_ref.at[pl.ds(t, 1), :], xs_ref.at[pl.ds(pos, 1), :], sem).start()
        return c

    lax.fori_loop(0, n // DMA_UNROLL, issue, 0)
    for _ in range(TOP_K):
        pltpu.make_async_copy(h_ref, xs_ref.at[pl.ds(0, tm), :], sem).wait()

    @pl.when(i == pl.num_programs(0) - 1)
    def _():
        zero_ref[...] = jnp.zeros_like(zero_ref)
        pad = EXPERT_PAD

        def row_fill(row):
            return pltpu.make_async_copy(zero_ref.at[pl.ds(0, 1), :], xs_ref.at[pl.ds(row, 1), :], zsem)

        def block_fill(blk):
            dst = pl.multiple_of(blk * pad, pad)
            return pltpu.make_async_copy(zero_ref, xs_ref.at[pl.ds(dst, pad), :], zsem)

        def for_each(lo, hi, fn):
            def step(r, c):
                fn(r)
                return c
            lax.fori_loop(lo, hi, step, 0)

        def per_expert(e, used):
            cnt = cnt_ref[e]
            hi = ((cnt + pad - 1) // pad) * pad
            base = seg_ref[e]
            for_each(cnt, hi, lambda r: row_fill(base + r).start())
            for_each(cnt, hi, lambda r: row_fill(base + r).wait())
            return used + hi // pad

        used = lax.fori_loop(0, N_EXPERTS, per_expert, jnp.int32(0))
        n_blocks = xs_ref.shape[0] // pad
        for_each(used, n_blocks, lambda blk: block_fill(blk).start())
        for_each(used, n_blocks, lambda blk: block_fill(blk).wait())


def _scatter(seg_start, counts, e_flat, rank_flat, h2, n_slots):
    t = h2.shape[0]
    tm = SCATTER_TM
    grid_spec = pltpu.PrefetchScalarGridSpec(
        num_scalar_prefetch=2,
        grid=(t // tm,),
        in_specs=[
            pl.BlockSpec((tm * TOP_K,), lambda i, seg, cnt: (i,), memory_space=pltpu.SMEM),
            pl.BlockSpec((tm * TOP_K,), lambda i, seg, cnt: (i,), memory_space=pltpu.SMEM),
            pl.BlockSpec((tm, D_MODEL), lambda i, seg, cnt: (i, 0)),
        ],
        out_specs=pl.BlockSpec(memory_space=pl.ANY),
        scratch_shapes=[pltpu.VMEM((EXPERT_PAD, D_MODEL), F32), pltpu.SemaphoreType.DMA(()),
                        pltpu.SemaphoreType.DMA(())],
    )
    return pl.pallas_call(
        _scatter_body,
        grid_spec=grid_spec,
        out_shape=jax.ShapeDtypeStruct((n_slots, D_MODEL), F32),
        compiler_params=_cparams(("arbitrary",)),
        name="scatter_rows",
    )(seg_start, counts, e_flat, rank_flat, h2)


def _experts_body(sbe_ref, sbs_ref, sbn_ref, used_ref, xs_ref, wg_ref, wu_ref, bg_ref, bu_ref, wd_ref, bd_ref,
                  ys_ref, xin_ref, xbuf_ref, acc_ref, wgu_ref, wdb_ref, in_sem, out_sem):
    del sbe_ref
    sb = pl.program_id(0)
    j = pl.program_id(1)
    nj = pl.num_programs(1)
    n_sb = pl.num_programs(0)
    nblk = sbn_ref[sb]
    start = sbs_ref[sb]
    cb = EXPERT_PAD
    bm = EXPERT_BM
    tf = wd_ref.shape[0]

    def in_copy(which, r):
        src = pl.multiple_of(sbs_ref[which] + r * cb, cb)
        dst = pl.multiple_of(r * cb, cb)
        return pltpu.make_async_copy(xs_ref.at[pl.ds(src, cb), :], xin_ref.at[pl.ds(dst, cb), :], in_sem.at[r])

    def out_copy(r):
        src = pl.multiple_of(r * cb, cb)
        dst = pl.multiple_of(start + r * cb, cb)
        return pltpu.make_async_copy(acc_ref.at[pl.ds(src, cb), :], ys_ref.at[pl.ds(dst, cb), :], out_sem)

    def zero_copy(blk):
        dst = pl.multiple_of(blk * cb, cb)
        return pltpu.make_async_copy(acc_ref.at[pl.ds(0, cb), :], ys_ref.at[pl.ds(dst, cb), :], out_sem)

    def for_each(lo, hi, fn):
        def step(r, c):
            fn(r)
            return c
        lax.fori_loop(lo, hi, step, 0)

    @pl.when((j == 0) & (sb == 0))
    def _():
        for_each(0, nblk, lambda r: in_copy(0, r).start())

    @pl.when(j == 0)
    def _():
        def stage(r):
            in_copy(sb, r).wait()
            r0 = pl.multiple_of(r * cb, cb)
            xbuf_ref[pl.ds(r0, cb), :] = xin_ref[pl.ds(r0, cb), :].astype(BF16)
            acc_ref[pl.ds(r0, cb), :] = jnp.broadcast_to(bd_ref[...], (cb, D_MODEL))

        for_each(0, nblk, stage)

        @pl.when(sb + 1 < n_sb)
        def _():
            nxt = jnp.minimum(sb + 1, n_sb - 1)
            for_each(0, sbn_ref[nxt], lambda r: in_copy(nxt, r).start())

    @pl.when(nblk > 0)
    def _():
        def cast_rows(s):
            r = pl.multiple_of(s * cb, cb)
            wgu_ref[pl.ds(r, cb), 0:tf] = wg_ref[pl.ds(r, cb), :].astype(BF16)
            wgu_ref[pl.ds(r, cb), tf:2 * tf] = wu_ref[pl.ds(r, cb), :].astype(BF16)

        for_each(0, D_MODEL // cb, cast_rows)
        wdb_ref[...] = wd_ref[...].astype(BF16)
        bias_gu = jnp.concatenate([bg_ref[...], bu_ref[...]], axis=1)

        def sub(r0, rows):
            x = xbuf_ref[pl.ds(r0, rows), :]
            gu = jnp.dot(x, wgu_ref[...], preferred_element_type=F32) + bias_gu
            gate = jnp.minimum(gu[:, :tf], SWIGLU_LIMIT)
            up = jnp.clip(gu[:, tf:], -SWIGLU_LIMIT, SWIGLU_LIMIT)
            glu = gate * _sigmoid(SWIGLU_ALPHA * gate)
            act = ((up + 1.0) * glu).astype(BF16)
            acc_ref[pl.ds(r0, rows), :] += jnp.dot(act, wdb_ref[...], preferred_element_type=F32)

            @pl.when(j == nj - 1)
            def _():
                for b in range(rows // cb):
                    out_copy(r0 // cb + b).start()

        per_sub = bm // cb
        n_full = nblk // per_sub
        for_each(0, n_full, lambda i: sub(pl.multiple_of(i * bm, bm), bm))
        done = n_full * per_sub
        left = nblk - done
        piece = 1
        while piece * 2 < per_sub:
            piece *= 2
        while piece >= 1:
            here = done

            @pl.when((left & piece) != 0)
            def _():
                sub(pl.multiple_of(here * cb, cb), piece * cb)

            done = done + (left & piece)
            piece //= 2

    @pl.when(j == nj - 1)
    def _():
        for_each(0, nblk, lambda r: out_copy(r).wait())

    @pl.when((sb == pl.num_programs(0) - 1) & (j == nj - 1))
    def _():
        acc_ref[0:cb, :] = jnp.zeros((cb, D_MODEL), F32)
        n_blocks = ys_ref.shape[0] // cb
        for_each(used_ref[0], n_blocks, lambda blk: zero_copy(blk).start())
        for_each(used_ref[0], n_blocks, lambda blk: zero_copy(blk).wait())


def _experts(sb_expert, sb_start, sb_nsub, used_blocks, xsorted, w_gu, b_gu, w_down, b_down):
    n_sb = sb_expert.shape[0]
    n_slots = xsorted.shape[0]
    tf = EXPERT_TF
    nj = D_FF // tf

    def spec(shape, fn):
        def index_map(sb, j, sbe, sbs, sbn, used):
            jj = jnp.where(sbn[sb] > 0, j, nj - 1)
            return fn(sbe[sb], jj)
        return pl.BlockSpec(shape, index_map)

    grid_spec = pltpu.PrefetchScalarGridSpec(
        num_scalar_prefetch=4,
        grid=(n_sb, nj),
        in_specs=[
            pl.BlockSpec(memory_space=pl.ANY),
            spec((None, D_MODEL, tf), lambda e, jj: (e, 0, jj)),
            spec((None, D_MODEL, tf), lambda e, jj: (e, 0, nj + jj)),
            spec((None, 1, tf), lambda e, jj: (e, 0, jj)),
            spec((None, 1, tf), lambda e, jj: (e, 0, nj + jj)),
            spec((None, tf, D_MODEL), lambda e, jj: (e, jj, 0)),
            spec((None, 1, D_MODEL), lambda e, jj: (e, 0, 0)),
        ],
        out_specs=pl.BlockSpec(memory_space=pl.ANY),
        scratch_shapes=[
            pltpu.VMEM((EXPERT_R, D_MODEL), F32),
            pltpu.VMEM((EXPERT_R, D_MODEL), BF16),
            pltpu.VMEM((EXPERT_R, D_MODEL), F32),
            pltpu.VMEM((D_MODEL, 2 * tf), BF16),
            pltpu.VMEM((tf, D_MODEL), BF16),
            pltpu.SemaphoreType.DMA((EXPERT_R // EXPERT_PAD,)),
            pltpu.SemaphoreType.DMA(()),
        ],
    )
    return pl.pallas_call(
        _experts_body,
        grid_spec=grid_spec,
        out_shape=jax.ShapeDtypeStruct((n_slots, D_MODEL), F32),
        compiler_params=_cparams(("arbitrary", "arbitrary")),
        name="experts",
    )(sb_expert, sb_start, sb_nsub, used_blocks, xsorted, w_gu, w_gu, b_gu, b_gu, w_down, b_down)


def _combine_body(seg_ref, e_ref, rank_ref, x1_ref, gate_ref, gf_ref, ys_ref, yp_ref, ysm_ref, buf_ref, sem,
                  *, n_prompt_tiles):
    i = pl.program_id(0)
    tm = x1_ref.shape[0]
    n = e_ref.shape[0]

    def issue(blk, c):
        for u in range(DMA_UNROLL):
            a = blk * DMA_UNROLL + u
            pos = seg_ref[e_ref[a]] + rank_ref[a]
            t = lax.shift_right_logical(a, 2)
            pltpu.make_async_copy(ys_ref.at[pl.ds(pos, 1), :], buf_ref.at[u % TOP_K, pl.ds(t, 1), :], sem).start()
        return c

    lax.fori_loop(0, n // DMA_UNROLL, issue, 0)
    for k in range(TOP_K):
        pltpu.make_async_copy(ys_ref.at[pl.ds(0, tm), :], buf_ref.at[k], sem).wait()

    g = gate_ref[...]
    acc = buf_ref[0] * g[:, 0:1]
    for k in range(1, TOP_K):
        acc = acc + buf_ref[k] * g[:, k:k + 1]
    x = x1_ref[...] + acc
    y = x * lax.rsqrt(jnp.mean(x * x, axis=-1, keepdims=True) + EPS) * gf_ref[...]

    @pl.when(i < n_prompt_tiles)
    def _():
        yp_ref[...] = y

    @pl.when(i >= n_prompt_tiles)
    def _():
        ysm_ref[...] = y


def _combine(seg_start, e_flat, rank_flat, x1, gates, g_final, ysorted, t_prompt):
    t = x1.shape[0]
    tm = COMBINE_TM
    npt = t_prompt // tm
    grid_spec = pltpu.PrefetchScalarGridSpec(
        num_scalar_prefetch=1,
        grid=(t // tm,),
        in_specs=[
            pl.BlockSpec((tm * TOP_K,), lambda i, seg: (i,), memory_space=pltpu.SMEM),
            pl.BlockSpec((tm * TOP_K,), lambda i, seg: (i,), memory_space=pltpu.SMEM),
            pl.BlockSpec((tm, D_MODEL), lambda i, seg: (i, 0)),
            pl.BlockSpec((tm, TOP_K), lambda i, seg: (i, 0)),
            pl.BlockSpec((1, D_MODEL), lambda i, seg: (0, 0)),
            pl.BlockSpec(memory_space=pl.ANY),
        ],
        out_specs=[
            pl.BlockSpec((tm, D_MODEL), lambda i, seg: (jnp.minimum(i, npt - 1), 0)),
            pl.BlockSpec((tm, D_MODEL), lambda i, seg: (jnp.maximum(i - npt, 0), 0)),
        ],
        scratch_shapes=[pltpu.VMEM((TOP_K, tm, D_MODEL), F32), pltpu.SemaphoreType.DMA(())],
    )
    return pl.pallas_call(
        functools.partial(_combine_body, n_prompt_tiles=npt),
        grid_spec=grid_spec,
        out_shape=[
            jax.ShapeDtypeStruct((t_prompt, D_MODEL), F32),
            jax.ShapeDtypeStruct((t - t_prompt, D_MODEL), F32),
        ],
        compiler_params=_cparams(("arbitrary",)),
        name="combine",
    )(seg_start, e_flat, rank_flat, x1, gates, g_final, ysorted)


def _expert_tables(counts, n_assign):
    pad = EXPERT_PAD
    padded = ((counts + pad - 1) // pad) * pad
    seg_start = jnp.cumsum(padded) - padded
    n_sb_e = (counts + EXPERT_R - 1) // EXPERT_R
    sb_cum = jnp.cumsum(n_sb_e)
    n_sb = n_assign // EXPERT_R + N_EXPERTS
    s = jnp.arange(n_sb, dtype=I32)
    total = sb_cum[-1]
    s_eff = jnp.minimum(s, total - 1)
    e = jnp.sum((sb_cum[None, :] <= s_eff[:, None]).astype(I32), axis=1)
    k = s_eff - (sb_cum[e] - n_sb_e[e])
    rows = jnp.clip(counts[e] - k * EXPERT_R, 0, EXPERT_R)
    valid = s < total
    nsub = jnp.where(valid, (rows + pad - 1) // pad, 0)
    start = seg_start[e] + k * EXPERT_R
    used_blocks = (jnp.sum(padded) // pad).reshape(1)
    return seg_start.astype(I32), e, start.astype(I32), nsub.astype(I32), used_blocks.astype(I32)


def kernel(x_prompt, x_sample, state_rglru_h, state_rglru_conv, state_ssd, state_ssd_conv, g_mix, w_in, conv_rg_w, conv_rg_b, rg_wa, rg_ba, rg_wi, rg_bi, rg_lambda, conv_ssd_w, conv_ssd_b, ssd_dt_bias, ssd_a_log, ssd_d, ssd_norm_g, w_out, g_ffn, w_router, b_router, w_gu, b_gu, w_down, b_down, g_final):
    depth = g_mix.shape[0]
    assert depth == 1
    bp, lp, _ = x_prompt.shape
    bs, ls, _ = x_sample.shape
    tp, ts = bp * lp, bs * ls
    t = tp + ts
    hp = SSD_HEADS * SSD_HEAD_DIM
    l = 0

    xp = x_prompt.reshape(tp, D_MODEL)
    xs = jnp.transpose(x_sample, (1, 0, 2)).reshape(ts, D_MODEL)
    w_in_t = jnp.transpose(w_in[l])
    row = lambda v: v.reshape(1, -1)
    pad_heads = lambda v: jnp.pad(v.reshape(1, -1), ((0, 0), (0, LANES - SSD_HEADS)))

    hb, dt_raw = _norm_dt(xp, xs, row(g_mix[l]), w_in_t)
    proj = _in_proj(hb, w_in_t)

    rg_params = (conv_rg_w[l], row(conv_rg_b[l]), rg_wa[l], row(rg_ba[l]), rg_wi[l], row(rg_bi[l]),
                 row(rg_lambda[l]))
    d_exp = row(jnp.repeat(ssd_d[l], SSD_HEAD_DIM))
    ssd_params = (conv_ssd_w[l], row(conv_ssd_b[l]), pad_heads(ssd_dt_bias[l]), pad_heads(ssd_a_log[l]),
                  d_exp, row(ssd_norm_g[l]))

    y_rg, rgh_p, rgc_p = _rglru_prompt(proj, bp, lp, *rg_params)
    y_ssd, ssd_p, ssdc_p = _ssd_prompt(proj, dt_raw, bp, lp, *ssd_params)

    assert tp % (bs * ls) == 0 and bs % 8 == 0
    mix_s3, rgh_s, rgc_s, ssd_s, ssdc_s = _mix_sample(
        proj.reshape(t // bs, bs, D_MAIN), dt_raw.reshape(t // bs, bs, LANES), tp, ls, bs,
        state_rglru_h[l], jnp.transpose(state_rglru_conv[l], (1, 0, 2)),
        state_ssd[l].reshape(bs, hp, SSD_STATE), jnp.transpose(state_ssd_conv[l], (1, 0, 2)),
        rg_params, ssd_params)
    mix_s = mix_s3.reshape(ts, 2 * D_MODEL)

    x1 = _out_proj(y_rg, y_ssd, mix_s, w_out[l], xp, xs)

    wr = jnp.pad(w_router[l], ((0, 0), (0, LANES - N_EXPERTS)))
    br = jnp.pad(b_router[l].reshape(1, -1), ((0, 0), (0, LANES - N_EXPERTS)))
    h2, e_idx, gates, rank, counts = _route(x1, row(g_ffn[l]), wr, br)

    n_assign = t * TOP_K
    n_slots = n_assign + N_EXPERTS * EXPERT_PAD
    counts = counts[0, :N_EXPERTS]
    seg_start, sb_expert, sb_start, sb_nsub, used_blocks = _expert_tables(counts, n_assign)
    e_flat = e_idx.reshape(-1)
    rank_flat = rank.reshape(-1)
    xsorted = _scatter(seg_start, counts, e_flat, rank_flat, h2, n_slots)
    ysorted = _experts(sb_expert, sb_start, sb_nsub, used_blocks, xsorted, w_gu[l], b_gu[l].reshape(N_EXPERTS, 1, -1),
                       w_down[l], b_down[l].reshape(N_EXPERTS, 1, -1))
    y_p, y_s = _combine(seg_start, e_flat, rank_flat, x1, gates, row(g_final), ysorted, tp)

    return (y_p.reshape(x_prompt.shape).astype(x_prompt.dtype),
            jnp.transpose(y_s.reshape(ls, bs, D_MODEL), (1, 0, 2)).astype(x_sample.dtype),
            rgh_p.reshape(depth, bp, D_RG),
            rgc_p.reshape(depth, bp, CONV_W - 1, D_RG),
            ssd_p.reshape(depth, bp, SSD_HEADS, SSD_HEAD_DIM, SSD_STATE),
            ssdc_p.reshape(depth, bp, CONV_W - 1, D_XBC),
            rgh_s.reshape(depth, bs, D_RG),
            jnp.transpose(rgc_s, (1, 0, 2)).reshape(depth, bs, CONV_W - 1, D_RG),
            ssd_s.reshape(depth, bs, SSD_HEADS, SSD_HEAD_DIM, SSD_STATE),
            jnp.transpose(ssdc_s, (1, 0, 2)).reshape(depth, bs, CONV_W - 1, D_XBC))
```

```python
import functools

import jax
import jax.numpy as jnp
from jax import lax
from jax.experimental import pallas as pl
from jax.experimental.pallas import tpu as pltpu

F32 = jnp.float32
BF16 = jnp.bfloat16
I32 = jnp.int32
HIGHEST = lax.Precision.HIGHEST

EPS = 1e-6
D_MODEL = 2048
D_RG = 2048
RG_BLOCKS = 16
RG_BLOCK_W = 128
RG_C = 8.0
D_SSD = 2048
SSD_HEAD_DIM = 64
SSD_HEADS = 32
SSD_GROUPS = 4
SSD_STATE = 128
SSD_CHUNK = 128
CONV_W = 4
D_BC = SSD_GROUPS * SSD_STATE
D_XBC = D_SSD + 2 * D_BC
D_MAIN = 2 * D_RG + D_SSD + D_XBC
N_EXPERTS = 32
TOP_K = 4
D_FF = 2048
SWIGLU_ALPHA = 1.702
SWIGLU_LIMIT = 7.0

LANES = 128
VMEM_LIMIT = 56 * 1024 * 1024
PROJ_VMEM_LIMIT = 58 * 1024 * 1024

ROW_TILE = 512
PROJ_TM = 1088
PROJ_TN = 1536
OUT_TN = 512
RG_TL = 256
SAMPLE_BB = 8
ROUTE_TM = 512
SCATTER_TM = 512
COMBINE_TM = 512
EXPERT_PAD = 128
EXPERT_BM = 512
EXPERT_R = 1152
EXPERT_TF = 512
DMA_UNROLL = 8


def _cparams(sem, vmem=VMEM_LIMIT):
    return pltpu.CompilerParams(dimension_semantics=sem, vmem_limit_bytes=vmem)


def _softplus(x):
    return jnp.maximum(x, 0.0) + jnp.log1p(jnp.exp(-jnp.abs(x)))


_sigmoid = jax.nn.sigmoid
_silu = jax.nn.silu


def _gelu_tanh(x):
    return jax.nn.gelu(x, approximate=True)


def _head_expand_matrix(dtype):
    r = lax.broadcasted_iota(I32, (LANES, D_SSD), 0)
    c = lax.broadcasted_iota(I32, (LANES, D_SSD), 1)
    return jnp.where(lax.shift_right_logical(c, 6) == r, 1.0, 0.0).astype(dtype)


def _expand_heads(v, e_b16):
    rows = v.shape[0]
    hi = v.astype(BF16)
    r1 = v - hi.astype(F32)
    mid = r1.astype(BF16)
    lo = (r1 - mid.astype(F32)).astype(BF16)
    o = jnp.dot(jnp.concatenate([hi, mid, lo], axis=0), e_b16, preferred_element_type=F32)
    return (o[0:rows] + o[rows:2 * rows]) + o[2 * rows:3 * rows]


def _norm_dt_body(xp_ref, xs_ref, g_ref, wdt_ref, hb_ref, dt_ref, *, n_prompt_tiles):
    i = pl.program_id(0)

    def run(x_ref):
        x = x_ref[...]
        h = x * lax.rsqrt(jnp.mean(x * x, axis=-1, keepdims=True) + EPS) * g_ref[...]
        hb = h.astype(BF16)
        hb_ref[...] = hb
        r = lax.broadcasted_iota(I32, (LANES, 1), 0)
        wdt = jnp.where(r < SSD_HEADS, wdt_ref[...], 0.0).astype(BF16)
        dt_ref[...] = lax.dot_general(hb, wdt, (((1,), (1,)), ((), ())), preferred_element_type=F32)

    @pl.when(i < n_prompt_tiles)
    def _():
        run(xp_ref)

    @pl.when(i >= n_prompt_tiles)
    def _():
        run(xs_ref)


def _norm_dt(xp, xs, g, w_in_t):
    tp, ts = xp.shape[0], xs.shape[0]
    npt, nst = tp // ROW_TILE, ts // ROW_TILE
    t = tp + ts
    return pl.pallas_call(
        functools.partial(_norm_dt_body, n_prompt_tiles=npt),
        grid=(npt + nst,),
        in_specs=[
            pl.BlockSpec((ROW_TILE, D_MODEL), lambda i: (jnp.minimum(i, npt - 1), 0)),
            pl.BlockSpec((ROW_TILE, D_MODEL), lambda i: (jnp.maximum(i - npt, 0), 0)),
            pl.BlockSpec((1, D_MODEL), lambda i: (0, 0)),
            pl.BlockSpec((LANES, D_MODEL), lambda i: (D_MAIN // LANES, 0)),
        ],
        out_specs=[
            pl.BlockSpec((ROW_TILE, D_MODEL), lambda i: (i, 0)),
            pl.BlockSpec((ROW_TILE, LANES), lambda i: (i, 0)),
        ],
        out_shape=[jax.ShapeDtypeStruct((t, D_MODEL), BF16), jax.ShapeDtypeStruct((t, LANES), F32)],
        compiler_params=_cparams(("parallel",)),
        name="norm_dt",
    )(xp, xs, g, w_in_t)


def _cast_weight_tile(w_ref, wb_ref, rows_per_step=256):
    k = w_ref.shape[0]

    def step(s, c):
        r = pl.multiple_of(s * rows_per_step, rows_per_step)
        wb_ref[pl.ds(r, rows_per_step), :] = w_ref[pl.ds(r, rows_per_step), :].astype(BF16)
        return c

    lax.fori_loop(0, k // rows_per_step, step, 0)


def _in_proj_body(x_ref, w_ref, o_ref, wb_ref):
    @pl.when(pl.program_id(1) == 0)
    def _():
        _cast_weight_tile(w_ref, wb_ref)

    o_ref[...] = lax.dot_general(x_ref[...], wb_ref[...], (((1,), (1,)), ((), ())), preferred_element_type=F32)


def _in_proj(hb, w_in_t):
    t = hb.shape[0]
    return pl.pallas_call(
        _in_proj_body,
        grid=(D_MAIN // PROJ_TN, t // PROJ_TM),
        in_specs=[
            pl.BlockSpec((PROJ_TM, D_MODEL), lambda j, i: (i, 0)),
            pl.BlockSpec((PROJ_TN, D_MODEL), lambda j, i: (j, 0)),
        ],
        out_specs=pl.BlockSpec((PROJ_TM, PROJ_TN), lambda j, i: (i, j)),
        out_shape=jax.ShapeDtypeStruct((t, D_MAIN), F32),
        scratch_shapes=[pltpu.VMEM((PROJ_TN, D_MODEL), BF16)],
        compiler_params=_cparams(("arbitrary", "arbitrary"), vmem=PROJ_VMEM_LIMIT),
        name="in_proj",
    )(hb, w_in_t)


def _out_proj_body(rg_ref, ssd_ref, ms_ref, w_ref, xp_ref, xs_ref, o_ref, wb_ref, *, n_prompt_tiles):
    i = pl.program_id(1)

    @pl.when(i == 0)
    def _():
        _cast_weight_tile(w_ref, wb_ref)

    @pl.when(i < n_prompt_tiles)
    def _():
        m = jnp.concatenate([rg_ref[...], ssd_ref[...]], axis=1)
        o_ref[...] = xp_ref[...] + jnp.dot(m, wb_ref[...], preferred_element_type=F32)

    @pl.when(i >= n_prompt_tiles)
    def _():
        o_ref[...] = xs_ref[...] + jnp.dot(ms_ref[...].astype(BF16), wb_ref[...], preferred_element_type=F32)


def _out_proj(y_rg, y_ssd, mix_s, w_out, xp, xs):
    tp, ts = y_rg.shape[0], mix_s.shape[0]
    npt, nst = tp // ROW_TILE, ts // ROW_TILE
    prompt_rows = lambda j, i: (jnp.minimum(i, npt - 1), 0)
    return pl.pallas_call(
        functools.partial(_out_proj_body, n_prompt_tiles=npt),
        grid=(D_MODEL // OUT_TN, npt + nst),
        in_specs=[
            pl.BlockSpec((ROW_TILE, D_MODEL), prompt_rows),
            pl.BlockSpec((ROW_TILE, D_MODEL), prompt_rows),
            pl.BlockSpec((ROW_TILE, 2 * D_MODEL), lambda j, i: (jnp.maximum(i - npt, 0), 0)),
            pl.BlockSpec((2 * D_MODEL, OUT_TN), lambda j, i: (0, j)),
            pl.BlockSpec((ROW_TILE, OUT_TN), lambda j, i: (jnp.minimum(i, npt - 1), j)),
            pl.BlockSpec((ROW_TILE, OUT_TN), lambda j, i: (jnp.maximum(i - npt, 0), j)),
        ],
        out_specs=pl.BlockSpec((ROW_TILE, OUT_TN), lambda j, i: (i, j)),
        out_shape=jax.ShapeDtypeStruct((tp + ts, D_MODEL), F32),
        scratch_shapes=[pltpu.VMEM((2 * D_MODEL, OUT_TN), BF16)],
        compiler_params=_cparams(("arbitrary", "arbitrary")),
        name="out_proj",
    )(y_rg, y_ssd, mix_s, w_out, xp, xs)


def _rg_gates(xc, wa_ref, ba_ref, wi_ref, bi_ref, lam_ref):
    xcb = xc.astype(BF16)
    rs, is_ = [], []
    for h in range(RG_BLOCKS):
        xh = xcb[:, h * RG_BLOCK_W:(h + 1) * RG_BLOCK_W]
        rs.append(jnp.dot(xh, wa_ref[h].astype(BF16), preferred_element_type=F32))
        is_.append(jnp.dot(xh, wi_ref[h].astype(BF16), preferred_element_type=F32))
    r = _sigmoid(jnp.concatenate(rs, axis=1) + ba_ref[...])
    ig = _sigmoid(jnp.concatenate(is_, axis=1) + bi_ref[...])
    log_a = (-RG_C * r) * _softplus(-lam_ref[...])
    a = jnp.exp(log_a)
    u = jnp.sqrt(-jnp.tanh(log_a) * (a * a + 1.0)) * (ig * xc)
    return a, u


def _rglru_prompt_body(x_ref, gate_ref, cw_ref, cb_ref, wa_ref, ba_ref, wi_ref, bi_ref, lam_ref,
                       y_ref, h_ref, cs_ref, ext_ref, a_ref, u_ref, car_ref):
    c = pl.program_id(1)
    tl = x_ref.shape[0]

    @pl.when(c == 0)
    def _():
        ext_ref[0:8, :] = jnp.zeros((8, D_RG), F32)
        car_ref[...] = jnp.zeros((1, D_RG), F32)

    ext_ref[8:8 + tl, :] = x_ref[...]
    xc = cb_ref[...] + ext_ref[5:5 + tl, :] * cw_ref[0:1, :]
    xc = xc + ext_ref[6:6 + tl, :] * cw_ref[1:2, :]
    xc = xc + ext_ref[7:7 + tl, :] * cw_ref[2:3, :]
    xc = xc + ext_ref[8:8 + tl, :] * cw_ref[3:4, :]
    tail = ext_ref[tl:tl + 8, :]
    ext_ref[0:8, :] = tail

    a, u = _rg_gates(xc, wa_ref, ba_ref, wi_ref, bi_ref, lam_ref)
    a_ref[...] = a
    u_ref[...] = u

    row = lax.broadcasted_iota(I32, (8, D_RG), 0)

    def group(g, carry):
        r0 = pl.multiple_of(g * 8, 8)
        a8 = a_ref[pl.ds(r0, 8), :]
        u8 = u_ref[pl.ds(r0, 8), :]
        for s in (1, 2, 4):
            keep = row >= s
            a_sh = jnp.where(keep, pltpu.roll(a8, s, 0), 1.0)
            u_sh = jnp.where(keep, pltpu.roll(u8, s, 0), 0.0)
            u8 = a8 * u_sh + u8
            a8 = a8 * a_sh
        h8 = a8 * carry + u8
        u_ref[pl.ds(r0, 8), :] = h8
        return h8[7:8, :]

    carry = lax.fori_loop(0, tl // 8, group, car_ref[...])
    car_ref[...] = carry
    y_ref[...] = (u_ref[...] * _gelu_tanh(gate_ref[...])).astype(BF16)

    @pl.when(c == pl.num_programs(1) - 1)
    def _():
        h_ref[...] = carry
        cs_ref[...] = tail[5:8, :]


def _rglru_prompt(proj, batch, seq, conv_w, conv_b, wa, ba, wi, bi, lam):
    nc = seq // RG_TL
    t = batch * seq
    vec =pl.BlockSpec((1, D_RG), lambda b, c: (0, 0))
    blk = pl.BlockSpec((RG_BLOCKS, RG_BLOCK_W, RG_BLOCK_W), lambda b, c: (0, 0, 0))
    return pl.pallas_call(
        _rglru_prompt_body,
        grid=(batch, nc),
        in_specs=[
            pl.BlockSpec((RG_TL, D_RG), lambda b, c: (b * nc + c, 0)),
            pl.BlockSpec((RG_TL, D_RG), lambda b, c: (b * nc + c, 1)),
            pl.BlockSpec((CONV_W, D_RG), lambda b, c: (0, 0)),
            vec, blk, vec, blk, vec, vec,
        ],
        out_specs=[
            pl.BlockSpec((RG_TL, D_RG), lambda b, c: (b * nc + c, 0)),
            pl.BlockSpec((None, 1, D_RG), lambda b, c: (b, 0, 0)),
            pl.BlockSpec((None, CONV_W - 1, D_RG), lambda b, c: (b, 0, 0)),
        ],
        out_shape=[
            jax.ShapeDtypeStruct((t, D_RG), BF16),
            jax.ShapeDtypeStruct((batch, 1, D_RG), F32),
            jax.ShapeDtypeStruct((batch, CONV_W - 1, D_RG), F32),
        ],
        scratch_shapes=[
            pltpu.VMEM((RG_TL + 8, D_RG), F32),
            pltpu.VMEM((RG_TL, D_RG), F32),
            pltpu.VMEM((RG_TL, D_RG), F32),
            pltpu.VMEM((1, D_RG), F32),
        ],
        compiler_params=_cparams(("arbitrary", "arbitrary")),
        name="rglru_prompt",
    )(proj, proj, conv_w, conv_b, wa, ba, wi, bi, lam)


def _group_rmsnorm(y, g_row):
    outs = []
    for g in range(SSD_GROUPS):
        yg = y[:, g * D_BC:(g + 1) * D_BC]
        outs.append(yg * lax.rsqrt(jnp.mean(yg * yg, axis=-1, keepdims=True) + EPS))
    return jnp.concatenate(outs, axis=1) * g_row


def _ssd_prompt_body(xbc_ref, z_ref, dt_ref, cw_ref, cb_ref, dtb_ref, alog_ref, dexp_ref, ng_ref,
                     y_ref, s_out_ref, cs_ref, ext_ref, s_ref):
    c = pl.program_id(1)
    L = SSD_CHUNK

    @pl.when(c == 0)
    def _():
        ext_ref[0:8, :] = jnp.zeros((8, D_XBC), F32)
        s_ref[...] = jnp.zeros_like(s_ref)

    ext_ref[8:8 + L, :] = xbc_ref[...]
    xc = cb_ref[...] + ext_ref[5:5 + L, :] * cw_ref[0:1, :]
    xc = xc + ext_ref[6:6 + L, :] * cw_ref[1:2, :]
    xc = xc + ext_ref[7:7 + L, :] * cw_ref[2:3, :]
    xc = xc + ext_ref[8:8 + L, :] * cw_ref[3:4, :]
    tail = ext_ref[L:L + 8, :]
    ext_ref[0:8, :] = tail
    xc = _silu(xc)
    xs = xc[:, :D_SSD]
    bm = xc[:, D_SSD:D_SSD + D_BC].astype(BF16)
    cm = xc[:, D_SSD + D_BC:].astype(BF16)

    lane = lax.broadcasted_iota(I32, (1, LANES), 1)
    a_neg = jnp.where(lane < SSD_HEADS, -jnp.exp(alog_ref[...]), 0.0)
    dt = _softplus(dt_ref[...] + dtb_ref[...])
    da = dt * a_neg
    ri = lax.broadcasted_iota(I32, (L, L), 0)
    ci = lax.broadcasted_iota(I32, (L, L), 1)
    causal = ci <= ri
    tri = jnp.where(causal, 1.0, 0.0).astype(F32)
    acum = jnp.dot(tri, da, precision=HIGHEST, preferred_element_type=F32)
    acum_t = acum.T
    a_last = acum[L - 1:L, :]

    e_b16 = _head_expand_matrix(BF16)
    stacked = jnp.concatenate([dt, jnp.exp(a_last - acum), jnp.exp(acum)], axis=0)
    expd = _expand_heads(stacked, e_b16)
    dt_e, dend_e, ea_e = expd[0:L], expd[L:2 * L], expd[2 * L:3 * L]
    xdt = xs * dt_e
    xdt_b = xdt.astype(BF16)
    w_b = (xdt * dend_e)

    lane_l = lax.broadcasted_iota(I32, (L, LANES), 1)
    lo = lane_l < SSD_HEAD_DIM
    y_parts = []
    for g in range(SSD_GROUPS):
        cg = cm[:, g * SSD_STATE:(g + 1) * SSD_STATE]
        bg = bm[:, g * SSD_STATE:(g + 1) * SSD_STATE]
        cb = lax.dot_general(cg, bg, (((1,), (1,)), ((), ())), preferred_element_type=F32)
        hpg = SSD_HEADS // SSD_GROUPS
        for jp in range(hpg // 2):
            ms = []
            for h in (g * hpg + 2 * jp, g * hpg + 2 * jp + 1):
                seg = acum[:, h:h + 1] - acum_t[h:h + 1, :]
                decay = jnp.exp(jnp.where(causal, seg, -jnp.inf))
                ms.append((cb * decay).astype(BF16))
            col = (g * hpg + 2 * jp) * SSD_HEAD_DIM
            xp = xdt_b[:, col:col + LANES]
            zero = jnp.zeros_like(xp)
            rhs = jnp.concatenate([jnp.where(lo, xp, zero), jnp.where(lo, zero, xp)], axis=0)
            y_parts.append(jnp.dot(jnp.concatenate(ms, axis=1), rhs, preferred_element_type=F32))
    y_diag = jnp.concatenate(y_parts, axis=1)

    y_off_parts = []
    for g in range(SSD_GROUPS):
        cg = cm[:, g * SSD_STATE:(g + 1) * SSD_STATE]
        sg = s_ref[g * D_BC:(g + 1) * D_BC, :].astype(BF16)
        y_off_parts.append(lax.dot_general(cg, sg, (((1,), (1,)), ((), ())), preferred_element_type=F32))
    y_off = jnp.concatenate(y_off_parts, axis=1) * ea_e

    cd_col = jnp.exp(acum_t[:, L - 1:L])
    for g in range(SSD_GROUPS):
        bg = bm[:, g * SSD_STATE:(g + 1) * SSD_STATE]
        wg_t = w_b[:, g * D_BC:(g + 1) * D_BC].T.astype(BF16)
        upd = jnp.dot(wg_t, bg, preferred_element_type=F32)
        hpg = SSD_HEADS // SSD_GROUPS
        for e in range(hpg):
            h = g * hpg + e
            r0 = h * SSD_HEAD_DIM
            cd = jnp.broadcast_to(cd_col[h:h + 1, :], (SSD_HEAD_DIM, SSD_STATE))
            s_ref[r0:r0 + SSD_HEAD_DIM, :] = (cd * s_ref[r0:r0 + SSD_HEAD_DIM, :]
                                              + upd[e * SSD_HEAD_DIM:(e + 1) * SSD_HEAD_DIM, :])

    y = (y_diag + y_off + dexp_ref[...] * xs) * _silu(z_ref[...])
    y_ref[...] = _group_rmsnorm(y, ng_ref[...]).astype(BF16)

    @pl.when(c == pl.num_programs(1) - 1)
    def _():
        s_out_ref[...] = s_ref[...]
        cs_ref[...] = tail[5:8, :]


def _ssd_prompt(proj, dt_raw, batch, seq, conv_w, conv_b, dt_bias, a_log, d_exp, norm_g):
    nc = seq // SSD_CHUNK
    L = SSD_CHUNK
    hp = SSD_HEADS * SSD_HEAD_DIM
    const = lambda b, c: (0, 0)
    return pl.pallas_call(
        _ssd_prompt_body,
        grid=(batch, nc),
        in_specs=[
            pl.BlockSpec((L, D_XBC), lambda b, c: (b * nc + c, 2)),
            pl.BlockSpec((L, D_SSD), lambda b, c: (b * nc + c, 2)),
            pl.BlockSpec((L, LANES), lambda b, c: (b * nc + c, 0)),
            pl.BlockSpec((CONV_W, D_XBC), const),
            pl.BlockSpec((1, D_XBC), const),
            pl.BlockSpec((1, LANES), const),
            pl.BlockSpec((1, LANES), const),
            pl.BlockSpec((1, D_SSD), const),
            pl.BlockSpec((1, D_SSD), const),
        ],
        out_specs=[
            pl.BlockSpec((L, D_SSD), lambda b, c: (b * nc + c, 0)),
            pl.BlockSpec((None, hp, SSD_STATE), lambda b, c: (b, 0, 0)),
            pl.BlockSpec((None, CONV_W - 1, D_XBC), lambda b, c: (b, 0, 0)),
        ],
        out_shape=[
            jax.ShapeDtypeStruct((batch * seq, D_SSD), BF16),
            jax.ShapeDtypeStruct((batch, hp, SSD_STATE), F32),
            jax.ShapeDtypeStruct((batch, CONV_W - 1, D_XBC), F32),
        ],
        scratch_shapes=[
            pltpu.VMEM((L + 8, D_XBC), F32),
            pltpu.VMEM((hp, SSD_STATE), F32),
        ],
        compiler_params=_cparams(("arbitrary", "arbitrary")),
        name="ssd_prompt",
    )(proj, proj, dt_raw, conv_w, conv_b, dt_bias, a_log, d_exp, norm_g)


def _mix_sample_body(p_ref, dt_ref, h0_ref, rgc0_ref, s0_ref, sc0_ref,
                     rcw_ref, rcb_ref, wa_ref, ba_ref, wi_ref, bi_ref, lam_ref,
                     scw_ref, scb_ref, dtb_ref, alog_ref, dexp_ref, ng_ref,
                     y_ref, h_ref, rgc_ref, s_ref, sc_ref):
    nl = p_ref.shape[0]
    bb = p_ref.shape[1]
    o_gate, o_z, o_xbc = D_RG, 2 * D_RG, 2 * D_RG + D_SSD

    def col(l, off, width):
        return p_ref[l, :, off:off + width]

    hist = [rgc0_ref[k] for k in range(CONV_W - 1)]
    raw = hist + [col(l, 0, D_RG) for l in range(nl)]
    xcs = []
    for l in range(nl):
        acc = rcb_ref[...] + raw[l] * rcw_ref[0:1, :]
        for k in range(1, CONV_W):
            acc = acc + raw[l + k] * rcw_ref[k:k + 1, :]
        xcs.append(acc)
    a, u = _rg_gates(jnp.concatenate(xcs, axis=0), wa_ref, ba_ref, wi_ref, bi_ref, lam_ref)
    h = h0_ref[...]
    for l in range(nl):
        h = a[l * bb:(l + 1) * bb] * h + u[l * bb:(l + 1) * bb]
        y_ref[l, :, 0:D_RG] = h * _gelu_tanh(col(l, o_gate, D_RG))
    h_ref[...] = h
    for k in range(CONV_W - 1):
        rgc_ref[k] = raw[nl + k]

    hist = [sc0_ref[k] for k in range(CONV_W - 1)]
    raw = hist + [col(l, o_xbc, D_XBC) for l in range(nl)]
    for k in range(CONV_W - 1):
        sc_ref[k] = raw[nl + k]
    xcs = []
    for l in range(nl):
        acc = scb_ref[...] + raw[l] * scw_ref[0:1, :]
        for k in range(1, CONV_W):
            acc = acc + raw[l + k] * scw_ref[k:k + 1, :]
        xcs.append(_silu(acc))
    xs = [x[:, :D_SSD] for x in xcs]
    bms = [x[:, D_SSD:D_SSD + D_BC].astype(BF16) for x in xcs]
    cms = [x[:, D_SSD + D_BC:].astype(BF16) for x in xcs]

    lane = lax.broadcasted_iota(I32, (1, LANES), 1)
    a_neg = jnp.where(lane < SSD_HEADS, -jnp.exp(alog_ref[...]), 0.0)
    dts = [_softplus(dt_ref[l] + dtb_ref[...]) for l in range(nl)]
    acums = []
    run = jnp.zeros((bb, LANES), F32)
    for l in range(nl):
        run = run + dts[l] * a_neg
        acums.append(run)
    a_last = acums[nl - 1]

    e_b16 = _head_expand_matrix(BF16)
    stacked = jnp.concatenate(dts + [jnp.exp(a_last - ac) for ac in acums] + [jnp.exp(ac) for ac in acums], axis=0)
    expd = _expand_heads(stacked, e_b16)
    dt_e = [expd[l * bb:(l + 1) * bb] for l in range(nl)]
    dend_e = [expd[(nl + l) * bb:(nl + l + 1) * bb] for l in range(nl)]
    ea_e = [expd[(2 * nl + l) * bb:(2 * nl + l + 1) * bb] for l in range(nl)]
    xdt = [xs[l] * dt_e[l] for l in range(nl)]
    xdt_r = [x.astype(BF16).astype(F32) for x in xdt]
    rows = nl * bb
    pad_rows = LANES - rows
    w_all = jnp.concatenate([(xdt[l] * dend_e[l]) for l in range(nl)]
                            + [jnp.zeros((pad_rows, D_SSD), F32)], axis=0)
    c_all = jnp.concatenate(cms, axis=0)
    b_all = jnp.concatenate(bms + [jnp.zeros((pad_rows, D_BC), BF16)], axis=0)

    r = lax.broadcasted_iota(I32, (D_BC, LANES), 0)
    cidx = lax.broadcasted_iota(I32, (D_BC, LANES), 1)
    gsum = jnp.where((lax.shift_right_logical(r, 7) == lax.shift_right_logical(cidx, 3)) & (cidx < SSD_HEADS),
                     1.0, 0.0).astype(F32)
    pairs = [(l, s) for l in range(nl) for s in range(l + 1)]
    prods = jnp.concatenate([cms[l].astype(F32) * bms[s].astype(F32) for (l, s) in pairs], axis=0)
    cbh = jnp.dot(prods, gsum, precision=HIGHEST, preferred_element_type=F32)
    m_list = []
    for idx, (l, s) in enumerate(pairs):
        decay = jnp.exp(acums[l] - acums[s])
        m_list.append((cbh[idx * bb:(idx + 1) * bb] * decay).astype(BF16))
    m_e = jnp.dot(jnp.concatenate(m_list, axis=0), e_b16, preferred_element_type=F32)
    y_diag = []
    for l in range(nl):
        acc = None
        for idx, (ll, s) in enumerate(pairs):
            if ll != l:
                continue
            term = m_e[idx * bb:(idx + 1) * bb] * xdt_r[s]
            acc = term if acc is None else acc + term
        y_diag.append(acc)

    cd_t = jnp.concatenate([jnp.exp(a_last), jnp.zeros((LANES - bb, LANES), F32)], axis=0).T
    row_seq = lax.broadcasted_iota(I32, (rows, 1), 0) & (bb - 1)
    row_pad = lax.broadcasted_iota(I32, (LANES, 1), 0)
    lane_q = lax.broadcasted_iota(I32, (LANES, LANES), 1)
    hpg = SSD_HEADS // SSD_GROUPS

    def seq_step(q, y_off):
        mine = row_seq == q
        mine_pad = ((row_pad & (bb - 1)) == q) & (row_pad < rows)
        cd_q = jnp.sum(jnp.where(lane_q == q, cd_t, 0.0), axis=1, keepdims=True)
        cd_q = jnp.broadcast_to(cd_q, (LANES, SSD_STATE))
        parts = []
        for g in range(SSD_GROUPS):
            s0 = s0_ref[q, g * D_BC:(g + 1) * D_BC, :]
            cg = c_all[:, g * SSD_STATE:(g + 1) * SSD_STATE]
            yq = lax.dot_general(cg, s0.astype(BF16), (((1,), (1,)), ((), ())), preferred_element_type=F32)
            parts.append(yq)
            bg = b_all[:, g * SSD_STATE:(g + 1) * SSD_STATE]
            wq = jnp.where(mine_pad, w_all[:, g * D_BC:(g + 1) * D_BC], 0.0)
            upd = jnp.dot(wq.T.astype(BF16), bg, preferred_element_type=F32)
            for e in range(hpg):
                hh = g * hpg + e
                cd = jnp.broadcast_to(cd_q[hh:hh + 1, :], (SSD_HEAD_DIM, SSD_STATE))
                s_ref[q, hh * SSD_HEAD_DIM:(hh + 1) * SSD_HEAD_DIM, :] = (
                    cd * s0[e * SSD_HEAD_DIM:(e + 1) * SSD_HEAD_DIM, :]
                    + upd[e * SSD_HEAD_DIM:(e + 1) * SSD_HEAD_DIM, :])
        yq_all = jnp.concatenate(parts, axis=1)
        return jnp.where(mine, yq_all, y_off)

    y_off = lax.fori_loop(0, bb, seq_step, jnp.zeros((rows, D_SSD), F32))

    for l in range(nl):
        y = (y_diag[l] + y_off[l * bb:(l + 1) * bb] * ea_e[l] + dexp_ref[...] * xs[l]) * _silu(col(l, o_z, D_SSD))
        y_ref[l, :, D_RG:2 * D_MODEL] = _group_rmsnorm(y, ng_ref[...])


def _mix_sample(proj3, dt3, row0, nl, n_seq, h0, rgc0, s0, sc0, rg_params, ssd_params):
    bb = SAMPLE_BB
    hp = SSD_HEADS * SSD_HEAD_DIM
    blk0 = row0 // (n_seq * nl)
    const2 = lambda i: (0, 0)
    rg_specs = [
        pl.BlockSpec((CONV_W, D_RG), const2), pl.BlockSpec((1, D_RG), const2),
        pl.BlockSpec((RG_BLOCKS, RG_BLOCK_W, RG_BLOCK_W), lambda i: (0, 0, 0)), pl.BlockSpec((1, D_RG), const2),
        pl.BlockSpec((RG_BLOCKS, RG_BLOCK_W, RG_BLOCK_W), lambda i: (0, 0, 0)), pl.BlockSpec((1, D_RG), const2),
        pl.BlockSpec((1, D_RG), const2),
    ]
    ssd_specs = [
        pl.BlockSpec((CONV_W, D_XBC), const2), pl.BlockSpec((1, D_XBC), const2),
        pl.BlockSpec((1, LANES), const2), pl.BlockSpec((1, LANES), const2),
        pl.BlockSpec((1, D_SSD), const2), pl.BlockSpec((1, D_SSD), const2),
    ]
    return pl.pallas_call(
        _mix_sample_body,
        grid=(n_seq // bb,),
        in_specs=[
            pl.BlockSpec((nl, bb, D_MAIN), lambda i: (blk0, i, 0)),
            pl.BlockSpec((nl, bb, LANES), lambda i: (blk0, i, 0)),
            pl.BlockSpec((bb, D_RG), lambda i: (i, 0)),
            pl.BlockSpec((CONV_W - 1, bb, D_RG), lambda i: (0, i, 0)),
            pl.BlockSpec((bb, hp, SSD_STATE), lambda i: (i, 0, 0)),
            pl.BlockSpec((CONV_W - 1, bb, D_XBC), lambda i: (0, i, 0)),
        ] + rg_specs + ssd_specs,
        out_specs=[
            pl.BlockSpec((nl, bb, 2 * D_MODEL), lambda i: (0, i, 0)),
            pl.BlockSpec((bb, D_RG), lambda i: (i, 0)),
            pl.BlockSpec((CONV_W - 1, bb, D_RG), lambda i: (0, i, 0)),
            pl.BlockSpec((bb, hp, SSD_STATE), lambda i: (i, 0, 0)),
            pl.BlockSpec((CONV_W - 1, bb, D_XBC), lambda i: (0, i, 0)),
        ],
        out_shape=[
            jax.ShapeDtypeStruct((nl, n_seq, 2 * D_MODEL), F32),
            jax.ShapeDtypeStruct((n_seq, D_RG), F32),
            jax.ShapeDtypeStruct((CONV_W - 1, n_seq, D_RG), F32),
            jax.ShapeDtypeStruct((n_seq, hp, SSD_STATE), F32),
            jax.ShapeDtypeStruct((CONV_W - 1, n_seq, D_XBC), F32),
        ],
        compiler_params=_cparams(("arbitrary",)),
        name="mix_sample",
    )(proj3, dt3, h0, rgc0, s0, sc0, *rg_params, *ssd_params)


def _route_body(x_ref, g_ref, wr_ref, br_ref, hp_ref, e_ref, gate_ref, rank_ref, cnt_ref, run_ref):
    i = pl.program_id(0)
    tm = x_ref.shape[0]

    @pl.when(i == 0)
    def _():
        run_ref[...] = jnp.zeros_like(run_ref)

    x = x_ref[...]
    h = x * lax.rsqrt(jnp.mean(x * x, axis=-1, keepdims=True) + EPS) * g_ref[...]
    hp_ref[...] = h

    lane = lax.broadcasted_iota(I32, (tm, LANES), 1).astype(F32)
    h_hi = h.astype(BF16)
    h_lo = (h - h_hi.astype(F32)).astype(BF16)
    w = wr_ref[...]
    w_hi = w.astype(BF16)
    w_lo = (w - w_hi.astype(F32)).astype(BF16)
    logits = (jnp.dot(h_hi, w_hi, preferred_element_type=F32)
              + (jnp.dot(h_hi, w_lo, preferred_element_type=F32) + jnp.dot(h_lo, w_hi, preferred_element_type=F32))
              + br_ref[...])
    work = jnp.where(lane < N_EXPERTS, logits, -jnp.inf)
    vals, idxs = [], []
    multi = jnp.zeros((tm, LANES), F32)
    for _ in range(TOP_K):
        m = jnp.max(work, axis=-1, keepdims=True)
        idx = jnp.min(jnp.where(work == m, lane, float(LANES)), axis=-1, keepdims=True)
        hit = lane == idx
        vals.append(m)
        idxs.append(idx)
        multi = jnp.where(hit, 1.0, multi)
        work = jnp.where(hit, -jnp.inf, work)
    ex = [jnp.exp(v - vals[0]) for v in vals]
    den = ex[0] + ex[1] + ex[2] + ex[3]
    gates = [e / den for e in ex]

    ri = lax.broadcasted_iota(I32, (tm, tm), 0)
    ci = lax.broadcasted_iota(I32, (tm, tm), 1)
    strict = jnp.where(ci < ri, 1.0, 0.0).astype(BF16)
    before = jnp.dot(strict, multi.astype(BF16), preferred_element_type=F32) + run_ref[...]
    ranks = [jnp.sum(jnp.where(lane == idx, before, 0.0), axis=-1, keepdims=True) for idx in idxs]
    run_ref[...] = run_ref[...] + jnp.sum(multi, axis=0, keepdims=True)

    lane4 = lax.broadcasted_iota(I32, (tm, TOP_K), 1)
    e_out = jnp.zeros((tm, TOP_K), I32)
    g_out = jnp.zeros((tm, TOP_K), F32)
    r_out = jnp.zeros((tm, TOP_K), I32)
    for k in range(TOP_K):
        e_out = jnp.where(lane4 == k, idxs[k].astype(I32), e_out)
        g_out = jnp.where(lane4 == k, gates[k], g_out)
        r_out = jnp.where(lane4 == k, ranks[k].astype(I32), r_out)
    e_ref[...] = e_out
    gate_ref[...] = g_out
    rank_ref[...] = r_out
    cnt_ref[...] = run_ref[...].astype(I32)


def _route(x1, g_ffn, w_router_pad, b_router_pad):
    t = x1.shape[0]
    tm = ROUTE_TM
    const = lambda i: (0, 0)
    return pl.pallas_call(
        _route_body,
        grid=(t // tm,),
        in_specs=[
            pl.BlockSpec((tm, D_MODEL), lambda i: (i, 0)),
            pl.BlockSpec((1, D_MODEL), const),
            pl.BlockSpec((D_MODEL, LANES), const),
            pl.BlockSpec((1, LANES), const),
        ],
        out_specs=[
            pl.BlockSpec((tm, D_MODEL), lambda i: (i, 0)),
            pl.BlockSpec((tm, TOP_K), lambda i: (i, 0)),
            pl.BlockSpec((tm, TOP_K), lambda i: (i, 0)),
            pl.BlockSpec((tm, TOP_K), lambda i: (i, 0)),
            pl.BlockSpec((1, LANES), const),
        ],
        out_shape=[
            jax.ShapeDtypeStruct((t, D_MODEL), F32),
            jax.ShapeDtypeStruct((t, TOP_K), I32),
            jax.ShapeDtypeStruct((t, TOP_K), F32),
            jax.ShapeDtypeStruct((t, TOP_K), I32),
            jax.ShapeDtypeStruct((1, LANES), I32),
        ],
        scratch_shapes=[pltpu.VMEM((1, LANES), F32)],
        compiler_params=_cparams(("arbitrary",)),
        name="route",
    )(x1, g_ffn, w_router_pad, b_router_pad)


def _scatter_body(seg_ref, cnt_ref, e_ref, rank_ref, h_ref, xs_ref, zero_ref, sem, zsem):
    i = pl.program_id(0)
    tm = h_ref.shape[0]
    n = e_ref.shape[0]

    def issue(blk, c):
        for u in range(DMA_UNROLL):
            a = blk * DMA_UNROLL + u
            pos = seg_ref[e_ref[a]] + rank_ref[a]
            t = lax.shift_right_logical(a, 2)
            pltpu.make_async_copy(h_ref.at[pl.ds(t, 1), :], xs_ref.at[pl.ds(pos, 1), :], sem).start()
        return c

    lax.fori_loop(0, n // DMA_UNROLL, issue, 0)
    for _ in range(TOP_K):
        pltpu.make_async_copy(h_ref, xs_ref.at[pl.ds(0, tm), :], sem).wait()

    @pl.when(i == pl.num_programs(0) - 1)
    def _():
        zero_ref[...] = jnp.zeros_like(zero_ref)
        pad = EXPERT_PAD

        def row_fill(row):
            return pltpu.make_async_copy(zero_ref.at[pl.ds(0, 1), :], xs_ref.at[pl.ds(row, 1), :], zsem)

        def block_fill(blk):
            dst = pl.multiple_of(blk * pad, pad)
            return pltpu.make_async_copy(zero_ref, xs_ref.at[pl.ds(dst, pad), :], zsem)

        def for_each(lo, hi, fn):
            def step(r, c):
                fn(r)
                return c
            lax.fori_loop(lo, hi, step, 0)

        def per_expert(e, used):
            cnt = cnt_ref[e]
            hi = ((cnt + pad - 1) // pad) * pad
            base = seg_ref[e]
            for_each(cnt, hi, lambda r: row_fill(base + r).start())
            for_each(cnt, hi, lambda r: row_fill(base + r).wait())
            return used + hi // pad

        used = lax.fori_loop(0, N_EXPERTS, per_expert, jnp.int32(0))
        n_blocks = xs_ref.shape[0] // pad
        for_each(used, n_blocks, lambda blk: block_fill(blk).start())
        for_each(used, n_blocks, lambda blk: block_fill(blk).wait())


def _scatter(seg_start, counts, e_flat, rank_flat, h2, n_slots):
    t = h2.shape[0]
    tm = SCATTER_TM
    grid_spec = pltpu.PrefetchScalarGridSpec(
        num_scalar_prefetch=2,
        grid=(t // tm,),
        in_specs=[
            pl.BlockSpec((tm * TOP_K,), lambda i, seg, cnt: (i,), memory_space=pltpu.SMEM),
            pl.BlockSpec((tm * TOP_K,), lambda i, seg, cnt: (i,), memory_space=pltpu.SMEM),
            pl.BlockSpec((tm, D_MODEL), lambda i, seg, cnt: (i, 0)),
        ],
        out_specs=pl.BlockSpec(memory_space=pl.ANY),
        scratch_shapes=[pltpu.VMEM((EXPERT_PAD, D_MODEL), F32), pltpu.SemaphoreType.DMA(()),
                        pltpu.SemaphoreType.DMA(())],
    )
    return pl.pallas_call(
        _scatter_body,
        grid_spec=grid_spec,
        out_shape=jax.ShapeDtypeStruct((n_slots, D_MODEL), F32),
        compiler_params=_cparams(("arbitrary",)),
        name="scatter_rows",
    )(seg_start, counts, e_flat, rank_flat, h2)


def _experts_body(sbe_ref, sbs_ref, sbn_ref, used_ref, xs_ref, wg_ref, wu_ref, bg_ref, bu_ref, wd_ref, bd_ref,
                  ys_ref, xin_ref, xbuf_ref, acc_ref, wgu_ref, wdb_ref, in_sem, out_sem):
    del sbe_ref
    sb = pl.program_id(0)
    j = pl.program_id(1)
    nj = pl.num_programs(1)
    n_sb = pl.num_programs(0)
    nblk = sbn_ref[sb]
    start = sbs_ref[sb]
    cb = EXPERT_PAD
    bm = EXPERT_BM
    tf = wd_ref.shape[0]

    def in_copy(which, r):
        src = pl.multiple_of(sbs_ref[which] + r * cb, cb)
        dst = pl.multiple_of(r * cb, cb)
        return pltpu.make_async_copy(xs_ref.at[pl.ds(src, cb), :], xin_ref.at[pl.ds(dst, cb), :], in_sem.at[r])

    def out_copy(r):
        src = pl.multiple_of(r * cb, cb)
        dst = pl.multiple_of(start + r * cb, cb)
        return pltpu.make_async_copy(acc_ref.at[pl.ds(src, cb), :], ys_ref.at[pl.ds(dst, cb), :], out_sem)

    def zero_copy(blk):
        dst = pl.multiple_of(blk * cb, cb)
        return pltpu.make_async_copy(acc_ref.at[pl.ds(0, cb), :], ys_ref.at[pl.ds(dst, cb), :], out_sem)

    def for_each(lo, hi, fn):
        def step(r, c):
            fn(r)
            return c
        lax.fori_loop(lo, hi, step, 0)

    @pl.when((j == 0) & (sb == 0))
    def _():
        for_each(0, nblk, lambda r: in_copy(0, r).start())

    @pl.when(j == 0)
    def _():
        def stage(r):
            in_copy(sb, r).wait()
            r0 = pl.multiple_of(r * cb, cb)
            xbuf_ref[pl.ds(r0, cb), :] = xin_ref[pl.ds(r0, cb), :].astype(BF16)
            acc_ref[pl.ds(r0, cb), :] = jnp.broadcast_to(bd_ref[...], (cb, D_MODEL))

        for_each(0, nblk, stage)

        @pl.when(sb + 1 < n_sb)
        def _():
            nxt = jnp.minimum(sb + 1, n_sb - 1)
            for_each(0, sbn_ref[nxt], lambda r: in_copy(nxt, r).start())

    @pl.when(nblk > 0)
    def _():
        def cast_rows(s):
            r = pl.multiple_of(s * cb, cb)
            wgu_ref[pl.ds(r, cb), 0:tf] = wg_ref[pl.ds(r, cb), :].astype(BF16)
            wgu_ref[pl.ds(r, cb), tf:2 * tf] = wu_ref[pl.ds(r, cb), :].astype(BF16)

        for_each(0, D_MODEL // cb, cast_rows)
        wdb_ref[...] = wd_ref[...].astype(BF16)
        bias_gu = jnp.concatenate([bg_ref[...], bu_ref[...]], axis=1)

        def sub(r0, rows):
            x = xbuf_ref[pl.ds(r0, rows), :]
            gu = jnp.dot(x, wgu_ref[...], preferred_element_type=F32) + bias_gu
            gate = jnp.minimum(gu[:, :tf], SWIGLU_LIMIT)
            up = jnp.clip(gu[:, tf:], -SWIGLU_LIMIT, SWIGLU_LIMIT)
            glu = gate * _sigmoid(SWIGLU_ALPHA * gate)
            act = ((up + 1.0) * glu).astype(BF16)
            acc_ref[pl.ds(r0, rows), :] += jnp.dot(act, wdb_ref[...], preferred_element_type=F32)

            @pl.when(j == nj - 1)
            def _():
                for b in range(rows // cb):
                    out_copy(r0 // cb + b).start()

        per_sub = bm // cb
        n_full = nblk // per_sub
        for_each(0, n_full, lambda i: sub(pl.multiple_of(i * bm, bm), bm))
        done = n_full * per_sub
        left = nblk - done
        piece = 1
        while piece * 2 < per_sub:
            piece *= 2
        while piece >= 1:
            here = done

            @pl.when((left & piece) != 0)
            def _():
                sub(pl.multiple_of(here * cb, cb), piece * cb)

            done = done + (left & piece)
            piece //= 2

    @pl.when(j == nj - 1)
    def _():
        for_each(0, nblk, lambda r: out_copy(r).wait())

    @pl.when((sb == pl.num_programs(0) - 1) & (j == nj - 1))
    def _():
        acc_ref[0:cb, :] = jnp.zeros((cb, D_MODEL), F32)
        n_blocks = ys_ref.shape[0] // cb
        for_each(used_ref[0], n_blocks, lambda blk: zero_copy(blk).start())
        for_each(used_ref[0], n_blocks, lambda blk: zero_copy(blk).wait())


def _experts(sb_expert, sb_start, sb_nsub, used_blocks, xsorted, w_gu, b_gu, w_down, b_down):
    n_sb = sb_expert.shape[0]
    n_slots = xsorted.shape[0]
    tf = EXPERT_TF
    nj = D_FF // tf

    def spec(shape, fn):
        def index_map(sb, j, sbe, sbs, sbn, used):
            jj = jnp.where(sbn[sb] > 0, j, nj - 1)
            return fn(sbe[sb], jj)
        return pl.BlockSpec(shape, index_map)

    grid_spec = pltpu.PrefetchScalarGridSpec(
        num_scalar_prefetch=4,
        grid=(n_sb, nj),
        in_specs=[
            pl.BlockSpec(memory_space=pl.ANY),
            spec((None, D_MODEL, tf), lambda e, jj: (e, 0, jj)),
            spec((None, D_MODEL, tf), lambda e, jj: (e, 0, nj + jj)),
            spec((None, 1, tf), lambda e, jj: (e, 0, jj)),
            spec((None, 1, tf), lambda e, jj: (e, 0, nj + jj)),
            spec((None, tf, D_MODEL), lambda e, jj: (e, jj, 0)),
            spec((None, 1, D_MODEL), lambda e, jj: (e, 0, 0)),
        ],
        out_specs=pl.BlockSpec(memory_space=pl.ANY),
        scratch_shapes=[
            pltpu.VMEM((EXPERT_R, D_MODEL), F32),
            pltpu.VMEM((EXPERT_R, D_MODEL), BF16),
            pltpu.VMEM((EXPERT_R, D_MODEL), F32),
            pltpu.VMEM((D_MODEL, 2 * tf), BF16),
            pltpu.VMEM((tf, D_MODEL), BF16),
            pltpu.SemaphoreType.DMA((EXPERT_R // EXPERT_PAD,)),
            pltpu.SemaphoreType.DMA(()),
        ],
    )
    return pl.pallas_call(
        _experts_body,
        grid_spec=grid_spec,
        out_shape=jax.ShapeDtypeStruct((n_slots, D_MODEL), F32),
        compiler_params=_cparams(("arbitrary", "arbitrary")),
        name="experts",
    )(sb_expert, sb_start, sb_nsub, used_blocks, xsorted, w_gu, w_gu, b_gu, b_gu, w_down, b_down)


def _combine_body(seg_ref, e_ref, rank_ref, x1_ref, gate_ref, gf_ref, ys_ref, yp_ref, ysm_ref, buf_ref, sem,
                  *, n_prompt_tiles):
    i = pl.program_id(0)
    tm = x1_ref.shape[0]
    n = e_ref.shape[0]

    def issue(blk, c):
        for u in range(DMA_UNROLL):
            a = blk * DMA_UNROLL + u
            pos = seg_ref[e_ref[a]] + rank_ref[a]
            t = lax.shift_right_logical(a, 2)
            pltpu.make_async_copy(ys_ref.at[pl.ds(pos, 1), :], buf_ref.at[u % TOP_K, pl.ds(t, 1), :], sem).start()
        return c

    lax.fori_loop(0, n // DMA_UNROLL, issue, 0)
    for k in range(TOP_K):
        pltpu.make_async_copy(ys_ref.at[pl.ds(0, tm), :], buf_ref.at[k], sem).wait()

    g = gate_ref[...]
    acc = buf_ref[0] * g[:, 0:1]
    for k in range(1, TOP_K):
        acc = acc + buf_ref[k] * g[:, k:k + 1]
    x = x1_ref[...] + acc
    y = x * lax.rsqrt(jnp.mean(x * x, axis=-1, keepdims=True) + EPS) * gf_ref[...]

    @pl.when(i < n_prompt_tiles)
    def _():
        yp_ref[...] = y

    @pl.when(i >= n_prompt_tiles)
    def _():
        ysm_ref[...] = y


def _combine(seg_start, e_flat, rank_flat, x1, gates, g_final, ysorted, t_prompt):
    t = x1.shape[0]
    tm = COMBINE_TM
    npt = t_prompt // tm
    grid_spec = pltpu.PrefetchScalarGridSpec(
        num_scalar_prefetch=1,
        grid=(t // tm,),
        in_specs=[
            pl.BlockSpec((tm * TOP_K,), lambda i, seg: (i,), memory_space=pltpu.SMEM),
            pl.BlockSpec((tm * TOP_K,), lambda i, seg: (i,), memory_space=pltpu.SMEM),
            pl.BlockSpec((tm, D_MODEL), lambda i, seg: (i, 0)),
            pl.BlockSpec((tm, TOP_K), lambda i, seg: (i, 0)),
            pl.BlockSpec((1, D_MODEL), lambda i, seg: (0, 0)),
            pl.BlockSpec(memory_space=pl.ANY),
        ],
        out_specs=[
            pl.BlockSpec((tm, D_MODEL), lambda i, seg: (jnp.minimum(i, npt - 1), 0)),
            pl.BlockSpec((tm, D_MODEL), lambda i, seg: (jnp.maximum(i - npt, 0), 0)),
        ],
        scratch_shapes=[pltpu.VMEM((TOP_K, tm, D_MODEL), F32), pltpu.SemaphoreType.DMA(())],
    )
    return pl.pallas_call(
        functools.partial(_combine_body, n_prompt_tiles=npt),
        grid_spec=grid_spec,
        out_shape=[
            jax.ShapeDtypeStruct((t_prompt, D_MODEL), F32),
            jax.ShapeDtypeStruct((t - t_prompt, D_MODEL), F32),
        ],
        compiler_params=_cparams(("arbitrary",)),
        name="combine",
    )(seg_start, e_flat, rank_flat, x1, gates, g_final, ysorted)


def _expert_tables(counts, n_assign):
    pad = EXPERT_PAD
    padded = ((counts + pad - 1) // pad) * pad
    seg_start = jnp.cumsum(padded) - padded
    n_sb_e = (counts + EXPERT_R - 1) // EXPERT_R
    sb_cum = jnp.cumsum(n_sb_e)
    n_sb = n_assign // EXPERT_R + N_EXPERTS
    s = jnp.arange(n_sb, dtype=I32)
    total = sb_cum[-1]
    s_eff = jnp.minimum(s, total - 1)
    e = jnp.sum((sb_cum[None, :] <= s_eff[:, None]).astype(I32), axis=1)
    k = s_eff - (sb_cum[e] - n_sb_e[e])
    rows = jnp.clip(counts[e] - k * EXPERT_R, 0, EXPERT_R)
    valid = s < total
    nsub = jnp.where(valid, (rows + pad - 1) // pad, 0)
    start = seg_start[e] + k * EXPERT_R
    used_blocks = (jnp.sum(padded) // pad).reshape(1)
    return seg_start.astype(I32), e, start.astype(I32), nsub.astype(I32), used_blocks.astype(I32)


def kernel(x_prompt, x_sample, state_rglru_h, state_rglru_conv, state_ssd, state_ssd_conv, g_mix, w_in, conv_rg_w, conv_rg_b, rg_wa, rg_ba, rg_wi, rg_bi, rg_lambda, conv_ssd_w, conv_ssd_b, ssd_dt_bias, ssd_a_log, ssd_d, ssd_norm_g, w_out, g_ffn, w_router, b_router, w_gu, b_gu, w_down, b_down, g_final):
    depth = g_mix.shape[0]
    assert depth == 1
    bp, lp, _ = x_prompt.shape
    bs, ls, _ = x_sample.shape
    tp, ts = bp * lp, bs * ls
    t = tp + ts
    hp = SSD_HEADS * SSD_HEAD_DIM
    l = 0

    xp = x_prompt.reshape(tp, D_MODEL)
    xs = jnp.transpose(x_sample, (1, 0, 2)).reshape(ts, D_MODEL)
    w_in_t = jnp.transpose(w_in[l])
    row = lambda v: v.reshape(1, -1)
    pad_heads = lambda v: jnp.pad(v.reshape(1, -1), ((0, 0), (0, LANES - SSD_HEADS)))

    hb, dt_raw = _norm_dt(xp, xs, row(g_mix[l]), w_in_t)
    proj = _in_proj(hb, w_in_t)

    rg_params = (conv_rg_w[l], row(conv_rg_b[l]), rg_wa[l], row(rg_ba[l]), rg_wi[l], row(rg_bi[l]),
                 row(rg_lambda[l]))
    d_exp = row(jnp.repeat(ssd_d[l], SSD_HEAD_DIM))
    ssd_params = (conv_ssd_w[l], row(conv_ssd_b[l]), pad_heads(ssd_dt_bias[l]), pad_heads(ssd_a_log[l]),
                  d_exp, row(ssd_norm_g[l]))

    y_rg, rgh_p, rgc_p = _rglru_prompt(proj, bp, lp, *rg_params)
    y_ssd, ssd_p, ssdc_p = _ssd_prompt(proj, dt_raw, bp, lp, *ssd_params)

    assert tp % (bs * ls) == 0 and bs % 8 == 0
    mix_s3, rgh_s, rgc_s, ssd_s, ssdc_s = _mix_sample(
        proj.reshape(t // bs, bs, D_MAIN), dt_raw.reshape(t // bs, bs, LANES), tp, ls, bs,
        state_rglru_h[l], jnp.transpose(state_rglru_conv[l], (1, 0, 2)),
        state_ssd[l].reshape(bs, hp, SSD_STATE), jnp.transpose(state_ssd_conv[l], (1, 0, 2)),
        rg_params, ssd_params)
    mix_s = mix_s3.reshape(ts, 2 * D_MODEL)

    x1 = _out_proj(y_rg, y_ssd, mix_s, w_out[l], xp, xs)

    wr = jnp.pad(w_router[l], ((0, 0), (0, LANES - N_EXPERTS)))
    br = jnp.pad(b_router[l].reshape(1, -1), ((0, 0), (0, LANES - N_EXPERTS)))
    h2, e_idx, gates, rank, counts = _route(x1, row(g_ffn[l]), wr, br)

    n_assign = t * TOP_K
    n_slots = n_assign + N_EXPERTS * EXPERT_PAD
    counts = counts[0, :N_EXPERTS]
    seg_start, sb_expert, sb_start, sb_nsub, used_blocks = _expert_tables(counts, n_assign)
    e_flat = e_idx.reshape(-1)
    rank_flat = rank.reshape(-1)
    xsorted = _scatter(seg_start, counts, e_flat, rank_flat, h2, n_slots)
    ysorted = _experts(sb_expert, sb_start, sb_nsub, used_blocks, xsorted, w_gu[l], b_gu[l].reshape(N_EXPERTS, 1, -1),
                       w_down[l], b_down[l].reshape(N_EXPERTS, 1, -1))
    y_p, y_s = _combine(seg_start, e_flat, rank_flat, x1, gates, row(g_final), ysorted, tp)

    return (y_p.reshape(x_prompt.shape).astype(x_prompt.dtype),
            jnp.transpose(y_s.reshape(ls, bs, D_MODEL), (1, 0, 2)).astype(x_sample.dtype),
            rgh_p.reshape(depth, bp, D_RG),
            rgc_p.reshape(depth, bp, CONV_W - 1, D_RG),
            ssd_p.reshape(depth, bp, SSD_HEADS, SSD_HEAD_DIM, SSD_STATE),
            ssdc_p.reshape(depth, bp, CONV_W - 1, D_XBC),
            rgh_s.reshape(depth, bs, D_RG),
            jnp.transpose(rgc_s, (1, 0, 2)).reshape(depth, bs, CONV_W - 1, D_RG),
            ssd_s.reshape(depth, bs, SSD_HEADS, SSD_HEAD_DIM, SSD_STATE),
            jnp.transpose(ssdc_s, (1, 0, 2)).reshape(depth, bs, CONV_W - 1, D_XBC))
```

```python
import functools

import jax
import jax.numpy as jnp
from jax import lax
from jax.experimental import pallas as pl
from jax.experimental.pallas import tpu as pltpu

F32 = jnp.float32
BF16 = jnp.bfloat16
I32 = jnp.int32
HIGHEST = lax.Precision.HIGHEST

EPS = 1e-6
D_MODEL = 2048
D_RG = 2048
RG_BLOCKS = 16
RG_BLOCK_W = 128
RG_C = 8.0
D_SSD = 2048
SSD_HEAD_DIM = 64
SSD_HEADS = 32
SSD_GROUPS = 4
SSD_STATE = 128
SSD_CHUNK = 128
CONV_W = 4
D_BC = SSD_GROUPS * SSD_STATE
D_XBC = D_SSD + 2 * D_BC
D_MAIN = 2 * D_RG + D_SSD + D_XBC
N_EXPERTS = 32
TOP_K = 4
D_FF = 2048
SWIGLU_ALPHA = 1.702
SWIGLU_LIMIT = 7.0

LANES = 128
VMEM_LIMIT = 56 * 1024 * 1024
PROJ_VMEM_LIMIT = 58 * 1024 * 1024

ROW_TILE = 512
PROJ_TM = 1088
PROJ_TN = 1536
OUT_TN = 512
RG_TL = 512
SAMPLE_BB = 8
ROUTE_TM = 512
SCATTER_TM = 512
COMBINE_TM = 512
EXPERT_PAD = 128
EXPERT_BM = 512
EXPERT_R = 1152
EXPERT_TF = 512
DMA_UNROLL = 8


def _cparams(sem, vmem=VMEM_LIMIT):
    return pltpu.CompilerParams(dimension_semantics=sem, vmem_limit_bytes=vmem)


def _softplus(x):
    return jnp.maximum(x, 0.0) + jnp.log1p(jnp.exp(-jnp.abs(x)))


_sigmoid = jax.nn.sigmoid
_silu = jax.nn.silu


def _gelu_tanh(x):
    return jax.nn.gelu(x, approximate=True)


def _head_expand_matrix(dtype):
    r = lax.broadcasted_iota(I32, (LANES, D_SSD), 0)
    c = lax.broadcasted_iota(I32, (LANES, D_SSD), 1)
    return jnp.where(lax.shift_right_logical(c, 6) == r, 1.0, 0.0).astype(dtype)


def _expand_heads(v, e_b16):
    rows = v.shape[0]
    hi = v.astype(BF16)
    r1 = v - hi.astype(F32)
    mid = r1.astype(BF16)
    lo = (r1 - mid.astype(F32)).astype(BF16)
    o = jnp.dot(jnp.concatenate([hi, mid, lo], axis=0), e_b16, preferred_element_type=F32)
    return (o[0:rows] + o[rows:2 * rows]) + o[2 * rows:3 * rows]


def _norm_dt_body(xp_ref, xs_ref, g_ref, wdt_ref, hb_ref, dt_ref, *, n_prompt_tiles):
    i = pl.program_id(0)

    def run(x_ref):
        x = x_ref[...]
        h = x * lax.rsqrt(jnp.mean(x * x, axis=-1, keepdims=True) + EPS) * g_ref[...]
        hb = h.astype(BF16)
        hb_ref[...] = hb
        r = lax.broadcasted_iota(I32, (LANES, 1), 0)
        wdt = jnp.where(r < SSD_HEADS, wdt_ref[...], 0.0).astype(BF16)
        dt_ref[...] = lax.dot_general(hb, wdt, (((1,), (1,)), ((), ())), preferred_element_type=F32)

    @pl.when(i < n_prompt_tiles)
    def _():
        run(xp_ref)

    @pl.when(i >= n_prompt_tiles)
    def _():
        run(xs_ref)


def _norm_dt(xp, xs, g, w_in_t):
    tp, ts = xp.shape[0], xs.shape[0]
    npt, nst = tp // ROW_TILE, ts // ROW_TILE
    t = tp + ts
    return pl.pallas_call(
        functools.partial(_norm_dt_body, n_prompt_tiles=npt),
        grid=(npt + nst,),
        in_specs=[
            pl.BlockSpec((ROW_TILE, D_MODEL), lambda i: (jnp.minimum(i, npt - 1), 0)),
            pl.BlockSpec((ROW_TILE, D_MODEL), lambda i: (jnp.maximum(i - npt, 0), 0)),
            pl.BlockSpec((1, D_MODEL), lambda i: (0, 0)),
            pl.BlockSpec((LANES, D_MODEL), lambda i: (D_MAIN // LANES, 0)),
        ],
        out_specs=[
            pl.BlockSpec((ROW_TILE, D_MODEL), lambda i: (i, 0)),
            pl.BlockSpec((ROW_TILE, LANES), lambda i: (i, 0)),
        ],
        out_shape=[jax.ShapeDtypeStruct((t, D_MODEL), BF16), jax.ShapeDtypeStruct((t, LANES), F32)],
        compiler_params=_cparams(("parallel",)),
        name="norm_dt",
    )(xp, xs, g, w_in_t)


def _cast_weight_tile(w_ref, wb_ref, rows_per_step=256):
    k = w_ref.shape[0]

    def step(s, c):
        r = pl.multiple_of(s * rows_per_step, rows_per_step)
        wb_ref[pl.ds(r, rows_per_step), :] = w_ref[pl.ds(r, rows_per_step), :].astype(BF16)
        return c

    lax.fori_loop(0, k // rows_per_step, step, 0)


def _in_proj_body(x_ref, w_ref, o_ref, wb_ref):
    @pl.when(pl.program_id(1) == 0)
    def _():
        _cast_weight_tile(w_ref, wb_ref)

    o_ref[...] = lax.dot_general(x_ref[...], wb_ref[...], (((1,), (1,)), ((), ())), preferred_element_type=F32)


def _in_proj(hb, w_in_t):
    t = hb.shape[0]
    return pl.pallas_call(
        _in_proj_body,
        grid=(D_MAIN // PROJ_TN, t // PROJ_TM),
        in_specs=[
            pl.BlockSpec((PROJ_TM, D_MODEL), lambda j, i: (i, 0)),
            pl.BlockSpec((PROJ_TN, D_MODEL), lambda j, i: (j, 0)),
        ],
        out_specs=pl.BlockSpec((PROJ_TM, PROJ_TN), lambda j, i: (i, j)),
        out_shape=jax.ShapeDtypeStruct((t, D_MAIN), F32),
        scratch_shapes=[pltpu.VMEM((PROJ_TN, D_MODEL), BF16)],
        compiler_params=_cparams(("arbitrary", "arbitrary"), vmem=PROJ_VMEM_LIMIT),
        name="in_proj",
    )(hb, w_in_t)


def _out_proj_body(rg_ref, ssd_ref, ms_ref, w_ref, xp_ref, xs_ref, o_ref, wb_ref, *, n_prompt_tiles):
    i = pl.program_id(1)

    @pl.when(i == 0)
    def _():
        _cast_weight_tile(w_ref, wb_ref)

    @pl.when(i < n_prompt_tiles)
    def _():
        m = jnp.concatenate([rg_ref[...], ssd_ref[...]], axis=1)
        o_ref[...] = xp_ref[...] + jnp.dot(m, wb_ref[...], preferred_element_type=F32)

    @pl.when(i >= n_prompt_tiles)
    def _():
        o_ref[...] = xs_ref[...] + jnp.dot(ms_ref[...].astype(BF16), wb_ref[...], preferred_element_type=F32)


def _out_proj(y_rg, y_ssd, mix_s, w_out, xp, xs):
    tp, ts = y_rg.shape[0], mix_s.shape[0]
    npt, nst = tp // ROW_TILE, ts // ROW_TILE
    prompt_rows = lambda j, i: (jnp.minimum(i, npt - 1), 0)
    return pl.pallas_call(
        functools.partial(_out_proj_body, n_prompt_tiles=npt),
        grid=(D_MODEL // OUT_TN, npt + nst),
        in_specs=[
            pl.BlockSpec((ROW_TILE, D_MODEL), prompt_rows),
            pl.BlockSpec((ROW_TILE, D_MODEL), prompt_rows),
            pl.BlockSpec((ROW_TILE, 2 * D_MODEL), lambda j, i: (jnp.maximum(i - npt, 0), 0)),
            pl.BlockSpec((2 * D_MODEL, OUT_TN), lambda j, i: (0, j)),
            pl.BlockSpec((ROW_TILE, OUT_TN), lambda j, i: (jnp.minimum(i, npt - 1), j)),
            pl.BlockSpec((ROW_TILE, OUT_TN), lambda j, i: (jnp.maximum(i - npt, 0), j)),
        ],
        out_specs=pl.BlockSpec((ROW_TILE, OUT_TN), lambda j, i: (i, j)),
        out_shape=jax.ShapeDtypeStruct((tp + ts, D_MODEL), F32),
        scratch_shapes=[pltpu.VMEM((2 * D_MODEL, OUT_TN), BF16)],
        compiler_params=_cparams(("arbitrary", "arbitrary")),
        name="out_proj",
    )(y_rg, y_ssd, mix_s, w_out, xp, xs)


def _rg_gates(xc, wa_ref, ba_ref, wi_ref, bi_ref, lam_ref):
    xcb = xc.astype(BF16)
    rs, is_ = [], []
    for h in range(RG_BLOCKS):
        xh = xcb[:, h * RG_BLOCK_W:(h + 1) * RG_BLOCK_W]
        rs.append(jnp.dot(xh, wa_ref[h].astype(BF16), preferred_element_type=F32))
        is_.append(jnp.dot(xh, wi_ref[h].astype(BF16), preferred_element_type=F32))
    r = _sigmoid(jnp.concatenate(rs, axis=1) + ba_ref[...])
    ig = _sigmoid(jnp.concatenate(is_, axis=1) + bi_ref[...])
    log_a = (-RG_C * r) * _softplus(-lam_ref[...])
    a = jnp.exp(log_a)
    u = jnp.sqrt(-jnp.tanh(log_a) * (a * a + 1.0)) * (ig * xc)
    return a, u


def _rglru_prompt_body(x_ref, gate_ref, cw_ref, cb_ref, wa_ref, ba_ref, wi_ref, bi_ref, lam_ref,
                       y_ref, h_ref, cs_ref, ext_ref, a_ref, u_ref, car_ref):
    c = pl.program_id(1)
    tl = x_ref.shape[0]

    @pl.when(c == 0)
    def _():
        ext_ref[0:8, :] = jnp.zeros((8, D_RG), F32)
        car_ref[...] = jnp.zeros((1, D_RG), F32)

    ext_ref[8:8 + tl, :] = x_ref[...]
    xc = cb_ref[...] + ext_ref[5:5 + tl, :] * cw_ref[0:1, :]
    xc = xc + ext_ref[6:6 + tl, :] * cw_ref[1:2, :]
    xc = xc + ext_ref[7:7 + tl, :] * cw_ref[2:3, :]
    xc = xc + ext_ref[8:8 + tl, :] * cw_ref[3:4, :]
    tail = ext_ref[tl:tl + 8, :]
    ext_ref[0:8, :] = tail

    a, u = _rg_gates(xc, wa_ref, ba_ref, wi_ref, bi_ref, lam_ref)
    a_ref[...] = a
    u_ref[...] = u

    row = lax.broadcasted_iota(I32, (8, D_RG), 0)

    def group(g, carry):
        r0 = pl.multiple_of(g * 8, 8)
        a8 = a_ref[pl.ds(r0, 8), :]
        u8 = u_ref[pl.ds(r0, 8), :]
        for s in (1, 2, 4):
            keep = row >= s
            a_sh = jnp.where(keep, pltpu.roll(a8, s, 0), 1.0)
            u_sh = jnp.where(keep, pltpu.roll(u8, s, 0), 0.0)
            u8 = a8 * u_sh + u8
            a8 = a8 * a_sh
        h8 = a8 * carry + u8
        u_ref[pl.ds(r0, 8), :] = h8
        return h8[7:8, :]

    carry = lax.fori_loop(0, tl // 8, group, car_ref[...])
    car_ref[...] = carry
    y_ref[...] = (u_ref[...] * _gelu_tanh(gate_ref[...])).astype(BF16)

    @pl.when(c == pl.num_programs(1) - 1)
    def _():
        h_ref[...] = carry
        cs_ref[...] = tail[5:8, :]


def _rglru_prompt(proj, batch, seq, conv_w, conv_b, wa, ba, wi, bi, lam):
    nc = seq // RG_TL
    t = batch * seq
    vec =pl.BlockSpec((1, D_RG), lambda b, c: (0, 0))
    blk = pl.BlockSpec((RG_BLOCKS, RG_BLOCK_W, RG_BLOCK_W), lambda b, c: (0, 0, 0))
    return pl.pallas_call(
        _rglru_prompt_body,
        grid=(batch, nc),
        in_specs=[
            pl.BlockSpec((RG_TL, D_RG), lambda b, c: (b * nc + c, 0)),
            pl.BlockSpec((RG_TL, D_RG), lambda b, c: (b * nc + c, 1)),
            pl.BlockSpec((CONV_W, D_RG), lambda b, c: (0, 0)),
            vec, blk, vec, blk, vec, vec,
        ],
        out_specs=[
            pl.BlockSpec((RG_TL, D_RG), lambda b, c: (b * nc + c, 0)),
            pl.BlockSpec((None, 1, D_RG), lambda b, c: (b, 0, 0)),
            pl.BlockSpec((None, CONV_W - 1, D_RG), lambda b, c: (b, 0, 0)),
        ],
        out_shape=[
            jax.ShapeDtypeStruct((t, D_RG), BF16),
            jax.ShapeDtypeStruct((batch, 1, D_RG), F32),
            jax.ShapeDtypeStruct((batch, CONV_W - 1, D_RG), F32),
        ],
        scratch_shapes=[
            pltpu.VMEM((RG_TL + 8, D_RG), F32),
            pltpu.VMEM((RG_TL, D_RG), F32),
            pltpu.VMEM((RG_TL, D_RG), F32),
            pltpu.VMEM((1, D_RG), F32),
        ],
        compiler_params=_cparams(("arbitrary", "arbitrary")),
        name="rglru_prompt",
    )(proj, proj, conv_w, conv_b, wa, ba, wi, bi, lam)


def _group_rmsnorm(y, g_row):
    outs = []
    for g in range(SSD_GROUPS):
        yg = y[:, g * D_BC:(g + 1) * D_BC]
        outs.append(yg * lax.rsqrt(jnp.mean(yg * yg, axis=-1, keepdims=True) + EPS))
    return jnp.concatenate(outs, axis=1) * g_row


def _ssd_prompt_body(xbc_ref, z_ref, dt_ref, cw_ref, cb_ref, dtb_ref, alog_ref, dexp_ref, ng_ref,
                     y_ref, s_out_ref, cs_ref, ext_ref, s_ref):
    c = pl.program_id(1)
    L = SSD_CHUNK

    @pl.when(c == 0)
    def _():
        ext_ref[0:8, :] = jnp.zeros((8, D_XBC), F32)
        s_ref[...] = jnp.zeros_like(s_ref)

    ext_ref[8:8 + L, :] = xbc_ref[...]
    xc = cb_ref[...] + ext_ref[5:5 + L, :] * cw_ref[0:1, :]
    xc = xc + ext_ref[6:6 + L, :] * cw_ref[1:2, :]
    xc = xc + ext_ref[7:7 + L, :] * cw_ref[2:3, :]
    xc = xc + ext_ref[8:8 + L, :] * cw_ref[3:4, :]
    tail = ext_ref[L:L + 8, :]
    ext_ref[0:8, :] = tail
    xc = _silu(xc)
    xs = xc[:, :D_SSD]
    bm = xc[:, D_SSD:D_SSD + D_BC].astype(BF16)
    cm = xc[:, D_SSD + D_BC:].astype(BF16)

    lane = lax.broadcasted_iota(I32, (1, LANES), 1)
    a_neg = jnp.where(lane < SSD_HEADS, -jnp.exp(alog_ref[...]), 0.0)
    dt = _softplus(dt_ref[...] + dtb_ref[...])
    da = dt * a_neg
    ri = lax.broadcasted_iota(I32, (L, L), 0)
    ci = lax.broadcasted_iota(I32, (L, L), 1)
    causal = ci <= ri
    tri = jnp.where(causal, 1.0, 0.0).astype(F32)
    acum = jnp.dot(tri, da, precision=HIGHEST, preferred_element_type=F32)
    acum_t = acum.T
    a_last = acum[L - 1:L, :]

    e_b16 = _head_expand_matrix(BF16)
    stacked = jnp.concatenate([dt, jnp.exp(a_last - acum), jnp.exp(acum)], axis=0)
    expd = _expand_heads(stacked, e_b16)
    dt_e, dend_e, ea_e = expd[0:L], expd[L:2 * L], expd[2 * L:3 * L]
    xdt = xs * dt_e
    xdt_b = xdt.astype(BF16)
    w_b = (xdt * dend_e)

    lane_l = lax.broadcasted_iota(I32, (L, LANES), 1)
    lo = lane_l < SSD_HEAD_DIM
    y_parts = []
    for g in range(SSD_GROUPS):
        cg = cm[:, g * SSD_STATE:(g + 1) * SSD_STATE]
        bg = bm[:, g * SSD_STATE:(g + 1) * SSD_STATE]
        cb = lax.dot_general(cg, bg, (((1,), (1,)), ((), ())), preferred_element_type=F32)
        hpg = SSD_HEADS // SSD_GROUPS
        for jp in range(hpg // 2):
            ms = []
            for h in (g * hpg + 2 * jp, g * hpg + 2 * jp + 1):
                seg = acum[:, h:h + 1] - acum_t[h:h + 1, :]
                decay = jnp.exp(jnp.where(causal, seg, -jnp.inf))
                ms.append((cb * decay).astype(BF16))
            col = (g * hpg + 2 * jp) * SSD_HEAD_DIM
            xp = xdt_b[:, col:col + LANES]
            zero = jnp.zeros_like(xp)
            rhs = jnp.concatenate([jnp.where(lo, xp, zero), jnp.where(lo, zero, xp)], axis=0)
            y_parts.append(jnp.dot(jnp.concatenate(ms, axis=1), rhs, preferred_element_type=F32))
    y_diag = jnp.concatenate(y_parts, axis=1)

    y_off_parts = []
    for g in range(SSD_GROUPS):
        cg = cm[:, g * SSD_STATE:(g + 1) * SSD_STATE]
        sg = s_ref[g * D_BC:(g + 1) * D_BC, :].astype(BF16)
        y_off_parts.append(lax.dot_general(cg, sg, (((1,), (1,)), ((), ())), preferred_element_type=F32))
    y_off = jnp.concatenate(y_off_parts, axis=1) * ea_e

    cd_col = jnp.exp(acum_t[:, L - 1:L])
    for g in range(SSD_GROUPS):
        bg = bm[:, g * SSD_STATE:(g + 1) * SSD_STATE]
        wg_t = w_b[:, g * D_BC:(g + 1) * D_BC].T.astype(BF16)
        upd = jnp.dot(wg_t, bg, preferred_element_type=F32)
        hpg = SSD_HEADS // SSD_GROUPS
        for e in range(hpg):
            h = g * hpg + e
            r0 = h * SSD_HEAD_DIM
            cd = jnp.broadcast_to(cd_col[h:h + 1, :], (SSD_HEAD_DIM, SSD_STATE))
            s_ref[r0:r0 + SSD_HEAD_DIM, :] = (cd * s_ref[r0:r0 + SSD_HEAD_DIM, :]
                                              + upd[e * SSD_HEAD_DIM:(e + 1) * SSD_HEAD_DIM, :])

    y = (y_diag + y_off + dexp_ref[...] * xs) * _silu(z_ref[...])
    y_ref[...] = _group_rmsnorm(y, ng_ref[...]).astype(BF16)

    @pl.when(c == pl.num_programs(1) - 1)
    def _():
        s_out_ref[...] = s_ref[...]
        cs_ref[...] = tail[5:8, :]


def _ssd_prompt(proj, dt_raw, batch, seq, conv_w, conv_b, dt_bias, a_log, d_exp, norm_g):
    nc = seq // SSD_CHUNK
    L = SSD_CHUNK
    hp = SSD_HEADS * SSD_HEAD_DIM
    const = lambda b, c: (0, 0)
    return pl.pallas_call(
        _ssd_prompt_body,
        grid=(batch, nc),
        in_specs=[
            pl.BlockSpec((L, D_XBC), lambda b, c: (b * nc + c, 2)),
            pl.BlockSpec((L, D_SSD), lambda b, c: (b * nc + c, 2)),
            pl.BlockSpec((L, LANES), lambda b, c: (b * nc + c, 0)),
            pl.BlockSpec((CONV_W, D_XBC), const),
            pl.BlockSpec((1, D_XBC), const),
            pl.BlockSpec((1, LANES), const),
            pl.BlockSpec((1, LANES), const),
            pl.BlockSpec((1, D_SSD), const),
            pl.BlockSpec((1, D_SSD), const),
        ],
        out_specs=[
            pl.BlockSpec((L, D_SSD), lambda b, c: (b * nc + c, 0)),
            pl.BlockSpec((None, hp, SSD_STATE), lambda b, c: (b, 0, 0)),
            pl.BlockSpec((None, CONV_W - 1, D_XBC), lambda b, c: (b, 0, 0)),
        ],
        out_shape=[
            jax.ShapeDtypeStruct((batch * seq, D_SSD), BF16),
            jax.ShapeDtypeStruct((batch, hp, SSD_STATE), F32),
            jax.ShapeDtypeStruct((batch, CONV_W - 1, D_XBC), F32),
        ],
        scratch_shapes=[
            pltpu.VMEM((L + 8, D_XBC), F32),
            pltpu.VMEM((hp, SSD_STATE), F32),
        ],
        compiler_params=_cparams(("arbitrary", "arbitrary")),
        name="ssd_prompt",
    )(proj, proj, dt_raw, conv_w, conv_b, dt_bias, a_log, d_exp, norm_g)


def _mix_sample_body(p_ref, dt_ref, h0_ref, rgc0_ref, s0_ref, sc0_ref,
                     rcw_ref, rcb_ref, wa_ref, ba_ref, wi_ref, bi_ref, lam_ref,
                     scw_ref, scb_ref, dtb_ref, alog_ref, dexp_ref, ng_ref,
                     y_ref, h_ref, rgc_ref, s_ref, sc_ref):
    nl = p_ref.shape[0]
    bb = p_ref.shape[1]
    o_gate, o_z, o_xbc = D_RG, 2 * D_RG, 2 * D_RG + D_SSD

    def col(l, off, width):
        return p_ref[l, :, off:off + width]

    hist = [rgc0_ref[k] for k in range(CONV_W - 1)]
    raw = hist + [col(l, 0, D_RG) for l in range(nl)]
    xcs = []
    for l in range(nl):
        acc = rcb_ref[...] + raw[l] * rcw_ref[0:1, :]
        for k in range(1, CONV_W):
            acc = acc + raw[l + k] * rcw_ref[k:k + 1, :]
        xcs.append(acc)
    a, u = _rg_gates(jnp.concatenate(xcs, axis=0), wa_ref, ba_ref, wi_ref, bi_ref, lam_ref)
    h = h0_ref[...]
    for l in range(nl):
        h = a[l * bb:(l + 1) * bb] * h + u[l * bb:(l + 1) * bb]
        y_ref[l, :, 0:D_RG] = h * _gelu_tanh(col(l, o_gate, D_RG))
    h_ref[...] = h
    for k in range(CONV_W - 1):
        rgc_ref[k] = raw[nl + k]

    hist = [sc0_ref[k] for k in range(CONV_W - 1)]
    raw = hist + [col(l, o_xbc, D_XBC) for l in range(nl)]
    for k in range(CONV_W - 1):
        sc_ref[k] = raw[nl + k]
    xcs = []
    for l in range(nl):
        acc = scb_ref[...] + raw[l] * scw_ref[0:1, :]
        for k in range(1, CONV_W):
            acc = acc + raw[l + k] * scw_ref[k:k + 1, :]
        xcs.append(_silu(acc))
    xs = [x[:, :D_SSD] for x in xcs]
    bms = [x[:, D_SSD:D_SSD + D_BC].astype(BF16) for x in xcs]
    cms = [x[:, D_SSD + D_BC:].astype(BF16) for x in xcs]

    lane = lax.broadcasted_iota(I32, (1, LANES), 1)
    a_neg = jnp.where(lane < SSD_HEADS, -jnp.exp(alog_ref[...]), 0.0)
    dts = [_softplus(dt_ref[l] + dtb_ref[...]) for l in range(nl)]
    acums = []
    run = jnp.zeros((bb, LANES), F32)
    for l in range(nl):
        run = run + dts[l] * a_neg
        acums.append(run)
    a_last = acums[nl - 1]

    e_b16 = _head_expand_matrix(BF16)
    stacked = jnp.concatenate(dts + [jnp.exp(a_last - ac) for ac in acums] + [jnp.exp(ac) for ac in acums], axis=0)
    expd = _expand_heads(stacked, e_b16)
    dt_e = [expd[l * bb:(l + 1) * bb] for l in range(nl)]
    dend_e = [expd[(nl + l) * bb:(nl + l + 1) * bb] for l in range(nl)]
    ea_e = [expd[(2 * nl + l) * bb:(2 * nl + l + 1) * bb] for l in range(nl)]
    xdt = [xs[l] * dt_e[l] for l in range(nl)]
    xdt_r = [x.astype(BF16).astype(F32) for x in xdt]
    rows = nl * bb
    pad_rows = LANES - rows
    w_all = jnp.concatenate([(xdt[l] * dend_e[l]) for l in range(nl)]
                            + [jnp.zeros((pad_rows, D_SSD), F32)], axis=0)
    c_all = jnp.concatenate(cms, axis=0)
    b_all = jnp.concatenate(bms + [jnp.zeros((pad_rows, D_BC), BF16)], axis=0)

    r = lax.broadcasted_iota(I32, (D_BC, LANES), 0)
    cidx = lax.broadcasted_iota(I32, (D_BC, LANES), 1)
    gsum = jnp.where((lax.shift_right_logical(r, 7) == lax.shift_right_logical(cidx, 3)) & (cidx < SSD_HEADS),
                     1.0, 0.0).astype(F32)
    pairs = [(l, s) for l in range(nl) for s in range(l + 1)]
    prods = jnp.concatenate([cms[l].astype(F32) * bms[s].astype(F32) for (l, s) in pairs], axis=0)
    cbh = jnp.dot(prods, gsum, precision=HIGHEST, preferred_element_type=F32)
    m_list = []
    for idx, (l, s) in enumerate(pairs):
        decay = jnp.exp(acums[l] - acums[s])
        m_list.append((cbh[idx * bb:(idx + 1) * bb] * decay).astype(BF16))
    m_e = jnp.dot(jnp.concatenate(m_list, axis=0), e_b16, preferred_element_type=F32)
    y_diag = []
    for l in range(nl):
        acc = None
        for idx, (ll, s) in enumerate(pairs):
            if ll != l:
                continue
            term = m_e[idx * bb:(idx + 1) * bb] * xdt_r[s]
            acc = term if acc is None else acc + term
        y_diag.append(acc)

    cd_t = jnp.concatenate([jnp.exp(a_last), jnp.zeros((LANES - bb, LANES), F32)], axis=0).T
    row_seq = lax.broadcasted_iota(I32, (rows, 1), 0) & (bb - 1)
    row_pad = lax.broadcasted_iota(I32, (LANES, 1), 0)
    lane_q = lax.broadcasted_iota(I32, (LANES, LANES), 1)
    hpg = SSD_HEADS // SSD_GROUPS

    def seq_step(q, y_off):
        mine = row_seq == q
        mine_pad = ((row_pad & (bb - 1)) == q) & (row_pad < rows)
        cd_q = jnp.sum(jnp.where(lane_q == q, cd_t, 0.0), axis=1, keepdims=True)
        cd_q = jnp.broadcast_to(cd_q, (LANES, SSD_STATE))
        parts = []
        for g in range(SSD_GROUPS):
            s0 = s0_ref[q, g * D_BC:(g + 1) * D_BC, :]
            cg = c_all[:, g * SSD_STATE:(g + 1) * SSD_STATE]
            yq = lax.dot_general(cg, s0.astype(BF16), (((1,), (1,)), ((), ())), preferred_element_type=F32)
            parts.append(yq)
            bg = b_all[:, g * SSD_STATE:(g + 1) * SSD_STATE]
            wq = jnp.where(mine_pad, w_all[:, g * D_BC:(g + 1) * D_BC], 0.0)
            upd = jnp.dot(wq.T.astype(BF16), bg, preferred_element_type=F32)
            for e in range(hpg):
                hh = g * hpg + e
                cd = jnp.broadcast_to(cd_q[hh:hh + 1, :], (SSD_HEAD_DIM, SSD_STATE))
                s_ref[q, hh * SSD_HEAD_DIM:(hh + 1) * SSD_HEAD_DIM, :] = (
                    cd * s0[e * SSD_HEAD_DIM:(e + 1) * SSD_HEAD_DIM, :]
                    + upd[e * SSD_HEAD_DIM:(e + 1) * SSD_HEAD_DIM, :])
        yq_all = jnp.concatenate(parts, axis=1)
        return jnp.where(mine, yq_all, y_off)

    y_off = lax.fori_loop(0, bb, seq_step, jnp.zeros((rows, D_SSD), F32))

    for l in range(nl):
        y = (y_diag[l] + y_off[l * bb:(l + 1) * bb] * ea_e[l] + dexp_ref[...] * xs[l]) * _silu(col(l, o_z, D_SSD))
        y_ref[l, :, D_RG:2 * D_MODEL] = _group_rmsnorm(y, ng_ref[...])


def _mix_sample(proj3, dt3, row0, nl, n_seq, h0, rgc0, s0, sc0, rg_params, ssd_params):
    bb = SAMPLE_BB
    hp = SSD_HEADS * SSD_HEAD_DIM
    blk0 = row0 // (n_seq * nl)
    const2 = lambda i: (0, 0)
    rg_specs = [
        pl.BlockSpec((CONV_W, D_RG), const2), pl.BlockSpec((1, D_RG), const2),
        pl.BlockSpec((RG_BLOCKS, RG_BLOCK_W, RG_BLOCK_W), lambda i: (0, 0, 0)), pl.BlockSpec((1, D_RG), const2),
        pl.BlockSpec((RG_BLOCKS, RG_BLOCK_W, RG_BLOCK_W), lambda i: (0, 0, 0)), pl.BlockSpec((1, D_RG), const2),
        pl.BlockSpec((1, D_RG), const2),
    ]
    ssd_specs = [
        pl.BlockSpec((CONV_W, D_XBC), const2), pl.BlockSpec((1, D_XBC), const2),
        pl.BlockSpec((1, LANES), const2), pl.BlockSpec((1, LANES), const2),
        pl.BlockSpec((1, D_SSD), const2), pl.BlockSpec((1, D_SSD), const2),
    ]
    return pl.pallas_call(
        _mix_sample_body,
        grid=(n_seq // bb,),
        in_specs=[
            pl.BlockSpec((nl, bb, D_MAIN), lambda i: (blk0, i, 0)),
            pl.BlockSpec((nl, bb, LANES), lambda i: (blk0, i, 0)),
            pl.BlockSpec((bb, D_RG), lambda i: (i, 0)),
            pl.BlockSpec((CONV_W - 1, bb, D_RG), lambda i: (0, i, 0)),
            pl.BlockSpec((bb, hp, SSD_STATE), lambda i: (i, 0, 0)),
            pl.BlockSpec((CONV_W - 1, bb, D_XBC), lambda i: (0, i, 0)),
        ] + rg_specs + ssd_specs,
        out_specs=[
            pl.BlockSpec((nl, bb, 2 * D_MODEL), lambda i: (0, i, 0)),
            pl.BlockSpec((bb, D_RG), lambda i: (i, 0)),
            pl.BlockSpec((CONV_W - 1, bb, D_RG), lambda i: (0, i, 0)),
            pl.BlockSpec((bb, hp, SSD_STATE), lambda i: (i, 0, 0)),
            pl.BlockSpec((CONV_W - 1, bb, D_XBC), lambda i: (0, i, 0)),
        ],
        out_shape=[
            jax.ShapeDtypeStruct((nl, n_seq, 2 * D_MODEL), F32),
            jax.ShapeDtypeStruct((n_seq, D_RG), F32),
            jax.ShapeDtypeStruct((CONV_W - 1, n_seq, D_RG), F32),
            jax.ShapeDtypeStruct((n_seq, hp, SSD_STATE), F32),
            jax.ShapeDtypeStruct((CONV_W - 1, n_seq, D_XBC), F32),
        ],
        compiler_params=_cparams(("arbitrary",)),
        name="mix_sample",
    )(proj3, dt3, h0, rgc0, s0, sc0, *rg_params, *ssd_params)


def _route_body(x_ref, g_ref, wr_ref, br_ref, hp_ref, e_ref, gate_ref, rank_ref, cnt_ref, run_ref):
    i = pl.program_id(0)
    tm = x_ref.shape[0]

    @pl.when(i == 0)
    def _():
        run_ref[...] = jnp.zeros_like(run_ref)

    x = x_ref[...]
    h = x * lax.rsqrt(jnp.mean(x * x, axis=-1, keepdims=True) + EPS) * g_ref[...]
    hp_ref[...] = h

    lane = lax.broadcasted_iota(I32, (tm, LANES), 1).astype(F32)
    h_hi = h.astype(BF16)
    h_lo = (h - h_hi.astype(F32)).astype(BF16)
    w = wr_ref[...]
    w_hi = w.astype(BF16)
    w_lo = (w - w_hi.astype(F32)).astype(BF16)
    logits = (jnp.dot(h_hi, w_hi, preferred_element_type=F32)
              + (jnp.dot(h_hi, w_lo, preferred_element_type=F32) + jnp.dot(h_lo, w_hi, preferred_element_type=F32))
              + br_ref[...])
    work = jnp.where(lane < N_EXPERTS, logits, -jnp.inf)
    vals, idxs = [], []
    multi = jnp.zeros((tm, LANES), F32)
    for _ in range(TOP_K):
        m = jnp.max(work, axis=-1, keepdims=True)
        idx = jnp.min(jnp.where(work == m, lane, float(LANES)), axis=-1, keepdims=True)
        hit = lane == idx
        vals.append(m)
        idxs.append(idx)
        multi = jnp.where(hit, 1.0, multi)
        work = jnp.where(hit, -jnp.inf, work)
    ex = [jnp.exp(v - vals[0]) for v in vals]
    den = ex[0] + ex[1] + ex[2] + ex[3]
    gates = [e / den for e in ex]

    ri = lax.broadcasted_iota(I32, (tm, tm), 0)
    ci = lax.broadcasted_iota(I32, (tm, tm), 1)
    strict = jnp.where(ci < ri, 1.0, 0.0).astype(BF16)
    before = jnp.dot(strict, multi.astype(BF16), preferred_element_type=F32) + run_ref[...]
    ranks = [jnp.sum(jnp.where(lane == idx, before, 0.0), axis=-1, keepdims=True) for idx in idxs]
    run_ref[...] = run_ref[...] + jnp.sum(multi, axis=0, keepdims=True)

    lane4 = lax.broadcasted_iota(I32, (tm, TOP_K), 1)
    e_out = jnp.zeros((tm, TOP_K), I32)
    g_out = jnp.zeros((tm, TOP_K), F32)
    r_out = jnp.zeros((tm, TOP_K), I32)
    for k in range(TOP_K):
        e_out = jnp.where(lane4 == k, idxs[k].astype(I32), e_out)
        g_out = jnp.where(lane4 == k, gates[k], g_out)
        r_out = jnp.where(lane4 == k, ranks[k].astype(I32), r_out)
    e_ref[...] = e_out
    gate_ref[...] = g_out
    rank_ref[...] = r_out
    cnt_ref[...] = run_ref[...].astype(I32)


def _route(x1, g_ffn, w_router_pad, b_router_pad):
    t = x1.shape[0]
    tm = ROUTE_TM
    const = lambda i: (0, 0)
    return pl.pallas_call(
        _route_body,
        grid=(t // tm,),
        in_specs=[
            pl.BlockSpec((tm, D_MODEL), lambda i: (i, 0)),
            pl.BlockSpec((1, D_MODEL), const),
            pl.BlockSpec((D_MODEL, LANES), const),
            pl.BlockSpec((1, LANES), const),
        ],
        out_specs=[
            pl.BlockSpec((tm, D_MODEL), lambda i: (i, 0)),
            pl.BlockSpec((tm, TOP_K), lambda i: (i, 0)),
            pl.BlockSpec((tm, TOP_K), lambda i: (i, 0)),
            pl.BlockSpec((tm, TOP_K), lambda i: (i, 0)),
            pl.BlockSpec((1, LANES), const),
        ],
        out_shape=[
            jax.ShapeDtypeStruct((t, D_MODEL), F32),
            jax.ShapeDtypeStruct((t, TOP_K), I32),
            jax.ShapeDtypeStruct((t, TOP_K), F32),
            jax.ShapeDtypeStruct((t, TOP_K), I32),
            jax.ShapeDtypeStruct((1, LANES), I32),
        ],
        scratch_shapes=[pltpu.VMEM((1, LANES), F32)],
        compiler_params=_cparams(("arbitrary",)),
        name="route",
    )(x1, g_ffn, w_router_pad, b_router_pad)


def _scatter_body(seg_ref, cnt_ref, e_ref, rank_ref, h_ref, xs_ref, zero_ref, sem, zsem):
    i = pl.program_id(0)
    tm = h_ref.shape[0]
    n = e_ref.shape[0]

    def issue(blk, c):
        for u in range(DMA_UNROLL):
            a = blk * DMA_UNROLL + u
            pos = seg_ref[e_ref[a]] + rank_ref[a]
            t = lax.shift_right_logical(a, 2)
            pltpu.make_async_copy(h_ref.at[pl.ds(t, 1), :], xs_ref.at[pl.ds(pos, 1), :], sem).start()
        return c

    lax.fori_loop(0, n // DMA_UNROLL, issue, 0)
    for _ in range(TOP_K):
        pltpu.make_async_copy(h_ref, xs_ref.at[pl.ds(0, tm), :], sem).wait()

    @pl.when(i == pl.num_programs(0) - 1)
    def _():
        zero_ref[...] = jnp.zeros_like(zero_ref)
        pad = EXPERT_PAD

        def row_fill(row):
            return pltpu.make_async_copy(zero_ref.at[pl.ds(0, 1), :], xs_ref.at[pl.ds(row, 1), :], zsem)

        def block_fill(blk):
            dst = pl.multiple_of(blk * pad, pad)
            return pltpu.make_async_copy(zero_ref, xs_ref.at[pl.ds(dst, pad), :], zsem)

        def for_each(lo, hi, fn):
            def step(r, c):
                fn(r)
                return c
            lax.fori_loop(lo, hi, step, 0)

        def per_expert(e, used):
            cnt = cnt_ref[e]
            hi = ((cnt + pad - 1) // pad) * pad
            base = seg_ref[e]
            for_each(cnt, hi, lambda r: row_fill(base + r).start())
            for_each(cnt, hi, lambda r: row_fill(base + r).wait())
            return used + hi // pad

        used = lax.fori_loop(0, N_EXPERTS, per_expert, jnp.int32(0))
        n_blocks = xs_ref.shape[0] // pad
        for_each(used, n_blocks, lambda blk: block_fill(blk).start())
        for_each(used, n_blocks, lambda blk: block_fill(blk).wait())


def _scatter(seg_start, counts, e_flat, rank_flat, h2, n_slots):
    t = h2.shape[0]
    tm = SCATTER_TM
    grid_spec = pltpu.PrefetchScalarGridSpec(
        num_scalar_prefetch=2,
        grid=(t // tm,),
        in_specs=[
            pl.BlockSpec((tm * TOP_K,), lambda i, seg, cnt: (i,), memory_space=pltpu.SMEM),
            pl.BlockSpec((tm * TOP_K,), lambda i, seg, cnt: (i,), memory_space=pltpu.SMEM),
            pl.BlockSpec((tm, D_MODEL), lambda i, seg, cnt: (i, 0)),
        ],
        out_specs=pl.BlockSpec(memory_space=pl.ANY),
        scratch_shapes=[pltpu.VMEM((EXPERT_PAD, D_MODEL), F32), pltpu.SemaphoreType.DMA(()),
                        pltpu.SemaphoreType.DMA(())],
    )
    return pl.pallas_call(
        _scatter_body,
        grid_spec=grid_spec,
        out_shape=jax.ShapeDtypeStruct((n_slots, D_MODEL), F32),
        compiler_params=_cparams(("arbitrary",)),
        name="scatter_rows",
    )(seg_start, counts, e_flat, rank_flat, h2)


def _experts_body(sbe_ref, sbs_ref, sbn_ref, used_ref, xs_ref, wg_ref, wu_ref, bg_ref, bu_ref, wd_ref, bd_ref,
                  ys_ref, xin_ref, xbuf_ref, acc_ref, wgu_ref, wdb_ref, in_sem, out_sem):
    del sbe_ref
    sb = pl.program_id(0)
    j = pl.program_id(1)
    nj = pl.num_programs(1)
    n_sb = pl.num_programs(0)
    nblk = sbn_ref[sb]
    start = sbs_ref[sb]
    cb = EXPERT_PAD
    bm = EXPERT_BM
    tf = wd_ref.shape[0]

    def in_copy(which, r):
        src = pl.multiple_of(sbs_ref[which] + r * cb, cb)
        dst = pl.multiple_of(r * cb, cb)
        return pltpu.make_async_copy(xs_ref.at[pl.ds(src, cb), :], xin_ref.at[pl.ds(dst, cb), :], in_sem.at[r])

    def out_copy(r):
        src = pl.multiple_of(r * cb, cb)
        dst = pl.multiple_of(start + r * cb, cb)
        return pltpu.make_async_copy(acc_ref.at[pl.ds(src, cb), :], ys_ref.at[pl.ds(dst, cb), :], out_sem)

    def zero_copy(blk):
        dst = pl.multiple_of(blk * cb, cb)
        return pltpu.make_async_copy(acc_ref.at[pl.ds(0, cb), :], ys_ref.at[pl.ds(dst, cb), :], out_sem)

    def for_each(lo, hi, fn):
        def step(r, c):
            fn(r)
            return c
        lax.fori_loop(lo, hi, step, 0)

    @pl.when((j == 0) & (sb == 0))
    def _():
        for_each(0, nblk, lambda r: in_copy(0, r).start())

    @pl.when(j == 0)
    def _():
        def stage(r):
            in_copy(sb, r).wait()
            r0 = pl.multiple_of(r * cb, cb)
            xbuf_ref[pl.ds(r0, cb), :] = xin_ref[pl.ds(r0, cb), :].astype(BF16)
            acc_ref[pl.ds(r0, cb), :] = jnp.broadcast_to(bd_ref[...], (cb, D_MODEL))

        for_each(0, nblk, stage)

        @pl.when(sb + 1 < n_sb)
        def _():
            nxt = jnp.minimum(sb + 1, n_sb - 1)
            for_each(0, sbn_ref[nxt], lambda r: in_copy(nxt, r).start())

    @pl.when(nblk > 0)
    def _():
        def cast_rows(s):
            r = pl.multiple_of(s * cb, cb)
            wgu_ref[pl.ds(r, cb), 0:tf] = wg_ref[pl.ds(r, cb), :].astype(BF16)
            wgu_ref[pl.ds(r, cb), tf:2 * tf] = wu_ref[pl.ds(r, cb), :].astype(BF16)

        for_each(0, D_MODEL // cb, cast_rows)
        wdb_ref[...] = wd_ref[...].astype(BF16)
        bias_gu = jnp.concatenate([bg_ref[...], bu_ref[...]], axis=1)

        def sub(r0, rows):
            x = xbuf_ref[pl.ds(r0, rows), :]
            gu = jnp.dot(x, wgu_ref[...], preferred_element_type=F32) + bias_gu
            gate = jnp.minimum(gu[:, :tf], SWIGLU_LIMIT)
            up = jnp.clip(gu[:, tf:], -SWIGLU_LIMIT, SWIGLU_LIMIT)
            glu = gate * _sigmoid(SWIGLU_ALPHA * gate)
            act = ((up + 1.0) * glu).astype(BF16)
            acc_ref[pl.ds(r0, rows), :] += jnp.dot(act, wdb_ref[...], preferred_element_type=F32)

            @pl.when(j == nj - 1)
            def _():
                for b in range(rows // cb):
                    out_copy(r0 // cb + b).start()

        per_sub = bm // cb
        n_full = nblk // per_sub
        for_each(0, n_full, lambda i: sub(pl.multiple_of(i * bm, bm), bm))
        done = n_full * per_sub
        left = nblk - done
        piece = 1
        while piece * 2 < per_sub:
            piece *= 2
        while piece >= 1:
            here = done

            @pl.when((left & piece) != 0)
            def _():
                sub(pl.multiple_of(here * cb, cb), piece * cb)

            done = done + (left & piece)
            piece //= 2

    @pl.when(j == nj - 1)
    def _():
        for_each(0, nblk, lambda r: out_copy(r).wait())

    @pl.when((sb == pl.num_programs(0) - 1) & (j == nj - 1))
    def _():
        acc_ref[0:cb, :] = jnp.zeros((cb, D_MODEL), F32)
        n_blocks = ys_ref.shape[0] // cb
        for_each(used_ref[0], n_blocks, lambda blk: zero_copy(blk).start())
        for_each(used_ref[0], n_blocks, lambda blk: zero_copy(blk).wait())


def _experts(sb_expert, sb_start, sb_nsub, used_blocks, xsorted, w_gu, b_gu, w_down, b_down):
    n_sb = sb_expert.shape[0]
    n_slots = xsorted.shape[0]
    tf = EXPERT_TF
    nj = D_FF // tf

    def spec(shape, fn):
        def index_map(sb, j, sbe, sbs, sbn, used):
            jj = jnp.where(sbn[sb] > 0, j, nj - 1)
            return fn(sbe[sb], jj)
        return pl.BlockSpec(shape, index_map)

    grid_spec = pltpu.PrefetchScalarGridSpec(
        num_scalar_prefetch=4,
        grid=(n_sb, nj),
        in_specs=[
            pl.BlockSpec(memory_space=pl.ANY),
            spec((None, D_MODEL, tf), lambda e, jj: (e, 0, jj)),
            spec((None, D_MODEL, tf), lambda e, jj: (e, 0, nj + jj)),
            spec((None, 1, tf), lambda e, jj: (e, 0, jj)),
            spec((None, 1, tf), lambda e, jj: (e, 0, nj + jj)),
            spec((None, tf, D_MODEL), lambda e, jj: (e, jj, 0)),
            spec((None, 1, D_MODEL), lambda e, jj: (e, 0, 0)),
        ],
        out_specs=pl.BlockSpec(memory_space=pl.ANY),
        scratch_shapes=[
            pltpu.VMEM((EXPERT_R, D_MODEL), F32),
            pltpu.VMEM((EXPERT_R, D_MODEL), BF16),
            pltpu.VMEM((EXPERT_R, D_MODEL), F32),
            pltpu.VMEM((D_MODEL, 2 * tf), BF16),
            pltpu.VMEM((tf, D_MODEL), BF16),
            pltpu.SemaphoreType.DMA((EXPERT_R // EXPERT_PAD,)),
            pltpu.SemaphoreType.DMA(()),
        ],
    )
    return pl.pallas_call(
        _experts_body,
        grid_spec=grid_spec,
        out_shape=jax.ShapeDtypeStruct((n_slots, D_MODEL), F32),
        compiler_params=_cparams(("arbitrary", "arbitrary")),
        name="experts",
    )(sb_expert, sb_start, sb_nsub, used_blocks, xsorted, w_gu, w_gu, b_gu, b_gu, w_down, b_down)


def _combine_body(seg_ref, e_ref, rank_ref, x1_ref, gate_ref, gf_ref, ys_ref, yp_ref, ysm_ref, buf_ref, sem,
                  *, n_prompt_tiles):
    i = pl.program_id(0)
    tm = x1_ref.shape[0]
    n = e_ref.shape[0]

    def issue(blk, c):
        for u in range(DMA_UNROLL):
            a = blk * DMA_UNROLL + u
            pos = seg_ref[e_ref[a]] + rank_ref[a]
            t = lax.shift_right_logical(a, 2)
            pltpu.make_async_copy(ys_ref.at[pl.ds(pos, 1), :], buf_ref.at[u % TOP_K, pl.ds(t, 1), :], sem).start()
        return c

    lax.fori_loop(0, n // DMA_UNROLL, issue, 0)
    for k in range(TOP_K):
        pltpu.make_async_copy(ys_ref.at[pl.ds(0, tm), :], buf_ref.at[k], sem).wait()

    g = gate_ref[...]
    acc = buf_ref[0] * g[:, 0:1]
    for k in range(1, TOP_K):
        acc = acc + buf_ref[k] * g[:, k:k + 1]
    x = x1_ref[...] + acc
    y = x * lax.rsqrt(jnp.mean(x * x, axis=-1, keepdims=True) + EPS) * gf_ref[...]

    @pl.when(i < n_prompt_tiles)
    def _():
        yp_ref[...] = y

    @pl.when(i >= n_prompt_tiles)
    def _():
        ysm_ref[...] = y


def _combine(seg_start, e_flat, rank_flat, x1, gates, g_final, ysorted, t_prompt):
    t = x1.shape[0]
    tm = COMBINE_TM
    npt = t_prompt // tm
    grid_spec = pltpu.PrefetchScalarGridSpec(
        num_scalar_prefetch=1,
        grid=(t // tm,),
        in_specs=[
            pl.BlockSpec((tm * TOP_K,), lambda i, seg: (i,), memory_space=pltpu.SMEM),
            pl.BlockSpec((tm * TOP_K,), lambda i, seg: (i,), memory_space=pltpu.SMEM),
            pl.BlockSpec((tm, D_MODEL), lambda i, seg: (i, 0)),
            pl.BlockSpec((tm, TOP_K), lambda i, seg: (i, 0)),
            pl.BlockSpec((1, D_MODEL), lambda i, seg: (0, 0)),
            pl.BlockSpec(memory_space=pl.ANY),
        ],
        out_specs=[
            pl.BlockSpec((tm, D_MODEL), lambda i, seg: (jnp.minimum(i, npt - 1), 0)),
            pl.BlockSpec((tm, D_MODEL), lambda i, seg: (jnp.maximum(i - npt, 0), 0)),
        ],
        scratch_shapes=[pltpu.VMEM((TOP_K, tm, D_MODEL), F32), pltpu.SemaphoreType.DMA(())],
    )
    return pl.pallas_call(
        functools.partial(_combine_body, n_prompt_tiles=npt),
        grid_spec=grid_spec,
        out_shape=[
            jax.ShapeDtypeStruct((t_prompt, D_MODEL), F32),
            jax.ShapeDtypeStruct((t - t_prompt, D_MODEL), F32),
        ],
        compiler_params=_cparams(("arbitrary",)),
        name="combine",
    )(seg_start, e_flat, rank_flat, x1, gates, g_final, ysorted)


def _expert_tables(counts, n_assign):
    pad = EXPERT_PAD
    padded = ((counts + pad - 1) // pad) * pad
    seg_start = jnp.cumsum(padded) - padded
    n_sb_e = (counts + EXPERT_R - 1) // EXPERT_R
    sb_cum = jnp.cumsum(n_sb_e)
    n_sb = n_assign // EXPERT_R + N_EXPERTS
    s = jnp.arange(n_sb, dtype=I32)
    total = sb_cum[-1]
    s_eff = jnp.minimum(s, total - 1)
    e = jnp.sum((sb_cum[None, :] <= s_eff[:, None]).astype(I32), axis=1)
    k = s_eff - (sb_cum[e] - n_sb_e[e])
    rows = jnp.clip(counts[e] - k * EXPERT_R, 0, EXPERT_R)
    valid = s < total
    nsub = jnp.where(valid, (rows + pad - 1) // pad, 0)
    start = seg_start[e] + k * EXPERT_R
    used_blocks = (jnp.sum(padded) // pad).reshape(1)
    return seg_start.astype(I32), e, start.astype(I32), nsub.astype(I32), used_blocks.astype(I32)


def kernel(x_prompt, x_sample, state_rglru_h, state_rglru_conv, state_ssd, state_ssd_conv, g_mix, w_in, conv_rg_w, conv_rg_b, rg_wa, rg_ba, rg_wi, rg_bi, rg_lambda, conv_ssd_w, conv_ssd_b, ssd_dt_bias, ssd_a_log, ssd_d, ssd_norm_g, w_out, g_ffn, w_router, b_router, w_gu, b_gu, w_down, b_down, g_final):
    depth = g_mix.shape[0]
    assert depth == 1
    bp, lp, _ = x_prompt.shape
    bs, ls, _ = x_sample.shape
    tp, ts = bp * lp, bs * ls
    t = tp + ts
    hp = SSD_HEADS * SSD_HEAD_DIM
    l = 0

    xp = x_prompt.reshape(tp, D_MODEL)
    xs = jnp.transpose(x_sample, (1, 0, 2)).reshape(ts, D_MODEL)
    w_in_t = jnp.transpose(w_in[l])
    row = lambda v: v.reshape(1, -1)
    pad_heads = lambda v: jnp.pad(v.reshape(1, -1), ((0, 0), (0, LANES - SSD_HEADS)))

    hb, dt_raw = _norm_dt(xp, xs, row(g_mix[l]), w_in_t)
    proj = _in_proj(hb, w_in_t)

    rg_params = (conv_rg_w[l], row(conv_rg_b[l]), rg_wa[l], row(rg_ba[l]), rg_wi[l], row(rg_bi[l]),
                 row(rg_lambda[l]))
    d_exp = row(jnp.repeat(ssd_d[l], SSD_HEAD_DIM))
    ssd_params = (conv_ssd_w[l], row(conv_ssd_b[l]), pad_heads(ssd_dt_bias[l]), pad_heads(ssd_a_log[l]),
                  d_exp, row(ssd_norm_g[l]))

    y_rg, rgh_p, rgc_p = _rglru_prompt(proj, bp, lp, *rg_params)
    y_ssd, ssd_p, ssdc_p = _ssd_prompt(proj, dt_raw, bp, lp, *ssd_params)

    assert tp % (bs * ls) == 0 and bs % 8 == 0
    mix_s3, rgh_s, rgc_s, ssd_s, ssdc_s = _mix_sample(
        proj.reshape(t // bs, bs, D_MAIN), dt_raw.reshape(t // bs, bs, LANES), tp, ls, bs,
        state_rglru_h[l], jnp.transpose(state_rglru_conv[l], (1, 0, 2)),
        state_ssd[l].reshape(bs, hp, SSD_STATE), jnp.transpose(state_ssd_conv[l], (1, 0, 2)),
        rg_params, ssd_params)
    mix_s = mix_s3.reshape(ts, 2 * D_MODEL)

    x1 = _out_proj(y_rg, y_ssd, mix_s, w_out[l], xp, xs)

    wr = jnp.pad(w_router[l], ((0, 0), (0, LANES - N_EXPERTS)))
    br = jnp.pad(b_router[l].reshape(1, -1), ((0, 0), (0, LANES - N_EXPERTS)))
    h2, e_idx, gates, rank, counts = _route(x1, row(g_ffn[l]), wr, br)

    n_assign = t * TOP_K
    n_slots = n_assign + N_EXPERTS * EXPERT_PAD
    counts = counts[0, :N_EXPERTS]
    seg_start, sb_expert, sb_start, sb_nsub, used_blocks = _expert_tables(counts, n_assign)
    e_flat = e_idx.reshape(-1)
    rank_flat = rank.reshape(-1)
    xsorted = _scatter(seg_start, counts, e_flat, rank_flat, h2, n_slots)
    ysorted = _experts(sb_expert, sb_start, sb_nsub, used_blocks, xsorted, w_gu[l], b_gu[l].reshape(N_EXPERTS, 1, -1),
                       w_down[l], b_down[l].reshape(N_EXPERTS, 1, -1))
    y_p, y_s = _combine(seg_start, e_flat, rank_flat, x1, gates, row(g_final), ysorted, tp)

    return (y_p.reshape(x_prompt.shape).astype(x_prompt.dtype),
            jnp.transpose(y_s.reshape(ls, bs, D_MODEL), (1, 0, 2)).astype(x_sample.dtype),
            rgh_p.reshape(depth, bp, D_RG),
            rgc_p.reshape(depth, bp, CONV_W - 1, D_RG),
            ssd_p.reshape(depth, bp, SSD_HEADS, SSD_HEAD_DIM, SSD_STATE),
            ssdc_p.reshape(depth, bp, CONV_W - 1, D_XBC),
            rgh_s.reshape(depth, bs, D_RG),
            jnp.transpose(rgc_s, (1, 0, 2)).reshape(depth, bs, CONV_W - 1, D_RG),
            ssd_s.reshape(depth, bs, SSD_HEADS, SSD_HEAD_DIM, SSD_STATE),
            jnp.transpose(ssdc_s, (1, 0, 2)).reshape(depth, bs, CONV_W - 1, D_XBC))
```

```python
import functools

import jax
import jax.numpy as jnp
from jax import lax
from jax.experimental import pallas as pl
from jax.experimental.pallas import tpu as pltpu

F32 = jnp.float32
BF16 = jnp.bfloat16
I32 = jnp.int32
HIGHEST = lax.Precision.HIGHEST

EPS = 1e-6
D_MODEL = 2048
D_RG = 2048
RG_BLOCKS = 16
RG_BLOCK_W = 128
RG_C = 8.0
D_SSD = 2048
SSD_HEAD_DIM = 64
SSD_HEADS = 32
SSD_GROUPS = 4
SSD_STATE = 128
SSD_CHUNK = 128
CONV_W = 4
D_BC = SSD_GROUPS * SSD_STATE
D_XBC = D_SSD + 2 * D_BC
D_MAIN = 2 * D_RG + D_SSD + D_XBC
N_EXPERTS = 32
TOP_K = 4
D_FF = 2048
SWIGLU_ALPHA = 1.702
SWIGLU_LIMIT = 7.0

LANES = 128
VMEM_LIMIT = 56 * 1024 * 1024
PROJ_VMEM_LIMIT = 58 * 1024 * 1024

ROW_TILE = 512
PROJ_TM = 1088
PROJ_TN = 1536
OUT_TN = 512
RG_TL = 512
SAMPLE_BB = 8
ROUTE_TM = 512
SCATTER_TM = 512
COMBINE_TM = 512
EXPERT_PAD = 128
EXPERT_BM = 512
EXPERT_R = 1152
EXPERT_TF = 512
DMA_UNROLL = 8


def _cparams(sem, vmem=VMEM_LIMIT):
    return pltpu.CompilerParams(dimension_semantics=sem, vmem_limit_bytes=vmem)


def _softplus(x):
    return jnp.maximum(x, 0.0) + jnp.log1p(jnp.exp(-jnp.abs(x)))


_sigmoid = jax.nn.sigmoid
_silu = jax.nn.silu


def _gelu_tanh(x):
    return jax.nn.gelu(x, approximate=True)


def _head_expand_matrix(dtype):
    r = lax.broadcasted_iota(I32, (LANES, D_SSD), 0)
    c = lax.broadcasted_iota(I32, (LANES, D_SSD), 1)
    return jnp.where(lax.shift_right_logical(c, 6) == r, 1.0, 0.0).astype(dtype)


def _expand_heads(v, e_b16):
    rows = v.shape[0]
    hi = v.astype(BF16)
    r1 = v - hi.astype(F32)
    mid = r1.astype(BF16)
    lo = (r1 - mid.astype(F32)).astype(BF16)
    o = jnp.dot(jnp.concatenate([hi, mid, lo], axis=0), e_b16, preferred_element_type=F32)
    return (o[0:rows] + o[rows:2 * rows]) + o[2 * rows:3 * rows]


def _norm_dt_body(xp_ref, xs_ref, g_ref, wdt_ref, hb_ref, dt_ref, *, n_prompt_tiles):
    i = pl.program_id(0)

    def run(x_ref):
        x = x_ref[...]
        h = x * lax.rsqrt(jnp.mean(x * x, axis=-1, keepdims=True) + EPS) * g_ref[...]
        hb = h.astype(BF16)
        hb_ref[...] = hb
        r = lax.broadcasted_iota(I32, (LANES, 1), 0)
        wdt = jnp.where(r < SSD_HEADS, wdt_ref[...], 0.0).astype(BF16)
        dt_ref[...] = lax.dot_general(hb, wdt, (((1,), (1,)), ((), ())), preferred_element_type=F32)

    @pl.when(i < n_prompt_tiles)
    def _():
        run(xp_ref)

    @pl.when(i >= n_prompt_tiles)
    def _():
        run(xs_ref)


def _norm_dt(xp, xs, g, w_in_t):
    tp, ts = xp.shape[0], xs.shape[0]
    npt, nst = tp // ROW_TILE, ts // ROW_TILE
    t = tp + ts
    return pl.pallas_call(
        functools.partial(_norm_dt_body, n_prompt_tiles=npt),
        grid=(npt + nst,),
        in_specs=[
            pl.BlockSpec((ROW_TILE, D_MODEL), lambda i: (jnp.minimum(i, npt - 1), 0)),
            pl.BlockSpec((ROW_TILE, D_MODEL), lambda i: (jnp.maximum(i - npt, 0), 0)),
            pl.BlockSpec((1, D_MODEL), lambda i: (0, 0)),
            pl.BlockSpec((LANES, D_MODEL), lambda i: (D_MAIN // LANES, 0)),
        ],
        out_specs=[
            pl.BlockSpec((ROW_TILE, D_MODEL), lambda i: (i, 0)),
            pl.BlockSpec((ROW_TILE, LANES), lambda i: (i, 0)),
        ],
        out_shape=[jax.ShapeDtypeStruct((t, D_MODEL), BF16), jax.ShapeDtypeStruct((t, LANES), F32)],
        compiler_params=_cparams(("parallel",)),
        name="norm_dt",
    )(xp, xs, g, w_in_t)


def _cast_weight_tile(w_ref, wb_ref, rows_per_step=256):
    k = w_ref.shape[0]

    def step(s, c):
        r = pl.multiple_of(s * rows_per_step, rows_per_step)
        wb_ref[pl.ds(r, rows_per_step), :] = w_ref[pl.ds(r, rows_per_step), :].astype(BF16)
        return c

    lax.fori_loop(0, k // rows_per_step, step, 0)


def _in_proj_body(x_ref, w_ref, o_ref, wb_ref):
    @pl.when(pl.program_id(1) == 0)
    def _():
        _cast_weight_tile(w_ref, wb_ref)

    o_ref[...] = lax.dot_general(x_ref[...], wb_ref[...], (((1,), (1,)), ((), ())), preferred_element_type=F32)


def _in_proj(hb, w_in_t):
    t = hb.shape[0]
    return pl.pallas_call(
        _in_proj_body,
        grid=(D_MAIN // PROJ_TN, t // PROJ_TM),
        in_specs=[
            pl.BlockSpec((PROJ_TM, D_MODEL), lambda j, i: (i, 0)),
            pl.BlockSpec((PROJ_TN, D_MODEL), lambda j, i: (j, 0)),
        ],
        out_specs=pl.BlockSpec((PROJ_TM, PROJ_TN), lambda j, i: (i, j)),
        out_shape=jax.ShapeDtypeStruct((t, D_MAIN), F32),
        scratch_shapes=[pltpu.VMEM((PROJ_TN, D_MODEL), BF16)],
        compiler_params=_cparams(("arbitrary", "arbitrary"), vmem=PROJ_VMEM_LIMIT),
        name="in_proj",
    )(hb, w_in_t)


def _out_proj_body(rg_ref, ssd_ref, ms_ref, w_ref, xp_ref, xs_ref, o_ref, wb_ref, *, n_prompt_tiles):
    i = pl.program_id(1)

    @pl.when(i == 0)
    def _():
        _cast_weight_tile(w_ref, wb_ref)

    @pl.when(i < n_prompt_tiles)
    def _():
        m = jnp.concatenate([rg_ref[...], ssd_ref[...]], axis=1)
        o_ref[...] = xp_ref[...] + jnp.dot(m, wb_ref[...], preferred_element_type=F32)

    @pl.when(i >= n_prompt_tiles)
    def _():
        o_ref[...] = xs_ref[...] + jnp.dot(ms_ref[...].astype(BF16), wb_ref[...], preferred_element_type=F32)


def _out_proj(y_rg, y_ssd, mix_s, w_out, xp, xs):
    tp, ts = y_rg.shape[0], mix_s.shape[0]
    npt, nst = tp // ROW_TILE, ts // ROW_TILE
    prompt_rows = lambda j, i: (jnp.minimum(i, npt - 1), 0)
    return pl.pallas_call(
        functools.partial(_out_proj_body, n_prompt_tiles=npt),
        grid=(D_MODEL // OUT_TN, npt + nst),
        in_specs=[
            pl.BlockSpec((ROW_TILE, D_MODEL), prompt_rows),
            pl.BlockSpec((ROW_TILE, D_MODEL), prompt_rows),
            pl.BlockSpec((ROW_TILE, 2 * D_MODEL), lambda j, i: (jnp.maximum(i - npt, 0), 0)),
            pl.BlockSpec((2 * D_MODEL, OUT_TN), lambda j, i: (0, j)),
            pl.BlockSpec((ROW_TILE, OUT_TN), lambda j, i: (jnp.minimum(i, npt - 1), j)),
            pl.BlockSpec((ROW_TILE, OUT_TN), lambda j, i: (jnp.maximum(i - npt, 0), j)),
        ],
        out_specs=pl.BlockSpec((ROW_TILE, OUT_TN), lambda j, i: (i, j)),
        out_shape=jax.ShapeDtypeStruct((tp + ts, D_MODEL), F32),
        scratch_shapes=[pltpu.VMEM((2 * D_MODEL, OUT_TN), BF16)],
        compiler_params=_cparams(("arbitrary", "arbitrary")),
        name="out_proj",
    )(y_rg, y_ssd, mix_s, w_out, xp, xs)


def _rg_gates(xc, wa_ref, ba_ref, wi_ref, bi_ref, lam_ref):
    xcb = xc.astype(BF16)
    rs, is_ = [], []
    for h in range(RG_BLOCKS):
        xh = xcb[:, h * RG_BLOCK_W:(h + 1) * RG_BLOCK_W]
        rs.append(jnp.dot(xh, wa_ref[h].astype(BF16), preferred_element_type=F32))
        is_.append(jnp.dot(xh, wi_ref[h].astype(BF16), preferred_element_type=F32))
    r = _sigmoid(jnp.concatenate(rs, axis=1) + ba_ref[...])
    ig = _sigmoid(jnp.concatenate(is_, axis=1) + bi_ref[...])
    log_a = (-RG_C * r) * _softplus(-lam_ref[...])
    a = jnp.exp(log_a)
    u = jnp.sqrt(-jnp.tanh(log_a) * (a * a + 1.0)) * (ig * xc)
    return a, u


def _rglru_prompt_body(x_ref, gate_ref, cw_ref, cb_ref, wa_ref, ba_ref, wi_ref, bi_ref, lam_ref,
                       y_ref, h_ref, cs_ref, ext_ref, a_ref, u_ref, car_ref):
    c = pl.program_id(1)
    tl = x_ref.shape[0]

    @pl.when(c == 0)
    def _():
        ext_ref[0:8, :] = jnp.zeros((8, D_RG), F32)
        car_ref[...] = jnp.zeros((1, D_RG), F32)

    ext_ref[8:8 + tl, :] = x_ref[...]
    xc = cb_ref[...] + ext_ref[5:5 + tl, :] * cw_ref[0:1, :]
    xc = xc + ext_ref[6:6 + tl, :] * cw_ref[1:2, :]
    xc = xc + ext_ref[7:7 + tl, :] * cw_ref[2:3, :]
    xc = xc + ext_ref[8:8 + tl, :] * cw_ref[3:4, :]
    tail = ext_ref[tl:tl + 8, :]
    ext_ref[0:8, :] = tail

    a, u = _rg_gates(xc, wa_ref, ba_ref, wi_ref, bi_ref, lam_ref)
    a_ref[...] = a
    u_ref[...] = u

    row = lax.broadcasted_iota(I32, (8, D_RG), 0)

    def group(g, carry):
        r0 = pl.multiple_of(g * 8, 8)
        a8 = a_ref[pl.ds(r0, 8), :]
        u8 = u_ref[pl.ds(r0, 8), :]
        for s in (1, 2, 4):
            keep = row >= s
            a_sh = jnp.where(keep, pltpu.roll(a8, s, 0), 1.0)
            u_sh = jnp.where(keep, pltpu.roll(u8, s, 0), 0.0)
            u8 = a8 * u_sh + u8
            a8 = a8 * a_sh
        h8 = a8 * carry + u8
        u_ref[pl.ds(r0, 8), :] = h8
        return h8[7:8, :]

    carry = lax.fori_loop(0, tl // 8, group, car_ref[...])
    car_ref[...] = carry
    y_ref[...] = (u_ref[...] * _gelu_tanh(gate_ref[...])).astype(BF16)

    @pl.when(c == pl.num_programs(1) - 1)
    def _():
        h_ref[...] = carry
        cs_ref[...] = tail[5:8, :]


def _rglru_prompt(proj, batch, seq, conv_w, conv_b, wa, ba, wi, bi, lam):
    nc = seq // RG_TL
    t = batch * seq
    vec =pl.BlockSpec((1, D_RG), lambda b, c: (0, 0))
    blk = pl.BlockSpec((RG_BLOCKS, RG_BLOCK_W, RG_BLOCK_W), lambda b, c: (0, 0, 0))
    return pl.pallas_call(
        _rglru_prompt_body,
        grid=(batch, nc),
        in_specs=[
            pl.BlockSpec((RG_TL, D_RG), lambda b, c: (b * nc + c, 0)),
            pl.BlockSpec((RG_TL, D_RG), lambda b, c: (b * nc + c, 1)),
            pl.BlockSpec((CONV_W, D_RG), lambda b, c: (0, 0)),
            vec, blk, vec, blk, vec, vec,
        ],
        out_specs=[
            pl.BlockSpec((RG_TL, D_RG), lambda b, c: (b * nc + c, 0)),
            pl.BlockSpec((None, 1, D_RG), lambda b, c: (b, 0, 0)),
            pl.BlockSpec((None, CONV_W - 1, D_RG), lambda b, c: (b, 0, 0)),
        ],
        out_shape=[
            jax.ShapeDtypeStruct((t, D_RG), BF16),
            jax.ShapeDtypeStruct((batch, 1, D_RG), F32),
            jax.ShapeDtypeStruct((batch, CONV_W - 1, D_RG), F32),
        ],
        scratch_shapes=[
            pltpu.VMEM((RG_TL + 8, D_RG), F32),
            pltpu.VMEM((RG_TL, D_RG), F32),
            pltpu.VMEM((RG_TL, D_RG), F32),
            pltpu.VMEM((1, D_RG), F32),
        ],
        compiler_params=_cparams(("arbitrary", "arbitrary")),
        name="rglru_prompt",
    )(proj, proj, conv_w, conv_b, wa, ba, wi, bi, lam)


def _group_rmsnorm(y, g_row):
    outs = []
    for g in range(SSD_GROUPS):
        yg = y[:, g * D_BC:(g + 1) * D_BC]
        outs.append(yg * lax.rsqrt(jnp.mean(yg * yg, axis=-1, keepdims=True) + EPS))
    return jnp.concatenate(outs, axis=1) * g_row


def _ssd_prompt_body(xbc_ref, z_ref, dt_ref, cw_ref, cb_ref, dtb_ref, alog_ref, dexp_ref, ng_ref,
                     y_ref, s_out_ref, cs_ref, ext_ref, s_ref):
    c = pl.program_id(1)
    L = SSD_CHUNK

    @pl.when(c == 0)
    def _():
        ext_ref[0:8, :] = jnp.zeros((8, D_XBC), F32)
        s_ref[...] = jnp.zeros_like(s_ref)

    ext_ref[8:8 + L, :] = xbc_ref[...]
    xc = cb_ref[...] + ext_ref[5:5 + L, :] * cw_ref[0:1, :]
    xc = xc + ext_ref[6:6 + L, :] * cw_ref[1:2, :]
    xc = xc + ext_ref[7:7 + L, :] * cw_ref[2:3, :]
    xc = xc + ext_ref[8:8 + L, :] * cw_ref[3:4, :]
    tail = ext_ref[L:L + 8, :]
    ext_ref[0:8, :] = tail
    xc = _silu(xc)
    xs = xc[:, :D_SSD]
    bm = xc[:, D_SSD:D_SSD + D_BC].astype(BF16)
    cm = xc[:, D_SSD + D_BC:].astype(BF16)

    lane = lax.broadcasted_iota(I32, (1, LANES), 1)
    a_neg = jnp.where(lane < SSD_HEADS, -jnp.exp(alog_ref[...]), 0.0)
    dt = _softplus(dt_ref[...] + dtb_ref[...])
    da = dt * a_neg
    ri = lax.broadcasted_iota(I32, (L, L), 0)
    ci = lax.broadcasted_iota(I32, (L, L), 1)
    causal = ci <= ri
    tri = jnp.where(causal, 1.0, 0.0).astype(F32)
    acum = jnp.dot(tri, da, precision=HIGHEST, preferred_element_type=F32)
    acum_t = acum.T
    a_last = acum[L - 1:L, :]

    e_b16 = _head_expand_matrix(BF16)
    stacked = jnp.concatenate([dt, jnp.exp(a_last - acum), jnp.exp(acum)], axis=0)
    expd = _expand_heads(stacked, e_b16)
    dt_e, dend_e, ea_e = expd[0:L], expd[L:2 * L], expd[2 * L:3 * L]
    xdt = xs * dt_e
    xdt_b = xdt.astype(BF16)
    w_b = (xdt * dend_e)

    lane_l = lax.broadcasted_iota(I32, (L, LANES), 1)
    lo = lane_l < SSD_HEAD_DIM
    y_parts = []
    for g in range(SSD_GROUPS):
        cg = cm[:, g * SSD_STATE:(g + 1) * SSD_STATE]
        bg = bm[:, g * SSD_STATE:(g + 1) * SSD_STATE]
        cb = lax.dot_general(cg, bg, (((1,), (1,)), ((), ())), preferred_element_type=F32)
        hpg = SSD_HEADS // SSD_GROUPS
        for jp in range(hpg // 2):
            ms = []
            for h in (g * hpg + 2 * jp, g * hpg + 2 * jp + 1):
                seg = acum[:, h:h + 1] - acum_t[h:h + 1, :]
                decay = jnp.exp(jnp.where(causal, seg, -jnp.inf))
                ms.append((cb * decay).astype(BF16))
            col = (g * hpg + 2 * jp) * SSD_HEAD_DIM
            xp = xdt_b[:, col:col + LANES]
            zero = jnp.zeros_like(xp)
            rhs = jnp.concatenate([jnp.where(lo, xp, zero), jnp.where(lo, zero, xp)], axis=0)
            y_parts.append(jnp.dot(jnp.concatenate(ms, axis=1), rhs, preferred_element_type=F32))
    y_diag = jnp.concatenate(y_parts, axis=1)

    y_off_parts = []
    for g in range(SSD_GROUPS):
        cg = cm[:, g * SSD_STATE:(g + 1) * SSD_STATE]
        sg = s_ref[g * D_BC:(g + 1) * D_BC, :].astype(BF16)
        y_off_parts.append(lax.dot_general(cg, sg, (((1,), (1,)), ((), ())), preferred_element_type=F32))
    y_off = jnp.concatenate(y_off_parts, axis=1) * ea_e

    cd_col = jnp.exp(acum_t[:, L - 1:L])
    for g in range(SSD_GROUPS):
        bg = bm[:, g * SSD_STATE:(g + 1) * SSD_STATE]
        wg_t = w_b[:, g * D_BC:(g + 1) * D_BC].T.astype(BF16)
        upd = jnp.dot(wg_t, bg, preferred_element_type=F32)
        hpg = SSD_HEADS // SSD_GROUPS
        for e in range(hpg):
            h = g * hpg + e
            r0 = h * SSD_HEAD_DIM
            cd = jnp.broadcast_to(cd_col[h:h + 1, :], (SSD_HEAD_DIM, SSD_STATE))
            s_ref[r0:r0 + SSD_HEAD_DIM, :] = (cd * s_ref[r0:r0 + SSD_HEAD_DIM, :]
                                              + upd[e * SSD_HEAD_DIM:(e + 1) * SSD_HEAD_DIM, :])

    y = (y_diag + y_off + dexp_ref[...] * xs) * _silu(z_ref[...])
    y_ref[...] = _group_rmsnorm(y, ng_ref[...]).astype(BF16)

    @pl.when(c == pl.num_programs(1) - 1)
    def _():
        s_out_ref[...] = s_ref[...]
        cs_ref[...] = tail[5:8, :]


def _ssd_prompt(proj, dt_raw, batch, seq, conv_w, conv_b, dt_bias, a_log, d_exp, norm_g):
    nc = seq // SSD_CHUNK
    L = SSD_CHUNK
    hp = SSD_HEADS * SSD_HEAD_DIM
    const = lambda b, c: (0, 0)
    return pl.pallas_call(
        _ssd_prompt_body,
        grid=(batch, nc),
        in_specs=[
            pl.BlockSpec((L, D_XBC), lambda b, c: (b * nc + c, 2)),
            pl.BlockSpec((L, D_SSD), lambda b, c: (b * nc + c, 2)),
            pl.BlockSpec((L, LANES), lambda b, c: (b * nc + c, 0)),
            pl.BlockSpec((CONV_W, D_XBC), const),
            pl.BlockSpec((1, D_XBC), const),
            pl.BlockSpec((1, LANES), const),
            pl.BlockSpec((1, LANES), const),
            pl.BlockSpec((1, D_SSD), const),
            pl.BlockSpec((1, D_SSD), const),
        ],
        out_specs=[
            pl.BlockSpec((L, D_SSD), lambda b, c: (b * nc + c, 0)),
            pl.BlockSpec((None, hp, SSD_STATE), lambda b, c: (b, 0, 0)),
            pl.BlockSpec((None, CONV_W - 1, D_XBC), lambda b, c: (b, 0, 0)),
        ],
        out_shape=[
            jax.ShapeDtypeStruct((batch * seq, D_SSD), BF16),
            jax.ShapeDtypeStruct((batch, hp, SSD_STATE), F32),
            jax.ShapeDtypeStruct((batch, CONV_W - 1, D_XBC), F32),
        ],
        scratch_shapes=[
            pltpu.VMEM((L + 8, D_XBC), F32),
            pltpu.VMEM((hp, SSD_STATE), F32),
        ],
        compiler_params=_cparams(("arbitrary", "arbitrary")),
        name="ssd_prompt",
    )(proj, proj, dt_raw, conv_w, conv_b, dt_bias, a_log, d_exp, norm_g)


def _mix_sample_body(p_ref, dt_ref, h0_ref, rgc0_ref, s0_ref, sc0_ref,
                     rcw_ref, rcb_ref, wa_ref, ba_ref, wi_ref, bi_ref, lam_ref,
                     scw_ref, scb_ref, dtb_ref, alog_ref, dexp_ref, ng_ref,
                     y_ref, h_ref, rgc_ref, s_ref, sc_ref):
    nl = p_ref.shape[0]
    bb = p_ref.shape[1]
    o_gate, o_z, o_xbc = D_RG, 2 * D_RG, 2 * D_RG + D_SSD

    def col(l, off, width):
        return p_ref[l, :, off:off + width]

    hist = [rgc0_ref[k] for k in range(CONV_W - 1)]
    raw = hist + [col(l, 0, D_RG) for l in range(nl)]
    xcs = []
    for l in range(nl):
        acc = rcb_ref[...] + raw[l] * rcw_ref[0:1, :]
        for k in range(1, CONV_W):
            acc = acc + raw[l + k] * rcw_ref[k:k + 1, :]
        xcs.append(acc)
    a, u = _rg_gates(jnp.concatenate(xcs, axis=0), wa_ref, ba_ref, wi_ref, bi_ref, lam_ref)
    h = h0_ref[...]
    for l in range(nl):
        h = a[l * bb:(l + 1) * bb] * h + u[l * bb:(l + 1) * bb]
        y_ref[l, :, 0:D_RG] = h * _gelu_tanh(col(l, o_gate, D_RG))
    h_ref[...] = h
    for k in range(CONV_W - 1):
        rgc_ref[k] = raw[nl + k]

    hist = [sc0_ref[k] for k in range(CONV_W - 1)]
    raw = hist + [col(l, o_xbc, D_XBC) for l in range(nl)]
    for k in range(CONV_W - 1):
        sc_ref[k] = raw[nl + k]
    xcs = []
    for l in range(nl):
        acc = scb_ref[...] + raw[l] * scw_ref[0:1, :]
        for k in range(1, CONV_W):
            acc = acc + raw[l + k] * scw_ref[k:k + 1, :]
        xcs.append(_silu(acc))
    xs = [x[:, :D_SSD] for x in xcs]
    bms = [x[:, D_SSD:D_SSD + D_BC].astype(BF16) for x in xcs]
    cms = [x[:, D_SSD + D_BC:].astype(BF16) for x in xcs]

    lane = lax.broadcasted_iota(I32, (1, LANES), 1)
    a_neg = jnp.where(lane < SSD_HEADS, -jnp.exp(alog_ref[...]), 0.0)
    dts = [_softplus(dt_ref[l] + dtb_ref[...]) for l in range(nl)]
    acums = []
    run = jnp.zeros((bb, LANES), F32)
    for l in range(nl):
        run = run + dts[l] * a_neg
        acums.append(run)
    a_last = acums[nl - 1]

    e_b16 = _head_expand_matrix(BF16)
    stacked = jnp.concatenate(dts + [jnp.exp(a_last - ac) for ac in acums] + [jnp.exp(ac) for ac in acums], axis=0)
    expd = _expand_heads(stacked, e_b16)
    dt_e = [expd[l * bb:(l + 1) * bb] for l in range(nl)]
    dend_e = [expd[(nl + l) * bb:(nl + l + 1) * bb] for l in range(nl)]
    ea_e = [expd[(2 * nl + l) * bb:(2 * nl + l + 1) * bb] for l in range(nl)]
    xdt = [xs[l] * dt_e[l] for l in range(nl)]
    xdt_r = [x.astype(BF16).astype(F32) for x in xdt]
    rows = nl * bb
    pad_rows = LANES - rows
    w_all = jnp.concatenate([(xdt[l] * dend_e[l]) for l in range(nl)]
                            + [jnp.zeros((pad_rows, D_SSD), F32)], axis=0)
    c_all = jnp.concatenate(cms, axis=0)
    b_all = jnp.concatenate(bms + [jnp.zeros((pad_rows, D_BC), BF16)], axis=0)

    r = lax.broadcasted_iota(I32, (D_BC, LANES), 0)
    cidx = lax.broadcasted_iota(I32, (D_BC, LANES), 1)
    gsum = jnp.where((lax.shift_right_logical(r, 7) == lax.shift_right_logical(cidx, 3)) & (cidx < SSD_HEADS),
                     1.0, 0.0).astype(F32)
    pairs = [(l, s) for l in range(nl) for s in range(l + 1)]
    prods = jnp.concatenate([cms[l].astype(F32) * bms[s].astype(F32) for (l, s) in pairs], axis=0)
    cbh = jnp.dot(prods, gsum, precision=HIGHEST, preferred_element_type=F32)
    m_list = []
    for idx, (l, s) in enumerate(pairs):
        decay = jnp.exp(acums[l] - acums[s])
        m_list.append((cbh[idx * bb:(idx + 1) * bb] * decay).astype(BF16))
    m_e = jnp.dot(jnp.concatenate(m_list, axis=0), e_b16, preferred_element_type=F32)
    y_diag = []
    for l in range(nl):
        acc = None
        for idx, (ll, s) in enumerate(pairs):
            if ll != l:
                continue
            term = m_e[idx * bb:(idx + 1) * bb] * xdt_r[s]
            acc = term if acc is None else acc + term
        y_diag.append(acc)

    cd_t = jnp.concatenate([jnp.exp(a_last), jnp.zeros((LANES - bb, LANES), F32)], axis=0).T
    row_seq = lax.broadcasted_iota(I32, (rows, 1), 0) & (bb - 1)
    row_pad = lax.broadcasted_iota(I32, (LANES, 1), 0)
    lane_q = lax.broadcasted_iota(I32, (LANES, LANES), 1)
    hpg = SSD_HEADS // SSD_GROUPS

    def seq_step(q, y_off):
        mine = row_seq == q
        mine_pad = ((row_pad & (bb - 1)) == q) & (row_pad < rows)
        cd_q = jnp.sum(jnp.where(lane_q == q, cd_t, 0.0), axis=1, keepdims=True)
        cd_q = jnp.broadcast_to(cd_q, (LANES, SSD_STATE))
        parts = []
        for g in range(SSD_GROUPS):
            s0 = s0_ref[q, g * D_BC:(g + 1) * D_BC, :]
            cg = c_all[:, g * SSD_STATE:(g + 1) * SSD_STATE]
            yq = lax.dot_general(cg, s0.astype(BF16), (((1,), (1,)), ((), ())), preferred_element_type=F32)
            parts.append(yq)
            bg = b_all[:, g * SSD_STATE:(g + 1) * SSD_STATE]
            wq = jnp.where(mine_pad, w_all[:, g * D_BC:(g + 1) * D_BC], 0.0)
            upd = jnp.dot(wq.T.astype(BF16), bg, preferred_element_type=F32)
            for e in range(hpg):
                hh = g * hpg + e
                cd = jnp.broadcast_to(cd_q[hh:hh + 1, :], (SSD_HEAD_DIM, SSD_STATE))
                s_ref[q, hh * SSD_HEAD_DIM:(hh + 1) * SSD_HEAD_DIM, :] = (
                    cd * s0[e * SSD_HEAD_DIM:(e + 1) * SSD_HEAD_DIM, :]
                    + upd[e * SSD_HEAD_DIM:(e + 1) * SSD_HEAD_DIM, :])
        yq_all = jnp.concatenate(parts, axis=1)
        return jnp.where(mine, yq_all, y_off)

    y_off = lax.fori_loop(0, bb, seq_step, jnp.zeros((rows, D_SSD), F32))

    for l in range(nl):
        y = (y_diag[l] + y_off[l * bb:(l + 1) * bb] * ea_e[l] + dexp_ref[...] * xs[l]) * _silu(col(l, o_z, D_SSD))
        y_ref[l, :, D_RG:2 * D_MODEL] = _group_rmsnorm(y, ng_ref[...])


def _mix_sample(proj3, dt3, row0, nl, n_seq, h0, rgc0, s0, sc0, rg_params, ssd_params):
    bb = SAMPLE_BB
    hp = SSD_HEADS * SSD_HEAD_DIM
    blk0 = row0 // (n_seq * nl)
    const2 = lambda i: (0, 0)
    rg_specs = [
        pl.BlockSpec((CONV_W, D_RG), const2), pl.BlockSpec((1, D_RG), const2),
        pl.BlockSpec((RG_BLOCKS, RG_BLOCK_W, RG_BLOCK_W), lambda i: (0, 0, 0)), pl.BlockSpec((1, D_RG), const2),
        pl.BlockSpec((RG_BLOCKS, RG_BLOCK_W, RG_BLOCK_W), lambda i: (0, 0, 0)), pl.BlockSpec((1, D_RG), const2),
        pl.BlockSpec((1, D_RG), const2),
    ]
    ssd_specs = [
        pl.BlockSpec((CONV_W, D_XBC), const2), pl.BlockSpec((1, D_XBC), const2),
        pl.BlockSpec((1, LANES), const2), pl.BlockSpec((1, LANES), const2),
        pl.BlockSpec((1, D_SSD), const2), pl.BlockSpec((1, D_SSD), const2),
    ]
    return pl.pallas_call(
        _mix_sample_body,
        grid=(n_seq // bb,),
        in_specs=[
            pl.BlockSpec((nl, bb, D_MAIN), lambda i: (blk0, i, 0)),
            pl.BlockSpec((nl, bb, LANES), lambda i: (blk0, i, 0)),
            pl.BlockSpec((bb, D_RG), lambda i: (i, 0)),
            pl.BlockSpec((CONV_W - 1, bb, D_RG), lambda i: (0, i, 0)),
            pl.BlockSpec((bb, hp, SSD_STATE), lambda i: (i, 0, 0)),
            pl.BlockSpec((CONV_W - 1, bb, D_XBC), lambda i: (0, i, 0)),
        ] + rg_specs + ssd_specs,
        out_specs=[
            pl.BlockSpec((nl, bb, 2 * D_MODEL), lambda i: (0, i, 0)),
            pl.BlockSpec((bb, D_RG), lambda i: (i, 0)),
            pl.BlockSpec((CONV_W - 1, bb, D_RG), lambda i: (0, i, 0)),
            pl.BlockSpec((bb, hp, SSD_STATE), lambda i: (i, 0, 0)),
            pl.BlockSpec((CONV_W - 1, bb, D_XBC), lambda i: (0, i, 0)),
        ],
        out_shape=[
            jax.ShapeDtypeStruct((nl, n_seq, 2 * D_MODEL), F32),
            jax.ShapeDtypeStruct((n_seq, D_RG), F32),
            jax.ShapeDtypeStruct((CONV_W - 1, n_seq, D_RG), F32),
            jax.ShapeDtypeStruct((n_seq, hp, SSD_STATE), F32),
            jax.ShapeDtypeStruct((CONV_W - 1, n_seq, D_XBC), F32),
        ],
        compiler_params=_cparams(("arbitrary",)),
        name="mix_sample",
    )(proj3, dt3, h0, rgc0, s0, sc0, *rg_params, *ssd_params)


def _route_body(x_ref, g_ref, wr_ref, br_ref, hp_ref, e_ref, gate_ref, rank_ref, cnt_ref, run_ref):
    i = pl.program_id(0)
    tm = x_ref.shape[0]

    @pl.when(i == 0)
    def _():
        run_ref[...] = jnp.zeros_like(run_ref)

    x = x_ref[...]
    h = x * lax.rsqrt(jnp.mean(x * x, axis=-1, keepdims=True) + EPS) * g_ref[...]
    hp_ref[...] = h

    lane = lax.broadcasted_iota(I32, (tm, LANES), 1).astype(F32)
    h_hi = h.astype(BF16)
    h_lo = (h - h_hi.astype(F32)).astype(BF16)
    w = wr_ref[...]
    w_hi = w.astype(BF16)
    w_lo = (w - w_hi.astype(F32)).astype(BF16)
    logits = (jnp.dot(h_hi, w_hi, preferred_element_type=F32)
              + (jnp.dot(h_hi, w_lo, preferred_element_type=F32) + jnp.dot(h_lo, w_hi, preferred_element_type=F32))
              + br_ref[...])
    work = jnp.where(lane < N_EXPERTS, logits, -jnp.inf)
    vals, idxs = [], []
    multi = jnp.zeros((tm, LANES), F32)
    for _ in range(TOP_K):
        m = jnp.max(work, axis=-1, keepdims=True)
        idx = jnp.min(jnp.where(work == m, lane, float(LANES)), axis=-1, keepdims=True)
        hit = lane == idx
        vals.append(m)
        idxs.append(idx)
        multi = jnp.where(hit, 1.0, multi)
        work = jnp.where(hit, -jnp.inf, work)
    ex = [jnp.exp(v - vals[0]) for v in vals]
    den = ex[0] + ex[1] + ex[2] + ex[3]
    gates = [e / den for e in ex]

    ri = lax.broadcasted_iota(I32, (tm, tm), 0)
    ci = lax.broadcasted_iota(I32, (tm, tm), 1)
    strict = jnp.where(ci < ri, 1.0, 0.0).astype(BF16)
    before = jnp.dot(strict, multi.astype(BF16), preferred_element_type=F32) + run_ref[...]
    ranks = [jnp.sum(jnp.where(lane == idx, before, 0.0), axis=-1, keepdims=True) for idx in idxs]
    run_ref[...] = run_ref[...] + jnp.sum(multi, axis=0, keepdims=True)

    lane4 = lax.broadcasted_iota(I32, (tm, TOP_K), 1)
    e_out = jnp.zeros((tm, TOP_K), I32)
    g_out = jnp.zeros((tm, TOP_K), F32)
    r_out = jnp.zeros((tm, TOP_K), I32)
    for k in range(TOP_K):
        e_out = jnp.where(lane4 == k, idxs[k].astype(I32), e_out)
        g_out = jnp.where(lane4 == k, gates[k], g_out)
        r_out = jnp.where(lane4 == k, ranks[k].astype(I32), r_out)
    e_ref[...] = e_out
    gate_ref[...] = g_out
    rank_ref[...] = r_out
    cnt_ref[...] = run_ref[...].astype(I32)


def _route(x1, g_ffn, w_router_pad, b_router_pad):
    t = x1.shape[0]
    tm = ROUTE_TM
    const = lambda i: (0, 0)
    return pl.pallas_call(
        _route_body,
        grid=(t // tm,),
        in_specs=[
            pl.BlockSpec((tm, D_MODEL), lambda i: (i, 0)),
            pl.BlockSpec((1, D_MODEL), const),
            pl.BlockSpec((D_MODEL, LANES), const),
            pl.BlockSpec((1, LANES), const),
        ],
        out_specs=[
            pl.BlockSpec((tm, D_MODEL), lambda i: (i, 0)),
            pl.BlockSpec((tm, TOP_K), lambda i: (i, 0)),
            pl.BlockSpec((tm, TOP_K), lambda i: (i, 0)),
            pl.BlockSpec((tm, TOP_K), lambda i: (i, 0)),
            pl.BlockSpec((1, LANES), const),
        ],
        out_shape=[
            jax.ShapeDtypeStruct((t, D_MODEL), F32),
            jax.ShapeDtypeStruct((t, TOP_K), I32),
            jax.ShapeDtypeStruct((t, TOP_K), F32),
            jax.ShapeDtypeStruct((t, TOP_K), I32),
            jax.ShapeDtypeStruct((1, LANES), I32),
        ],
        scratch_shapes=[pltpu.VMEM((1, LANES), F32)],
        compiler_params=_cparams(("arbitrary",)),
        name="route",
    )(x1, g_ffn, w_router_pad, b_router_pad)


def _scatter_body(seg_ref, cnt_ref, pos_ref, h_ref, xs_ref, zero_ref, sem, zsem):
    i = pl.program_id(0)
    tm = h_ref.shape[0]
    n = pos_ref.shape[0]

    def issue(blk, c):
        for u in range(DMA_UNROLL):
            a = blk * DMA_UNROLL + u
            t = lax.shift_right_logical(a, 2)
            pltpu.make_async_copy(h_ref.at[pl.ds(t, 1), :], xs_ref.at[pl.ds(pos_ref[a], 1), :], sem).start()
        return c

    lax.fori_loop(0, n // DMA_UNROLL, issue, 0)
    for _ in range(TOP_K):
        pltpu.make_async_copy(h_ref, xs_ref.at[pl.ds(0, tm), :], sem).wait()

    @pl.when(i == pl.num_programs(0) - 1)
    def _():
        zero_ref[...] = jnp.zeros_like(zero_ref)
        pad = EXPERT_PAD

        def row_fill(row):
            return pltpu.make_async_copy(zero_ref.at[pl.ds(0, 1), :], xs_ref.at[pl.ds(row, 1), :], zsem)

        def block_fill(blk):
            dst = pl.multiple_of(blk * pad, pad)
            return pltpu.make_async_copy(zero_ref, xs_ref.at[pl.ds(dst, pad), :], zsem)

        def for_each(lo, hi, fn):
            def step(r, c):
                fn(r)
                return c
            lax.fori_loop(lo, hi, step, 0)

        def per_expert(e, used):
            cnt = cnt_ref[e]
            hi = ((cnt + pad - 1) // pad) * pad
            base = seg_ref[e]
            for_each(cnt, hi, lambda r: row_fill(base + r).start())
            for_each(cnt, hi, lambda r: row_fill(base + r).wait())
            return used + hi // pad

        used = lax.fori_loop(0, N_EXPERTS, per_expert, jnp.int32(0))
        n_blocks = xs_ref.shape[0] // pad
        for_each(used, n_blocks, lambda blk: block_fill(blk).start())
        for_each(used, n_blocks, lambda blk: block_fill(blk).wait())


def _scatter(seg_start, counts, pos_flat, h2, n_slots):
    t = h2.shape[0]
    tm = SCATTER_TM
    grid_spec = pltpu.PrefetchScalarGridSpec(
        num_scalar_prefetch=2,
        grid=(t // tm,),
        in_specs=[
            pl.BlockSpec((tm * TOP_K,), lambda i, seg, cnt: (i,), memory_space=pltpu.SMEM),
            pl.BlockSpec((tm, D_MODEL), lambda i, seg, cnt: (i, 0)),
        ],
        out_specs=pl.BlockSpec(memory_space=pl.ANY),
        scratch_shapes=[pltpu.VMEM((EXPERT_PAD, D_MODEL), F32), pltpu.SemaphoreType.DMA(()),
                        pltpu.SemaphoreType.DMA(())],
    )
    return pl.pallas_call(
        _scatter_body,
        grid_spec=grid_spec,
        out_shape=jax.ShapeDtypeStruct((n_slots, D_MODEL), F32),
        compiler_params=_cparams(("arbitrary",)),
        name="scatter_rows",
    )(seg_start, counts, pos_flat, h2)


def _experts_body(sbe_ref, sbs_ref, sbn_ref, used_ref, xs_ref, wg_ref, wu_ref, bg_ref, bu_ref, wd_ref, bd_ref,
                  ys_ref, xin_ref, xbuf_ref, acc_ref, wgu_ref, wdb_ref, in_sem, out_sem):
    del sbe_ref
    sb = pl.program_id(0)
    j = pl.program_id(1)
    nj = pl.num_programs(1)
    n_sb = pl.num_programs(0)
    nblk = sbn_ref[sb]
    start = sbs_ref[sb]
    cb = EXPERT_PAD
    bm = EXPERT_BM
    tf = wd_ref.shape[0]

    def in_copy(which, r):
        src = pl.multiple_of(sbs_ref[which] + r * cb, cb)
        dst = pl.multiple_of(r * cb, cb)
        return pltpu.make_async_copy(xs_ref.at[pl.ds(src, cb), :], xin_ref.at[pl.ds(dst, cb), :], in_sem.at[r])

    def out_copy(r):
        src = pl.multiple_of(r * cb, cb)
        dst = pl.multiple_of(start + r * cb, cb)
        return pltpu.make_async_copy(acc_ref.at[pl.ds(src, cb), :], ys_ref.at[pl.ds(dst, cb), :], out_sem)

    def zero_copy(blk):
        dst = pl.multiple_of(blk * cb, cb)
        return pltpu.make_async_copy(acc_ref.at[pl.ds(0, cb), :], ys_ref.at[pl.ds(dst, cb), :], out_sem)

    def for_each(lo, hi, fn):
        def step(r, c):
            fn(r)
            return c
        lax.fori_loop(lo, hi, step, 0)

    @pl.when((j == 0) & (sb == 0))
    def _():
        for_each(0, nblk, lambda r: in_copy(0, r).start())

    @pl.when(j == 0)
    def _():
        def stage(r):
            in_copy(sb, r).wait()
            r0 = pl.multiple_of(r * cb, cb)
            xbuf_ref[pl.ds(r0, cb), :] = xin_ref[pl.ds(r0, cb), :].astype(BF16)
            acc_ref[pl.ds(r0, cb), :] = jnp.broadcast_to(bd_ref[...], (cb, D_MODEL))

        for_each(0, nblk, stage)

        @pl.when(sb + 1 < n_sb)
        def _():
            nxt = jnp.minimum(sb + 1, n_sb - 1)
            for_each(0, sbn_ref[nxt], lambda r: in_copy(nxt, r).start())

    @pl.when(nblk > 0)
    def _():
        def cast_rows(s):
            r = pl.multiple_of(s * cb, cb)
            wgu_ref[pl.ds(r, cb), 0:tf] = wg_ref[pl.ds(r, cb), :].astype(BF16)
            wgu_ref[pl.ds(r, cb), tf:2 * tf] = wu_ref[pl.ds(r, cb), :].astype(BF16)

        for_each(0, D_MODEL // cb, cast_rows)
        wdb_ref[...] = wd_ref[...].astype(BF16)
        bias_gu = jnp.concatenate([bg_ref[...], bu_ref[...]], axis=1)

        def sub(r0, rows):
            x = xbuf_ref[pl.ds(r0, rows), :]
            gu = jnp.dot(x, wgu_ref[...], preferred_element_type=F32) + bias_gu
            gate = jnp.minimum(gu[:, :tf], SWIGLU_LIMIT)
            up = jnp.clip(gu[:, tf:], -SWIGLU_LIMIT, SWIGLU_LIMIT)
            glu = gate * _sigmoid(SWIGLU_ALPHA * gate)
            act = ((up + 1.0) * glu).astype(BF16)
            acc_ref[pl.ds(r0, rows), :] += jnp.dot(act, wdb_ref[...], preferred_element_type=F32)

            @pl.when(j == nj - 1)
            def _():
                for b in range(rows // cb):
                    out_copy(r0 // cb + b).start()

        per_sub = bm // cb
        n_full = nblk // per_sub
        for_each(0, n_full, lambda i: sub(pl.multiple_of(i * bm, bm), bm))
        done = n_full * per_sub
        left = nblk - done
        piece = 1
        while piece * 2 < per_sub:
            piece *= 2
        while piece >= 1:
            here = done

            @pl.when((left & piece) != 0)
            def _():
                sub(pl.multiple_of(here * cb, cb), piece * cb)

            done = done + (left & piece)
            piece //= 2

    @pl.when(j == nj - 1)
    def _():
        for_each(0, nblk, lambda r: out_copy(r).wait())

    @pl.when((sb == pl.num_programs(0) - 1) & (j == nj - 1))
    def _():
        acc_ref[0:cb, :] = jnp.zeros((cb, D_MODEL), F32)
        n_blocks = ys_ref.shape[0] // cb
        for_each(used_ref[0], n_blocks, lambda blk: zero_copy(blk).start())
        for_each(used_ref[0], n_blocks, lambda blk: zero_copy(blk).wait())


def _experts(sb_expert, sb_start, sb_nsub, used_blocks, xsorted, w_gu, b_gu, w_down, b_down):
    n_sb = sb_expert.shape[0]
    n_slots = xsorted.shape[0]
    tf = EXPERT_TF
    nj = D_FF // tf

    def spec(shape, fn):
        def index_map(sb, j, sbe, sbs, sbn, used):
            jj = jnp.where(sbn[sb] > 0, j, nj - 1)
            return fn(sbe[sb], jj)
        return pl.BlockSpec(shape, index_map)

    grid_spec = pltpu.PrefetchScalarGridSpec(
        num_scalar_prefetch=4,
        grid=(n_sb, nj),
        in_specs=[
            pl.BlockSpec(memory_space=pl.ANY),
            spec((None, D_MODEL, tf), lambda e, jj: (e, 0, jj)),
            spec((None, D_MODEL, tf), lambda e, jj: (e, 0, nj + jj)),
            spec((None, 1, tf), lambda e, jj: (e, 0, jj)),
            spec((None, 1, tf), lambda e, jj: (e, 0, nj + jj)),
            spec((None, tf, D_MODEL), lambda e, jj: (e, jj, 0)),
            spec((None, 1, D_MODEL), lambda e, jj: (e, 0, 0)),
        ],
        out_specs=pl.BlockSpec(memory_space=pl.ANY),
        scratch_shapes=[
            pltpu.VMEM((EXPERT_R, D_MODEL), F32),
            pltpu.VMEM((EXPERT_R, D_MODEL), BF16),
            pltpu.VMEM((EXPERT_R, D_MODEL), F32),
            pltpu.VMEM((D_MODEL, 2 * tf), BF16),
            pltpu.VMEM((tf, D_MODEL), BF16),
            pltpu.SemaphoreType.DMA((EXPERT_R // EXPERT_PAD,)),
            pltpu.SemaphoreType.DMA(()),
        ],
    )
    return pl.pallas_call(
        _experts_body,
        grid_spec=grid_spec,
        out_shape=jax.ShapeDtypeStruct((n_slots, D_MODEL), F32),
        compiler_params=_cparams(("arbitrary", "arbitrary")),
        name="experts",
    )(sb_expert, sb_start, sb_nsub, used_blocks, xsorted, w_gu, w_gu, b_gu, b_gu, w_down, b_down)


def _combine_body(pos_ref, x1_ref, gate_ref, gf_ref, ys_ref, yp_ref, ysm_ref, buf_ref, sem, *, n_prompt_tiles):
    i = pl.program_id(0)
    tm = x1_ref.shape[0]
    n = pos_ref.shape[0]

    def issue(blk, c):
        for u in range(DMA_UNROLL):
            a = blk * DMA_UNROLL + u
            t = lax.shift_right_logical(a, 2)
            pltpu.make_async_copy(ys_ref.at[pl.ds(pos_ref[a], 1), :], buf_ref.at[u % TOP_K, pl.ds(t, 1), :],
                                  sem).start()
        return c

    lax.fori_loop(0, n // DMA_UNROLL, issue, 0)
    for k in range(TOP_K):
        pltpu.make_async_copy(ys_ref.at[pl.ds(0, tm), :], buf_ref.at[k], sem).wait()

    g = gate_ref[...]
    acc = buf_ref[0] * g[:, 0:1]
    for k in range(1, TOP_K):
        acc = acc + buf_ref[k] * g[:, k:k + 1]
    x = x1_ref[...] + acc
    y = x * lax.rsqrt(jnp.mean(x * x, axis=-1, keepdims=True) + EPS) * gf_ref[...]

    @pl.when(i < n_prompt_tiles)
    def _():
        yp_ref[...] = y

    @pl.when(i >= n_prompt_tiles)
    def _():
        ysm_ref[...] = y


def _combine(pos_flat, x1, gates, g_final, ysorted, t_prompt):
    t = x1.shape[0]
    tm = COMBINE_TM
    npt = t_prompt // tm
    grid_spec = pltpu.PrefetchScalarGridSpec(
        num_scalar_prefetch=0,
        grid=(t // tm,),
        in_specs=[
            pl.BlockSpec((tm * TOP_K,), lambda i: (i,), memory_space=pltpu.SMEM),
            pl.BlockSpec((tm, D_MODEL), lambda i: (i, 0)),
            pl.BlockSpec((tm, TOP_K), lambda i: (i, 0)),
            pl.BlockSpec((1, D_MODEL), lambda i: (0, 0)),
            pl.BlockSpec(memory_space=pl.ANY),
        ],
        out_specs=[
            pl.BlockSpec((tm, D_MODEL), lambda i: (jnp.minimum(i, npt - 1), 0)),
            pl.BlockSpec((tm, D_MODEL), lambda i: (jnp.maximum(i - npt, 0), 0)),
        ],
        scratch_shapes=[pltpu.VMEM((TOP_K, tm, D_MODEL), F32), pltpu.SemaphoreType.DMA(())],
    )
    return pl.pallas_call(
        functools.partial(_combine_body, n_prompt_tiles=npt),
        grid_spec=grid_spec,
        out_shape=[
            jax.ShapeDtypeStruct((t_prompt, D_MODEL), F32),
            jax.ShapeDtypeStruct((t - t_prompt, D_MODEL), F32),
        ],
        compiler_params=_cparams(("arbitrary",)),
        name="combine",
    )(pos_flat, x1, gates, g_final, ysorted)


def _expert_tables(counts, n_assign):
    pad = EXPERT_PAD
    padded = ((counts + pad - 1) // pad) * pad
    seg_start = jnp.cumsum(padded) - padded
    n_sb_e = (counts + EXPERT_R - 1) // EXPERT_R
    sb_cum = jnp.cumsum(n_sb_e)
    n_sb = n_assign // EXPERT_R + N_EXPERTS
    s = jnp.arange(n_sb, dtype=I32)
    total = sb_cum[-1]
    s_eff = jnp.minimum(s, total - 1)
    e = jnp.sum((sb_cum[None, :] <= s_eff[:, None]).astype(I32), axis=1)
    k = s_eff - (sb_cum[e] - n_sb_e[e])
    rows = jnp.clip(counts[e] - k * EXPERT_R, 0, EXPERT_R)
    valid = s < total
    nsub = jnp.where(valid, (rows + pad - 1) // pad, 0)
    start = seg_start[e] + k * EXPERT_R
    used_blocks = (jnp.sum(padded) // pad).reshape(1)
    return seg_start.astype(I32), e, start.astype(I32), nsub.astype(I32), used_blocks.astype(I32)


def kernel(x_prompt, x_sample, state_rglru_h, state_rglru_conv, state_ssd, state_ssd_conv, g_mix, w_in, conv_rg_w, conv_rg_b, rg_wa, rg_ba, rg_wi, rg_bi, rg_lambda, conv_ssd_w, conv_ssd_b, ssd_dt_bias, ssd_a_log, ssd_d, ssd_norm_g, w_out, g_ffn, w_router, b_router, w_gu, b_gu, w_down, b_down, g_final):
    depth = g_mix.shape[0]
    assert depth == 1
    bp, lp, _ = x_prompt.shape
    bs, ls, _ = x_sample.shape
    tp, ts = bp * lp, bs * ls
    t = tp + ts
    hp = SSD_HEADS * SSD_HEAD_DIM
    l = 0

    xp = x_prompt.reshape(tp, D_MODEL)
    xs = jnp.transpose(x_sample, (1, 0, 2)).reshape(ts, D_MODEL)
    w_in_t = jnp.transpose(w_in[l])
    row = lambda v: v.reshape(1, -1)
    pad_heads = lambda v: jnp.pad(v.reshape(1, -1), ((0, 0), (0, LANES - SSD_HEADS)))

    hb, dt_raw = _norm_dt(xp, xs, row(g_mix[l]), w_in_t)
    proj = _in_proj(hb, w_in_t)

    rg_params = (conv_rg_w[l], row(conv_rg_b[l]), rg_wa[l], row(rg_ba[l]), rg_wi[l], row(rg_bi[l]),
                 row(rg_lambda[l]))
    d_exp = row(jnp.repeat(ssd_d[l], SSD_HEAD_DIM))
    ssd_params = (conv_ssd_w[l], row(conv_ssd_b[l]), pad_heads(ssd_dt_bias[l]), pad_heads(ssd_a_log[l]),
                  d_exp, row(ssd_norm_g[l]))

    y_rg, rgh_p, rgc_p = _rglru_prompt(proj, bp, lp, *rg_params)
    y_ssd, ssd_p, ssdc_p = _ssd_prompt(proj, dt_raw, bp, lp, *ssd_params)

    assert tp % (bs * ls) == 0 and bs % 8 == 0
    mix_s3, rgh_s, rgc_s, ssd_s, ssdc_s = _mix_sample(
        proj.reshape(t // bs, bs, D_MAIN), dt_raw.reshape(t // bs, bs, LANES), tp, ls, bs,
        state_rglru_h[l], jnp.transpose(state_rglru_conv[l], (1, 0, 2)),
        state_ssd[l].reshape(bs, hp, SSD_STATE), jnp.transpose(state_ssd_conv[l], (1, 0, 2)),
        rg_params, ssd_params)
    mix_s = mix_s3.reshape(ts, 2 * D_MODEL)

    x1 = _out_proj(y_rg, y_ssd, mix_s, w_out[l], xp, xs)

    wr = jnp.pad(w_router[l], ((0, 0), (0, LANES - N_EXPERTS)))
    br = jnp.pad(b_router[l].reshape(1, -1), ((0, 0), (0, LANES - N_EXPERTS)))
    h2, e_idx, gates, rank, counts = _route(x1, row(g_ffn[l]), wr, br)

    n_assign = t * TOP_K
    n_slots = n_assign + N_EXPERTS * EXPERT_PAD
    counts = counts[0, :N_EXPERTS]
    seg_start, sb_expert, sb_start, sb_nsub, used_blocks = _expert_tables(counts, n_assign)
    pos_flat = (seg_start[e_idx] + rank).reshape(-1)
    xsorted = _scatter(seg_start, counts, pos_flat, h2, n_slots)
    ysorted = _experts(sb_expert, sb_start, sb_nsub, used_blocks, xsorted, w_gu[l], b_gu[l].reshape(N_EXPERTS, 1, -1),
                       w_down[l], b_down[l].reshape(N_EXPERTS, 1, -1))
    y_p, y_s = _combine(pos_flat, x1, gates, row(g_final), ysorted, tp)

    return (y_p.reshape(x_prompt.shape).astype(x_prompt.dtype),
            jnp.transpose(y_s.reshape(ls, bs, D_MODEL), (1, 0, 2)).astype(x_sample.dtype),
            rgh_p.reshape(depth, bp, D_RG),
            rgc_p.reshape(depth, bp, CONV_W - 1, D_RG),
            ssd_p.reshape(depth, bp, SSD_HEADS, SSD_HEAD_DIM, SSD_STATE),
            ssdc_p.reshape(depth, bp, CONV_W - 1, D_XBC),
            rgh_s.reshape(depth, bs, D_RG),
            jnp.transpose(rgc_s, (1, 0, 2)).reshape(depth, bs, CONV_W - 1, D_RG),
            ssd_s.reshape(depth, bs, SSD_HEADS, SSD_HEAD_DIM, SSD_STATE),
            jnp.transpose(ssdc_s, (1, 0, 2)).reshape(depth, bs, CONV_W - 1, D_XBC))
```
